```python
import math
import jax
import jax.numpy as jnp
from jax import lax
import numpy as np

D_MODEL = 1024
BATCH = 8
SEQ = 2048
DEPTH = 2
DEC_BATCH = 128
DEC_SEQ = 1
PAST_LEN = 16384
PAGE_SIZE = 128

RET_HEADS = 4
RET_DK = 64
RET_DV = 128
RET_CHUNK = 128
ROPE_BASE = 10000.0
GDN_HEADS = 4
GDN_DK = 128
GDN_DV = 128
GDN_QKV = 2 * GDN_HEADS * GDN_DK + GDN_HEADS * GDN_DV
GDN_CHUNK = 64
CONV_W = 4
SSM_GROUP = 16
SSM_GROUPS = 32
SSM_WIDTH = SSM_GROUP * SSM_GROUPS
SSM_P = 64
LRU_WIDTH = 512
LRU_BLOCKS = 8
LRU_BLOCK = LRU_WIDTH // LRU_BLOCKS
LRU_C = 8.0
N_BRANCH = 4
BRANCH_W = 512
D_FF = 2816
N_SUB = 3
DEEPNORM_ALPHA = (2 * DEPTH) ** 0.25
DEEPNORM_BETA = (8 * DEPTH) ** -0.25
LN_EPS = 1e-5
IN_SIZES = (RET_HEADS * RET_DK, RET_HEADS * RET_DK, RET_HEADS * RET_DV, RET_HEADS * RET_DV,
            GDN_QKV, GDN_HEADS, GDN_HEADS, GDN_HEADS * GDN_DV,
            SSM_WIDTH, LRU_WIDTH, LRU_WIDTH, N_BRANCH * D_MODEL)
D_IN = sum(IN_SIZES)

kernel_name = "hybrid_parallel_retention_gdn_s5_rglru_step"


def layer_norm(x):
    x = x.astype(jnp.float32)
    mu = jnp.mean(x, axis=-1, keepdims=True)
    xc = x - mu
    return xc * lax.rsqrt(jnp.mean(xc * xc, axis=-1, keepdims=True) + LN_EPS)


def rms_norm(x):
    x = x.astype(jnp.float32)
    return x * lax.rsqrt(jnp.mean(x * x, axis=-1, keepdims=True) + 1e-6)


def l2_norm(x):
    return x * lax.rsqrt(jnp.sum(x * x, axis=-1, keepdims=True) + 1e-6)


def causal_dwconv(x, buf, w):
    T = x.shape[1]
    xp = jnp.concatenate([buf.astype(jnp.float32), x], axis=1)
    y = w[0] * xp[:, 0:T]
    for j in range(1, CONV_W):
        y = y + w[j] * xp[:, j:j + T]
    return y, xp[:, -(CONV_W - 1):]


def rotary(x, pos):
    half = x.shape[-1] // 2
    inv = ROPE_BASE ** (-jnp.arange(half, dtype=jnp.float32) / half)
    ang = pos.astype(jnp.float32)[:, None] * inv[None, :]
    cos = jnp.cos(ang)[None, :, None, :]
    sin = jnp.sin(ang)[None, :, None, :]
    x1, x2 = x[..., :half], x[..., half:]
    return jnp.concatenate([x1 * cos - x2 * sin, x1 * sin + x2 * cos], axis=-1)


def to_chunks(a, n, c):
    a = a.reshape((a.shape[0], n, c) + a.shape[2:])
    return a.transpose((1, 0, 3, 2) + tuple(range(4, a.ndim)))


def from_chunks(o):
    n, B, H, c, d = o.shape
    return o.transpose(1, 0, 3, 2, 4).reshape(B, n * c, H, d)


def retention(q, k, v, s0, chunk):
    B, T, H, _ = q.shape
    n = T // chunk
    log_g = jnp.log(1.0 - 2.0 ** (-5.0 - jnp.arange(H, dtype=jnp.float32)))
    idx = jnp.arange(chunk, dtype=jnp.float32)
    rel = idx[:, None] - idx[None, :]
    dmask = jnp.where(rel[None] >= 0, jnp.exp(log_g[:, None, None] * jnp.maximum(rel, 0.0)[None]), 0.0)
    q_dec = jnp.exp(log_g[:, None] * (idx[None, :] + 1.0))
    k_dec = jnp.exp(log_g[:, None] * (chunk - 1.0 - idx[None, :]))
    c_dec = jnp.exp(log_g * chunk)

    def step(s, inp):
        qc, kc, vc = inp
        inner = jnp.einsum("bhid,bhjd->bhij", qc, kc) * dmask
        o = (jnp.einsum("bhij,bhjv->bhiv", inner, vc)
             + jnp.einsum("bhid,bhdv->bhiv", qc * q_dec[..., None], s))
        s = s * c_dec[:, None, None] + jnp.einsum("bhjd,bhjv->bhdv", kc * k_dec[..., None], vc)
        return s, o

    s, o = lax.scan(step, s0, (to_chunks(q, n, chunk), to_chunks(k, n, chunk), to_chunks(v, n, chunk)))
    return from_chunks(o), s


def gated_delta_rule(q, k, v, g, beta, s0, chunk):
    B, T, H, _ = q.shape
    n = T // chunk
    qc, kc, vc = to_chunks(q, n, chunk), to_chunks(k, n, chunk), to_chunks(v, n, chunk)
    gc, bc = to_chunks(g, n, chunk), to_chunks(beta, n, chunk)
    G = jnp.cumsum(gc, axis=-1)
    tril = jnp.tril(jnp.ones((chunk, chunk), dtype=bool))
    strict = jnp.tril(jnp.ones((chunk, chunk), dtype=bool), -1)
    diff = G[..., :, None] - G[..., None, :]
    decay = jnp.where(tril, jnp.exp(jnp.where(tril, diff, 0.0)), 0.0)
    kb = kc * bc[..., None]
    m = jnp.where(strict, jnp.einsum("nbhid,nbhjd->nbhij", kb, kc) * decay, 0.0)
    eye = jnp.eye(chunk, dtype=m.dtype)
    t_inv = lax.linalg.triangular_solve(m + eye, jnp.broadcast_to(eye, m.shape), left_side=True, lower=True)
    u = jnp.einsum("nbhij,nbhjv->nbhiv", t_inv, vc * bc[..., None])
    w = jnp.einsum("nbhij,nbhjd->nbhid", t_inv, kb * jnp.exp(G)[..., None])
    attn = jnp.einsum("nbhid,nbhjd->nbhij", qc, kc) * decay

    def step(s, inp):
        q_i, k_i, u_i, w_i, g_i, a_i = inp
        v_new = u_i - jnp.einsum("bhcd,bhdv->bhcv", w_i, s)
        o = (jnp.einsum("bhcd,bhdv->bhcv", q_i * jnp.exp(g_i)[..., None], s)
             + jnp.einsum("bhij,bhjv->bhiv", a_i, v_new))
        g_last = g_i[..., -1:]
        s = s * jnp.exp(g_last)[..., None] + jnp.einsum("bhcd,bhcv->bhdv", k_i * jnp.exp(g_last - g_i)[..., None], v_new)
        return s, o

    s, o = lax.scan(step, s0, (qc, kc, u, w, G, attn))
    return from_chunks(o), s


def complex_combine(e1, e2):
    a1r, a1i, b1r, b1i = e1
    a2r, a2i, b2r, b2i = e2
    return (a2r * a1r - a2i * a1i, a2r * a1i + a2i * a1r,
            a2r * b1r - a2i * b1i + b2r, a2r * b1i + a2i * b1r + b2i)


def real_combine(e1, e2):
    a1, b1 = e1
    a2, b2 = e2
    return (a2 * a1, a2 * b1 + b2)


def s5_scan(u, h_re0, h_im0, lam_re, lam_im, log_step, b_re, b_im, c_re, c_im, d_skip):
    f32 = jnp.float32
    lam_re, lam_im = lam_re.astype(f32), lam_im.astype(f32)
    step = jnp.exp(log_step.astype(f32))[:, None]
    mag = jnp.exp(lam_re * step)
    ab_re = mag * jnp.cos(lam_im * step)
    ab_im = mag * jnp.sin(lam_im * step)
    den = lam_re * lam_re + lam_im * lam_im
    nr = ab_re - 1.0
    f_re = (nr * lam_re + ab_im * lam_im) / den
    f_im = (ab_im * lam_re - nr * lam_im) / den
    bb_re = f_re[..., None] * b_re - f_im[..., None] * b_im
    bb_im = f_re[..., None] * b_im + f_im[..., None] * b_re
    bu_re = jnp.einsum("btgc,gpc->btgp", u, bb_re)
    bu_im = jnp.einsum("btgc,gpc->btgp", u, bb_im)
    h_re0, h_im0 = h_re0.astype(f32), h_im0.astype(f32)
    bu_re = bu_re.at[:, 0].add(ab_re * h_re0 - ab_im * h_im0)
    bu_im = bu_im.at[:, 0].add(ab_re * h_im0 + ab_im * h_re0)
    a_re = jnp.broadcast_to(ab_re, bu_re.shape)
    a_im = jnp.broadcast_to(ab_im, bu_im.shape)
    _, _, h_re, h_im = lax.associative_scan(complex_combine, (a_re, a_im, bu_re, bu_im), axis=1)
    y = (jnp.einsum("btgp,gcp->btgc", h_re, c_re) - jnp.einsum("btgp,gcp->btgc", h_im, c_im)
         + d_skip * u)
    return y, h_re[:, -1], h_im[:, -1]


def rg_lru(x, h0, w_a, b_a, w_x, b_x, lam):
    B, T, W = x.shape
    xb = x.reshape(B, T, LRU_BLOCKS, LRU_BLOCK)
    r = jax.nn.sigmoid(jnp.einsum("btni,nij->btnj", xb, w_a).reshape(B, T, W) + b_a)
    i = jax.nn.sigmoid(jnp.einsum("btni,nij->btnj", xb, w_x).reshape(B, T, W) + b_x)
    log_a = -LRU_C * r * jax.nn.softplus(-lam.astype(jnp.float32))
    a = jnp.exp(log_a)
    b = jnp.sqrt(-jnp.expm1(2.0 * log_a)) * (i * x)
    b = b.at[:, 0].add(a[:, 0] * h0.astype(jnp.float32))
    _, h = lax.associative_scan(real_combine, (a, b), axis=1)
    return h, h[:, -1]


def swiglu(u, w_up, w_down):
    a, b = jnp.split(u @ w_up, 2, axis=-1)
    return (jax.nn.silu(a) * b) @ w_down


def token_mixing(u, pos0, states, lp):
    ret_s, gdn_s, gdn_conv, ssm_re, ssm_im, lru_h, lru_conv = states
    B, T, _ = u.shape
    f32 = jnp.float32
    points = np.cumsum(np.array(IN_SIZES))[:-1].tolist()
    (rq, rk, rv, rg, gqkv, ga, gb, gz, su, lx, lg, mg) = jnp.split(u @ lp["w_in"], points, axis=-1)
    pos = pos0 + jnp.arange(T)

    q = rotary(rq.reshape(B, T, RET_HEADS, RET_DK), pos)
    k = rotary(rk.reshape(B, T, RET_HEADS, RET_DK), pos) * (RET_DK ** -0.5)
    v = rv.reshape(B, T, RET_HEADS, RET_DV)
    o, ret_new = retention(q, k, v, ret_s.astype(f32), RET_CHUNK if T % RET_CHUNK == 0 else T)
    y_a = layer_norm(o).reshape(B, T, -1) * jax.nn.silu(rg)

    cqkv, gdn_conv_new = causal_dwconv(gqkv, gdn_conv, lp["gdn_conv_w"])
    cqkv = jax.nn.silu(cqkv)
    nk = GDN_HEADS * GDN_DK
    gq, gk, gv = jnp.split(cqkv, [nk, 2 * nk], axis=-1)
    gq = l2_norm(gq.reshape(B, T, GDN_HEADS, GDN_DK)) * (GDN_DK ** -0.5)
    gk = l2_norm(gk.reshape(B, T, GDN_HEADS, GDN_DK))
    gv = gv.reshape(B, T, GDN_HEADS, GDN_DV)
    g = -jnp.exp(lp["gdn_a_log"].astype(f32)) * jax.nn.softplus(ga + lp["gdn_dt_bias"])
    beta = jax.nn.sigmoid(gb)
    o, gdn_new = gated_delta_rule(gq, gk, gv, g, beta, gdn_s.astype(f32), GDN_CHUNK if T % GDN_CHUNK == 0 else T)
    y_b = (rms_norm(o) * lp["gdn_norm_w"]).reshape(B, T, -1) * jax.nn.silu(gz)

    ys, ssm_re_new, ssm_im_new = s5_scan(su.reshape(B, T, SSM_GROUPS, SSM_GROUP), ssm_re, ssm_im,
                                         lp["ssm_lam_re"], lp["ssm_lam_im"], lp["ssm_log_step"],
                                         lp["ssm_b_re"], lp["ssm_b_im"], lp["ssm_c_re"], lp["ssm_c_im"], lp["ssm_d"])
    ys = jax.nn.gelu(ys.reshape(B, T, SSM_WIDTH))
    y_c = ys * jax.nn.sigmoid(ys @ lp["ssm_glu_w"] + lp["ssm_glu_b"])

    cx, lru_conv_new = causal_dwconv(lx, lru_conv, lp["lru_conv_w"])
    hl, lru_new = rg_lru(cx + lp["lru_conv_b"], lru_h, lp["lru_wa"], lp["lru_ba"], lp["lru_wx"], lp["lru_bx"], lp["lru_lam"])
    y_d = hl * jax.nn.gelu(lg)

    branches = jnp.stack([y_a, y_b, y_c, y_d], axis=2)
    proj = jnp.einsum("btnw,nwd->btnd", branches, lp["w_branch"])
    gates = jax.nn.sigmoid(mg.reshape(B, T, N_BRANCH, D_MODEL))
    out = jnp.sum(gates * proj, axis=2) @ lp["w_out"]
    return out, (ret_new, gdn_new, gdn_conv_new, ssm_re_new, ssm_im_new, lru_new, lru_conv_new)


def trunk_layer(x, c, pos0, states, lp):
    Bc = c.shape[0]
    mod = (jax.nn.silu(c) @ lp["w_ada"] + lp["b_ada"]).reshape(Bc, N_SUB, 3, D_MODEL)

    def modulate(h, s):
        return layer_norm(h) * (1.0 + mod[:, s, 1][:, None]) + mod[:, s, 0][:, None]

    def post(h, s, r, out):
        z = DEEPNORM_ALPHA * h + r * mod[:, s, 2][:, None] * out
        return layer_norm(z) * lp["ln_g"][s] + lp["ln_b"][s]

    x = post(x, 0, 0.5, swiglu(modulate(x, 0), lp["w_ffn_up"][0], lp["w_ffn_down"][0]))
    out, new_states = token_mixing(modulate(x, 1), pos0, states, lp)
    x = post(x, 1, 1.0, out)
    x = post(x, 2, 0.5, swiglu(modulate(x, 2), lp["w_ffn_up"][1], lp["w_ffn_down"][1]))
    return x, new_states


def zero_states(b):
    f32 = jnp.float32
    return (jnp.zeros((b, RET_HEADS, RET_DK, RET_DV), f32),
            jnp.zeros((b, GDN_HEADS, GDN_DK, GDN_DV), f32),
            jnp.zeros((b, CONV_W - 1, GDN_QKV), f32),
            jnp.zeros((b, SSM_GROUPS, SSM_P), f32),
            jnp.zeros((b, SSM_GROUPS, SSM_P), f32),
            jnp.zeros((b, LRU_WIDTH), f32),
            jnp.zeros((b, CONV_W - 1, LRU_WIDTH), f32))


def setup_inputs(seed: int = 0) -> dict:
    key = jax.random.key(seed)
    ks = iter(jax.random.split(key, 64))
    f32 = jnp.float32
    L = DEPTH

    def nrm(shape, scale):
        return jax.random.normal(next(ks), shape, f32) * scale

    def unif(shape, lo, hi):
        return jax.random.uniform(next(ks), shape, f32, lo, hi)

    dt = jnp.exp(unif((L, GDN_HEADS), math.log(1e-3), math.log(1e-1)))
    a0 = jnp.sqrt(unif((L, LRU_WIDTH), 0.81, 0.998))
    s_lam = a0 ** (1.0 / LRU_C)
    return {
        "x_prompt": nrm((BATCH, SEQ, D_MODEL), 1.0),
        "x_sample": nrm((DEC_BATCH, DEC_SEQ, D_MODEL), 1.0),
        "c_prompt": nrm((BATCH, D_MODEL), 1.0),
        "c_sample": nrm((DEC_BATCH, D_MODEL), 1.0),
        "state_ret": nrm((L, DEC_BATCH, RET_HEADS, RET_DK, RET_DV), 1.0),
        "state_gdn": nrm((L, DEC_BATCH, GDN_HEADS, GDN_DK, GDN_DV), 0.1),
        "state_gdn_conv": nrm((L, DEC_BATCH, CONV_W - 1, GDN_QKV), 1.0),
        "state_ssm_re": nrm((L, DEC_BATCH, SSM_GROUPS, SSM_P), 0.5),
        "state_ssm_im": nrm((L, DEC_BATCH, SSM_GROUPS, SSM_P), 0.5),
        "state_lru": nrm((L, DEC_BATCH, LRU_WIDTH), 0.5),
        "state_lru_conv": nrm((L, DEC_BATCH, CONV_W - 1, LRU_WIDTH), 1.0),
        "w_ada": nrm((L, D_MODEL, N_SUB * 3 * D_MODEL), 0.5 * D_MODEL ** -0.5),
        "b_ada": nrm((L, N_SUB * 3 * D_MODEL), 0.02),
        "ln_g": 1.0 + nrm((L, N_SUB, D_MODEL), 0.01),
        "ln_b": nrm((L, N_SUB, D_MODEL), 0.01),
        "w_ffn_up": nrm((L, 2, D_MODEL, 2 * D_FF), D_MODEL ** -0.5),
        "w_ffn_down": nrm((L, 2, D_FF, D_MODEL), DEEPNORM_BETA * D_FF ** -0.5),
        "w_in": nrm((L, D_MODEL, D_IN), D_MODEL ** -0.5),
        "gdn_conv_w": nrm((L, CONV_W, GDN_QKV), CONV_W ** -0.5),
        "gdn_a_log": jnp.log(unif((L, GDN_HEADS), 1.0, 16.0)),
        "gdn_dt_bias": dt + jnp.log(-jnp.expm1(-dt)),
        "gdn_norm_w": 1.0 + nrm((L, GDN_DV), 0.01),
        "ssm_lam_re": -0.5 + nrm((L, SSM_GROUPS, SSM_P), 0.01),
        "ssm_lam_im": jnp.pi * jnp.arange(SSM_P, dtype=f32)[None, None, :] + nrm((L, SSM_GROUPS, SSM_P), 0.01),
        "ssm_log_step": unif((L, SSM_GROUPS), math.log(1e-3), math.log(1e-1)),
        "ssm_b_re": nrm((L, SSM_GROUPS, SSM_P, SSM_GROUP), (2.0 * SSM_GROUP) ** -0.5),
        "ssm_b_im": nrm((L, SSM_GROUPS, SSM_P, SSM_GROUP), (2.0 * SSM_GROUP) ** -0.5),
        "ssm_c_re": nrm((L, SSM_GROUPS, SSM_GROUP, SSM_P), (2.0 * SSM_P) ** -0.5),
        "ssm_c_im": nrm((L, SSM_GROUPS, SSM_GROUP, SSM_P), (2.0 * SSM_P) ** -0.5),
        "ssm_d": nrm((L, SSM_GROUPS, SSM_GROUP), 1.0),
        "ssm_glu_w": nrm((L, SSM_WIDTH, SSM_WIDTH), SSM_WIDTH ** -0.5),
        "ssm_glu_b": nrm((L, SSM_WIDTH), 0.01),
        "lru_conv_w": nrm((L, CONV_W, LRU_WIDTH), CONV_W ** -0.5),
        "lru_conv_b": nrm((L, LRU_WIDTH), 0.01),
        "lru_wa": nrm((L, LRU_BLOCKS, LRU_BLOCK, LRU_BLOCK), LRU_BLOCK ** -0.5),
        "lru_ba": nrm((L, LRU_WIDTH), 0.01),
        "lru_wx": nrm((L, LRU_BLOCKS, LRU_BLOCK, LRU_BLOCK), LRU_BLOCK ** -0.5),
        "lru_bx": nrm((L, LRU_WIDTH), 0.01),
        "lru_lam": jnp.log(s_lam) - jnp.log1p(-s_lam),
        "w_branch": nrm((L, N_BRANCH, BRANCH_W, D_MODEL), BRANCH_W ** -0.5),
        "w_out": nrm((L, D_MODEL, D_MODEL), DEEPNORM_BETA * D_MODEL ** -0.5),
    }


def reference(x_prompt, x_sample, c_prompt, c_sample, state_ret, state_gdn, state_gdn_conv,
              state_ssm_re, state_ssm_im, state_lru, state_lru_conv, w_ada, b_ada, ln_g, ln_b,
              w_ffn_up, w_ffn_down, w_in, gdn_conv_w, gdn_a_log, gdn_dt_bias, gdn_norm_w,
              ssm_lam_re, ssm_lam_im, ssm_log_step, ssm_b_re, ssm_b_im, ssm_c_re, ssm_c_im, ssm_d,
              ssm_glu_w, ssm_glu_b, lru_conv_w, lru_conv_b, lru_wa, lru_ba, lru_wx, lru_bx, lru_lam,
              w_branch, w_out):
    yp, ys = x_prompt, x_sample
    new_p, new_s = [], []
    for l in range(DEPTH):
        lp = dict(w_ada=w_ada[l], b_ada=b_ada[l], ln_g=ln_g[l], ln_b=ln_b[l],
                  w_ffn_up=w_ffn_up[l], w_ffn_down=w_ffn_down[l], w_in=w_in[l],
                  gdn_conv_w=gdn_conv_w[l], gdn_a_log=gdn_a_log[l], gdn_dt_bias=gdn_dt_bias[l],
                  gdn_norm_w=gdn_norm_w[l], ssm_lam_re=ssm_lam_re[l], ssm_lam_im=ssm_lam_im[l],
                  ssm_log_step=ssm_log_step[l], ssm_b_re=ssm_b_re[l], ssm_b_im=ssm_b_im[l],
                  ssm_c_re=ssm_c_re[l], ssm_c_im=ssm_c_im[l], ssm_d=ssm_d[l],
                  ssm_glu_w=ssm_glu_w[l], ssm_glu_b=ssm_glu_b[l], lru_conv_w=lru_conv_w[l],
                  lru_conv_b=lru_conv_b[l], lru_wa=lru_wa[l], lru_ba=lru_ba[l], lru_wx=lru_wx[l],
                  lru_bx=lru_bx[l], lru_lam=lru_lam[l], w_branch=w_branch[l], w_out=w_out[l])
        yp, sp = trunk_layer(yp, c_prompt, 0, zero_states(x_prompt.shape[0]), lp)
        sample_states = (state_ret[l], state_gdn[l], state_gdn_conv[l], state_ssm_re[l],
                         state_ssm_im[l], state_lru[l], state_lru_conv[l])
        ys, ss = trunk_layer(ys, c_sample, PAST_LEN, sample_states, lp)
        new_p.append(sp)
        new_s.append(ss)
    ret_p, gdn_p, gdn_conv_p, ssm_re_p, ssm_im_p, lru_p, lru_conv_p = [jnp.stack(z) for z in zip(*new_p)]
    ret_s, gdn_s, gdn_conv_s, ssm_re_s, ssm_im_s, lru_s, lru_conv_s = [jnp.stack(z) for z in zip(*new_s)]
    return (yp, ys, ret_p, ret_s, gdn_p, gdn_s, gdn_conv_p, gdn_conv_s, ssm_re_p, ssm_re_s,
            ssm_im_p, ssm_im_s, lru_p, lru_s, lru_conv_p, lru_conv_s)
```

```python
import functools
import math

import numpy as np
import jax
import jax.numpy as jnp
from jax import lax
from jax.experimental import pallas as pl
from jax.experimental.pallas import tpu as pltpu

F32 = jnp.float32
BF16 = jnp.bfloat16
HI = lax.Precision.HIGHEST

D_MODEL = 1024
RET_HEADS, RET_DK, RET_DV, RET_CHUNK = 4, 64, 128, 128
ROPE_BASE = 10000.0
GDN_HEADS, GDN_DK, GDN_DV, GDN_CHUNK = 4, 128, 128, 64
GDN_QKV = 2 * GDN_HEADS * GDN_DK + GDN_HEADS * GDN_DV
CONV_W = 4
SSM_GROUP, SSM_GROUPS, SSM_P = 16, 32, 64
SSM_WIDTH = SSM_GROUP * SSM_GROUPS
SSM_STATE = SSM_GROUPS * SSM_P
LRU_WIDTH, LRU_BLOCKS = 512, 8
LRU_C = 8.0
N_BRANCH, BRANCH_W = 4, 512
D_FF = 2816
N_SUB = 3
LN_EPS = 1e-5
PAST_LEN = 16384

RET_QK = RET_HEADS * RET_DK
RET_V = RET_HEADS * RET_DV
RET_W = 2 * RET_QK + 2 * RET_V
GDN_K = GDN_HEADS * GDN_DK
QKT_ROWS = 2 * RET_QK + 2 * GDN_K
LANE = 128
SUBLANE = 8
FF_CHUNK = D_FF // 2
VMEM_LIMIT = 56 * 1024 * 1024


def _cparams(*sem):
    return pltpu.CompilerParams(dimension_semantics=sem, vmem_limit_bytes=VMEM_LIMIT)


def _resident(shape, index_map):
    return pl.BlockSpec(shape, index_map, pipeline_mode=pl.Buffered(1))


def _ln(x):
    mu = jnp.mean(x, axis=-1, keepdims=True)
    xc = x - mu
    return xc * lax.rsqrt(jnp.mean(xc * xc, axis=-1, keepdims=True) + LN_EPS)


def _silu(x):
    return x * jax.nn.sigmoid(x)


def _softplus(x):
    return jnp.maximum(x, 0.0) + jnp.log1p(jnp.exp(-jnp.abs(x)))


def _bdot(a, w):
    return jnp.dot(a.astype(BF16), w.astype(BF16), preferred_element_type=F32)


def _hdot(a, b):
    return jnp.dot(a, b, preferred_element_type=F32, precision=HI)


def _dot_nt(a, b, precision=None):
    return lax.dot_general(a, b, (((1,), (1,)), ((), ())),
                           preferred_element_type=F32, precision=precision)


def _iota(shape, dim):
    return lax.broadcasted_iota(jnp.int32, shape, dim)


def _ada_kernel(c_ref, w_ref, b_ref, o_ref):
    o_ref[...] = _bdot(_silu(c_ref[...]), w_ref[...]) + b_ref[...]


def _ada(c_all, w_ada, b_ada):
    depth, _, n_out = w_ada.shape
    rows = c_all.shape[0]
    tn = 1152
    return pl.pallas_call(
        _ada_kernel,
        grid=(depth, n_out // tn),
        in_specs=[
            pl.BlockSpec((rows, D_MODEL), lambda l, j: (0, 0)),
            pl.BlockSpec((None, D_MODEL, tn), lambda l, j: (l, 0, j)),
            pl.BlockSpec((None, 1, tn), lambda l, j: (l, 0, j)),
        ],
        out_specs=pl.BlockSpec((None, rows, tn), lambda l, j: (l, 0, j)),
        out_shape=jax.ShapeDtypeStruct((depth, rows, n_out), F32),
        compiler_params=_cparams("parallel", "parallel"),
        name="ada",
    )(c_all, w_ada, b_ada.reshape(depth, 1, n_out))


class _Mod:
    def __init__(self, arr, layer, per_row, rows_per_batch=None):
        self.arr, self.layer, self.per_row, self.rpb = arr, layer, per_row, rows_per_batch

    def spec(self, sub, which, tm):
        k, l = 3 * sub + which, self.layer
        if self.per_row:
            return pl.BlockSpec((None, tm, D_MODEL), lambda i: (l, i, k))
        tiles = self.rpb // tm
        return pl.BlockSpec((None, None, None, 1, D_MODEL), lambda i: (l, k, i // tiles, 0, 0))


def _ffn_kernel(x_ref, sh_ref, sc_ref, gt_ref, wup_ref, wdn_ref, g_ref, b_ref, o_ref, *, alpha):
    x = x_ref[...]
    h = (_ln(x) * (1.0 + sc_ref[...]) + sh_ref[...]).astype(BF16)
    acc = jnp.zeros(x.shape, F32)
    for j in range(D_FF // FF_CHUNK):
        lo = j * FF_CHUNK
        a = jnp.dot(h, wup_ref[:, lo:lo + FF_CHUNK], preferred_element_type=F32)
        b = jnp.dot(h, wup_ref[:, D_FF + lo:D_FF + lo + FF_CHUNK], preferred_element_type=F32)
        act = (_silu(a) * b).astype(BF16)
        acc = acc + jnp.dot(act, wdn_ref[lo:lo + FF_CHUNK, :], preferred_element_type=F32)
    z = alpha * x + 0.5 * gt_ref[...] * acc
    o_ref[...] = _ln(z) * g_ref[...] + b_ref[...]


def _ffn(x, mod, sub, which, wup, wdn, ln_g, ln_b, alpha, tm):
    n = x.shape[0]
    l = mod.layer
    row = pl.BlockSpec((tm, D_MODEL), lambda i: (i, 0))
    return pl.pallas_call(
        functools.partial(_ffn_kernel, alpha=alpha),
        grid=(n // tm,),
        in_specs=[
            row, mod.spec(sub, 0, tm), mod.spec(sub, 1, tm), mod.spec(sub, 2, tm),
            _resident((None, None, D_MODEL, 2 * D_FF), lambda i: (l, which, 0, 0)),
            _resident((None, None, D_FF, D_MODEL), lambda i: (l, which, 0, 0)),
            pl.BlockSpec((None, None, 1, D_MODEL), lambda i: (l, sub, 0, 0)),
            pl.BlockSpec((None, None, 1, D_MODEL), lambda i: (l, sub, 0, 0)),
        ],
        out_specs=row,
        out_shape=jax.ShapeDtypeStruct((n, D_MODEL), F32),
        compiler_params=_cparams("parallel"),
        name="ffn",
    )(x, mod.arr, mod.arr, mod.arr, wup, wdn, ln_g, ln_b)


_IN_PIECES = (("ret", RET_W), ("gqkv", GDN_QKV), ("gz", 512), ("su", SSM_WIDTH),
              ("lx", LRU_WIDTH), ("lg", LRU_WIDTH), ("mg", N_BRANCH * D_MODEL))
IN_MAIN = sum(w for _, w in _IN_PIECES)


def _inproj_kernel(x_ref, sh_ref, sc_ref, w_ref, ws_ref, *o_refs):
    h = (_ln(x_ref[...]) * (1.0 + sc_ref[...]) + sh_ref[...]).astype(BF16)
    lo = 0
    for (_, width), o_ref in zip(_IN_PIECES, o_refs[:-1]):
        o_ref[...] = jnp.dot(h, w_ref[:, lo:lo + width], preferred_element_type=F32)
        lo += width
    o_refs[-1][...] = jnp.dot(h, ws_ref[...], preferred_element_type=F32)


def _inproj(x, mod, w_main, w_small, tm):
    n = x.shape[0]
    l = mod.layer
    widths = [w for _, w in _IN_PIECES] + [LANE]
    outs = pl.pallas_call(
        _inproj_kernel,
        grid=(n // tm,),
        in_specs=[
            pl.BlockSpec((tm, D_MODEL), lambda i: (i, 0)),
            mod.spec(1, 0, tm), mod.spec(1, 1, tm),
            _resident((None, D_MODEL, IN_MAIN), lambda i: (l, 0, 0)),
            _resident((None, D_MODEL, LANE), lambda i: (l, 0, 0)),
        ],
        out_specs=[pl.BlockSpec((tm, w), lambda i: (i, 0)) for w in widths],
        out_shape=[jax.ShapeDtypeStruct((n, w), F32) for w in widths],
        compiler_params=_cparams("parallel"),
        name="inproj",
    )(x, mod.arr, mod.arr, w_main, w_small)
    names = [nm for nm, _ in _IN_PIECES] + ["gab"]
    return dict(zip(names, outs))


def _rope_tables(pos):
    half = RET_DK // 2
    inv = ROPE_BASE ** (-jnp.arange(half, dtype=F32) / half)
    ang = pos.astype(F32)[:, None] * inv[None, :]
    cos, sin = jnp.cos(ang), jnp.sin(ang)
    cos_t = jnp.tile(jnp.concatenate([cos, cos], axis=-1), (1, RET_HEADS))
    sin_t = jnp.tile(jnp.concatenate([-sin, sin], axis=-1), (1, RET_HEADS))
    return cos_t, sin_t


def _rotate(z, cos, sin):
    half = RET_DK // 2
    first = (_iota(z.shape, 1) % RET_DK) < half
    swapped = jnp.where(first, pltpu.roll(z, RET_QK - half, 1), pltpu.roll(z, half, 1))
    return z * cos + swapped * sin


def _ret_gammas():
    return [1.0 - 2.0 ** (-5.0 - h) for h in range(RET_HEADS)]


def _ret_consts(chunk):
    log_g = np.log(np.array(_ret_gammas(), np.float64))
    idx = np.arange(chunk, dtype=np.float64)
    rel = idx[:, None] - idx[None, :]
    dmask = np.where(rel[None] >= 0, np.exp(log_g[:, None, None] * np.maximum(rel, 0.0)[None]), 0.0)
    qdec = np.repeat(np.exp(log_g[None, :] * (idx[:, None] + 1.0)), RET_DK, axis=1)
    kdec = np.repeat(np.exp(log_g[None, :] * (chunk - 1.0 - idx[:, None])), RET_DK, axis=1)
    head_r = np.arange(RET_QK) // RET_DK
    head_c = np.arange(RET_V) // RET_DV
    bd = (head_r[:, None] == head_c[None, :]).astype(np.float64)
    cd = bd * np.exp(log_g * chunk)[head_r][:, None]
    f = lambda a: jnp.asarray(a, F32)
    return f(dmask), f(qdec), f(kdec), f(cd), f(bd)


def _ret_kernel(x_ref, cos_ref, sin_ref, dmask_ref, qdec_ref, kdec_ref, cd_ref, bd_ref,
                y_ref, st_ref, s_scr):
    c = pl.program_id(1)

    @pl.when(c == 0)
    def _():
        s_scr[...] = jnp.zeros(s_scr.shape, F32)

    x = x_ref[...]
    q = _rotate(x[:, 0:RET_QK], cos_ref[...], sin_ref[...])
    k = _rotate(x[:, RET_QK:2 * RET_QK], cos_ref[...], sin_ref[...]) * (RET_DK ** -0.5)
    v = x[:, 2 * RET_QK:2 * RET_QK + RET_V].astype(BF16)
    gate = x[:, 2 * RET_QK + RET_V:]
    s = s_scr[...]
    cross = _bdot(q * qdec_ref[...], s)
    kb = k.astype(BF16)
    head = _iota(q.shape, 1) // RET_DK
    outs = []
    for h in range(RET_HEADS):
        qh = jnp.where(head == h, q, 0.0).astype(BF16)
        inner = _dot_nt(qh, kb) * dmask_ref[h]
        oh = _bdot(inner, v[:, h * RET_DV:(h + 1) * RET_DV]) + cross[:, h * RET_DV:(h + 1) * RET_DV]
        outs.append(_ln(oh))
    y_ref[...] = jnp.concatenate(outs, axis=1) * _silu(gate)
    kd_t = (k * kdec_ref[...]).T.astype(BF16)
    upd = jnp.dot(kd_t, v, preferred_element_type=F32)
    s_new = s * cd_ref[...] + upd * bd_ref[...]
    s_scr[...] = s_new

    @pl.when(c == pl.num_programs(1) - 1)
    def _():
        for h in range(RET_HEADS):
            st_ref[0, h] = s_new[h * RET_DK:(h + 1) * RET_DK, h * RET_DV:(h + 1) * RET_DV]


def _retention_prompt(ret, nb, t):
    chunk = RET_CHUNK
    nc = t // chunk
    cos_t, sin_t = _rope_tables(jnp.arange(t))
    consts = _ret_consts(chunk)
    full = lambda a: pl.BlockSpec(a.shape, lambda b, c: (0,) * a.ndim)
    return pl.pallas_call(
        _ret_kernel,
        grid=(nb, nc),
        in_specs=[
            pl.BlockSpec((chunk, RET_W), lambda b, c: (b * nc + c, 0)),
            pl.BlockSpec((chunk, RET_QK), lambda b, c: (c, 0)),
            pl.BlockSpec((chunk, RET_QK), lambda b, c: (c, 0)),
        ] + [full(a) for a in consts],
        out_specs=[
            pl.BlockSpec((chunk, RET_V), lambda b, c: (b * nc + c, 0)),
            pl.BlockSpec((1, RET_HEADS, RET_DK, RET_DV), lambda b, c: (b, 0, 0, 0)),
        ],
        out_shape=[
            jax.ShapeDtypeStruct((nb * t, RET_V), F32),
            jax.ShapeDtypeStruct((nb, RET_HEADS, RET_DK, RET_DV), F32),
        ],
        scratch_shapes=[pltpu.VMEM((RET_QK, RET_V), F32)],
        compiler_params=_cparams("parallel", "arbitrary"),
        name="retention",
    )(ret, cos_t, sin_t, *consts)


def _l2n(x):
    return x * lax.rsqrt(jnp.sum(x * x, axis=-1, keepdims=True) + 1e-6)


def _rms(x):
    return x * lax.rsqrt(jnp.mean(x * x, axis=-1, keepdims=True) + 1e-6)


def _gate_lanes(gab, alog_row, dtb_row):
    g = -jnp.exp(alog_row) * _softplus(gab + dtb_row)
    return jnp.where(_iota(gab.shape, 1) < GDN_HEADS, g, jax.nn.sigmoid(gab))


def _expand8():
    return (_iota((LANE, 8 * LANE), 1) // LANE == _iota((LANE, 8 * LANE), 0)).astype(F32)


def _gdn_kernel(x_ref, gab_ref, gz_ref, cw_ref, alog_ref, dtb_ref, nw_ref,
                y_ref, st_ref, cv_ref, xbuf, s_scr):
    c = pl.program_id(1)
    ch = GDN_CHUNK

    @pl.when(c == 0)
    def _():
        xbuf[0:SUBLANE, :] = jnp.zeros((SUBLANE, GDN_QKV), F32)
        s_scr[...] = jnp.zeros(s_scr.shape, F32)

    x = x_ref[...]
    xbuf[SUBLANE:SUBLANE + ch, :] = x
    conv = cw_ref[3:4, :] * x
    for j in range(CONV_W - 1):
        lo = SUBLANE - (CONV_W - 1) + j
        conv = conv + cw_ref[j:j + 1, :] * xbuf[lo:lo + ch, :]
    cv_ref[0] = xbuf[SUBLANE + ch - (CONV_W - 1):SUBLANE + ch, :]
    xbuf[0:SUBLANE, :] = x[ch - SUBLANE:, :]
    cq = _silu(conv)

    gl = _gate_lanes(gab_ref[...], alog_ref[...], dtb_ref[...])
    tril = _iota((ch, ch), 0) >= _iota((ch, ch), 1)
    strict = _iota((ch, ch), 0) > _iota((ch, ch), 1)
    is_g = _iota(gl.shape, 1) < GDN_HEADS
    csum = _hdot(tril.astype(F32), jnp.where(is_g, gl, 0.0))
    gx = _hdot(jnp.where(is_g, csum, gl), _expand8())
    pick = (_iota((GDN_HEADS * ch, LANE), 1) == _iota((GDN_HEADS * ch, LANE), 0) // ch).astype(F32)
    g_rows = _dot_nt(pick, csum, precision=HI)
    eye = (_iota((ch, ch), 0) == _iota((ch, ch), 1)).astype(F32)
    norm_w = nw_ref[...]
    gz = gz_ref[...]
    outs = []
    for h in range(GDN_HEADS):
        qh = _l2n(cq[:, h * GDN_DK:(h + 1) * GDN_DK]) * (GDN_DK ** -0.5)
        kh = _l2n(cq[:, GDN_K + h * GDN_DK:GDN_K + (h + 1) * GDN_DK])
        vh = cq[:, 2 * GDN_K + h * GDN_DV:2 * GDN_K + (h + 1) * GDN_DV]
        g_b = gx[:, h * LANE:(h + 1) * LANE]
        beta = gx[:, (GDN_HEADS + h) * LANE:(GDN_HEADS + h + 1) * LANE]
        diff = g_b[:, :ch] - g_rows[h * ch:(h + 1) * ch, :]
        decay = jnp.where(tril, jnp.exp(jnp.where(tril, diff, 0.0)), 0.0)
        kbeta = kh * beta
        m = jnp.where(strict, _dot_nt(kbeta, kh, precision=HI) * decay, 0.0)
        p = -m
        t_inv = eye + p
        for _ in range(int(math.log2(ch)) - 1):
            p = _hdot(p, p)
            t_inv = t_inv + _hdot(t_inv, p)
        e_g = jnp.exp(g_b)
        uw = _hdot(t_inv, jnp.concatenate([vh * beta, kbeta * e_g], axis=1))
        attn = _dot_nt(qh.astype(BF16), kh.astype(BF16)) * decay
        s = s_scr[h]
        ws_qs = _bdot(jnp.concatenate([uw[:, GDN_DV:], qh * e_g], axis=0), s)
        v_new = uw[:, :GDN_DV] - ws_qs[:ch]
        o = ws_qs[ch:] + _bdot(attn, v_new)
        g_last = g_b[ch - 1:ch, :]
        k_dec_t = (kh * jnp.exp(g_last - g_b)).T.astype(BF16)
        s_scr[h] = s * jnp.exp(g_last) + jnp.dot(k_dec_t, v_new.astype(BF16), preferred_element_type=F32)
        outs.append(_rms(o) * norm_w * _silu(gz[:, h * GDN_DV:(h + 1) * GDN_DV]))
    y_ref[...] = jnp.concatenate(outs, axis=1)

    @pl.when(c == pl.num_programs(1) - 1)
    def _():
        st_ref[0] = s_scr[...]


def _gdn_prompt(gqkv, gab, gz, conv_w, alog_row, dtb_row, norm_w, nb, t):
    ch = GDN_CHUNK
    nc = t // ch
    row = lambda w: pl.BlockSpec((ch, w), lambda b, c: (b * nc + c, 0))
    full = lambda a: pl.BlockSpec(a.shape, lambda b, c: (0,) * a.ndim)
    return pl.pallas_call(
        _gdn_kernel,
        grid=(nb, nc),
        in_specs=[row(GDN_QKV), row(LANE), row(GDN_HEADS * GDN_DV),
                  full(conv_w), full(alog_row), full(dtb_row), full(norm_w)],
        out_specs=[
            row(GDN_HEADS * GDN_DV),
            pl.BlockSpec((1, GDN_HEADS, GDN_DK, GDN_DV), lambda b, c: (b, 0, 0, 0)),
            pl.BlockSpec((1, CONV_W - 1, GDN_QKV), lambda b, c: (b, 0, 0)),
        ],
        out_shape=[
            jax.ShapeDtypeStruct((nb * t, GDN_HEADS * GDN_DV), F32),
            jax.ShapeDtypeStruct((nb, GDN_HEADS, GDN_DK, GDN_DV), F32),
            jax.ShapeDtypeStruct((nb, CONV_W - 1, GDN_QKV), F32),
        ],
        scratch_shapes=[pltpu.VMEM((SUBLANE + ch, GDN_QKV), F32),
                        pltpu.VMEM((GDN_HEADS, GDN_DK, GDN_DV), F32)],
        compiler_params=_cparams("parallel", "arbitrary"),
        name="gdn",
    )(gqkv, gab, gz, conv_w, alog_row, dtb_row, norm_w)


SSM_BLK = 4
SSM_BLK_STATE = SSM_STATE // SSM_BLK


def _s5_params(lam_re, lam_im, log_step, b_re, b_im, c_re, c_im):
    step = jnp.exp(log_step.astype(F32))[:, None]
    mag = jnp.exp(lam_re * step)
    ab_re = mag * jnp.cos(lam_im * step)
    ab_im = mag * jnp.sin(lam_im * step)
    den = lam_re * lam_re + lam_im * lam_im
    nr = ab_re - 1.0
    f_re = (nr * lam_re + ab_im * lam_im) / den
    f_im = (ab_im * lam_re - nr * lam_im) / den
    bb_re = f_re[..., None] * b_re - f_im[..., None] * b_im
    bb_im = f_re[..., None] * b_im + f_im[..., None] * b_re
    gpb = SSM_GROUPS // SSM_BLK
    eye = jnp.eye(gpb, dtype=F32)

    def in_mat(bb):
        bb = bb.reshape(SSM_BLK, gpb, SSM_P, SSM_GROUP)
        m = jnp.einsum("jgpc,gk->jgckp", bb, eye)
        return m.reshape(SSM_BLK, gpb * SSM_GROUP, gpb * SSM_P).astype(BF16)

    def out_mat(cc):
        cc = cc.reshape(SSM_BLK, gpb, SSM_GROUP, SSM_P)
        m = jnp.einsum("jgcp,gk->jgpkc", cc, eye)
        return m.reshape(SSM_BLK, gpb * SSM_P, gpb * SSM_GROUP).astype(BF16)

    return (ab_re.reshape(1, SSM_STATE), ab_im.reshape(1, SSM_STATE),
            in_mat(bb_re), in_mat(bb_im), out_mat(c_re), out_mat(c_im))


def _s5_readout(u, h_re, h_im, cre_ref, cim_ref, d_row, gw_ref, gb_row):
    ys = []
    for j in range(SSM_BLK):
        sl = slice(j * SSM_BLK_STATE, (j + 1) * SSM_BLK_STATE)
        ys.append(_bdot(h_re[:, sl], cre_ref[j]) - _bdot(h_im[:, sl], cim_ref[j]))
    y = jax.nn.gelu(jnp.concatenate(ys, axis=1) + d_row * u)
    return y * jax.nn.sigmoid(_bdot(y, gw_ref[...]) + gb_row)


def _s5_kernel(u_ref, are_ref, aim_ref, bre_ref, bim_ref, cre_ref, cim_ref, d_ref, gw_ref, gb_ref,
               y_ref, hre_ref, him_ref, bu_re, bu_im, h_re, h_im, *, nb):
    i = pl.program_id(0)
    tc = u_ref.shape[0]
    rows = tc * nb

    @pl.when(i == 0)
    def _():
        h_re[...] = jnp.zeros(h_re.shape, F32)
        h_im[...] = jnp.zeros(h_im.shape, F32)

    u = u_ref[...].reshape(rows, SSM_WIDTH)
    for j in range(SSM_BLK):
        uj = u[:, j * LANE:(j + 1) * LANE]
        sl = slice(j * SSM_BLK_STATE, (j + 1) * SSM_BLK_STATE)
        bu_re[:, sl] = _bdot(uj, bre_ref[j])
        bu_im[:, sl] = _bdot(uj, bim_ref[j])
    for j in range(SSM_BLK):
        sl = slice(j * SSM_BLK_STATE, (j + 1) * SSM_BLK_STATE)
        a_re = jnp.broadcast_to(are_ref[:, sl], (nb, SSM_BLK_STATE))
        a_im = jnp.broadcast_to(aim_ref[:, sl], (nb, SSM_BLK_STATE))

        def body(t, carry):
            hr, hi = carry
            r = pl.ds(pl.multiple_of(t * nb, nb), nb)
            nr = a_re * hr - a_im * hi + bu_re[r, sl]
            ni = a_re * hi + a_im * hr + bu_im[r, sl]
            bu_re[r, sl] = nr
            bu_im[r, sl] = ni
            return nr, ni

        hr, hi = lax.fori_loop(0, tc, body, (h_re[:, sl], h_im[:, sl]), unroll=4)
        h_re[:, sl] = hr
        h_im[:, sl] = hi
    y = _s5_readout(u, bu_re[...], bu_im[...], cre_ref, cim_ref, d_ref[...], gw_ref, gb_ref[...])
    y_ref[...] = y.reshape(tc, nb, SSM_WIDTH)
    hre_ref[...] = h_re[...]
    him_ref[...] = h_im[...]


def _s5_prompt(u_t, params, d_row, glu_w, glu_b, tc):
    t, nb, _ = u_t.shape
    a_re, a_im, bre, bim, cre, cim = params
    full = lambda a: pl.BlockSpec(a.shape, lambda i: (0,) * a.ndim)
    blk = pl.BlockSpec((tc, nb, SSM_WIDTH), lambda i: (i, 0, 0))
    st = pl.BlockSpec((nb, SSM_STATE), lambda i: (0, 0))
    return pl.pallas_call(
        functools.partial(_s5_kernel, nb=nb),
        grid=(t // tc,),
        in_specs=[blk] + [full(a) for a in (a_re, a_im, bre, bim, cre, cim, d_row, glu_w, glu_b)],
        out_specs=[blk, st, st],
        out_shape=[jax.ShapeDtypeStruct((t, nb, SSM_WIDTH), F32),
                   jax.ShapeDtypeStruct((nb, SSM_STATE), F32),
                   jax.ShapeDtypeStruct((nb, SSM_STATE), F32)],
        scratch_shapes=[pltpu.VMEM((tc * nb, SSM_STATE), F32), pltpu.VMEM((tc * nb, SSM_STATE), F32),
                        pltpu.VMEM((nb, SSM_STATE), F32), pltpu.VMEM((nb, SSM_STATE), F32)],
        compiler_params=_cparams("arbitrary"),
        name="s5",
    )(u_t, a_re, a_im, bre, bim, cre, cim, d_row, glu_w, glu_b)


def _lru_gates(cx, wa_ref, wx_ref, ba_row, bx_row, lam_row):
    r = jax.nn.sigmoid(_bdot(cx, wa_ref[...]) + ba_row)
    i = jax.nn.sigmoid(_bdot(cx, wx_ref[...]) + bx_row)
    log_a = -LRU_C * r * _softplus(-lam_row)
    a = jnp.exp(log_a)
    th = jnp.tanh(log_a)
    b = jnp.sqrt(-2.0 * th / (1.0 - th)) * (i * cx)
    return a, b


def _lru_kernel(x_ref, g_ref, cw_ref, cb_ref, wa_ref, wx_ref, ba_ref, bx_ref, lam_ref,
                y_ref, h_out, cv_out, xbuf, a_buf, b_buf, h_scr, *, nb):
    i = pl.program_id(0)
    tc = x_ref.shape[0]
    rows = tc * nb
    tail = (CONV_W - 1) * nb

    @pl.when(i == 0)
    def _():
        xbuf[0:tail, :] = jnp.zeros((tail, LRU_WIDTH), F32)
        h_scr[...] = jnp.zeros(h_scr.shape, F32)

    x = x_ref[...].reshape(rows, LRU_WIDTH)
    xbuf[tail:tail + rows, :] = x
    conv = cw_ref[3:4, :] * x
    for j in range(CONV_W - 1):
        conv = conv + cw_ref[j:j + 1, :] * xbuf[j * nb:j * nb + rows, :]
    cv_out[...] = xbuf[rows:rows + tail, :].reshape(CONV_W - 1, nb, LRU_WIDTH)
    xbuf[0:tail, :] = x[rows - tail:, :]
    a, b = _lru_gates(conv + cb_ref[...], wa_ref, wx_ref, ba_ref[...], bx_ref[...], lam_ref[...])
    a_buf[...] = a
    b_buf[...] = b

    def body(t, h):
        r = pl.ds(pl.multiple_of(t * nb, nb), nb)
        h = a_buf[r, :] * h + b_buf[r, :]
        b_buf[r, :] = h
        return h

    h = lax.fori_loop(0, tc, body, h_scr[...], unroll=8)
    h_scr[...] = h
    h_out[...] = h
    y = b_buf[...] * jax.nn.gelu(g_ref[...].reshape(rows, LRU_WIDTH))
    y_ref[...] = y.reshape(tc, nb, LRU_WIDTH)


def _lru_prompt(x_t, g_t, conv_w, conv_b, wa, wx, ba, bx, lam, tc):
    t, nb, _ = x_t.shape
    full = lambda a: pl.BlockSpec(a.shape, lambda i: (0,) * a.ndim)
    blk = pl.BlockSpec((tc, nb, LRU_WIDTH), lambda i: (i, 0, 0))
    rows = tc * nb
    return pl.pallas_call(
        functools.partial(_lru_kernel, nb=nb),
        grid=(t // tc,),
        in_specs=[blk, blk] + [full(a) for a in (conv_w, conv_b, wa, wx, ba, bx, lam)],
        out_specs=[blk,
                   pl.BlockSpec((nb, LRU_WIDTH), lambda i: (0, 0)),
                   pl.BlockSpec((CONV_W - 1, nb, LRU_WIDTH), lambda i: (0, 0, 0))],
        out_shape=[jax.ShapeDtypeStruct((t, nb, LRU_WIDTH), F32),
                   jax.ShapeDtypeStruct((nb, LRU_WIDTH), F32),
                   jax.ShapeDtypeStruct((CONV_W - 1, nb, LRU_WIDTH), F32)],
        scratch_shapes=[pltpu.VMEM(((CONV_W - 1) * nb + rows, LRU_WIDTH), F32),
                        pltpu.VMEM((rows, LRU_WIDTH), F32), pltpu.VMEM((rows, LRU_WIDTH), F32),
                        pltpu.VMEM((nb, LRU_WIDTH), F32)],
        compiler_params=_cparams("arbitrary"),
        name="lru",
    )(x_t, g_t, conv_w, conv_b, wa, wx, ba, bx, lam)


def _merge_kernel(x_ref, gt_ref, ya_ref, yb_ref, yc_ref, yd_ref, mg_ref, wb_ref, wo_ref, g_ref, b_ref,
                  o_ref, *, alpha):
    acc = jnp.zeros(x_ref.shape, F32)
    for n, y_ref in enumerate((ya_ref, yb_ref, yc_ref, yd_ref)):
        gate = jax.nn.sigmoid(mg_ref[:, n * D_MODEL:(n + 1) * D_MODEL])
        acc = acc + gate * _bdot(y_ref[...], wb_ref[n])
    out = _bdot(acc, wo_ref[...])
    z = alpha * x_ref[...] + gt_ref[...] * out
    o_ref[...] = _ln(z) * g_ref[...] + b_ref[...]


def _merge(x, mod, ya, yb, yc, yd, mg, w_branch, w_out, ln_g, ln_b, alpha, tm):
    n = x.shape[0]
    l = mod.layer
    row = lambda w: pl.BlockSpec((tm, w), lambda i: (i, 0))
    return pl.pallas_call(
        functools.partial(_merge_kernel, alpha=alpha),
        grid=(n // tm,),
        in_specs=[
            row(D_MODEL), mod.spec(1, 2, tm),
            row(BRANCH_W), row(BRANCH_W), row(BRANCH_W), row(BRANCH_W), row(N_BRANCH * D_MODEL),
            _resident((None, N_BRANCH, BRANCH_W, D_MODEL), lambda i: (l, 0, 0, 0)),
            _resident((None, D_MODEL, D_MODEL), lambda i: (l, 0, 0)),
            pl.BlockSpec((None, None, 1, D_MODEL), lambda i: (l, 1, 0, 0)),
            pl.BlockSpec((None, None, 1, D_MODEL), lambda i: (l, 1, 0, 0)),
        ],
        out_specs=row(D_MODEL),
        out_shape=jax.ShapeDtypeStruct((n, D_MODEL), F32),
        compiler_params=_cparams("parallel"),
        name="merge",
    )(x, mod.arr, ya, yb, yc, yd, mg, w_branch, w_out, ln_g, ln_b)


def _smix1_kernel(ret_ref, gqkv_ref, gab_ref, su_ref, lx_ref, cos_ref, sin_ref,
                  gcv_ref, sre_ref, sim_ref, lru_ref, lcv_ref,
                  gcw_ref, alog_ref, dtb_ref,
                  are_ref, aim_ref, bre_ref, bim_ref, cre_ref, cim_ref, d_ref, gw_ref, gb_ref,
                  lcw_ref, lcb_ref, wa_ref, wx_ref, ba_ref, bx_ref, lam_ref,
                  qkt_ref, gv_ref, gx_ref, yc_ref, lh_ref, gcv_out, sre_out, sim_out, lcv_out):
    ret = ret_ref[...]
    rq = _rotate(ret[:, 0:RET_QK], cos_ref[...], sin_ref[...])
    rk = _rotate(ret[:, RET_QK:2 * RET_QK], cos_ref[...], sin_ref[...]) * (RET_DK ** -0.5)

    x = gqkv_ref[...]
    conv = gcw_ref[3:4, :] * x
    for j in range(CONV_W - 1):
        conv = conv + gcw_ref[j:j + 1, :] * gcv_ref[j]
    gcv_out[0] = gcv_ref[1]
    gcv_out[1] = gcv_ref[2]
    gcv_out[2] = x
    cq = _silu(conv)
    gq = jnp.concatenate([_l2n(cq[:, h * GDN_DK:(h + 1) * GDN_DK]) * (GDN_DK ** -0.5)
                          for h in range(GDN_HEADS)], axis=1)
    gk = jnp.concatenate([_l2n(cq[:, GDN_K + h * GDN_DK:GDN_K + (h + 1) * GDN_DK])
                          for h in range(GDN_HEADS)], axis=1)
    gv_ref[...] = cq[:, 2 * GDN_K:]
    gl = _gate_lanes(gab_ref[...], alog_ref[...], dtb_ref[...])
    gx = _hdot(gl, _expand8())
    gx_ref[...] = jnp.where(_iota(gx.shape, 1) < GDN_HEADS * LANE, jnp.exp(gx), gx)
    qkt_ref[...] = jnp.concatenate([rq, rk, gq, gk], axis=1).T

    u = su_ref[...]
    h_re, h_im = [], []
    for j in range(SSM_BLK):
        sl = slice(j * SSM_BLK_STATE, (j + 1) * SSM_BLK_STATE)
        uj = u[:, j * LANE:(j + 1) * LANE]
        a_re, a_im = are_ref[:, sl], aim_ref[:, sl]
        p_re, p_im = sre_ref[:, sl], sim_ref[:, sl]
        h_re.append(a_re * p_re - a_im * p_im + _bdot(uj, bre_ref[j]))
        h_im.append(a_re * p_im + a_im * p_re + _bdot(uj, bim_ref[j]))
    h_re = jnp.concatenate(h_re, axis=1)
    h_im = jnp.concatenate(h_im, axis=1)
    sre_out[...] = h_re
    sim_out[...] = h_im
    yc_ref[...] = _s5_readout(u, h_re, h_im, cre_ref, cim_ref, d_ref[...], gw_ref, gb_ref[...])

    lx = lx_ref[...]
    conv = lcw_ref[3:4, :] * lx
    for j in range(CONV_W - 1):
        conv = conv + lcw_ref[j:j + 1, :] * lcv_ref[j]
    lcv_out[0] = lcv_ref[1]
    lcv_out[1] = lcv_ref[2]
    lcv_out[2] = lx
    a, b = _lru_gates(conv + lcb_ref[...], wa_ref, wx_ref, ba_ref[...], bx_ref[...], lam_ref[...])
    lh_ref[...] = a * lru_ref[...] + b


def _smix1(pieces, cos_row, sin_row, states, gdn_w, s5_w, lru_w):
    rows = pieces["ret"].shape[0]
    ins = [pieces["ret"], pieces["gqkv"], pieces["gab"], pieces["su"], pieces["lx"], cos_row, sin_row,
           *states, *gdn_w, *s5_w, *lru_w]
    full = lambda a: pl.BlockSpec(a.shape, lambda i: (0,) * a.ndim)
    sds = lambda *s: jax.ShapeDtypeStruct(s, F32)
    out_shape = [sds(QKT_ROWS, rows), sds(rows, GDN_HEADS * GDN_DV), sds(rows, 8 * LANE),
                 sds(rows, SSM_WIDTH), sds(rows, LRU_WIDTH),
                 sds(CONV_W - 1, rows, GDN_QKV), sds(rows, SSM_STATE), sds(rows, SSM_STATE),
                 sds(CONV_W - 1, rows, LRU_WIDTH)]
    return pl.pallas_call(
        _smix1_kernel,
        grid=(1,),
        in_specs=[full(a) for a in ins],
        out_specs=[pl.BlockSpec(s.shape, lambda i, nd=len(s.shape): (0,) * nd) for s in out_shape],
        out_shape=out_shape,
        compiler_params=_cparams("arbitrary"),
        name="sample_mix",
    )(*ins)


def _smix2_kernel(qkt_ref, rv_ref, gv_ref, gx_ref, rg_ref, gz_ref, lh_ref, lg_ref, nw_ref, sret_ref, sgdn_ref,
                  ya_ref, yb_ref, yd_ref, nret_ref, ngdn_ref, o_ret, o_gdn):
    i = pl.program_id(0)
    bt = rv_ref.shape[0]
    rows = qkt_ref.shape[1]
    gammas = _ret_gammas()
    qkt = qkt_ref[...]

    for j in range(bt):
        b = i * bt + j
        onehot = (_iota((rows, LANE), 0) == b).astype(F32)
        cols = _hdot(qkt, onehot)
        r = slice(j, j + 1)
        for h in range(RET_HEADS):
            q_c = cols[h * RET_DK:(h + 1) * RET_DK]
            k_c = cols[RET_QK + h * RET_DK:RET_QK + (h + 1) * RET_DK]
            v_r = rv_ref[r, h * RET_DV:(h + 1) * RET_DV]
            s_new = gammas[h] * sret_ref[j, h] + k_c * v_r
            nret_ref[j, h] = s_new
            o_ret[r, h * RET_DV:(h + 1) * RET_DV] = jnp.sum(q_c * s_new, axis=0, keepdims=True)
        base = 2 * RET_QK
        for h in range(GDN_HEADS):
            q_c = cols[base + h * GDN_DK:base + (h + 1) * GDN_DK]
            k_c = cols[base + GDN_K + h * GDN_DK:base + GDN_K + (h + 1) * GDN_DK]
            v_r = gv_ref[r, h * GDN_DV:(h + 1) * GDN_DV]
            e_g = gx_ref[r, h * LANE:(h + 1) * LANE]
            beta = gx_ref[r, (GDN_HEADS + h) * LANE:(GDN_HEADS + h + 1) * LANE]
            s = sgdn_ref[j, h]
            v_new = beta * (v_r - e_g * jnp.sum(k_c * s, axis=0, keepdims=True))
            s_new = e_g * s + k_c * v_new
            ngdn_ref[j, h] = s_new
            o_gdn[r, h * GDN_DV:(h + 1) * GDN_DV] = jnp.sum(q_c * s_new, axis=0, keepdims=True)
    rg, gz = rg_ref[...], gz_ref[...]
    o_r, o_g = o_ret[...], o_gdn[...]
    ya_ref[...] = jnp.concatenate(
        [_ln(o_r[:, h * RET_DV:(h + 1) * RET_DV]) for h in range(RET_HEADS)], axis=1) * _silu(rg)
    yb_ref[...] = jnp.concatenate(
        [_rms(o_g[:, h * GDN_DV:(h + 1) * GDN_DV]) * nw_ref[...] for h in range(GDN_HEADS)], axis=1) * _silu(gz)
    yd_ref[...] = lh_ref[...] * jax.nn.gelu(lg_ref[...])


def _smix2(qkt, rv, gv, gx, rg, gz, lh, lg, norm_w, s_ret, s_gdn, layer, bt):
    rows = rv.shape[0]
    row = lambda w: pl.BlockSpec((bt, w), lambda i: (i, 0))
    sds = lambda *s: jax.ShapeDtypeStruct(s, F32)
    ret_blk = pl.BlockSpec((None, bt, RET_HEADS, RET_DK, RET_DV), lambda i: (layer, i, 0, 0, 0))
    gdn_blk = pl.BlockSpec((None, bt, GDN_HEADS, GDN_DK, GDN_DV), lambda i: (layer, i, 0, 0, 0))
    return pl.pallas_call(
        _smix2_kernel,
        grid=(rows // bt,),
        in_specs=[pl.BlockSpec(qkt.shape, lambda i: (0, 0)),
                  pl.BlockSpec((bt, RET_V), lambda i: (i, 1)),
                  row(GDN_HEADS * GDN_DV), row(8 * LANE),
                  pl.BlockSpec((bt, RET_V), lambda i: (i, 2)),
                  row(GDN_HEADS * GDN_DV), row(LRU_WIDTH), row(LRU_WIDTH),
                  pl.BlockSpec(norm_w.shape, lambda i: (0, 0)), ret_blk, gdn_blk],
        out_specs=[row(RET_V), row(GDN_HEADS * GDN_DV), row(LRU_WIDTH),
                   pl.BlockSpec((bt, RET_HEADS, RET_DK, RET_DV), lambda i: (i, 0, 0, 0)),
                   pl.BlockSpec((bt, GDN_HEADS, GDN_DK, GDN_DV), lambda i: (i, 0, 0, 0))],
        out_shape=[sds(rows, RET_V), sds(rows, GDN_HEADS * GDN_DV), sds(rows, LRU_WIDTH),
                   sds(rows, RET_HEADS, RET_DK, RET_DV), sds(rows, GDN_HEADS, GDN_DK, GDN_DV)],
        scratch_shapes=[pltpu.VMEM((bt, RET_V), F32), pltpu.VMEM((bt, GDN_HEADS * GDN_DV), F32)],
        compiler_params=_cparams("parallel"),
        name="sample_state",
    )(qkt, rv, gv, gx, rg, gz, lh, lg, norm_w, s_ret, s_gdn)


def _block_diag(w):
    nb, bs, _ = w.shape
    return jnp.einsum("nij,nm->nimj", w, jnp.eye(nb, dtype=w.dtype)).reshape(nb * bs, nb * bs)


def _lane_row(v):
    return jnp.zeros((1, LANE), F32).at[0, :v.shape[0]].set(v)


def kernel(x_prompt, x_sample, c_prompt, c_sample, state_ret, state_gdn, state_gdn_conv, state_ssm_re, state_ssm_im, state_lru, state_lru_conv, w_ada, b_ada, ln_g, ln_b, w_ffn_up, w_ffn_down, w_in, gdn_conv_w, gdn_a_log, gdn_dt_bias, gdn_norm_w, ssm_lam_re, ssm_lam_im, ssm_log_step, ssm_b_re, ssm_b_im, ssm_c_re, ssm_c_im, ssm_d, ssm_glu_w, ssm_glu_b, lru_conv_w, lru_conv_b, lru_wa, lru_ba, lru_wx, lru_bx, lru_lam, w_branch, w_out):
    nb, t, _ = x_prompt.shape
    ns = x_sample.shape[0]
    depth = w_ada.shape[0]
    assert t % RET_CHUNK == 0 and x_sample.shape[1] == 1
    alpha = (2 * depth) ** 0.25
    tm = 256 if t % 256 == 0 else RET_CHUNK
    tc = 64

    wup = w_ffn_up.astype(BF16)
    wdn = w_ffn_down.astype(BF16)
    wbr = w_branch.astype(BF16)
    wout = w_out.astype(BF16)
    in_sizes = (RET_QK, RET_QK, RET_V, RET_V, GDN_QKV, GDN_HEADS, GDN_HEADS, GDN_HEADS * GDN_DV,
                SSM_WIDTH, LRU_WIDTH, LRU_WIDTH, N_BRANCH * D_MODEL)
    offs = np.concatenate([[0], np.cumsum(in_sizes)])
    seg = lambda a, b: w_in[:, :, offs[a]:offs[b]]
    w_main = jnp.concatenate([seg(0, 5), seg(7, 12)], axis=-1).astype(BF16)
    w_small = jnp.pad(seg(5, 7), ((0, 0), (0, 0), (0, LANE - 2 * GDN_HEADS))).astype(BF16)
    ln_g4 = ln_g.reshape(depth, N_SUB, 1, D_MODEL)
    ln_b4 = ln_b.reshape(depth, N_SUB, 1, D_MODEL)

    mod_all = _ada(jnp.concatenate([c_prompt, c_sample], axis=0), w_ada, b_ada)
    mod_p = mod_all[:, :nb].reshape(depth, nb, 3 * N_SUB, 1, D_MODEL).transpose(0, 2, 1, 3, 4)
    mod_s = mod_all[:, nb:]

    cos_s, sin_s = _rope_tables(jnp.full((1,), PAST_LEN))
    xp = x_prompt.reshape(nb * t, D_MODEL)
    xs = x_sample.reshape(ns, D_MODEL)
    new_p, new_s = [], []
    for l in range(depth):
        mp = _Mod(mod_p, l, False, t)
        ms = _Mod(mod_s, l, True)
        alog_row, dtb_row = _lane_row(gdn_a_log[l]), _lane_row(gdn_dt_bias[l])
        norm_w = gdn_norm_w[l].reshape(1, GDN_DV)
        s5p = _s5_params(ssm_lam_re[l], ssm_lam_im[l], ssm_log_step[l],
                         ssm_b_re[l], ssm_b_im[l], ssm_c_re[l], ssm_c_im[l])
        d_row = ssm_d[l].reshape(1, SSM_WIDTH)
        glu_w, glu_b = ssm_glu_w[l].astype(BF16), ssm_glu_b[l].reshape(1, SSM_WIDTH)
        lru_w = (lru_conv_w[l], lru_conv_b[l].reshape(1, LRU_WIDTH),
                 _block_diag(lru_wa[l]).astype(BF16), _block_diag(lru_wx[l]).astype(BF16),
                 lru_ba[l].reshape(1, LRU_WIDTH), lru_bx[l].reshape(1, LRU_WIDTH),
                 lru_lam[l].reshape(1, LRU_WIDTH))

        xp = _ffn(xp, mp, 0, 0, wup, wdn, ln_g4, ln_b4, alpha, tm)
        pc = _inproj(xp, mp, w_main, w_small, tm)
        ya, ret_p = _retention_prompt(pc["ret"], nb, t)
        yb, gdn_p, gcv_p = _gdn_prompt(pc["gqkv"], pc["gab"], pc["gz"], gdn_conv_w[l],
                                       alog_row, dtb_row, norm_w, nb, t)
        to_t = lambda a: a.reshape(nb, t, -1).transpose(1, 0, 2)
        from_t = lambda a: a.transpose(1, 0, 2).reshape(nb * t, -1)
        yc_t, sre_p, sim_p = _s5_prompt(to_t(pc["su"]), s5p, d_row, glu_w, glu_b, tc)
        yd_t, lru_p, lcv_p = _lru_prompt(to_t(pc["lx"]), to_t(pc["lg"]), *lru_w, tc)
        xp = _merge(xp, mp, ya, yb, from_t(yc_t), from_t(yd_t), pc["mg"], wbr, wout, ln_g4, ln_b4, alpha, tm)
        xp = _ffn(xp, mp, 2, 1, wup, wdn, ln_g4, ln_b4, alpha, tm)
        new_p.append((ret_p, gdn_p, gcv_p,
                      sre_p.reshape(nb, SSM_GROUPS, SSM_P), sim_p.reshape(nb, SSM_GROUPS, SSM_P),
                      lru_p, lcv_p.transpose(1, 0, 2)))

        xs = _ffn(xs, ms, 0, 0, wup, wdn, ln_g4, ln_b4, alpha, ns)
        sc = _inproj(xs, ms, w_main, w_small, ns)
        states = (state_gdn_conv[l].transpose(1, 0, 2),
                  state_ssm_re[l].reshape(ns, SSM_STATE), state_ssm_im[l].reshape(ns, SSM_STATE),
                  state_lru[l], state_lru_conv[l].transpose(1, 0, 2))
        (qkt, gv, gx, yc, lh, gcv_s, sre_s, sim_s, lcv_s) = _smix1(
            sc, cos_s, sin_s, states, (gdn_conv_w[l], alog_row, dtb_row),
            (*s5p, d_row, glu_w, glu_b), lru_w)
        ya, yb, yd, ret_s, gdn_s = _smix2(qkt, sc["ret"], gv, gx, sc["ret"], sc["gz"], lh, sc["lg"],
                                          norm_w, state_ret, state_gdn, l, SUBLANE)
        xs = _merge(xs, ms, ya, yb, yc, yd, sc["mg"], wbr, wout, ln_g4, ln_b4, alpha, ns)
        xs = _ffn(xs, ms, 2, 1, wup, wdn, ln_g4, ln_b4, alpha, ns)
        new_s.append((ret_s, gdn_s, gcv_s.transpose(1, 0, 2),
                      sre_s.reshape(ns, SSM_GROUPS, SSM_P), sim_s.reshape(ns, SSM_GROUPS, SSM_P),
                      lh, lcv_s.transpose(1, 0, 2)))

    ret_p, gdn_p, gcv_p, sre_p, sim_p, lru_p, lcv_p = [jnp.stack(z) for z in zip(*new_p)]
    ret_s, gdn_s, gcv_s, sre_s, sim_s, lru_s, lcv_s = [jnp.stack(z) for z in zip(*new_s)]
    return (xp.reshape(nb, t, D_MODEL), xs.reshape(ns, 1, D_MODEL),
            ret_p, ret_s, gdn_p, gdn_s, gcv_p, gcv_s, sre_p, sre_s, sim_p, sim_s,
            lru_p, lru_s, lcv_p, lcv_s)
```

```python
import functools
import math

import numpy as np
import jax
import jax.numpy as jnp
from jax import lax
from jax.experimental import pallas as pl
from jax.experimental.pallas import tpu as pltpu

F32 = jnp.float32
BF16 = jnp.bfloat16
HI = lax.Precision.HIGHEST

D_MODEL = 1024
RET_HEADS, RET_DK, RET_DV, RET_CHUNK = 4, 64, 128, 128
ROPE_BASE = 10000.0
GDN_HEADS, GDN_DK, GDN_DV, GDN_CHUNK = 4, 128, 128, 64
GDN_STEP_CHUNKS = 4
GDN_QKV = 2 * GDN_HEADS * GDN_DK + GDN_HEADS * GDN_DV
CONV_W = 4
SSM_GROUP, SSM_GROUPS, SSM_P = 16, 32, 64
SSM_WIDTH = SSM_GROUP * SSM_GROUPS
SSM_STATE = SSM_GROUPS * SSM_P
LRU_WIDTH, LRU_BLOCKS = 512, 8
LRU_C = 8.0
N_BRANCH, BRANCH_W = 4, 512
D_FF = 2816
N_SUB = 3
LN_EPS = 1e-5
PAST_LEN = 16384

RET_QK = RET_HEADS * RET_DK
RET_V = RET_HEADS * RET_DV
RET_W = 2 * RET_QK + 2 * RET_V
GDN_K = GDN_HEADS * GDN_DK
QKT_ROWS = 2 * RET_QK + 2 * GDN_K
LANE = 128
SUBLANE = 8
FF_CHUNK = D_FF // 2
VMEM_LIMIT = 56 * 1024 * 1024


def _cparams(*sem):
    return pltpu.CompilerParams(dimension_semantics=sem, vmem_limit_bytes=VMEM_LIMIT)


def _resident(shape, index_map):
    return pl.BlockSpec(shape, index_map, pipeline_mode=pl.Buffered(1))


def _ln(x):
    mu = jnp.mean(x, axis=-1, keepdims=True)
    xc = x - mu
    return xc * lax.rsqrt(jnp.mean(xc * xc, axis=-1, keepdims=True) + LN_EPS)


def _silu(x):
    return x * jax.nn.sigmoid(x)


def _softplus(x):
    return jnp.maximum(x, 0.0) + jnp.log1p(jnp.exp(-jnp.abs(x)))


def _bdot(a, w):
    return jnp.dot(a.astype(BF16), w.astype(BF16), preferred_element_type=F32)


def _split(x):
    hi = x.astype(BF16)
    return hi, (x - hi.astype(F32)).astype(BF16)


def _hdot(a, b):
    return jnp.dot(a, b, preferred_element_type=F32, precision=HI)


def _dot_nt(a, b, precision=None):
    return lax.dot_general(a, b, (((1,), (1,)), ((), ())),
                           preferred_element_type=F32, precision=precision)


def _iota(shape, dim):
    return lax.broadcasted_iota(jnp.int32, shape, dim)


def _ada_kernel(c_ref, w_ref, b_ref, o_ref):
    o_ref[...] = _bdot(_silu(c_ref[...]), w_ref[...]) + b_ref[...]


def _ada(c_all, w_ada, b_ada):
    depth, _, n_out = w_ada.shape
    rows = c_all.shape[0]
    tn = 1152
    return pl.pallas_call(
        _ada_kernel,
        grid=(depth, n_out // tn),
        in_specs=[
            pl.BlockSpec((rows, D_MODEL), lambda l, j: (0, 0)),
            pl.BlockSpec((None, D_MODEL, tn), lambda l, j: (l, 0, j)),
            pl.BlockSpec((None, 1, tn), lambda l, j: (l, 0, j)),
        ],
        out_specs=pl.BlockSpec((None, rows, tn), lambda l, j: (l, 0, j)),
        out_shape=jax.ShapeDtypeStruct((depth, rows, n_out), F32),
        compiler_params=_cparams("parallel", "parallel"),
        name="ada",
    )(c_all, w_ada, b_ada.reshape(depth, 1, n_out))


class _Mod:
    def __init__(self, arr, layer, per_row, rows_per_batch=None):
        self.arr, self.layer, self.per_row, self.rpb = arr, layer, per_row, rows_per_batch

    def spec(self, sub, which, tm):
        k, l = 3 * sub + which, self.layer
        if self.per_row:
            return pl.BlockSpec((None, tm, D_MODEL), lambda i: (l, i, k))
        tiles = self.rpb // tm
        return pl.BlockSpec((None, None, None, 1, D_MODEL), lambda i: (l, k, i // tiles, 0, 0))


def _ffn_kernel(x_ref, sh_ref, sc_ref, gt_ref, wup_ref, wdn_ref, g_ref, b_ref, o_ref, *, alpha):
    x = x_ref[...]
    h = (_ln(x) * (1.0 + sc_ref[...]) + sh_ref[...]).astype(BF16)
    acc = jnp.zeros(x.shape, F32)
    for j in range(D_FF // FF_CHUNK):
        lo = j * FF_CHUNK
        a = jnp.dot(h, wup_ref[:, lo:lo + FF_CHUNK], preferred_element_type=F32)
        b = jnp.dot(h, wup_ref[:, D_FF + lo:D_FF + lo + FF_CHUNK], preferred_element_type=F32)
        act = (_silu(a) * b).astype(BF16)
        acc = acc + jnp.dot(act, wdn_ref[lo:lo + FF_CHUNK, :], preferred_element_type=F32)
    z = alpha * x + 0.5 * gt_ref[...] * acc
    o_ref[...] = _ln(z) * g_ref[...] + b_ref[...]


def _ffn(x, mod, sub, which, wup, wdn, ln_g, ln_b, alpha, tm):
    n = x.shape[0]
    l = mod.layer
    row = pl.BlockSpec((tm, D_MODEL), lambda i: (i, 0))
    return pl.pallas_call(
        functools.partial(_ffn_kernel, alpha=alpha),
        grid=(n // tm,),
        in_specs=[
            row, mod.spec(sub, 0, tm), mod.spec(sub, 1, tm), mod.spec(sub, 2, tm),
            _resident((None, None, D_MODEL, 2 * D_FF), lambda i: (l, which, 0, 0)),
            _resident((None, None, D_FF, D_MODEL), lambda i: (l, which, 0, 0)),
            pl.BlockSpec((None, None, 1, D_MODEL), lambda i: (l, sub, 0, 0)),
            pl.BlockSpec((None, None, 1, D_MODEL), lambda i: (l, sub, 0, 0)),
        ],
        out_specs=row,
        out_shape=jax.ShapeDtypeStruct((n, D_MODEL), F32),
        compiler_params=_cparams("parallel"),
        name="ffn",
    )(x, mod.arr, mod.arr, mod.arr, wup, wdn, ln_g, ln_b)


_IN_PIECES = (("ret", RET_W), ("gqkv", GDN_QKV), ("gz", 512), ("su", SSM_WIDTH),
              ("lx", LRU_WIDTH), ("lg", LRU_WIDTH), ("mg", N_BRANCH * D_MODEL))
IN_MAIN = sum(w for _, w in _IN_PIECES)


def _inproj_kernel(x_ref, sh_ref, sc_ref, w_ref, ws_ref, *o_refs):
    h = (_ln(x_ref[...]) * (1.0 + sc_ref[...]) + sh_ref[...]).astype(BF16)
    lo = 0
    for (_, width), o_ref in zip(_IN_PIECES, o_refs[:-1]):
        o_ref[...] = jnp.dot(h, w_ref[:, lo:lo + width], preferred_element_type=F32)
        lo += width
    o_refs[-1][...] = jnp.dot(h, ws_ref[...], preferred_element_type=F32)


def _inproj(x, mod, w_main, w_small, tm):
    n = x.shape[0]
    l = mod.layer
    widths = [w for _, w in _IN_PIECES] + [LANE]
    outs = pl.pallas_call(
        _inproj_kernel,
        grid=(n // tm,),
        in_specs=[
            pl.BlockSpec((tm, D_MODEL), lambda i: (i, 0)),
            mod.spec(1, 0, tm), mod.spec(1, 1, tm),
            _resident((None, D_MODEL, IN_MAIN), lambda i: (l, 0, 0)),
            _resident((None, D_MODEL, LANE), lambda i: (l, 0, 0)),
        ],
        out_specs=[pl.BlockSpec((tm, w), lambda i: (i, 0)) for w in widths],
        out_shape=[jax.ShapeDtypeStruct((n, w), F32) for w in widths],
        compiler_params=_cparams("parallel"),
        name="inproj",
    )(x, mod.arr, mod.arr, w_main, w_small)
    names = [nm for nm, _ in _IN_PIECES] + ["gab"]
    return dict(zip(names, outs))


def _rope_tables(pos):
    half = RET_DK // 2
    inv = ROPE_BASE ** (-jnp.arange(half, dtype=F32) / half)
    ang = pos.astype(F32)[:, None] * inv[None, :]
    cos, sin = jnp.cos(ang), jnp.sin(ang)
    cos_t = jnp.tile(jnp.concatenate([cos, cos], axis=-1), (1, RET_HEADS))
    sin_t = jnp.tile(jnp.concatenate([-sin, sin], axis=-1), (1, RET_HEADS))
    return cos_t, sin_t


def _rotate(z, cos, sin):
    half = RET_DK // 2
    first = (_iota(z.shape, 1) % RET_DK) < half
    swapped = jnp.where(first, pltpu.roll(z, RET_QK - half, 1), pltpu.roll(z, half, 1))
    return z * cos + swapped * sin


def _ret_gammas():
    return [1.0 - 2.0 ** (-5.0 - h) for h in range(RET_HEADS)]


def _ret_consts(chunk):
    log_g = np.log(np.array(_ret_gammas(), np.float64))
    idx = np.arange(chunk, dtype=np.float64)
    rel = idx[:, None] - idx[None, :]
    dmask = np.where(rel[None] >= 0, np.exp(log_g[:, None, None] * np.maximum(rel, 0.0)[None]), 0.0)
    qdec = np.repeat(np.exp(log_g[None, :] * (idx[:, None] + 1.0)), RET_DK, axis=1)
    kdec = np.repeat(np.exp(log_g[None, :] * (chunk - 1.0 - idx[:, None])), RET_DK, axis=1)
    head_r = np.arange(RET_QK) // RET_DK
    head_c = np.arange(RET_V) // RET_DV
    bd = (head_r[:, None] == head_c[None, :]).astype(np.float64)
    cd = bd * np.exp(log_g * chunk)[head_r][:, None]
    f = lambda a: jnp.asarray(a, F32)
    return f(dmask), f(qdec), f(kdec), f(cd), f(bd)


def _ret_kernel(x_ref, cos_ref, sin_ref, dmask_ref, qdec_ref, kdec_ref, cd_ref, bd_ref,
                y_ref, st_ref, s_scr):
    c = pl.program_id(1)

    @pl.when(c == 0)
    def _():
        s_scr[...] = jnp.zeros(s_scr.shape, F32)

    x = x_ref[...]
    q = _rotate(x[:, 0:RET_QK], cos_ref[...], sin_ref[...])
    k = _rotate(x[:, RET_QK:2 * RET_QK], cos_ref[...], sin_ref[...]) * (RET_DK ** -0.5)
    v = x[:, 2 * RET_QK:2 * RET_QK + RET_V].astype(BF16)
    gate = x[:, 2 * RET_QK + RET_V:]
    s = s_scr[...]
    cross = _bdot(q * qdec_ref[...], s)
    kb = k.astype(BF16)
    head = _iota(q.shape, 1) // RET_DK
    outs = []
    for h in range(RET_HEADS):
        qh = jnp.where(head == h, q, 0.0).astype(BF16)
        inner = _dot_nt(qh, kb) * dmask_ref[h]
        oh = _bdot(inner, v[:, h * RET_DV:(h + 1) * RET_DV]) + cross[:, h * RET_DV:(h + 1) * RET_DV]
        outs.append(_ln(oh))
    y_ref[...] = jnp.concatenate(outs, axis=1) * _silu(gate)
    kd_t = (k * kdec_ref[...]).T.astype(BF16)
    upd = jnp.dot(kd_t, v, preferred_element_type=F32)
    s_new = s * cd_ref[...] + upd * bd_ref[...]
    s_scr[...] = s_new

    @pl.when(c == pl.num_programs(1) - 1)
    def _():
        for h in range(RET_HEADS):
            st_ref[0, h] = s_new[h * RET_DK:(h + 1) * RET_DK, h * RET_DV:(h + 1) * RET_DV]


def _retention_prompt(ret, nb, t):
    chunk = RET_CHUNK
    nc = t // chunk
    cos_t, sin_t = _rope_tables(jnp.arange(t))
    consts = _ret_consts(chunk)
    full = lambda a: pl.BlockSpec(a.shape, lambda b, c: (0,) * a.ndim)
    return pl.pallas_call(
        _ret_kernel,
        grid=(nb, nc),
        in_specs=[
            pl.BlockSpec((chunk, RET_W), lambda b, c: (b * nc + c, 0)),
            pl.BlockSpec((chunk, RET_QK), lambda b, c: (c, 0)),
            pl.BlockSpec((chunk, RET_QK), lambda b, c: (c, 0)),
        ] + [full(a) for a in consts],
        out_specs=[
            pl.BlockSpec((chunk, RET_V), lambda b, c: (b * nc + c, 0)),
            pl.BlockSpec((1, RET_HEADS, RET_DK, RET_DV), lambda b, c: (b, 0, 0, 0)),
        ],
        out_shape=[
            jax.ShapeDtypeStruct((nb * t, RET_V), F32),
            jax.ShapeDtypeStruct((nb, RET_HEADS, RET_DK, RET_DV), F32),
        ],
        scratch_shapes=[pltpu.VMEM((RET_QK, RET_V), F32)],
        compiler_params=_cparams("parallel", "arbitrary"),
        name="retention",
    )(ret, cos_t, sin_t, *consts)


def _l2n(x):
    return x * lax.rsqrt(jnp.sum(x * x, axis=-1, keepdims=True) + 1e-6)


def _rms(x):
    return x * lax.rsqrt(jnp.mean(x * x, axis=-1, keepdims=True) + 1e-6)


def _gate_lanes(gab, alog_row, dtb_row):
    g = -jnp.exp(alog_row) * _softplus(gab + dtb_row)
    return jnp.where(_iota(gab.shape, 1) < GDN_HEADS, g, jax.nn.sigmoid(gab))


def _expand8():
    return (_iota((LANE, 8 * LANE), 1) // LANE == _iota((LANE, 8 * LANE), 0)).astype(F32)


def _gdn_kernel(x_ref, gab_ref, gz_ref, cw_ref, alog_ref, dtb_ref, nw_ref,
                y_ref, st_ref, cv_ref, xbuf, s_scr):
    c = pl.program_id(1)
    ch = GDN_CHUNK

    @pl.when(c == 0)
    def _():
        xbuf[0:SUBLANE, :] = jnp.zeros((SUBLANE, GDN_QKV), F32)
        s_scr[...] = jnp.zeros(s_scr.shape, F32)

    rows = x_ref.shape[0]
    x = x_ref[...]
    xbuf[SUBLANE:SUBLANE + rows, :] = x
    conv = cw_ref[3:4, :] * x
    for j in range(CONV_W - 1):
        lo = SUBLANE - (CONV_W - 1) + j
        conv = conv + cw_ref[j:j + 1, :] * xbuf[lo:lo + rows, :]
    cv_ref[0] = xbuf[SUBLANE + rows - (CONV_W - 1):SUBLANE + rows, :]
    xbuf[0:SUBLANE, :] = x[rows - SUBLANE:, :]
    cq = _silu(conv)

    gl = _gate_lanes(gab_ref[...], alog_ref[...], dtb_ref[...])
    col = _iota((ch, 2 * ch), 1) % ch
    tril = _iota((ch, 2 * ch), 0) >= col
    strict = _iota((ch, 2 * ch), 0) > col
    is_g = _iota(gl.shape, 1) < GDN_HEADS
    r_i, c_i = _iota((rows, rows), 0), _iota((rows, rows), 1)
    chunk_tril = ((r_i >= c_i) & (r_i // ch == c_i // ch)).astype(F32)
    csum = _hdot(chunk_tril, jnp.where(is_g, gl, 0.0))
    gsel = jnp.where(is_g, csum, gl)
    n_sub = rows // ch
    heads = range(GDN_HEADS)
    probs = [(h, n) for h in heads for n in range(n_sub)]

    q_all, k_all, g_all, qe_all, rhs_all, kbeta_all = {}, {}, {}, {}, {}, {}
    for h in heads:
        q_all[h] = _l2n(cq[:, h * GDN_DK:(h + 1) * GDN_DK]) * (GDN_DK ** -0.5)
        k_all[h] = _l2n(cq[:, GDN_K + h * GDN_DK:GDN_K + (h + 1) * GDN_DK])
        v_h = cq[:, 2 * GDN_K + h * GDN_DV:2 * GDN_K + (h + 1) * GDN_DV]
        g_all[h] = jnp.broadcast_to(gsel[:, h:h + 1], (rows, LANE))
        b_h = jnp.broadcast_to(gsel[:, GDN_HEADS + h:GDN_HEADS + h + 1], (rows, LANE))
        e_h = jnp.exp(g_all[h])
        kbeta_all[h] = k_all[h] * b_h
        rhs_all[h] = jnp.concatenate([v_h * b_h, kbeta_all[h] * e_h], axis=1)
        qe_all[h] = q_all[h] * e_h

    decay, pp, uw, attn, kd_t, e_last = {}, {}, {}, {}, {}, {}
    for h, n in probs:
        sl = slice(n * ch, (n + 1) * ch)
        g_b = g_all[h][sl]
        g_cols = jnp.concatenate([g_b, g_b], axis=0).T[:ch, :]
        diff = g_b - g_cols
        decay[h, n] = jnp.where(tril, jnp.exp(jnp.where(tril, diff, 0.0)), 0.0)
        g_last = g_b[ch - 1:ch, :]
        e_last[h, n] = jnp.exp(g_last)
        kd_t[h, n] = (k_all[h][sl] * jnp.exp(g_last - g_b)).T.astype(BF16)
    for h, n in probs:
        sl = slice(n * ch, (n + 1) * ch)
        kb_hi, kb_lo = _split(kbeta_all[h][sl])
        k_hi, k_lo = _split(jnp.concatenate([k_all[h][sl], k_all[h][sl]], axis=0))
        kk = _dot_nt(jnp.concatenate([kb_hi, kb_hi, kb_lo, kb_lo], axis=1),
                     jnp.concatenate([k_hi, k_lo, k_hi, k_lo], axis=1))
        pp[h, n] = -jnp.where(strict, kk * decay[h, n], 0.0)
        attn[h, n] = _dot_nt(q_all[h][sl].astype(BF16), k_all[h][sl].astype(BF16)) * decay[h, n][:, :ch]
    for stage in range(int(math.log2(ch))):
        for h, n in probs:
            p_hi, p_lo = _split(pp[h, n])
            lhs = jnp.concatenate([p_hi, p_lo], axis=1)
            y = rhs_all[h][n * ch:(n + 1) * ch] if stage == 0 else uw[h, n]
            y_hi, y_lo = _split(y)
            uw[h, n] = y + jnp.dot(lhs, jnp.concatenate([y_hi, y_lo, y_hi, y_lo], axis=0),
                                   preferred_element_type=F32)
            if stage + 1 < int(math.log2(ch)):
                pp[h, n] = jnp.dot(lhs, jnp.concatenate([p_hi, p_lo, p_hi, p_lo], axis=0),
                                   preferred_element_type=F32)
    s = {h: s_scr[h] for h in heads}
    o_parts = {h: [] for h in heads}
    for n in range(n_sub):
        for h in heads:
            sl = slice(n * ch, (n + 1) * ch)
            u, w = uw[h, n][:, :GDN_DV], uw[h, n][:, GDN_DV:]
            ws_qs = _bdot(jnp.concatenate([w, qe_all[h][sl]], axis=0), s[h])
            v_new = u - ws_qs[:ch]
            o_parts[h].append(ws_qs[ch:] + _bdot(attn[h, n], v_new))
            s[h] = s[h] * e_last[h, n] + jnp.dot(kd_t[h, n], v_new.astype(BF16), preferred_element_type=F32)
    norm_w = nw_ref[...]
    gz = gz_ref[...]
    outs = []
    for h in heads:
        s_scr[h] = s[h]
        o = jnp.concatenate(o_parts[h], axis=0)
        outs.append(_rms(o) * norm_w * _silu(gz[:, h * GDN_DV:(h + 1) * GDN_DV]))
    y_ref[...] = jnp.concatenate(outs, axis=1)

    @pl.when(c == pl.num_programs(1) - 1)
    def _():
        st_ref[0] = s_scr[...]


def _gdn_prompt(gqkv, gab, gz, conv_w, alog_row, dtb_row, norm_w, nb, t):
    ch = GDN_STEP_CHUNKS * GDN_CHUNK if t % (GDN_STEP_CHUNKS * GDN_CHUNK) == 0 else GDN_CHUNK
    nc = t // ch
    row = lambda w: pl.BlockSpec((ch, w), lambda b, c: (b * nc + c, 0))
    full = lambda a: pl.BlockSpec(a.shape, lambda b, c: (0,) * a.ndim)
    return pl.pallas_call(
        _gdn_kernel,
        grid=(nb, nc),
        in_specs=[row(GDN_QKV), row(LANE), row(GDN_HEADS * GDN_DV),
                  full(conv_w), full(alog_row), full(dtb_row), full(norm_w)],
        out_specs=[
            row(GDN_HEADS * GDN_DV),
            pl.BlockSpec((1, GDN_HEADS, GDN_DK, GDN_DV), lambda b, c: (b, 0, 0, 0)),
            pl.BlockSpec((1, CONV_W - 1, GDN_QKV), lambda b, c: (b, 0, 0)),
        ],
        out_shape=[
            jax.ShapeDtypeStruct((nb * t, GDN_HEADS * GDN_DV), F32),
            jax.ShapeDtypeStruct((nb, GDN_HEADS, GDN_DK, GDN_DV), F32),
            jax.ShapeDtypeStruct((nb, CONV_W - 1, GDN_QKV), F32),
        ],
        scratch_shapes=[pltpu.VMEM((SUBLANE + ch, GDN_QKV), F32),
                        pltpu.VMEM((GDN_HEADS, GDN_DK, GDN_DV), F32)],
        compiler_params=_cparams("parallel", "arbitrary"),
        name="gdn",
    )(gqkv, gab, gz, conv_w, alog_row, dtb_row, norm_w)


SSM_BLK = 4
SSM_BLK_STATE = SSM_STATE // SSM_BLK


def _s5_params(lam_re, lam_im, log_step, b_re, b_im, c_re, c_im):
    step = jnp.exp(log_step.astype(F32))[:, None]
    mag = jnp.exp(lam_re * step)
    ab_re = mag * jnp.cos(lam_im * step)
    ab_im = mag * jnp.sin(lam_im * step)
    den = lam_re * lam_re + lam_im * lam_im
    nr = ab_re - 1.0
    f_re = (nr * lam_re + ab_im * lam_im) / den
    f_im = (ab_im * lam_re - nr * lam_im) / den
    bb_re = f_re[..., None] * b_re - f_im[..., None] * b_im
    bb_im = f_re[..., None] * b_im + f_im[..., None] * b_re
    gpb = SSM_GROUPS // SSM_BLK
    eye = jnp.eye(gpb, dtype=F32)

    def in_mat(bb):
        bb = bb.reshape(SSM_BLK, gpb, SSM_P, SSM_GROUP)
        m = jnp.einsum("jgpc,gk->jgckp", bb, eye)
        return m.reshape(SSM_BLK, gpb * SSM_GROUP, gpb * SSM_P).astype(BF16)

    def out_mat(cc):
        cc = cc.reshape(SSM_BLK, gpb, SSM_GROUP, SSM_P)
        m = jnp.einsum("jgcp,gk->jgpkc", cc, eye)
        return m.reshape(SSM_BLK, gpb * SSM_P, gpb * SSM_GROUP).astype(BF16)

    return (ab_re.reshape(1, SSM_STATE), ab_im.reshape(1, SSM_STATE),
            in_mat(bb_re), in_mat(bb_im), out_mat(c_re), out_mat(c_im))


def _s5_readout(u, h_re, h_im, cre_ref, cim_ref, d_row, gw_ref, gb_row):
    ys = []
    for j in range(SSM_BLK):
        sl = slice(j * SSM_BLK_STATE, (j + 1) * SSM_BLK_STATE)
        ys.append(_bdot(h_re[:, sl], cre_ref[j]) - _bdot(h_im[:, sl], cim_ref[j]))
    y = jax.nn.gelu(jnp.concatenate(ys, axis=1) + d_row * u)
    return y * jax.nn.sigmoid(_bdot(y, gw_ref[...]) + gb_row)


def _s5_kernel(u_ref, are_ref, aim_ref, bre_ref, bim_ref, cre_ref, cim_ref, d_ref, gw_ref, gb_ref,
               y_ref, hre_ref, him_ref, bu_re, bu_im, h_re, h_im, *, nb):
    i = pl.program_id(0)
    tc = u_ref.shape[0]
    rows = tc * nb

    @pl.when(i == 0)
    def _():
        h_re[...] = jnp.zeros(h_re.shape, F32)
        h_im[...] = jnp.zeros(h_im.shape, F32)

    u = u_ref[...].reshape(rows, SSM_WIDTH)
    for j in range(SSM_BLK):
        uj = u[:, j * LANE:(j + 1) * LANE]
        sl = slice(j * SSM_BLK_STATE, (j + 1) * SSM_BLK_STATE)
        bu_re[:, sl] = _bdot(uj, bre_ref[j])
        bu_im[:, sl] = _bdot(uj, bim_ref[j])
    for j in range(SSM_BLK):
        sl = slice(j * SSM_BLK_STATE, (j + 1) * SSM_BLK_STATE)
        a_re = jnp.broadcast_to(are_ref[:, sl], (nb, SSM_BLK_STATE))
        a_im = jnp.broadcast_to(aim_ref[:, sl], (nb, SSM_BLK_STATE))

        def body(t, carry):
            hr, hi = carry
            r = pl.ds(pl.multiple_of(t * nb, nb), nb)
            nr = a_re * hr - a_im * hi + bu_re[r, sl]
            ni = a_re * hi + a_im * hr + bu_im[r, sl]
            bu_re[r, sl] = nr
            bu_im[r, sl] = ni
            return nr, ni

        hr, hi = lax.fori_loop(0, tc, body, (h_re[:, sl], h_im[:, sl]), unroll=4)
        h_re[:, sl] = hr
        h_im[:, sl] = hi
    y = _s5_readout(u, bu_re[...], bu_im[...], cre_ref, cim_ref, d_ref[...], gw_ref, gb_ref[...])
    y_ref[...] = y.reshape(tc, nb, SSM_WIDTH)
    hre_ref[...] = h_re[...]
    him_ref[...] = h_im[...]


def _s5_prompt(u_t, params, d_row, glu_w, glu_b, tc):
    t, nb, _ = u_t.shape
    a_re, a_im, bre, bim, cre, cim = params
    full = lambda a: pl.BlockSpec(a.shape, lambda i: (0,) * a.ndim)
    blk = pl.BlockSpec((tc, nb, SSM_WIDTH), lambda i: (i, 0, 0))
    st = pl.BlockSpec((nb, SSM_STATE), lambda i: (0, 0))
    return pl.pallas_call(
        functools.partial(_s5_kernel, nb=nb),
        grid=(t // tc,),
        in_specs=[blk] + [full(a) for a in (a_re, a_im, bre, bim, cre, cim, d_row, glu_w, glu_b)],
        out_specs=[blk, st, st],
        out_shape=[jax.ShapeDtypeStruct((t, nb, SSM_WIDTH), F32),
                   jax.ShapeDtypeStruct((nb, SSM_STATE), F32),
                   jax.ShapeDtypeStruct((nb, SSM_STATE), F32)],
        scratch_shapes=[pltpu.VMEM((tc * nb, SSM_STATE), F32), pltpu.VMEM((tc * nb, SSM_STATE), F32),
                        pltpu.VMEM((nb, SSM_STATE), F32), pltpu.VMEM((nb, SSM_STATE), F32)],
        compiler_params=_cparams("arbitrary"),
        name="s5",
    )(u_t, a_re, a_im, bre, bim, cre, cim, d_row, glu_w, glu_b)


def _lru_gates(cx, wa_ref, wx_ref, ba_row, bx_row, lam_row):
    r = jax.nn.sigmoid(_bdot(cx, wa_ref[...]) + ba_row)
    i = jax.nn.sigmoid(_bdot(cx, wx_ref[...]) + bx_row)
    log_a = -LRU_C * r * _softplus(-lam_row)
    a = jnp.exp(log_a)
    th = jnp.tanh(log_a)
    b = jnp.sqrt(-2.0 * th / (1.0 - th)) * (i * cx)
    return a, b


def _lru_kernel(x_ref, g_ref, cw_ref, cb_ref, wa_ref, wx_ref, ba_ref, bx_ref, lam_ref,
                y_ref, h_out, cv_out, xbuf, a_buf, b_buf, h_scr, *, nb):
    i = pl.program_id(0)
    tc = x_ref.shape[0]
    rows = tc * nb
    tail = (CONV_W - 1) * nb

    @pl.when(i == 0)
    def _():
        xbuf[0:tail, :] = jnp.zeros((tail, LRU_WIDTH), F32)
        h_scr[...] = jnp.zeros(h_scr.shape, F32)

    x = x_ref[...].reshape(rows, LRU_WIDTH)
    xbuf[tail:tail + rows, :] = x
    conv = cw_ref[3:4, :] * x
    for j in range(CONV_W - 1):
        conv = conv + cw_ref[j:j + 1, :] * xbuf[j * nb:j * nb + rows, :]
    cv_out[...] = xbuf[rows:rows + tail, :].reshape(CONV_W - 1, nb, LRU_WIDTH)
    xbuf[0:tail, :] = x[rows - tail:, :]
    a, b = _lru_gates(conv + cb_ref[...], wa_ref, wx_ref, ba_ref[...], bx_ref[...], lam_ref[...])
    a_buf[...] = a
    b_buf[...] = b

    def body(t, h):
        r = pl.ds(pl.multiple_of(t * nb, nb), nb)
        h = a_buf[r, :] * h + b_buf[r, :]
        b_buf[r, :] = h
        return h

    h = lax.fori_loop(0, tc, body, h_scr[...], unroll=8)
    h_scr[...] = h
    h_out[...] = h
    y = b_buf[...] * jax.nn.gelu(g_ref[...].reshape(rows, LRU_WIDTH))
    y_ref[...] = y.reshape(tc, nb, LRU_WIDTH)


def _lru_prompt(x_t, g_t, conv_w, conv_b, wa, wx, ba, bx, lam, tc):
    t, nb, _ = x_t.shape
    full = lambda a: pl.BlockSpec(a.shape, lambda i: (0,) * a.ndim)
    blk = pl.BlockSpec((tc, nb, LRU_WIDTH), lambda i: (i, 0, 0))
    rows = tc * nb
    return pl.pallas_call(
        functools.partial(_lru_kernel, nb=nb),
        grid=(t // tc,),
        in_specs=[blk, blk] + [full(a) for a in (conv_w, conv_b, wa, wx, ba, bx, lam)],
        out_specs=[blk,
                   pl.BlockSpec((nb, LRU_WIDTH), lambda i: (0, 0)),
                   pl.BlockSpec((CONV_W - 1, nb, LRU_WIDTH), lambda i: (0, 0, 0))],
        out_shape=[jax.ShapeDtypeStruct((t, nb, LRU_WIDTH), F32),
                   jax.ShapeDtypeStruct((nb, LRU_WIDTH), F32),
                   jax.ShapeDtypeStruct((CONV_W - 1, nb, LRU_WIDTH), F32)],
        scratch_shapes=[pltpu.VMEM(((CONV_W - 1) * nb + rows, LRU_WIDTH), F32),
                        pltpu.VMEM((rows, LRU_WIDTH), F32), pltpu.VMEM((rows, LRU_WIDTH), F32),
                        pltpu.VMEM((nb, LRU_WIDTH), F32)],
        compiler_params=_cparams("arbitrary"),
        name="lru",
    )(x_t, g_t, conv_w, conv_b, wa, wx, ba, bx, lam)


def _merge_kernel(x_ref, gt_ref, ya_ref, yb_ref, yc_ref, yd_ref, mg_ref, wb_ref, wo_ref, g_ref, b_ref,
                  o_ref, *, alpha):
    acc = jnp.zeros(x_ref.shape, F32)
    for n, y_ref in enumerate((ya_ref, yb_ref, yc_ref, yd_ref)):
        gate = jax.nn.sigmoid(mg_ref[:, n * D_MODEL:(n + 1) * D_MODEL])
        acc = acc + gate * _bdot(y_ref[...], wb_ref[n])
    out = _bdot(acc, wo_ref[...])
    z = alpha * x_ref[...] + gt_ref[...] * out
    o_ref[...] = _ln(z) * g_ref[...] + b_ref[...]


def _merge(x, mod, ya, yb, yc, yd, mg, w_branch, w_out, ln_g, ln_b, alpha, tm):
    n = x.shape[0]
    l = mod.layer
    row = lambda w: pl.BlockSpec((tm, w), lambda i: (i, 0))
    return pl.pallas_call(
        functools.partial(_merge_kernel, alpha=alpha),
        grid=(n // tm,),
        in_specs=[
            row(D_MODEL), mod.spec(1, 2, tm),
            row(BRANCH_W), row(BRANCH_W), row(BRANCH_W), row(BRANCH_W), row(N_BRANCH * D_MODEL),
            _resident((None, N_BRANCH, BRANCH_W, D_MODEL), lambda i: (l, 0, 0, 0)),
            _resident((None, D_MODEL, D_MODEL), lambda i: (l, 0, 0)),
            pl.BlockSpec((None, None, 1, D_MODEL), lambda i: (l, 1, 0, 0)),
            pl.BlockSpec((None, None, 1, D_MODEL), lambda i: (l, 1, 0, 0)),
        ],
        out_specs=row(D_MODEL),
        out_shape=jax.ShapeDtypeStruct((n, D_MODEL), F32),
        compiler_params=_cparams("parallel"),
        name="merge",
    )(x, mod.arr, ya, yb, yc, yd, mg, w_branch, w_out, ln_g, ln_b)


def _smix1_kernel(ret_ref, gqkv_ref, gab_ref, su_ref, lx_ref, cos_ref, sin_ref,
                  gcv_ref, sre_ref, sim_ref, lru_ref, lcv_ref,
                  gcw_ref, alog_ref, dtb_ref,
                  are_ref, aim_ref, bre_ref, bim_ref, cre_ref, cim_ref, d_ref, gw_ref, gb_ref,
                  lcw_ref, lcb_ref, wa_ref, wx_ref, ba_ref, bx_ref, lam_ref,
                  qkt_ref, gv_ref, gx_ref, yc_ref, lh_ref, gcv_out, sre_out, sim_out, lcv_out):
    ret = ret_ref[...]
    rq = _rotate(ret[:, 0:RET_QK], cos_ref[...], sin_ref[...])
    rk = _rotate(ret[:, RET_QK:2 * RET_QK], cos_ref[...], sin_ref[...]) * (RET_DK ** -0.5)

    x = gqkv_ref[...]
    conv = gcw_ref[3:4, :] * x
    for j in range(CONV_W - 1):
        conv = conv + gcw_ref[j:j + 1, :] * gcv_ref[j]
    gcv_out[0] = gcv_ref[1]
    gcv_out[1] = gcv_ref[2]
    gcv_out[2] = x
    cq = _silu(conv)
    gq = jnp.concatenate([_l2n(cq[:, h * GDN_DK:(h + 1) * GDN_DK]) * (GDN_DK ** -0.5)
                          for h in range(GDN_HEADS)], axis=1)
    gk = jnp.concatenate([_l2n(cq[:, GDN_K + h * GDN_DK:GDN_K + (h + 1) * GDN_DK])
                          for h in range(GDN_HEADS)], axis=1)
    gv_ref[...] = cq[:, 2 * GDN_K:]
    gl = _gate_lanes(gab_ref[...], alog_ref[...], dtb_ref[...])
    gx = _hdot(gl, _expand8())
    gx_ref[...] = jnp.where(_iota(gx.shape, 1) < GDN_HEADS * LANE, jnp.exp(gx), gx)
    qkt_ref[...] = jnp.concatenate([rq, rk, gq, gk], axis=1).T

    u = su_ref[...]
    h_re, h_im = [], []
    for j in range(SSM_BLK):
        sl = slice(j * SSM_BLK_STATE, (j + 1) * SSM_BLK_STATE)
        uj = u[:, j * LANE:(j + 1) * LANE]
        a_re, a_im = are_ref[:, sl], aim_ref[:, sl]
        p_re, p_im = sre_ref[:, sl], sim_ref[:, sl]
        h_re.append(a_re * p_re - a_im * p_im + _bdot(uj, bre_ref[j]))
        h_im.append(a_re * p_im + a_im * p_re + _bdot(uj, bim_ref[j]))
    h_re = jnp.concatenate(h_re, axis=1)
    h_im = jnp.concatenate(h_im, axis=1)
    sre_out[...] = h_re
    sim_out[...] = h_im
    yc_ref[...] = _s5_readout(u, h_re, h_im, cre_ref, cim_ref, d_ref[...], gw_ref, gb_ref[...])

    lx = lx_ref[...]
    conv = lcw_ref[3:4, :] * lx
    for j in range(CONV_W - 1):
        conv = conv + lcw_ref[j:j + 1, :] * lcv_ref[j]
    lcv_out[0] = lcv_ref[1]
    lcv_out[1] = lcv_ref[2]
    lcv_out[2] = lx
    a, b = _lru_gates(conv + lcb_ref[...], wa_ref, wx_ref, ba_ref[...], bx_ref[...], lam_ref[...])
    lh_ref[...] = a * lru_ref[...] + b


def _smix1(pieces, cos_row, sin_row, states, gdn_w, s5_w, lru_w):
    rows = pieces["ret"].shape[0]
    ins = [pieces["ret"], pieces["gqkv"], pieces["gab"], pieces["su"], pieces["lx"], cos_row, sin_row,
           *states, *gdn_w, *s5_w, *lru_w]
    full = lambda a: pl.BlockSpec(a.shape, lambda i: (0,) * a.ndim)
    sds = lambda *s: jax.ShapeDtypeStruct(s, F32)
    out_shape = [sds(QKT_ROWS, rows), sds(rows, GDN_HEADS * GDN_DV), sds(rows, 8 * LANE),
                 sds(rows, SSM_WIDTH), sds(rows, LRU_WIDTH),
                 sds(CONV_W - 1, rows, GDN_QKV), sds(rows, SSM_STATE), sds(rows, SSM_STATE),
                 sds(CONV_W - 1, rows, LRU_WIDTH)]
    return pl.pallas_call(
        _smix1_kernel,
        grid=(1,),
        in_specs=[full(a) for a in ins],
        out_specs=[pl.BlockSpec(s.shape, lambda i, nd=len(s.shape): (0,) * nd) for s in out_shape],
        out_shape=out_shape,
        compiler_params=_cparams("arbitrary"),
        name="sample_mix",
    )(*ins)


def _smix2_kernel(qkt_ref, rv_ref, gv_ref, gx_ref, rg_ref, gz_ref, lh_ref, lg_ref, nw_ref, sret_ref, sgdn_ref,
                  ya_ref, yb_ref, yd_ref, nret_ref, ngdn_ref, o_ret, o_gdn):
    i = pl.program_id(0)
    bt = rv_ref.shape[0]
    rows = qkt_ref.shape[1]
    gammas = _ret_gammas()
    qkt = qkt_ref[...]

    for j in range(bt):
        b = i * bt + j
        onehot = (_iota((rows, LANE), 0) == b).astype(F32)
        cols = _hdot(qkt, onehot)
        r = slice(j, j + 1)
        for h in range(RET_HEADS):
            q_c = cols[h * RET_DK:(h + 1) * RET_DK]
            k_c = cols[RET_QK + h * RET_DK:RET_QK + (h + 1) * RET_DK]
            v_r = rv_ref[r, h * RET_DV:(h + 1) * RET_DV]
            s_new = gammas[h] * sret_ref[j, h] + k_c * v_r
            nret_ref[j, h] = s_new
            o_ret[r, h * RET_DV:(h + 1) * RET_DV] = jnp.sum(q_c * s_new, axis=0, keepdims=True)
        base = 2 * RET_QK
        for h in range(GDN_HEADS):
            q_c = cols[base + h * GDN_DK:base + (h + 1) * GDN_DK]
            k_c = cols[base + GDN_K + h * GDN_DK:base + GDN_K + (h + 1) * GDN_DK]
            v_r = gv_ref[r, h * GDN_DV:(h + 1) * GDN_DV]
            e_g = gx_ref[r, h * LANE:(h + 1) * LANE]
            beta = gx_ref[r, (GDN_HEADS + h) * LANE:(GDN_HEADS + h + 1) * LANE]
            s = sgdn_ref[j, h]
            v_new = beta * (v_r - e_g * jnp.sum(k_c * s, axis=0, keepdims=True))
            s_new = e_g * s + k_c * v_new
            ngdn_ref[j, h] = s_new
            o_gdn[r, h * GDN_DV:(h + 1) * GDN_DV] = jnp.sum(q_c * s_new, axis=0, keepdims=True)
    rg, gz = rg_ref[...], gz_ref[...]
    o_r, o_g = o_ret[...], o_gdn[...]
    ya_ref[...] = jnp.concatenate(
        [_ln(o_r[:, h * RET_DV:(h + 1) * RET_DV]) for h in range(RET_HEADS)], axis=1) * _silu(rg)
    yb_ref[...] = jnp.concatenate(
        [_rms(o_g[:, h * GDN_DV:(h + 1) * GDN_DV]) * nw_ref[...] for h in range(GDN_HEADS)], axis=1) * _silu(gz)
    yd_ref[...] = lh_ref[...] * jax.nn.gelu(lg_ref[...])


def _smix2(qkt, rv, gv, gx, rg, gz, lh, lg, norm_w, s_ret, s_gdn, layer, bt):
    rows = rv.shape[0]
    row = lambda w: pl.BlockSpec((bt, w), lambda i: (i, 0))
    sds = lambda *s: jax.ShapeDtypeStruct(s, F32)
    ret_blk = pl.BlockSpec((None, bt, RET_HEADS, RET_DK, RET_DV), lambda i: (layer, i, 0, 0, 0))
    gdn_blk = pl.BlockSpec((None, bt, GDN_HEADS, GDN_DK, GDN_DV), lambda i: (layer, i, 0, 0, 0))
    return pl.pallas_call(
        _smix2_kernel,
        grid=(rows // bt,),
        in_specs=[pl.BlockSpec(qkt.shape, lambda i: (0, 0)),
                  pl.BlockSpec((bt, RET_V), lambda i: (i, 1)),
                  row(GDN_HEADS * GDN_DV), row(8 * LANE),
                  pl.BlockSpec((bt, RET_V), lambda i: (i, 2)),
                  row(GDN_HEADS * GDN_DV), row(LRU_WIDTH), row(LRU_WIDTH),
                  pl.BlockSpec(norm_w.shape, lambda i: (0, 0)), ret_blk, gdn_blk],
        out_specs=[row(RET_V), row(GDN_HEADS * GDN_DV), row(LRU_WIDTH),
                   pl.BlockSpec((bt, RET_HEADS, RET_DK, RET_DV), lambda i: (i, 0, 0, 0)),
                   pl.BlockSpec((bt, GDN_HEADS, GDN_DK, GDN_DV), lambda i: (i, 0, 0, 0))],
        out_shape=[sds(rows, RET_V), sds(rows, GDN_HEADS * GDN_DV), sds(rows, LRU_WIDTH),
                   sds(rows, RET_HEADS, RET_DK, RET_DV), sds(rows, GDN_HEADS, GDN_DK, GDN_DV)],
        scratch_shapes=[pltpu.VMEM((bt, RET_V), F32), pltpu.VMEM((bt, GDN_HEADS * GDN_DV), F32)],
        compiler_params=_cparams("parallel"),
        name="sample_state",
    )(qkt, rv, gv, gx, rg, gz, lh, lg, norm_w, s_ret, s_gdn)


def _block_diag(w):
    nb, bs, _ = w.shape
    return jnp.einsum("nij,nm->nimj", w, jnp.eye(nb, dtype=w.dtype)).reshape(nb * bs, nb * bs)


def _lane_row(v):
    return jnp.zeros((1, LANE), F32).at[0, :v.shape[0]].set(v)


def kernel(x_prompt, x_sample, c_prompt, c_sample, state_ret, state_gdn, state_gdn_conv, state_ssm_re, state_ssm_im, state_lru, state_lru_conv, w_ada, b_ada, ln_g, ln_b, w_ffn_up, w_ffn_down, w_in, gdn_conv_w, gdn_a_log, gdn_dt_bias, gdn_norm_w, ssm_lam_re, ssm_lam_im, ssm_log_step, ssm_b_re, ssm_b_im, ssm_c_re, ssm_c_im, ssm_d, ssm_glu_w, ssm_glu_b, lru_conv_w, lru_conv_b, lru_wa, lru_ba, lru_wx, lru_bx, lru_lam, w_branch, w_out):
    nb, t, _ = x_prompt.shape
    ns = x_sample.shape[0]
    depth = w_ada.shape[0]
    assert t % RET_CHUNK == 0 and x_sample.shape[1] == 1
    alpha = (2 * depth) ** 0.25
    tm = 256 if t % 256 == 0 else RET_CHUNK
    tc = 64

    wup = w_ffn_up.astype(BF16)
    wdn = w_ffn_down.astype(BF16)
    wbr = w_branch.astype(BF16)
    wout = w_out.astype(BF16)
    in_sizes = (RET_QK, RET_QK, RET_V, RET_V, GDN_QKV, GDN_HEADS, GDN_HEADS, GDN_HEADS * GDN_DV,
                SSM_WIDTH, LRU_WIDTH, LRU_WIDTH, N_BRANCH * D_MODEL)
    offs = np.concatenate([[0], np.cumsum(in_sizes)])
    seg = lambda a, b: w_in[:, :, offs[a]:offs[b]]
    w_main = jnp.concatenate([seg(0, 5), seg(7, 12)], axis=-1).astype(BF16)
    w_small = jnp.pad(seg(5, 7), ((0, 0), (0, 0), (0, LANE - 2 * GDN_HEADS))).astype(BF16)
    ln_g4 = ln_g.reshape(depth, N_SUB, 1, D_MODEL)
    ln_b4 = ln_b.reshape(depth, N_SUB, 1, D_MODEL)

    mod_all = _ada(jnp.concatenate([c_prompt, c_sample], axis=0), w_ada, b_ada)
    mod_p = mod_all[:, :nb].reshape(depth, nb, 3 * N_SUB, 1, D_MODEL).transpose(0, 2, 1, 3, 4)
    mod_s = mod_all[:, nb:]

    cos_s, sin_s = _rope_tables(jnp.full((1,), PAST_LEN))
    xp = x_prompt.reshape(nb * t, D_MODEL)
    xs = x_sample.reshape(ns, D_MODEL)
    new_p, new_s = [], []
    for l in range(depth):
        mp = _Mod(mod_p, l, False, t)
        ms = _Mod(mod_s, l, True)
        alog_row, dtb_row = _lane_row(gdn_a_log[l]), _lane_row(gdn_dt_bias[l])
        norm_w = gdn_norm_w[l].reshape(1, GDN_DV)
        s5p = _s5_params(ssm_lam_re[l], ssm_lam_im[l], ssm_log_step[l],
                         ssm_b_re[l], ssm_b_im[l], ssm_c_re[l], ssm_c_im[l])
        d_row = ssm_d[l].reshape(1, SSM_WIDTH)
        glu_w, glu_b = ssm_glu_w[l].astype(BF16), ssm_glu_b[l].reshape(1, SSM_WIDTH)
        lru_w = (lru_conv_w[l], lru_conv_b[l].reshape(1, LRU_WIDTH),
                 _block_diag(lru_wa[l]).astype(BF16), _block_diag(lru_wx[l]).astype(BF16),
                 lru_ba[l].reshape(1, LRU_WIDTH), lru_bx[l].reshape(1, LRU_WIDTH),
                 lru_lam[l].reshape(1, LRU_WIDTH))

        xp = _ffn(xp, mp, 0, 0, wup, wdn, ln_g4, ln_b4, alpha, tm)
        pc = _inproj(xp, mp, w_main, w_small, tm)
        ya, ret_p = _retention_prompt(pc["ret"], nb, t)
        yb, gdn_p, gcv_p = _gdn_prompt(pc["gqkv"], pc["gab"], pc["gz"], gdn_conv_w[l],
                                       alog_row, dtb_row, norm_w, nb, t)
        to_t = lambda a: a.reshape(nb, t, -1).transpose(1, 0, 2)
        from_t = lambda a: a.transpose(1, 0, 2).reshape(nb * t, -1)
        yc_t, sre_p, sim_p = _s5_prompt(to_t(pc["su"]), s5p, d_row, glu_w, glu_b, tc)
        yd_t, lru_p, lcv_p = _lru_prompt(to_t(pc["lx"]), to_t(pc["lg"]), *lru_w, tc)
        xp = _merge(xp, mp, ya, yb, from_t(yc_t), from_t(yd_t), pc["mg"], wbr, wout, ln_g4, ln_b4, alpha, tm)
        xp = _ffn(xp, mp, 2, 1, wup, wdn, ln_g4, ln_b4, alpha, tm)
        new_p.append((ret_p, gdn_p, gcv_p,
                      sre_p.reshape(nb, SSM_GROUPS, SSM_P), sim_p.reshape(nb, SSM_GROUPS, SSM_P),
                      lru_p, lcv_p.transpose(1, 0, 2)))

        xs = _ffn(xs, ms, 0, 0, wup, wdn, ln_g4, ln_b4, alpha, ns)
        sc = _inproj(xs, ms, w_main, w_small, ns)
        states = (state_gdn_conv[l].transpose(1, 0, 2),
                  state_ssm_re[l].reshape(ns, SSM_STATE), state_ssm_im[l].reshape(ns, SSM_STATE),
                  state_lru[l], state_lru_conv[l].transpose(1, 0, 2))
        (qkt, gv, gx, yc, lh, gcv_s, sre_s, sim_s, lcv_s) = _smix1(
            sc, cos_s, sin_s, states, (gdn_conv_w[l], alog_row, dtb_row),
            (*s5p, d_row, glu_w, glu_b), lru_w)
        ya, yb, yd, ret_s, gdn_s = _smix2(qkt, sc["ret"], gv, gx, sc["ret"], sc["gz"], lh, sc["lg"],
                                          norm_w, state_ret, state_gdn, l, SUBLANE)
        xs = _merge(xs, ms, ya, yb, yc, yd, sc["mg"], wbr, wout, ln_g4, ln_b4, alpha, ns)
        xs = _ffn(xs, ms, 2, 1, wup, wdn, ln_g4, ln_b4, alpha, ns)
        new_s.append((ret_s, gdn_s, gcv_s.transpose(1, 0, 2),
                      sre_s.reshape(ns, SSM_GROUPS, SSM_P), sim_s.reshape(ns, SSM_GROUPS, SSM_P),
                      lh, lcv_s.transpose(1, 0, 2)))

    ret_p, gdn_p, gcv_p, sre_p, sim_p, lru_p, lcv_p = [jnp.stack(z) for z in zip(*new_p)]
    ret_s, gdn_s, gcv_s, sre_s, sim_s, lru_s, lcv_s = [jnp.stack(z) for z in zip(*new_s)]
    return (xp.reshape(nb, t, D_MODEL), xs.reshape(ns, 1, D_MODEL),
            ret_p, ret_s, gdn_p, gdn_s, gcv_p, gcv_s, sre_p, sre_s, sim_p, sim_s,
            lru_p, lru_s, lcv_p, lcv_s)
```

```python
import functools
import math

import numpy as np
import jax
import jax.numpy as jnp
from jax import lax
from jax.experimental import pallas as pl
from jax.experimental.pallas import tpu as pltpu

F32 = jnp.float32
BF16 = jnp.bfloat16
HI = lax.Precision.HIGHEST

D_MODEL = 1024
RET_HEADS, RET_DK, RET_DV, RET_CHUNK = 4, 64, 128, 128
ROPE_BASE = 10000.0
GDN_HEADS, GDN_DK, GDN_DV, GDN_CHUNK = 4, 128, 128, 64
GDN_STEP_CHUNKS = 4
GDN_QKV = 2 * GDN_HEADS * GDN_DK + GDN_HEADS * GDN_DV
CONV_W = 4
SSM_GROUP, SSM_GROUPS, SSM_P = 16, 32, 64
SSM_WIDTH = SSM_GROUP * SSM_GROUPS
SSM_STATE = SSM_GROUPS * SSM_P
LRU_WIDTH, LRU_BLOCKS = 512, 8
LRU_C = 8.0
N_BRANCH, BRANCH_W = 4, 512
D_FF = 2816
N_SUB = 3
LN_EPS = 1e-5
PAST_LEN = 16384

RET_QK = RET_HEADS * RET_DK
RET_V = RET_HEADS * RET_DV
RET_W = 2 * RET_QK + 2 * RET_V
GDN_K = GDN_HEADS * GDN_DK
QKT_ROWS = 2 * RET_QK + 2 * GDN_K
LANE = 128
SUBLANE = 8
FF_CHUNKS = ((0, 768), (768, 1024), (1792, 1024))
VMEM_LIMIT = 56 * 1024 * 1024


def _cparams(*sem):
    return pltpu.CompilerParams(dimension_semantics=sem, vmem_limit_bytes=VMEM_LIMIT)


def _resident(shape, index_map):
    return pl.BlockSpec(shape, index_map, pipeline_mode=pl.Buffered(1))


def _ln(x):
    mu = jnp.mean(x, axis=-1, keepdims=True)
    xc = x - mu
    return xc * lax.rsqrt(jnp.mean(xc * xc, axis=-1, keepdims=True) + LN_EPS)


def _silu(x):
    return x * jax.nn.sigmoid(x)


def _softplus(x):
    return jnp.maximum(x, 0.0) + jnp.log1p(jnp.exp(-jnp.abs(x)))


def _bdot(a, w):
    return jnp.dot(a.astype(BF16), w.astype(BF16), preferred_element_type=F32)


def _split(x):
    hi = x.astype(BF16)
    return hi, (x - hi.astype(F32)).astype(BF16)


def _hdot(a, b):
    return jnp.dot(a, b, preferred_element_type=F32, precision=HI)


def _dot_nt(a, b, precision=None):
    return lax.dot_general(a, b, (((1,), (1,)), ((), ())),
                           preferred_element_type=F32, precision=precision)


def _iota(shape, dim):
    return lax.broadcasted_iota(jnp.int32, shape, dim)


def _to_time_major(x):
    nb, tc, width = x.shape
    return jnp.swapaxes(x, 0, 1).reshape(tc * nb, width)


def _from_time_major(y, nb):
    rows, width = y.shape
    return jnp.swapaxes(y.reshape(rows // nb, nb, width), 0, 1)


def _ada_kernel(c_ref, w_ref, b_ref, o_ref):
    o_ref[...] = _bdot(_silu(c_ref[...]), w_ref[...]) + b_ref[...]


def _ada(c_all, w_ada, b_ada):
    depth, _, n_out = w_ada.shape
    rows = c_all.shape[0]
    tn = 1152
    return pl.pallas_call(
        _ada_kernel,
        grid=(depth, n_out // tn),
        in_specs=[
            pl.BlockSpec((rows, D_MODEL), lambda l, j: (0, 0)),
            pl.BlockSpec((None, D_MODEL, tn), lambda l, j: (l, 0, j)),
            pl.BlockSpec((None, 1, tn), lambda l, j: (l, 0, j)),
        ],
        out_specs=pl.BlockSpec((None, rows, tn), lambda l, j: (l, 0, j)),
        out_shape=jax.ShapeDtypeStruct((depth, rows, n_out), F32),
        compiler_params=_cparams("parallel", "parallel"),
        name="ada",
    )(c_all, w_ada, b_ada.reshape(depth, 1, n_out))


class _Mod:
    def __init__(self, arr, layer, per_row, rows_per_batch=None):
        self.arr, self.layer, self.per_row, self.rpb = arr, layer, per_row, rows_per_batch

    def spec(self, sub, which, tm):
        k, l = 3 * sub + which, self.layer
        if self.per_row:
            return pl.BlockSpec((None, tm, D_MODEL), lambda i: (l, i, k))
        tiles = self.rpb // tm
        return pl.BlockSpec((None, None, None, 1, D_MODEL), lambda i: (l, k, i // tiles, 0, 0))


def _row_parts(ref, parts):
    if ref.shape[0] == 1:
        return [ref[...]] * parts
    hm = ref.shape[0] // parts
    return [ref[k * hm:(k + 1) * hm, :] for k in range(parts)]


def _ffn_kernel(x_ref, sh_ref, sc_ref, gt_ref, wup_ref, wdn_ref, g_ref, b_ref, o_ref, *, alpha, parts):
    ks = range(parts)
    xs, shs, scs, gts = (_row_parts(r, parts) for r in (x_ref, sh_ref, sc_ref, gt_ref))
    hs = [(_ln(xs[k]) * (1.0 + scs[k]) + shs[k]).astype(BF16) for k in ks]
    accs = [jnp.zeros(xs[k].shape, F32) for k in ks]
    for lo, width in FF_CHUNKS:
        a = [jnp.dot(hs[k], wup_ref[:, lo:lo + width], preferred_element_type=F32) for k in ks]
        b = [jnp.dot(hs[k], wup_ref[:, D_FF + lo:D_FF + lo + width], preferred_element_type=F32) for k in ks]
        act = [(_silu(a[k]) * b[k]).astype(BF16) for k in ks]
        accs = [accs[k] + jnp.dot(act[k], wdn_ref[lo:lo + width, :], preferred_element_type=F32) for k in ks]
    hm = x_ref.shape[0] // parts
    for k in ks:
        z = alpha * xs[k] + 0.5 * gts[k] * accs[k]
        o_ref[k * hm:(k + 1) * hm, :] = _ln(z) * g_ref[...] + b_ref[...]


def _ffn(x, mod, sub, which, wup, wdn, ln_g, ln_b, alpha, tm):
    n = x.shape[0]
    l = mod.layer
    row = pl.BlockSpec((tm, D_MODEL), lambda i: (i, 0))
    return pl.pallas_call(
        functools.partial(_ffn_kernel, alpha=alpha, parts=2 if tm >= 512 else 1),
        grid=(n // tm,),
        in_specs=[
            row, mod.spec(sub, 0, tm), mod.spec(sub, 1, tm), mod.spec(sub, 2, tm),
            _resident((None, None, D_MODEL, 2 * D_FF), lambda i: (l, which, 0, 0)),
            _resident((None, None, D_FF, D_MODEL), lambda i: (l, which, 0, 0)),
            pl.BlockSpec((None, None, 1, D_MODEL), lambda i: (l, sub, 0, 0)),
            pl.BlockSpec((None, None, 1, D_MODEL), lambda i: (l, sub, 0, 0)),
        ],
        out_specs=row,
        out_shape=jax.ShapeDtypeStruct((n, D_MODEL), F32),
        compiler_params=_cparams("parallel"),
        name="ffn",
    )(x, mod.arr, mod.arr, mod.arr, wup, wdn, ln_g, ln_b)


_IN_PIECES = (("ret", RET_W), ("gqkv", GDN_QKV), ("gz", 512), ("su", SSM_WIDTH),
              ("lx", LRU_WIDTH), ("lg", LRU_WIDTH))
IN_MAIN = sum(w for _, w in _IN_PIECES)


def _inproj_kernel(x_ref, sh_ref, sc_ref, w_ref, ws_ref, *o_refs):
    h = (_ln(x_ref[...]) * (1.0 + sc_ref[...]) + sh_ref[...]).astype(BF16)
    lo = 0
    for (_, width), o_ref in zip(_IN_PIECES, o_refs[:-1]):
        o_ref[...] = jnp.dot(h, w_ref[:, lo:lo + width], preferred_element_type=F32)
        lo += width
    o_refs[-1][...] = jnp.dot(h, ws_ref[...], preferred_element_type=F32)


def _inproj(x, mod, w_main, w_small, tm):
    n = x.shape[0]
    l = mod.layer
    widths = [w for _, w in _IN_PIECES] + [LANE]
    outs = pl.pallas_call(
        _inproj_kernel,
        grid=(n // tm,),
        in_specs=[
            pl.BlockSpec((tm, D_MODEL), lambda i: (i, 0)),
            mod.spec(1, 0, tm), mod.spec(1, 1, tm),
            _resident((None, D_MODEL, IN_MAIN), lambda i: (l, 0, 0)),
            _resident((None, D_MODEL, LANE), lambda i: (l, 0, 0)),
        ],
        out_specs=[pl.BlockSpec((tm, w), lambda i: (i, 0)) for w in widths],
        out_shape=[jax.ShapeDtypeStruct((n, w), F32) for w in widths],
        compiler_params=_cparams("parallel"),
        name="inproj",
    )(x, mod.arr, mod.arr, w_main, w_small)
    names = [nm for nm, _ in _IN_PIECES] + ["gab"]
    return dict(zip(names, outs))


def _rope_tables(pos):
    half = RET_DK // 2
    inv = ROPE_BASE ** (-jnp.arange(half, dtype=F32) / half)
    ang = pos.astype(F32)[:, None] * inv[None, :]
    cos, sin = jnp.cos(ang), jnp.sin(ang)
    cos_t = jnp.tile(jnp.concatenate([cos, cos], axis=-1), (1, RET_HEADS))
    sin_t = jnp.tile(jnp.concatenate([-sin, sin], axis=-1), (1, RET_HEADS))
    return cos_t, sin_t


def _rotate(z, cos, sin):
    half = RET_DK // 2
    first = (_iota(z.shape, 1) % RET_DK) < half
    swapped = jnp.where(first, pltpu.roll(z, RET_QK - half, 1), pltpu.roll(z, half, 1))
    return z * cos + swapped * sin


def _ret_gammas():
    return [1.0 - 2.0 ** (-5.0 - h) for h in range(RET_HEADS)]


def _ret_consts(chunk):
    log_g = np.log(np.array(_ret_gammas(), np.float64))
    idx = np.arange(chunk, dtype=np.float64)
    rel = idx[:, None] - idx[None, :]
    dmask = np.where(rel[None] >= 0, np.exp(log_g[:, None, None] * np.maximum(rel, 0.0)[None]), 0.0)
    qdec = np.repeat(np.exp(log_g[None, :] * (idx[:, None] + 1.0)), RET_DK, axis=1)
    kdec = np.repeat(np.exp(log_g[None, :] * (chunk - 1.0 - idx[:, None])), RET_DK, axis=1)
    head_r = np.arange(RET_QK) // RET_DK
    head_c = np.arange(RET_V) // RET_DV
    bd = (head_r[:, None] == head_c[None, :]).astype(np.float64)
    cd = bd * np.exp(log_g * chunk)[head_r][:, None]
    f = lambda a: jnp.asarray(a, F32)
    return f(dmask), f(qdec), f(kdec), f(cd), f(bd)


def _ret_kernel(x_ref, cos_ref, sin_ref, dmask_ref, qdec_ref, kdec_ref, cd_ref, bd_ref,
                y_ref, st_ref, s_scr):
    c = pl.program_id(1)

    @pl.when(c == 0)
    def _():
        s_scr[...] = jnp.zeros(s_scr.shape, F32)

    x = x_ref[...]
    q = _rotate(x[:, 0:RET_QK], cos_ref[...], sin_ref[...])
    k = _rotate(x[:, RET_QK:2 * RET_QK], cos_ref[...], sin_ref[...]) * (RET_DK ** -0.5)
    v = x[:, 2 * RET_QK:2 * RET_QK + RET_V].astype(BF16)
    gate = x[:, 2 * RET_QK + RET_V:]
    s = s_scr[...]
    cross = _bdot(q * qdec_ref[...], s)
    kb = k.astype(BF16)
    head = _iota(q.shape, 1) // RET_DK
    outs = []
    for h in range(RET_HEADS):
        qh = jnp.where(head == h, q, 0.0).astype(BF16)
        inner = _dot_nt(qh, kb) * dmask_ref[h]
        oh = _bdot(inner, v[:, h * RET_DV:(h + 1) * RET_DV]) + cross[:, h * RET_DV:(h + 1) * RET_DV]
        outs.append(_ln(oh))
    y_ref[...] = jnp.concatenate(outs, axis=1) * _silu(gate)
    kd_t = (k * kdec_ref[...]).T.astype(BF16)
    upd = jnp.dot(kd_t, v, preferred_element_type=F32)
    s_new = s * cd_ref[...] + upd * bd_ref[...]
    s_scr[...] = s_new

    @pl.when(c == pl.num_programs(1) - 1)
    def _():
        for h in range(RET_HEADS):
            st_ref[0, h] = s_new[h * RET_DK:(h + 1) * RET_DK, h * RET_DV:(h + 1) * RET_DV]


def _retention_prompt(ret, nb, t):
    chunk = RET_CHUNK
    nc = t // chunk
    cos_t, sin_t = _rope_tables(jnp.arange(t))
    consts = _ret_consts(chunk)
    full = lambda a: pl.BlockSpec(a.shape, lambda b, c: (0,) * a.ndim)
    return pl.pallas_call(
        _ret_kernel,
        grid=(nb, nc),
        in_specs=[
            pl.BlockSpec((chunk, RET_W), lambda b, c: (b * nc + c, 0)),
            pl.BlockSpec((chunk, RET_QK), lambda b, c: (c, 0)),
            pl.BlockSpec((chunk, RET_QK), lambda b, c: (c, 0)),
        ] + [full(a) for a in consts],
        out_specs=[
            pl.BlockSpec((chunk, RET_V), lambda b, c: (b * nc + c, 0)),
            pl.BlockSpec((1, RET_HEADS, RET_DK, RET_DV), lambda b, c: (b, 0, 0, 0)),
        ],
        out_shape=[
            jax.ShapeDtypeStruct((nb * t, RET_V), F32),
            jax.ShapeDtypeStruct((nb, RET_HEADS, RET_DK, RET_DV), F32),
        ],
        scratch_shapes=[pltpu.VMEM((RET_QK, RET_V), F32)],
        compiler_params=_cparams("parallel", "arbitrary"),
        name="retention",
    )(ret, cos_t, sin_t, *consts)


def _l2n(x):
    return x * lax.rsqrt(jnp.sum(x * x, axis=-1, keepdims=True) + 1e-6)


def _rms(x):
    return x * lax.rsqrt(jnp.mean(x * x, axis=-1, keepdims=True) + 1e-6)


def _gate_lanes(gab, alog_row, dtb_row):
    g = -jnp.exp(alog_row) * _softplus(gab + dtb_row)
    return jnp.where(_iota(gab.shape, 1) < GDN_HEADS, g, jax.nn.sigmoid(gab))


def _gdn_kernel(x_ref, gab_ref, gz_ref, cw_ref, alog_ref, dtb_ref, nw_ref,
                y_ref, st_ref, cv_ref, xbuf, s_scr):
    c = pl.program_id(1)
    ch = GDN_CHUNK

    @pl.when(c == 0)
    def _():
        xbuf[0:SUBLANE, :] = jnp.zeros((SUBLANE, GDN_QKV), F32)
        s_scr[...] = jnp.zeros(s_scr.shape, F32)

    rows = x_ref.shape[0]
    x = x_ref[...]
    xbuf[SUBLANE:SUBLANE + rows, :] = x
    conv = cw_ref[3:4, :] * x
    for j in range(CONV_W - 1):
        lo = SUBLANE - (CONV_W - 1) + j
        conv = conv + cw_ref[j:j + 1, :] * xbuf[lo:lo + rows, :]
    cv_ref[0] = xbuf[SUBLANE + rows - (CONV_W - 1):SUBLANE + rows, :]
    xbuf[0:SUBLANE, :] = x[rows - SUBLANE:, :]
    cq = _silu(conv)

    gl = _gate_lanes(gab_ref[...], alog_ref[...], dtb_ref[...])
    col = _iota((ch, 2 * ch), 1) % ch
    tril = _iota((ch, 2 * ch), 0) >= col
    strict = _iota((ch, 2 * ch), 0) > col
    is_g = _iota(gl.shape, 1) < GDN_HEADS
    r_i, c_i = _iota((rows, rows), 0), _iota((rows, rows), 1)
    chunk_tril = ((r_i >= c_i) & (r_i // ch == c_i // ch)).astype(F32)
    csum = _hdot(chunk_tril, jnp.where(is_g, gl, 0.0))
    gsel = jnp.where(is_g, csum, gl)
    n_sub = rows // ch
    heads = range(GDN_HEADS)
    probs = [(h, n) for h in heads for n in range(n_sub)]

    q_all, k_all, g_all, qe_all, rhs_all, kbeta_all = {}, {}, {}, {}, {}, {}
    for h in heads:
        q_all[h] = _l2n(cq[:, h * GDN_DK:(h + 1) * GDN_DK]) * (GDN_DK ** -0.5)
        k_all[h] = _l2n(cq[:, GDN_K + h * GDN_DK:GDN_K + (h + 1) * GDN_DK])
        v_h = cq[:, 2 * GDN_K + h * GDN_DV:2 * GDN_K + (h + 1) * GDN_DV]
        g_all[h] = jnp.broadcast_to(gsel[:, h:h + 1], (rows, LANE))
        b_h = jnp.broadcast_to(gsel[:, GDN_HEADS + h:GDN_HEADS + h + 1], (rows, LANE))
        e_h = jnp.exp(g_all[h])
        kbeta_all[h] = k_all[h] * b_h
        rhs_all[h] = jnp.concatenate([v_h * b_h, kbeta_all[h] * e_h], axis=1)
        qe_all[h] = q_all[h] * e_h

    decay, pp, uw, attn, kd_t, e_last = {}, {}, {}, {}, {}, {}
    for h, n in probs:
        sl = slice(n * ch, (n + 1) * ch)
        g_b = g_all[h][sl]
        g_cols = jnp.concatenate([g_b, g_b], axis=0).T[:ch, :]
        diff = g_b - g_cols
        decay[h, n] = jnp.where(tril, jnp.exp(jnp.where(tril, diff, 0.0)), 0.0)
        g_last = g_b[ch - 1:ch, :]
        e_last[h, n] = jnp.exp(g_last)
        kd_t[h, n] = (k_all[h][sl] * jnp.exp(g_last - g_b)).T.astype(BF16)
    for h, n in probs:
        sl = slice(n * ch, (n + 1) * ch)
        kb_hi, kb_lo = _split(kbeta_all[h][sl])
        k_hi, k_lo = _split(jnp.concatenate([k_all[h][sl], k_all[h][sl]], axis=0))
        kk = _dot_nt(jnp.concatenate([kb_hi, kb_hi, kb_lo, kb_lo], axis=1),
                     jnp.concatenate([k_hi, k_lo, k_hi, k_lo], axis=1))
        pp[h, n] = -jnp.where(strict, kk * decay[h, n], 0.0)
        attn[h, n] = _dot_nt(q_all[h][sl].astype(BF16), k_all[h][sl].astype(BF16)) * decay[h, n][:, :ch]
    for stage in range(int(math.log2(ch))):
        for h, n in probs:
            p_hi, p_lo = _split(pp[h, n])
            lhs = jnp.concatenate([p_hi, p_lo], axis=1)
            y = rhs_all[h][n * ch:(n + 1) * ch] if stage == 0 else uw[h, n]
            y_hi, y_lo = _split(y)
            uw[h, n] = y + jnp.dot(lhs, jnp.concatenate([y_hi, y_lo, y_hi, y_lo], axis=0),
                                   preferred_element_type=F32)
            if stage + 1 < int(math.log2(ch)):
                pp[h, n] = jnp.dot(lhs, jnp.concatenate([p_hi, p_lo, p_hi, p_lo], axis=0),
                                   preferred_element_type=F32)
    s = {h: s_scr[h] for h in heads}
    o_parts = {h: [] for h in heads}
    for n in range(n_sub):
        for h in heads:
            sl = slice(n * ch, (n + 1) * ch)
            u, w = uw[h, n][:, :GDN_DV], uw[h, n][:, GDN_DV:]
            ws_qs = _bdot(jnp.concatenate([w, qe_all[h][sl]], axis=0), s[h])
            v_new = u - ws_qs[:ch]
            o_parts[h].append(ws_qs[ch:] + _bdot(attn[h, n], v_new))
            s[h] = s[h] * e_last[h, n] + jnp.dot(kd_t[h, n], v_new.astype(BF16), preferred_element_type=F32)
    norm_w = nw_ref[...]
    gz = gz_ref[...]
    outs = []
    for h in heads:
        s_scr[h] = s[h]
        o = jnp.concatenate(o_parts[h], axis=0)
        outs.append(_rms(o) * norm_w * _silu(gz[:, h * GDN_DV:(h + 1) * GDN_DV]))
    y_ref[...] = jnp.concatenate(outs, axis=1)

    @pl.when(c == pl.num_programs(1) - 1)
    def _():
        st_ref[0] = s_scr[...]


def _gdn_prompt(gqkv, gab, gz, conv_w, alog_row, dtb_row, norm_w, nb, t):
    ch = GDN_STEP_CHUNKS * GDN_CHUNK if t % (GDN_STEP_CHUNKS * GDN_CHUNK) == 0 else GDN_CHUNK
    nc = t // ch
    row = lambda w: pl.BlockSpec((ch, w), lambda b, c: (b * nc + c, 0))
    full = lambda a: pl.BlockSpec(a.shape, lambda b, c: (0,) * a.ndim)
    return pl.pallas_call(
        _gdn_kernel,
        grid=(nb, nc),
        in_specs=[row(GDN_QKV), row(LANE), row(GDN_HEADS * GDN_DV),
                  full(conv_w), full(alog_row), full(dtb_row), full(norm_w)],
        out_specs=[
            row(GDN_HEADS * GDN_DV),
            pl.BlockSpec((1, GDN_HEADS, GDN_DK, GDN_DV), lambda b, c: (b, 0, 0, 0)),
            pl.BlockSpec((1, CONV_W - 1, GDN_QKV), lambda b, c: (b, 0, 0)),
        ],
        out_shape=[
            jax.ShapeDtypeStruct((nb * t, GDN_HEADS * GDN_DV), F32),
            jax.ShapeDtypeStruct((nb, GDN_HEADS, GDN_DK, GDN_DV), F32),
            jax.ShapeDtypeStruct((nb, CONV_W - 1, GDN_QKV), F32),
        ],
        scratch_shapes=[pltpu.VMEM((SUBLANE + ch, GDN_QKV), F32),
                        pltpu.VMEM((GDN_HEADS, GDN_DK, GDN_DV), F32)],
        compiler_params=_cparams("parallel", "arbitrary"),
        name="gdn",
    )(gqkv, gab, gz, conv_w, alog_row, dtb_row, norm_w)


SSM_BLK = 4
SSM_BLK_STATE = SSM_STATE // SSM_BLK


def _s5_params(lam_re, lam_im, log_step, b_re, b_im, c_re, c_im):
    step = jnp.exp(log_step.astype(F32))[:, None]
    mag = jnp.exp(lam_re * step)
    ab_re = mag * jnp.cos(lam_im * step)
    ab_im = mag * jnp.sin(lam_im * step)
    den = lam_re * lam_re + lam_im * lam_im
    nr = ab_re - 1.0
    f_re = (nr * lam_re + ab_im * lam_im) / den
    f_im = (ab_im * lam_re - nr * lam_im) / den
    bb_re = f_re[..., None] * b_re - f_im[..., None] * b_im
    bb_im = f_re[..., None] * b_im + f_im[..., None] * b_re
    gpb = SSM_GROUPS // SSM_BLK
    eye = jnp.eye(gpb, dtype=F32)

    def in_mat(bb):
        bb = bb.reshape(SSM_BLK, gpb, SSM_P, SSM_GROUP)
        m = jnp.einsum("jgpc,gk->jgckp", bb, eye)
        return m.reshape(SSM_BLK, gpb * SSM_GROUP, gpb * SSM_P).astype(BF16)

    def out_mat(cc):
        cc = cc.reshape(SSM_BLK, gpb, SSM_GROUP, SSM_P)
        m = jnp.einsum("jgcp,gk->jgpkc", cc, eye)
        return m.reshape(SSM_BLK, gpb * SSM_P, gpb * SSM_GROUP).astype(BF16)

    return (ab_re.reshape(1, SSM_STATE), ab_im.reshape(1, SSM_STATE),
            in_mat(bb_re), in_mat(bb_im), out_mat(c_re), out_mat(c_im))


def _s5_readout(u, h_re, h_im, cre_ref, cim_ref, d_row, gw_ref, gb_row):
    ys = []
    for j in range(SSM_BLK):
        sl = slice(j * SSM_BLK_STATE, (j + 1) * SSM_BLK_STATE)
        ys.append(_bdot(h_re[:, sl], cre_ref[j]) - _bdot(h_im[:, sl], cim_ref[j]))
    y = jax.nn.gelu(jnp.concatenate(ys, axis=1) + d_row * u)
    return y * jax.nn.sigmoid(_bdot(y, gw_ref[...]) + gb_row)


def _s5_kernel(u_ref, are_ref, aim_ref, bre_ref, bim_ref, cre_ref, cim_ref, d_ref, gw_ref, gb_ref,
               y_ref, hre_ref, him_ref, bu_re, bu_im, h_re, h_im, *, nb):
    i = pl.program_id(0)
    tc = u_ref.shape[1]
    rows = tc * nb

    @pl.when(i == 0)
    def _():
        h_re[...] = jnp.zeros(h_re.shape, F32)
        h_im[...] = jnp.zeros(h_im.shape, F32)

    u = _to_time_major(u_ref[...])
    for j in range(SSM_BLK):
        uj = u[:, j * LANE:(j + 1) * LANE]
        sl = slice(j * SSM_BLK_STATE, (j + 1) * SSM_BLK_STATE)
        bu_re[:, sl] = _bdot(uj, bre_ref[j])
        bu_im[:, sl] = _bdot(uj, bim_ref[j])
    for j in range(SSM_BLK):
        sl = slice(j * SSM_BLK_STATE, (j + 1) * SSM_BLK_STATE)
        a_re = jnp.broadcast_to(are_ref[:, sl], (nb, SSM_BLK_STATE))
        a_im = jnp.broadcast_to(aim_ref[:, sl], (nb, SSM_BLK_STATE))

        def body(t, carry):
            hr, hi = carry
            r = pl.ds(pl.multiple_of(t * nb, nb), nb)
            nr = a_re * hr - a_im * hi + bu_re[r, sl]
            ni = a_re * hi + a_im * hr + bu_im[r, sl]
            bu_re[r, sl] = nr
            bu_im[r, sl] = ni
            return nr, ni

        hr, hi = lax.fori_loop(0, tc, body, (h_re[:, sl], h_im[:, sl]), unroll=4)
        h_re[:, sl] = hr
        h_im[:, sl] = hi
    y = _s5_readout(u, bu_re[...], bu_im[...], cre_ref, cim_ref, d_ref[...], gw_ref, gb_ref[...])
    y_ref[...] = _from_time_major(y, nb)
    hre_ref[...] = h_re[...]
    him_ref[...] = h_im[...]


def _s5_prompt(u_t, params, d_row, glu_w, glu_b, tc):
    nb, t, _ = u_t.shape
    a_re, a_im, bre, bim, cre, cim = params
    full = lambda a: pl.BlockSpec(a.shape, lambda i: (0,) * a.ndim)
    blk = pl.BlockSpec((nb, tc, SSM_WIDTH), lambda i: (0, i, 0))
    st = pl.BlockSpec((nb, SSM_STATE), lambda i: (0, 0))
    return pl.pallas_call(
        functools.partial(_s5_kernel, nb=nb),
        grid=(t // tc,),
        in_specs=[blk] + [full(a) for a in (a_re, a_im, bre, bim, cre, cim, d_row, glu_w, glu_b)],
        out_specs=[blk, st, st],
        out_shape=[jax.ShapeDtypeStruct((nb, t, SSM_WIDTH), F32),
                   jax.ShapeDtypeStruct((nb, SSM_STATE), F32),
                   jax.ShapeDtypeStruct((nb, SSM_STATE), F32)],
        scratch_shapes=[pltpu.VMEM((tc * nb, SSM_STATE), F32), pltpu.VMEM((tc * nb, SSM_STATE), F32),
                        pltpu.VMEM((nb, SSM_STATE), F32), pltpu.VMEM((nb, SSM_STATE), F32)],
        compiler_params=_cparams("arbitrary"),
        name="s5",
    )(u_t, a_re, a_im, bre, bim, cre, cim, d_row, glu_w, glu_b)


def _lru_gates(cx, wa_ref, wx_ref, ba_row, bx_row, lam_row):
    r = jax.nn.sigmoid(_bdot(cx, wa_ref[...]) + ba_row)
    i = jax.nn.sigmoid(_bdot(cx, wx_ref[...]) + bx_row)
    log_a = -LRU_C * r * _softplus(-lam_row)
    a = jnp.exp(log_a)
    th = jnp.tanh(log_a)
    b = jnp.sqrt(-2.0 * th / (1.0 - th)) * (i * cx)
    return a, b


def _lru_kernel(x_ref, g_ref, cw_ref, cb_ref, wa_ref, wx_ref, ba_ref, bx_ref, lam_ref,
                y_ref, h_out, cv_out, xbuf, a_buf, b_buf, h_scr, *, nb):
    i = pl.program_id(0)
    tc = x_ref.shape[1]
    rows = tc * nb
    tail = (CONV_W - 1) * nb

    @pl.when(i == 0)
    def _():
        xbuf[0:tail, :] = jnp.zeros((tail, LRU_WIDTH), F32)
        h_scr[...] = jnp.zeros(h_scr.shape, F32)

    x = _to_time_major(x_ref[...])
    xbuf[tail:tail + rows, :] = x
    conv = cw_ref[3:4, :] * x
    for j in range(CONV_W - 1):
        conv = conv + cw_ref[j:j + 1, :] * xbuf[j * nb:j * nb + rows, :]
    cv_out[...] = xbuf[rows:rows + tail, :].reshape(CONV_W - 1, nb, LRU_WIDTH)
    xbuf[0:tail, :] = x[rows - tail:, :]
    a, b = _lru_gates(conv + cb_ref[...], wa_ref, wx_ref, ba_ref[...], bx_ref[...], lam_ref[...])
    a_buf[...] = a
    b_buf[...] = b

    def body(t, h):
        r = pl.ds(pl.multiple_of(t * nb, nb), nb)
        h = a_buf[r, :] * h + b_buf[r, :]
        b_buf[r, :] = h
        return h

    h = lax.fori_loop(0, tc, body, h_scr[...], unroll=8)
    h_scr[...] = h
    h_out[...] = h
    y = b_buf[...] * jax.nn.gelu(_to_time_major(g_ref[...]))
    y_ref[...] = _from_time_major(y, nb)


def _lru_prompt(x_t, g_t, conv_w, conv_b, wa, wx, ba, bx, lam, tc):
    nb, t, _ = x_t.shape
    full = lambda a: pl.BlockSpec(a.shape, lambda i: (0,) * a.ndim)
    blk = pl.BlockSpec((nb, tc, LRU_WIDTH), lambda i: (0, i, 0))
    rows = tc * nb
    return pl.pallas_call(
        functools.partial(_lru_kernel, nb=nb),
        grid=(t // tc,),
        in_specs=[blk, blk] + [full(a) for a in (conv_w, conv_b, wa, wx, ba, bx, lam)],
        out_specs=[blk,
                   pl.BlockSpec((nb, LRU_WIDTH), lambda i: (0, 0)),
                   pl.BlockSpec((CONV_W - 1, nb, LRU_WIDTH), lambda i: (0, 0, 0))],
        out_shape=[jax.ShapeDtypeStruct((nb, t, LRU_WIDTH), F32),
                   jax.ShapeDtypeStruct((nb, LRU_WIDTH), F32),
                   jax.ShapeDtypeStruct((CONV_W - 1, nb, LRU_WIDTH), F32)],
        scratch_shapes=[pltpu.VMEM(((CONV_W - 1) * nb + rows, LRU_WIDTH), F32),
                        pltpu.VMEM((rows, LRU_WIDTH), F32), pltpu.VMEM((rows, LRU_WIDTH), F32),
                        pltpu.VMEM((nb, LRU_WIDTH), F32)],
        compiler_params=_cparams("arbitrary"),
        name="lru",
    )(x_t, g_t, conv_w, conv_b, wa, wx, ba, bx, lam)


def _merge_kernel(x_ref, sh_ref, sc_ref, gt_ref, ya_ref, yb_ref, yc_ref, yd_ref, wmg_ref, wb_ref, wo_ref,
                  g_ref, b_ref, o_ref, *, alpha):
    x = x_ref[...]
    h = (_ln(x) * (1.0 + sc_ref[...]) + sh_ref[...]).astype(BF16)
    acc = jnp.zeros(x.shape, F32)
    for n, y_ref in enumerate((ya_ref, yb_ref, yc_ref, yd_ref)):
        logits = jnp.dot(h, wmg_ref[:, n * D_MODEL:(n + 1) * D_MODEL], preferred_element_type=F32)
        acc = acc + jax.nn.sigmoid(logits) * _bdot(y_ref[...], wb_ref[n])
    out = _bdot(acc, wo_ref[...])
    z = alpha * x + gt_ref[...] * out
    o_ref[...] = _ln(z) * g_ref[...] + b_ref[...]


def _merge(x, mod, ya, yb, yc, yd, w_mg, w_branch, w_out, ln_g, ln_b, alpha, tm):
    n = x.shape[0]
    l = mod.layer
    row = lambda w: pl.BlockSpec((tm, w), lambda i: (i, 0))
    return pl.pallas_call(
        functools.partial(_merge_kernel, alpha=alpha),
        grid=(n // tm,),
        in_specs=[
            row(D_MODEL), mod.spec(1, 0, tm), mod.spec(1, 1, tm), mod.spec(1, 2, tm),
            row(BRANCH_W), row(BRANCH_W), row(BRANCH_W), row(BRANCH_W),
            _resident((None, D_MODEL, N_BRANCH * D_MODEL), lambda i: (l, 0, 0)),
            _resident((None, N_BRANCH, BRANCH_W, D_MODEL), lambda i: (l, 0, 0, 0)),
            _resident((None, D_MODEL, D_MODEL), lambda i: (l, 0, 0)),
            pl.BlockSpec((None, None, 1, D_MODEL), lambda i: (l, 1, 0, 0)),
            pl.BlockSpec((None, None, 1, D_MODEL), lambda i: (l, 1, 0, 0)),
        ],
        out_specs=row(D_MODEL),
        out_shape=jax.ShapeDtypeStruct((n, D_MODEL), F32),
        compiler_params=_cparams("parallel"),
        name="merge",
    )(x, mod.arr, mod.arr, mod.arr, ya, yb, yc, yd, w_mg, w_branch, w_out, ln_g, ln_b)


def _smix1_kernel(ret_ref, gqkv_ref, gab_ref, su_ref, lx_ref, cos_ref, sin_ref,
                  gcv_ref, sre_ref, sim_ref, lru_ref, lcv_ref,
                  gcw_ref, alog_ref, dtb_ref,
                  are_ref, aim_ref, bre_ref, bim_ref, cre_ref, cim_ref, d_ref, gw_ref, gb_ref,
                  lcw_ref, lcb_ref, wa_ref, wx_ref, ba_ref, bx_ref, lam_ref,
                  qkt_ref, gv_ref, gx_ref, yc_ref, lh_ref, gcv_out, sre_out, sim_out, lcv_out):
    ret = ret_ref[...]
    rq = _rotate(ret[:, 0:RET_QK], cos_ref[...], sin_ref[...])
    rk = _rotate(ret[:, RET_QK:2 * RET_QK], cos_ref[...], sin_ref[...]) * (RET_DK ** -0.5)

    x = gqkv_ref[...]
    conv = gcw_ref[3:4, :] * x
    for j in range(CONV_W - 1):
        conv = conv + gcw_ref[j:j + 1, :] * gcv_ref[j]
    gcv_out[0] = gcv_ref[1]
    gcv_out[1] = gcv_ref[2]
    gcv_out[2] = x
    cq = _silu(conv)
    gq = jnp.concatenate([_l2n(cq[:, h * GDN_DK:(h + 1) * GDN_DK]) * (GDN_DK ** -0.5)
                          for h in range(GDN_HEADS)], axis=1)
    gk = jnp.concatenate([_l2n(cq[:, GDN_K + h * GDN_DK:GDN_K + (h + 1) * GDN_DK])
                          for h in range(GDN_HEADS)], axis=1)
    gv_ref[...] = cq[:, 2 * GDN_K:]
    gl = _gate_lanes(gab_ref[...], alog_ref[...], dtb_ref[...])
    gx = jnp.concatenate([jnp.broadcast_to(gl[:, r:r + 1], (gl.shape[0], LANE))
                          for r in range(2 * GDN_HEADS)], axis=1)
    gx_ref[...] = jnp.where(_iota(gx.shape, 1) < GDN_HEADS * LANE, jnp.exp(gx), gx)
    qkt_ref[...] = jnp.concatenate([rq, rk, gq, gk], axis=1).T

    u = su_ref[...]
    h_re, h_im = [], []
    for j in range(SSM_BLK):
        sl = slice(j * SSM_BLK_STATE, (j + 1) * SSM_BLK_STATE)
        uj = u[:, j * LANE:(j + 1) * LANE]
        a_re, a_im = are_ref[:, sl], aim_ref[:, sl]
        p_re, p_im = sre_ref[:, sl], sim_ref[:, sl]
        h_re.append(a_re * p_re - a_im * p_im + _bdot(uj, bre_ref[j]))
        h_im.append(a_re * p_im + a_im * p_re + _bdot(uj, bim_ref[j]))
    h_re = jnp.concatenate(h_re, axis=1)
    h_im = jnp.concatenate(h_im, axis=1)
    sre_out[...] = h_re
    sim_out[...] = h_im
    yc_ref[...] = _s5_readout(u, h_re, h_im, cre_ref, cim_ref, d_ref[...], gw_ref, gb_ref[...])

    lx = lx_ref[...]
    conv = lcw_ref[3:4, :] * lx
    for j in range(CONV_W - 1):
        conv = conv + lcw_ref[j:j + 1, :] * lcv_ref[j]
    lcv_out[0] = lcv_ref[1]
    lcv_out[1] = lcv_ref[2]
    lcv_out[2] = lx
    a, b = _lru_gates(conv + lcb_ref[...], wa_ref, wx_ref, ba_ref[...], bx_ref[...], lam_ref[...])
    lh_ref[...] = a * lru_ref[...] + b


def _smix1(pieces, cos_row, sin_row, states, gdn_w, s5_w, lru_w):
    rows = pieces["ret"].shape[0]
    ins = [pieces["ret"], pieces["gqkv"], pieces["gab"], pieces["su"], pieces["lx"], cos_row, sin_row,
           *states, *gdn_w, *s5_w, *lru_w]
    full = lambda a: pl.BlockSpec(a.shape, lambda i: (0,) * a.ndim)
    sds = lambda *s: jax.ShapeDtypeStruct(s, F32)
    out_shape = [sds(QKT_ROWS, rows), sds(rows, GDN_HEADS * GDN_DV), sds(rows, 8 * LANE),
                 sds(rows, SSM_WIDTH), sds(rows, LRU_WIDTH),
                 sds(CONV_W - 1, rows, GDN_QKV), sds(rows, SSM_STATE), sds(rows, SSM_STATE),
                 sds(CONV_W - 1, rows, LRU_WIDTH)]
    return pl.pallas_call(
        _smix1_kernel,
        grid=(1,),
        in_specs=[full(a) for a in ins],
        out_specs=[pl.BlockSpec(s.shape, lambda i, nd=len(s.shape): (0,) * nd) for s in out_shape],
        out_shape=out_shape,
        compiler_params=_cparams("arbitrary"),
        name="sample_mix",
    )(*ins)


def _smix2_kernel(qkt_ref, rv_ref, gv_ref, gx_ref, rg_ref, gz_ref, lh_ref, lg_ref, nw_ref, sret_ref, sgdn_ref,
                  ya_ref, yb_ref, yd_ref, nret_ref, ngdn_ref, o_ret, o_gdn):
    i = pl.program_id(0)
    bt = rv_ref.shape[0]
    rows = qkt_ref.shape[1]
    gammas = _ret_gammas()
    q_hi, q_lo = _split(qkt_ref[...])
    qk_split = jnp.concatenate([q_hi, q_lo], axis=1)

    for j in range(bt):
        b = i * bt + j
        onehot = jnp.where(_iota((2 * rows, LANE), 0) % rows == b, 1.0, 0.0).astype(BF16)
        cols = jnp.dot(qk_split, onehot, preferred_element_type=F32)
        r = slice(j, j + 1)
        for h in range(RET_HEADS):
            q_c = cols[h * RET_DK:(h + 1) * RET_DK]
            k_c = cols[RET_QK + h * RET_DK:RET_QK + (h + 1) * RET_DK]
            v_r = rv_ref[r, h * RET_DV:(h + 1) * RET_DV]
            s_new = gammas[h] * sret_ref[j, h] + k_c * v_r
            nret_ref[j, h] = s_new
            o_ret[r, h * RET_DV:(h + 1) * RET_DV] = jnp.sum(q_c * s_new, axis=0, keepdims=True)
        base = 2 * RET_QK
        for h in range(GDN_HEADS):
            q_c = cols[base + h * GDN_DK:base + (h + 1) * GDN_DK]
            k_c = cols[base + GDN_K + h * GDN_DK:base + GDN_K + (h + 1) * GDN_DK]
            v_r = gv_ref[r, h * GDN_DV:(h + 1) * GDN_DV]
            e_g = gx_ref[r, h * LANE:(h + 1) * LANE]
            beta = gx_ref[r, (GDN_HEADS + h) * LANE:(GDN_HEADS + h + 1) * LANE]
            s = sgdn_ref[j, h]
            v_new = beta * (v_r - e_g * jnp.sum(k_c * s, axis=0, keepdims=True))
            s_new = e_g * s + k_c * v_new
            ngdn_ref[j, h] = s_new
            o_gdn[r, h * GDN_DV:(h + 1) * GDN_DV] = jnp.sum(q_c * s_new, axis=0, keepdims=True)
    rg, gz = rg_ref[...], gz_ref[...]
    o_r, o_g = o_ret[...], o_gdn[...]
    ya_ref[...] = jnp.concatenate(
        [_ln(o_r[:, h * RET_DV:(h + 1) * RET_DV]) for h in range(RET_HEADS)], axis=1) * _silu(rg)
    yb_ref[...] = jnp.concatenate(
        [_rms(o_g[:, h * GDN_DV:(h + 1) * GDN_DV]) * nw_ref[...] for h in range(GDN_HEADS)], axis=1) * _silu(gz)
    yd_ref[...] = lh_ref[...] * jax.nn.gelu(lg_ref[...])


def _smix2(qkt, rv, gv, gx, rg, gz, lh, lg, norm_w, s_ret, s_gdn, layer, bt):
    rows = rv.shape[0]
    row = lambda w: pl.BlockSpec((bt, w), lambda i: (i, 0))
    sds = lambda *s: jax.ShapeDtypeStruct(s, F32)
    ret_blk = pl.BlockSpec((None, bt, RET_HEADS, RET_DK, RET_DV), lambda i: (layer, i, 0, 0, 0))
    gdn_blk = pl.BlockSpec((None, bt, GDN_HEADS, GDN_DK, GDN_DV), lambda i: (layer, i, 0, 0, 0))
    return pl.pallas_call(
        _smix2_kernel,
        grid=(rows // bt,),
        in_specs=[pl.BlockSpec(qkt.shape, lambda i: (0, 0)),
                  pl.BlockSpec((bt, RET_V), lambda i: (i, 1)),
                  row(GDN_HEADS * GDN_DV), row(8 * LANE),
                  pl.BlockSpec((bt, RET_V), lambda i: (i, 2)),
                  row(GDN_HEADS * GDN_DV), row(LRU_WIDTH), row(LRU_WIDTH),
                  pl.BlockSpec(norm_w.shape, lambda i: (0, 0)), ret_blk, gdn_blk],
        out_specs=[row(RET_V), row(GDN_HEADS * GDN_DV), row(LRU_WIDTH),
                   pl.BlockSpec((bt, RET_HEADS, RET_DK, RET_DV), lambda i: (i, 0, 0, 0)),
                   pl.BlockSpec((bt, GDN_HEADS, GDN_DK, GDN_DV), lambda i: (i, 0, 0, 0))],
        out_shape=[sds(rows, RET_V), sds(rows, GDN_HEADS * GDN_DV), sds(rows, LRU_WIDTH),
                   sds(rows, RET_HEADS, RET_DK, RET_DV), sds(rows, GDN_HEADS, GDN_DK, GDN_DV)],
        scratch_shapes=[pltpu.VMEM((bt, RET_V), F32), pltpu.VMEM((bt, GDN_HEADS * GDN_DV), F32)],
        compiler_params=_cparams("parallel"),
        name="sample_state",
    )(qkt, rv, gv, gx, rg, gz, lh, lg, norm_w, s_ret, s_gdn)


def _block_diag(w):
    nb, bs, _ = w.shape
    return jnp.einsum("nij,nm->nimj", w, jnp.eye(nb, dtype=w.dtype)).reshape(nb * bs, nb * bs)


def _lane_row(v):
    return jnp.zeros((1, LANE), F32).at[0, :v.shape[0]].set(v)


def kernel(x_prompt, x_sample, c_prompt, c_sample, state_ret, state_gdn, state_gdn_conv, state_ssm_re, state_ssm_im, state_lru, state_lru_conv, w_ada, b_ada, ln_g, ln_b, w_ffn_up, w_ffn_down, w_in, gdn_conv_w, gdn_a_log, gdn_dt_bias, gdn_norm_w, ssm_lam_re, ssm_lam_im, ssm_log_step, ssm_b_re, ssm_b_im, ssm_c_re, ssm_c_im, ssm_d, ssm_glu_w, ssm_glu_b, lru_conv_w, lru_conv_b, lru_wa, lru_ba, lru_wx, lru_bx, lru_lam, w_branch, w_out):
    nb, t, _ = x_prompt.shape
    ns = x_sample.shape[0]
    depth = w_ada.shape[0]
    assert t % RET_CHUNK == 0 and x_sample.shape[1] == 1
    alpha = (2 * depth) ** 0.25
    tm = 256 if t % 256 == 0 else RET_CHUNK
    tm_ffn = 512 if t % 512 == 0 else tm
    tc = 64

    wup = w_ffn_up.astype(BF16)
    wdn = w_ffn_down.astype(BF16)
    wbr = w_branch.astype(BF16)
    wout = w_out.astype(BF16)
    in_sizes = (RET_QK, RET_QK, RET_V, RET_V, GDN_QKV, GDN_HEADS, GDN_HEADS, GDN_HEADS * GDN_DV,
                SSM_WIDTH, LRU_WIDTH, LRU_WIDTH, N_BRANCH * D_MODEL)
    offs = np.concatenate([[0], np.cumsum(in_sizes)])
    seg = lambda a, b: w_in[:, :, offs[a]:offs[b]]
    w_main = jnp.concatenate([seg(0, 5), seg(7, 11)], axis=-1).astype(BF16)
    w_mg = seg(11, 12).astype(BF16)
    w_small = jnp.pad(seg(5, 7), ((0, 0), (0, 0), (0, LANE - 2 * GDN_HEADS))).astype(BF16)
    ln_g4 = ln_g.reshape(depth, N_SUB, 1, D_MODEL)
    ln_b4 = ln_b.reshape(depth, N_SUB, 1, D_MODEL)

    mod_all = _ada(jnp.concatenate([c_prompt, c_sample], axis=0), w_ada, b_ada)
    mod_p = mod_all[:, :nb].reshape(depth, nb, 3 * N_SUB, 1, D_MODEL).transpose(0, 2, 1, 3, 4)
    mod_s = mod_all[:, nb:]

    cos_s, sin_s = _rope_tables(jnp.full((1,), PAST_LEN))
    xp = x_prompt.reshape(nb * t, D_MODEL)
    xs = x_sample.reshape(ns, D_MODEL)
    new_p, new_s = [], []
    for l in range(depth):
        mp = _Mod(mod_p, l, False, t)
        ms = _Mod(mod_s, l, True)
        alog_row, dtb_row = _lane_row(gdn_a_log[l]), _lane_row(gdn_dt_bias[l])
        norm_w = gdn_norm_w[l].reshape(1, GDN_DV)
        s5p = _s5_params(ssm_lam_re[l], ssm_lam_im[l], ssm_log_step[l],
                         ssm_b_re[l], ssm_b_im[l], ssm_c_re[l], ssm_c_im[l])
        d_row = ssm_d[l].reshape(1, SSM_WIDTH)
        glu_w, glu_b = ssm_glu_w[l].astype(BF16), ssm_glu_b[l].reshape(1, SSM_WIDTH)
        lru_w = (lru_conv_w[l], lru_conv_b[l].reshape(1, LRU_WIDTH),
                 _block_diag(lru_wa[l]).astype(BF16), _block_diag(lru_wx[l]).astype(BF16),
                 lru_ba[l].reshape(1, LRU_WIDTH), lru_bx[l].reshape(1, LRU_WIDTH),
                 lru_lam[l].reshape(1, LRU_WIDTH))

        xp = _ffn(xp, mp, 0, 0, wup, wdn, ln_g4, ln_b4, alpha, tm_ffn)
        pc = _inproj(xp, mp, w_main, w_small, tm)
        ya, ret_p = _retention_prompt(pc["ret"], nb, t)
        yb, gdn_p, gcv_p = _gdn_prompt(pc["gqkv"], pc["gab"], pc["gz"], gdn_conv_w[l],
                                       alog_row, dtb_row, norm_w, nb, t)
        to_t = lambda a: a.reshape(nb, t, -1)
        from_t = lambda a: a.reshape(nb * t, -1)
        yc_t, sre_p, sim_p = _s5_prompt(to_t(pc["su"]), s5p, d_row, glu_w, glu_b, tc)
        yd_t, lru_p, lcv_p = _lru_prompt(to_t(pc["lx"]), to_t(pc["lg"]), *lru_w, tc)
        xp = _merge(xp, mp, ya, yb, from_t(yc_t), from_t(yd_t), w_mg, wbr, wout, ln_g4, ln_b4, alpha, tm)
        xp = _ffn(xp, mp, 2, 1, wup, wdn, ln_g4, ln_b4, alpha, tm_ffn)
        new_p.append((ret_p, gdn_p, gcv_p,
                      sre_p.reshape(nb, SSM_GROUPS, SSM_P), sim_p.reshape(nb, SSM_GROUPS, SSM_P),
                      lru_p, lcv_p.transpose(1, 0, 2)))

        xs = _ffn(xs, ms, 0, 0, wup, wdn, ln_g4, ln_b4, alpha, ns)
        sc = _inproj(xs, ms, w_main, w_small, ns)
        states = (state_gdn_conv[l].transpose(1, 0, 2),
                  state_ssm_re[l].reshape(ns, SSM_STATE), state_ssm_im[l].reshape(ns, SSM_STATE),
                  state_lru[l], state_lru_conv[l].transpose(1, 0, 2))
        (qkt, gv, gx, yc, lh, gcv_s, sre_s, sim_s, lcv_s) = _smix1(
            sc, cos_s, sin_s, states, (gdn_conv_w[l], alog_row, dtb_row),
            (*s5p, d_row, glu_w, glu_b), lru_w)
        ya, yb, yd, ret_s, gdn_s = _smix2(qkt, sc["ret"], gv, gx, sc["ret"], sc["gz"], lh, sc["lg"],
                                          norm_w, state_ret, state_gdn, l, SUBLANE)
        xs = _merge(xs, ms, ya, yb, yc, yd, w_mg, wbr, wout, ln_g4, ln_b4, alpha, ns)
        xs = _ffn(xs, ms, 2, 1, wup, wdn, ln_g4, ln_b4, alpha, ns)
        new_s.append((ret_s, gdn_s, gcv_s.transpose(1, 0, 2),
                      sre_s.reshape(ns, SSM_GROUPS, SSM_P), sim_s.reshape(ns, SSM_GROUPS, SSM_P),
                      lh, lcv_s.transpose(1, 0, 2)))

    ret_p, gdn_p, gcv_p, sre_p, sim_p, lru_p, lcv_p = [jnp.stack(z) for z in zip(*new_p)]
    ret_s, gdn_s, gcv_s, sre_s, sim_s, lru_s, lcv_s = [jnp.stack(z) for z in zip(*new_s)]
    return (xp.reshape(nb, t, D_MODEL), xs.reshape(ns, 1, D_MODEL),
            ret_p, ret_s, gdn_p, gdn_s, gcv_p, gcv_s, sre_p, sre_s, sim_p, sim_s,
            lru_p, lru_s, lcv_p, lcv_s)
```

```python
import functools
import math

import numpy as np
import jax
import jax.numpy as jnp
from jax import lax
from jax.experimental import pallas as pl
from jax.experimental.pallas import tpu as pltpu

F32 = jnp.float32
BF16 = jnp.bfloat16
HI = lax.Precision.HIGHEST

D_MODEL = 1024
RET_HEADS, RET_DK, RET_DV, RET_CHUNK = 4, 64, 128, 128
RET_STEP_CHUNKS = 4
ROPE_BASE = 10000.0
GDN_HEADS, GDN_DK, GDN_DV, GDN_CHUNK = 4, 128, 128, 64
GDN_STEP_CHUNKS = 4
GDN_QKV = 2 * GDN_HEADS * GDN_DK + GDN_HEADS * GDN_DV
CONV_W = 4
SSM_GROUP, SSM_GROUPS, SSM_P = 16, 32, 64
SSM_WIDTH = SSM_GROUP * SSM_GROUPS
SSM_STATE = SSM_GROUPS * SSM_P
LRU_WIDTH, LRU_BLOCKS = 512, 8
LRU_C = 8.0
N_BRANCH, BRANCH_W = 4, 512
D_FF = 2816
N_SUB = 3
LN_EPS = 1e-5
PAST_LEN = 16384

RET_QK = RET_HEADS * RET_DK
RET_V = RET_HEADS * RET_DV
RET_W = 2 * RET_QK + 2 * RET_V
GDN_K = GDN_HEADS * GDN_DK
QKT_ROWS = 2 * RET_QK + 2 * GDN_K
LANE = 128
SUBLANE = 8
FF_CHUNKS = ((0, 768), (768, 1024), (1792, 1024))
VMEM_LIMIT = 56 * 1024 * 1024


def _cparams(*sem):
    return pltpu.CompilerParams(dimension_semantics=sem, vmem_limit_bytes=VMEM_LIMIT)


def _layer_spec(a, layer):
    nd = a.ndim
    return pl.BlockSpec((None,) + a.shape[1:], lambda *_: (layer,) + (0,) * (nd - 1))


def _resident(shape, index_map):
    return pl.BlockSpec(shape, index_map, pipeline_mode=pl.Buffered(1))


def _ln(x):
    mu = jnp.mean(x, axis=-1, keepdims=True)
    xc = x - mu
    return xc * lax.rsqrt(jnp.mean(xc * xc, axis=-1, keepdims=True) + LN_EPS)


def _sigmoid(x):
    return 0.5 * jnp.tanh(0.5 * x) + 0.5


def _silu(x):
    return x * _sigmoid(x)


def _softplus(x):
    return jnp.maximum(x, 0.0) + jnp.log1p(jnp.exp(-jnp.abs(x)))


def _bdot(a, w):
    return jnp.dot(a.astype(BF16), w.astype(BF16), preferred_element_type=F32)


def _split(x):
    hi = x.astype(BF16)
    return hi, (x - hi.astype(F32)).astype(BF16)


def _hdot(a, b):
    return jnp.dot(a, b, preferred_element_type=F32, precision=HI)


def _dot_nt(a, b, precision=None):
    return lax.dot_general(a, b, (((1,), (1,)), ((), ())),
                           preferred_element_type=F32, precision=precision)


def _iota(shape, dim):
    return lax.broadcasted_iota(jnp.int32, shape, dim)


def _to_time_major(x):
    nb, tc, width = x.shape
    return jnp.swapaxes(x, 0, 1).reshape(tc * nb, width)


def _from_time_major(y, nb):
    rows, width = y.shape
    return jnp.swapaxes(y.reshape(rows // nb, nb, width), 0, 1)


def _ada_kernel(c_ref, w_ref, b_ref, o_ref):
    o_ref[...] = _bdot(_silu(c_ref[...]), w_ref[...]) + b_ref[...]


def _ada(c_all, w_ada, b_ada):
    depth, _, n_out = w_ada.shape
    rows = c_all.shape[0]
    tn = 1152
    return pl.pallas_call(
        _ada_kernel,
        grid=(depth, n_out // tn),
        in_specs=[
            pl.BlockSpec((rows, D_MODEL), lambda l, j: (0, 0)),
            pl.BlockSpec((None, D_MODEL, tn), lambda l, j: (l, 0, j)),
            pl.BlockSpec((None, 1, tn), lambda l, j: (l, 0, j)),
        ],
        out_specs=pl.BlockSpec((None, rows, tn), lambda l, j: (l, 0, j)),
        out_shape=jax.ShapeDtypeStruct((depth, rows, n_out), F32),
        compiler_params=_cparams("parallel", "parallel"),
        name="ada",
    )(c_all, w_ada, b_ada.reshape(depth, 1, n_out))


class _Mod:
    def __init__(self, arr, layer, per_row, rows_per_batch=None):
        self.arr, self.layer, self.per_row, self.rpb = arr, layer, per_row, rows_per_batch

    def spec(self, sub, which, tm):
        k, l = 3 * sub + which, self.layer
        if self.per_row:
            return pl.BlockSpec((None, tm, D_MODEL), lambda i: (l, i, k))
        tiles = self.rpb // tm
        return pl.BlockSpec((None, None, None, 1, D_MODEL), lambda i: (l, k, i // tiles, 0, 0))


def _row_parts(ref, parts):
    if ref.shape[0] == 1:
        return [ref[...]] * parts
    hm = ref.shape[0] // parts
    return [ref[k * hm:(k + 1) * hm, :] for k in range(parts)]


def _ffn_kernel(x_ref, sh_ref, sc_ref, gt_ref, wup_ref, wdn_ref, g_ref, b_ref, o_ref, *, alpha, parts):
    ks = range(parts)
    xs, shs, scs, gts = (_row_parts(r, parts) for r in (x_ref, sh_ref, sc_ref, gt_ref))
    hs = [(_ln(xs[k]) * (1.0 + scs[k]) + shs[k]).astype(BF16) for k in ks]
    accs = [jnp.zeros(xs[k].shape, F32) for k in ks]
    for lo, width in FF_CHUNKS:
        a = [jnp.dot(hs[k], wup_ref[:, lo:lo + width], preferred_element_type=F32) for k in ks]
        b = [jnp.dot(hs[k], wup_ref[:, D_FF + lo:D_FF + lo + width], preferred_element_type=F32) for k in ks]
        act = [(_silu(a[k]) * b[k]).astype(BF16) for k in ks]
        accs = [accs[k] + jnp.dot(act[k], wdn_ref[lo:lo + width, :], preferred_element_type=F32) for k in ks]
    hm = x_ref.shape[0] // parts
    for k in ks:
        z = alpha * xs[k] + 0.5 * gts[k] * accs[k]
        o_ref[k * hm:(k + 1) * hm, :] = _ln(z) * g_ref[...] + b_ref[...]


def _ffn(x, mod, sub, which, wup, wdn, ln_g, ln_b, alpha, tm):
    n = x.shape[0]
    l = mod.layer
    row = pl.BlockSpec((tm, D_MODEL), lambda i: (i, 0))
    return pl.pallas_call(
        functools.partial(_ffn_kernel, alpha=alpha, parts=2 if tm >= 512 else 1),
        grid=(n // tm,),
        in_specs=[
            row, mod.spec(sub, 0, tm), mod.spec(sub, 1, tm), mod.spec(sub, 2, tm),
            _resident((None, None, D_MODEL, 2 * D_FF), lambda i: (l, which, 0, 0)),
            _resident((None, None, D_FF, D_MODEL), lambda i: (l, which, 0, 0)),
            pl.BlockSpec((None, None, 1, D_MODEL), lambda i: (l, sub, 0, 0)),
            pl.BlockSpec((None, None, 1, D_MODEL), lambda i: (l, sub, 0, 0)),
        ],
        out_specs=row,
        out_shape=jax.ShapeDtypeStruct((n, D_MODEL), F32),
        compiler_params=_cparams("parallel"),
        name="ffn",
    )(x, mod.arr, mod.arr, mod.arr, wup, wdn, ln_g, ln_b)


_IN_PIECES = (("ret", RET_W), ("gqkv", GDN_QKV), ("gz", 512), ("su", SSM_WIDTH),
              ("lx", LRU_WIDTH), ("lg", LRU_WIDTH))
IN_MAIN = sum(w for _, w in _IN_PIECES)


def _inproj_kernel(x_ref, sh_ref, sc_ref, w_ref, ws_ref, *o_refs):
    h = (_ln(x_ref[...]) * (1.0 + sc_ref[...]) + sh_ref[...]).astype(BF16)
    lo = 0
    for (_, width), o_ref in zip(_IN_PIECES, o_refs[:-1]):
        o_ref[...] = jnp.dot(h, w_ref[:, lo:lo + width], preferred_element_type=F32)
        lo += width
    o_refs[-1][...] = jnp.dot(h, ws_ref[...], preferred_element_type=F32)


def _inproj(x, mod, w_main, w_small, tm):
    n = x.shape[0]
    l = mod.layer
    widths = [w for _, w in _IN_PIECES] + [LANE]
    outs = pl.pallas_call(
        _inproj_kernel,
        grid=(n // tm,),
        in_specs=[
            pl.BlockSpec((tm, D_MODEL), lambda i: (i, 0)),
            mod.spec(1, 0, tm), mod.spec(1, 1, tm),
            _resident((None, D_MODEL, IN_MAIN), lambda i: (l, 0, 0)),
            _resident((None, D_MODEL, LANE), lambda i: (l, 0, 0)),
        ],
        out_specs=[pl.BlockSpec((tm, w), lambda i: (i, 0)) for w in widths],
        out_shape=[jax.ShapeDtypeStruct((n, w), F32) for w in widths],
        compiler_params=_cparams("parallel"),
        name="inproj",
    )(x, mod.arr, mod.arr, w_main, w_small)
    names = [nm for nm, _ in _IN_PIECES] + ["gab"]
    return dict(zip(names, outs))


def _rope_tables(pos):
    half = RET_DK // 2
    inv = ROPE_BASE ** (-jnp.arange(half, dtype=F32) / half)
    ang = pos.astype(F32)[:, None] * inv[None, :]
    cos, sin = jnp.cos(ang), jnp.sin(ang)
    cos_t = jnp.tile(jnp.concatenate([cos, cos], axis=-1), (1, RET_HEADS))
    sin_t = jnp.tile(jnp.concatenate([-sin, sin], axis=-1), (1, RET_HEADS))
    return cos_t, sin_t


def _rotate(z, cos, sin):
    half = RET_DK // 2
    first = (_iota(z.shape, 1) % RET_DK) < half
    swapped = jnp.where(first, pltpu.roll(z, RET_QK - half, 1), pltpu.roll(z, half, 1))
    return z * cos + swapped * sin


def _ret_gammas():
    return [1.0 - 2.0 ** (-5.0 - h) for h in range(RET_HEADS)]


def _ret_consts(chunk, n_sub):
    log_g = np.log(np.array(_ret_gammas(), np.float64))
    idx = np.arange(chunk, dtype=np.float64)
    rel = idx[:, None] - idx[None, :]
    dmask = np.where(rel[None] >= 0, np.exp(log_g[:, None, None] * np.maximum(rel, 0.0)[None]), 0.0)
    qdec = np.repeat(np.exp(log_g[None, :] * (idx[:, None] + 1.0)), RET_DK, axis=1)
    kdec = np.repeat(np.exp(log_g[None, :] * (chunk - 1.0 - idx[:, None])), RET_DK, axis=1)
    qdec, kdec = np.tile(qdec, (n_sub, 1)), np.tile(kdec, (n_sub, 1))
    head_r = np.arange(RET_QK) // RET_DK
    head_c = np.arange(RET_V) // RET_DV
    bd = (head_r[:, None] == head_c[None, :]).astype(np.float64)
    cd = bd * np.exp(log_g * chunk)[head_r][:, None]
    f = lambda a: jnp.asarray(a, F32)
    return f(dmask), f(qdec), f(kdec), f(cd), f(bd)


def _ret_kernel(x_ref, cos_ref, sin_ref, dmask_ref, qdec_ref, kdec_ref, cd_ref, bd_ref,
                y_ref, st_ref, s_scr):
    c = pl.program_id(1)
    ch = RET_CHUNK
    n_sub = x_ref.shape[0] // ch

    @pl.when(c == 0)
    def _():
        s_scr[...] = jnp.zeros(s_scr.shape, F32)

    x = x_ref[...]
    q = _rotate(x[:, 0:RET_QK], cos_ref[...], sin_ref[...])
    k = _rotate(x[:, RET_QK:2 * RET_QK], cos_ref[...], sin_ref[...]) * (RET_DK ** -0.5)
    v = x[:, 2 * RET_QK:2 * RET_QK + RET_V].astype(BF16)
    gate = x[:, 2 * RET_QK + RET_V:]
    q_dec = (q * qdec_ref[...]).astype(BF16)
    k_dec = k * kdec_ref[...]
    kb = k.astype(BF16)
    head = _iota((ch, RET_QK), 1) // RET_DK
    intra, upd = {}, {}
    for n in range(n_sub):
        sl = slice(n * ch, (n + 1) * ch)
        for h in range(RET_HEADS):
            qh = jnp.where(head == h, q[sl], 0.0).astype(BF16)
            inner = _dot_nt(qh, kb[sl]) * dmask_ref[h]
            intra[n, h] = _bdot(inner, v[sl, h * RET_DV:(h + 1) * RET_DV])
        upd[n] = jnp.dot(k_dec[sl].T.astype(BF16), v[sl], preferred_element_type=F32) * bd_ref[...]
    s = s_scr[...]
    cross = {}
    for n in range(n_sub):
        cross[n] = jnp.dot(q_dec[n * ch:(n + 1) * ch], s.astype(BF16), preferred_element_type=F32)
        s = s * cd_ref[...] + upd[n]
    s_scr[...] = s
    for n in range(n_sub):
        sl = slice(n * ch, (n + 1) * ch)
        outs = [_ln(intra[n, h] + cross[n][:, h * RET_DV:(h + 1) * RET_DV]) for h in range(RET_HEADS)]
        y_ref[sl, :] = jnp.concatenate(outs, axis=1) * _silu(gate[sl])

    @pl.when(c == pl.num_programs(1) - 1)
    def _():
        for h in range(RET_HEADS):
            st_ref[0, h] = s[h * RET_DK:(h + 1) * RET_DK, h * RET_DV:(h + 1) * RET_DV]


def _retention_prompt(ret, nb, t):
    n_sub = RET_STEP_CHUNKS if t % (RET_STEP_CHUNKS * RET_CHUNK) == 0 else 1
    chunk = n_sub * RET_CHUNK
    nc = t // chunk
    cos_t, sin_t = _rope_tables(jnp.arange(t))
    consts = _ret_consts(RET_CHUNK, n_sub)
    full = lambda a: pl.BlockSpec(a.shape, lambda b, c: (0,) * a.ndim)
    return pl.pallas_call(
        _ret_kernel,
        grid=(nb, nc),
        in_specs=[
            pl.BlockSpec((chunk, RET_W), lambda b, c: (b * nc + c, 0)),
            pl.BlockSpec((chunk, RET_QK), lambda b, c: (c, 0)),
            pl.BlockSpec((chunk, RET_QK), lambda b, c: (c, 0)),
        ] + [full(a) for a in consts],
        out_specs=[
            pl.BlockSpec((chunk, RET_V), lambda b, c: (b * nc + c, 0)),
            pl.BlockSpec((1, RET_HEADS, RET_DK, RET_DV), lambda b, c: (b, 0, 0, 0)),
        ],
        out_shape=[
            jax.ShapeDtypeStruct((nb * t, RET_V), F32),
            jax.ShapeDtypeStruct((nb, RET_HEADS, RET_DK, RET_DV), F32),
        ],
        scratch_shapes=[pltpu.VMEM((RET_QK, RET_V), F32)],
        compiler_params=_cparams("parallel", "arbitrary"),
        name="retention",
    )(ret, cos_t, sin_t, *consts)


def _l2n(x):
    return x * lax.rsqrt(jnp.sum(x * x, axis=-1, keepdims=True) + 1e-6)


def _rms(x):
    return x * lax.rsqrt(jnp.mean(x * x, axis=-1, keepdims=True) + 1e-6)


def _gate_lanes(gab, alog_row, dtb_row):
    g = -jnp.exp(alog_row) * _softplus(gab + dtb_row)
    return jnp.where(_iota(gab.shape, 1) < GDN_HEADS, g, _sigmoid(gab))


def _gdn_kernel(x_ref, gab_ref, gz_ref, cw_ref, alog_ref, dtb_ref, nw_ref,
                y_ref, st_ref, cv_ref, xbuf, s_scr):
    c = pl.program_id(1)
    ch = GDN_CHUNK

    @pl.when(c == 0)
    def _():
        xbuf[0:SUBLANE, :] = jnp.zeros((SUBLANE, GDN_QKV), F32)
        s_scr[...] = jnp.zeros(s_scr.shape, F32)

    rows = x_ref.shape[0]
    x = x_ref[...]
    xbuf[SUBLANE:SUBLANE + rows, :] = x
    conv = cw_ref[3:4, :] * x
    for j in range(CONV_W - 1):
        lo = SUBLANE - (CONV_W - 1) + j
        conv = conv + cw_ref[j:j + 1, :] * xbuf[lo:lo + rows, :]
    cv_ref[0] = xbuf[SUBLANE + rows - (CONV_W - 1):SUBLANE + rows, :]
    xbuf[0:SUBLANE, :] = x[rows - SUBLANE:, :]
    cq = _silu(conv)

    gl = _gate_lanes(gab_ref[...], alog_ref[...], dtb_ref[...])
    col = _iota((ch, 2 * ch), 1) % ch
    tril = _iota((ch, 2 * ch), 0) >= col
    strict = _iota((ch, 2 * ch), 0) > col
    is_g = _iota(gl.shape, 1) < GDN_HEADS
    r_i, c_i = _iota((rows, rows), 0), _iota((rows, rows), 1)
    chunk_tril = ((r_i >= c_i) & (r_i // ch == c_i // ch)).astype(F32)
    csum = _hdot(chunk_tril, jnp.where(is_g, gl, 0.0))
    gsel = jnp.where(is_g, csum, gl)
    n_sub = rows // ch
    heads = range(GDN_HEADS)
    probs = [(h, n) for h in heads for n in range(n_sub)]

    q_all, k_all, g_all, qe_all, rhs_all, kbeta_all = {}, {}, {}, {}, {}, {}
    for h in heads:
        q_all[h] = _l2n(cq[:, h * GDN_DK:(h + 1) * GDN_DK]) * (GDN_DK ** -0.5)
        k_all[h] = _l2n(cq[:, GDN_K + h * GDN_DK:GDN_K + (h + 1) * GDN_DK])
        v_h = cq[:, 2 * GDN_K + h * GDN_DV:2 * GDN_K + (h + 1) * GDN_DV]
        g_all[h] = jnp.broadcast_to(gsel[:, h:h + 1], (rows, LANE))
        b_h = jnp.broadcast_to(gsel[:, GDN_HEADS + h:GDN_HEADS + h + 1], (rows, LANE))
        e_h = jnp.exp(g_all[h])
        kbeta_all[h] = k_all[h] * b_h
        rhs_all[h] = jnp.concatenate([v_h * b_h, kbeta_all[h] * e_h], axis=1)
        qe_all[h] = q_all[h] * e_h

    decay, pp, uw, attn, kd_t, e_last = {}, {}, {}, {}, {}, {}
    for h, n in probs:
        sl = slice(n * ch, (n + 1) * ch)
        g_b = g_all[h][sl]
        g_cols = jnp.concatenate([g_b, g_b], axis=0).T[:ch, :]
        diff = g_b - g_cols
        decay[h, n] = jnp.where(tril, jnp.exp(jnp.where(tril, diff, 0.0)), 0.0)
        g_last = g_b[ch - 1:ch, :]
        e_last[h, n] = jnp.exp(g_last)
        kd_t[h, n] = (k_all[h][sl] * jnp.exp(g_last - g_b)).T.astype(BF16)
    for h, n in probs:
        sl = slice(n * ch, (n + 1) * ch)
        kb_hi, kb_lo = _split(kbeta_all[h][sl])
        k_hi, k_lo = _split(jnp.concatenate([k_all[h][sl], k_all[h][sl]], axis=0))
        kk = _dot_nt(jnp.concatenate([kb_hi, kb_hi, kb_lo, kb_lo], axis=1),
                     jnp.concatenate([k_hi, k_lo, k_hi, k_lo], axis=1))
        pp[h, n] = -jnp.where(strict, kk * decay[h, n], 0.0)
        attn[h, n] = _dot_nt(q_all[h][sl].astype(BF16), k_all[h][sl].astype(BF16)) * decay[h, n][:, :ch]
    for stage in range(int(math.log2(ch))):
        for h, n in probs:
            p_hi, p_lo = _split(pp[h, n])
            lhs = jnp.concatenate([p_hi, p_lo], axis=1)
            y = rhs_all[h][n * ch:(n + 1) * ch] if stage == 0 else uw[h, n]
            y_hi, y_lo = _split(y)
            uw[h, n] = y + jnp.dot(lhs, jnp.concatenate([y_hi, y_lo, y_hi, y_lo], axis=0),
                                   preferred_element_type=F32)
            if stage + 1 < int(math.log2(ch)):
                pp[h, n] = jnp.dot(lhs, jnp.concatenate([p_hi, p_lo, p_hi, p_lo], axis=0),
                                   preferred_element_type=F32)
    s = {h: s_scr[h] for h in heads}
    o_parts = {h: [] for h in heads}
    for n in range(n_sub):
        for h in heads:
            sl = slice(n * ch, (n + 1) * ch)
            u, w = uw[h, n][:, :GDN_DV], uw[h, n][:, GDN_DV:]
            ws_qs = _bdot(jnp.concatenate([w, qe_all[h][sl]], axis=0), s[h])
            v_new = u - ws_qs[:ch]
            o_parts[h].append(ws_qs[ch:] + _bdot(attn[h, n], v_new))
            s[h] = s[h] * e_last[h, n] + jnp.dot(kd_t[h, n], v_new.astype(BF16), preferred_element_type=F32)
    norm_w = nw_ref[...]
    gz = gz_ref[...]
    outs = []
    for h in heads:
        s_scr[h] = s[h]
        o = jnp.concatenate(o_parts[h], axis=0)
        outs.append(_rms(o) * norm_w * _silu(gz[:, h * GDN_DV:(h + 1) * GDN_DV]))
    y_ref[...] = jnp.concatenate(outs, axis=1)

    @pl.when(c == pl.num_programs(1) - 1)
    def _():
        st_ref[0] = s_scr[...]


def _gdn_prompt(gqkv, gab, gz, conv_w, alog_row, dtb_row, norm_w, layer, nb, t):
    ch = GDN_STEP_CHUNKS * GDN_CHUNK if t % (GDN_STEP_CHUNKS * GDN_CHUNK) == 0 else GDN_CHUNK
    nc = t // ch
    row = lambda w: pl.BlockSpec((ch, w), lambda b, c: (b * nc + c, 0))
    return pl.pallas_call(
        _gdn_kernel,
        grid=(nb, nc),
        in_specs=[row(GDN_QKV), row(LANE), row(GDN_HEADS * GDN_DV)]
        + [_layer_spec(a, layer) for a in (conv_w, alog_row, dtb_row, norm_w)],
        out_specs=[
            row(GDN_HEADS * GDN_DV),
            pl.BlockSpec((1, GDN_HEADS, GDN_DK, GDN_DV), lambda b, c: (b, 0, 0, 0)),
            pl.BlockSpec((1, CONV_W - 1, GDN_QKV), lambda b, c: (b, 0, 0)),
        ],
        out_shape=[
            jax.ShapeDtypeStruct((nb * t, GDN_HEADS * GDN_DV), F32),
            jax.ShapeDtypeStruct((nb, GDN_HEADS, GDN_DK, GDN_DV), F32),
            jax.ShapeDtypeStruct((nb, CONV_W - 1, GDN_QKV), F32),
        ],
        scratch_shapes=[pltpu.VMEM((SUBLANE + ch, GDN_QKV), F32),
                        pltpu.VMEM((GDN_HEADS, GDN_DK, GDN_DV), F32)],
        compiler_params=_cparams("parallel", "arbitrary"),
        name="gdn",
    )(gqkv, gab, gz, conv_w, alog_row, dtb_row, norm_w)


SSM_BLK = 4
SSM_BLK_STATE = SSM_STATE // SSM_BLK


def _s5_params(lam_re, lam_im, log_step, b_re, b_im, c_re, c_im):
    step = jnp.exp(log_step.astype(F32))[:, None]
    mag = jnp.exp(lam_re * step)
    ab_re = mag * jnp.cos(lam_im * step)
    ab_im = mag * jnp.sin(lam_im * step)
    den = lam_re * lam_re + lam_im * lam_im
    nr = ab_re - 1.0
    f_re = (nr * lam_re + ab_im * lam_im) / den
    f_im = (ab_im * lam_re - nr * lam_im) / den
    bb_re = f_re[..., None] * b_re - f_im[..., None] * b_im
    bb_im = f_re[..., None] * b_im + f_im[..., None] * b_re
    gpb = SSM_GROUPS // SSM_BLK
    eye = jnp.eye(gpb, dtype=F32)

    def in_mat(bb):
        bb = bb.reshape(SSM_BLK, gpb, SSM_P, SSM_GROUP)
        m = jnp.einsum("jgpc,gk->jgckp", bb, eye)
        return m.reshape(SSM_BLK, gpb * SSM_GROUP, gpb * SSM_P).astype(BF16)

    def out_mat(cc):
        cc = cc.reshape(SSM_BLK, gpb, SSM_GROUP, SSM_P)
        m = jnp.einsum("jgcp,gk->jgpkc", cc, eye)
        return m.reshape(SSM_BLK, gpb * SSM_P, gpb * SSM_GROUP).astype(BF16)

    return (ab_re.reshape(1, SSM_STATE), ab_im.reshape(1, SSM_STATE),
            in_mat(bb_re), in_mat(bb_im), out_mat(c_re), out_mat(c_im))


def _s5_readout(u, h_re, h_im, cre_ref, cim_ref, d_row, gw_ref, gb_row):
    ys = []
    for j in range(SSM_BLK):
        sl = slice(j * SSM_BLK_STATE, (j + 1) * SSM_BLK_STATE)
        ys.append(_bdot(h_re[:, sl], cre_ref[j]) - _bdot(h_im[:, sl], cim_ref[j]))
    y = jax.nn.gelu(jnp.concatenate(ys, axis=1) + d_row * u)
    return y * _sigmoid(_bdot(y, gw_ref[...]) + gb_row)


def _s5_kernel(u_ref, are_ref, aim_ref, bre_ref, bim_ref, cre_ref, cim_ref, d_ref, gw_ref, gb_ref,
               y_ref, hre_ref, him_ref, bu_re, bu_im, h_re, h_im, *, nb):
    i = pl.program_id(0)
    tc = u_ref.shape[1]
    rows = tc * nb

    @pl.when(i == 0)
    def _():
        h_re[...] = jnp.zeros(h_re.shape, F32)
        h_im[...] = jnp.zeros(h_im.shape, F32)

    u = _to_time_major(u_ref[...])
    for j in range(SSM_BLK):
        uj = u[:, j * LANE:(j + 1) * LANE]
        sl = slice(j * SSM_BLK_STATE, (j + 1) * SSM_BLK_STATE)
        bu_re[:, sl] = _bdot(uj, bre_ref[j])
        bu_im[:, sl] = _bdot(uj, bim_ref[j])
    for j in range(SSM_BLK):
        sl = slice(j * SSM_BLK_STATE, (j + 1) * SSM_BLK_STATE)
        a_re = jnp.broadcast_to(are_ref[:, sl], (nb, SSM_BLK_STATE))
        a_im = jnp.broadcast_to(aim_ref[:, sl], (nb, SSM_BLK_STATE))

        def body(t, carry):
            hr, hi = carry
            r = pl.ds(pl.multiple_of(t * nb, nb), nb)
            nr = a_re * hr - a_im * hi + bu_re[r, sl]
            ni = a_re * hi + a_im * hr + bu_im[r, sl]
            bu_re[r, sl] = nr
            bu_im[r, sl] = ni
            return nr, ni

        hr, hi = lax.fori_loop(0, tc, body, (h_re[:, sl], h_im[:, sl]), unroll=4)
        h_re[:, sl] = hr
        h_im[:, sl] = hi
    y = _s5_readout(u, bu_re[...], bu_im[...], cre_ref, cim_ref, d_ref[...], gw_ref, gb_ref[...])
    y_ref[...] = _from_time_major(y, nb)
    hre_ref[...] = h_re[...]
    him_ref[...] = h_im[...]


def _s5_prompt(u_t, s5_w, layer, tc):
    nb, t, _ = u_t.shape
    blk = pl.BlockSpec((nb, tc, SSM_WIDTH), lambda i: (0, i, 0))
    st = pl.BlockSpec((nb, SSM_STATE), lambda i: (0, 0))
    return pl.pallas_call(
        functools.partial(_s5_kernel, nb=nb),
        grid=(t // tc,),
        in_specs=[blk] + [_layer_spec(a, layer) for a in s5_w],
        out_specs=[blk, st, st],
        out_shape=[jax.ShapeDtypeStruct((nb, t, SSM_WIDTH), F32),
                   jax.ShapeDtypeStruct((nb, SSM_STATE), F32),
                   jax.ShapeDtypeStruct((nb, SSM_STATE), F32)],
        scratch_shapes=[pltpu.VMEM((tc * nb, SSM_STATE), F32), pltpu.VMEM((tc * nb, SSM_STATE), F32),
                        pltpu.VMEM((nb, SSM_STATE), F32), pltpu.VMEM((nb, SSM_STATE), F32)],
        compiler_params=_cparams("arbitrary"),
        name="s5",
    )(u_t, *s5_w)


def _lru_gates(cx, wa_ref, wx_ref, ba_row, bx_row, lam_row):
    r = _sigmoid(_bdot(cx, wa_ref[...]) + ba_row)
    i = _sigmoid(_bdot(cx, wx_ref[...]) + bx_row)
    log_a = -LRU_C * r * _softplus(-lam_row)
    a = jnp.exp(log_a)
    th = jnp.tanh(log_a)
    b = jnp.sqrt(-2.0 * th / (1.0 - th)) * (i * cx)
    return a, b


def _lru_kernel(x_ref, g_ref, cw_ref, cb_ref, wa_ref, wx_ref, ba_ref, bx_ref, lam_ref,
                y_ref, h_out, cv_out, xbuf, a_buf, b_buf, h_scr, *, nb):
    i = pl.program_id(0)
    tc = x_ref.shape[1]
    rows = tc * nb
    tail = (CONV_W - 1) * nb

    @pl.when(i == 0)
    def _():
        xbuf[0:tail, :] = jnp.zeros((tail, LRU_WIDTH), F32)
        h_scr[...] = jnp.zeros(h_scr.shape, F32)

    x = _to_time_major(x_ref[...])
    xbuf[tail:tail + rows, :] = x
    conv = cw_ref[3:4, :] * x
    for j in range(CONV_W - 1):
        conv = conv + cw_ref[j:j + 1, :] * xbuf[j * nb:j * nb + rows, :]
    cv_out[...] = xbuf[rows:rows + tail, :].reshape(CONV_W - 1, nb, LRU_WIDTH)
    xbuf[0:tail, :] = x[rows - tail:, :]
    a, b = _lru_gates(conv + cb_ref[...], wa_ref, wx_ref, ba_ref[...], bx_ref[...], lam_ref[...])
    a_buf[...] = a
    b_buf[...] = b

    def body(t, h):
        r = pl.ds(pl.multiple_of(t * nb, nb), nb)
        h = a_buf[r, :] * h + b_buf[r, :]
        b_buf[r, :] = h
        return h

    h = lax.fori_loop(0, tc, body, h_scr[...], unroll=8)
    h_scr[...] = h
    h_out[...] = h
    y = b_buf[...] * jax.nn.gelu(_to_time_major(g_ref[...]))
    y_ref[...] = _from_time_major(y, nb)


def _lru_prompt(x_t, g_t, lru_w, layer, tc):
    nb, t, _ = x_t.shape
    blk = pl.BlockSpec((nb, tc, LRU_WIDTH), lambda i: (0, i, 0))
    rows = tc * nb
    return pl.pallas_call(
        functools.partial(_lru_kernel, nb=nb),
        grid=(t // tc,),
        in_specs=[blk, blk] + [_layer_spec(a, layer) for a in lru_w],
        out_specs=[blk,
                   pl.BlockSpec((nb, LRU_WIDTH), lambda i: (0, 0)),
                   pl.BlockSpec((CONV_W - 1, nb, LRU_WIDTH), lambda i: (0, 0, 0))],
        out_shape=[jax.ShapeDtypeStruct((nb, t, LRU_WIDTH), F32),
                   jax.ShapeDtypeStruct((nb, LRU_WIDTH), F32),
                   jax.ShapeDtypeStruct((CONV_W - 1, nb, LRU_WIDTH), F32)],
        scratch_shapes=[pltpu.VMEM(((CONV_W - 1) * nb + rows, LRU_WIDTH), F32),
                        pltpu.VMEM((rows, LRU_WIDTH), F32), pltpu.VMEM((rows, LRU_WIDTH), F32),
                        pltpu.VMEM((nb, LRU_WIDTH), F32)],
        compiler_params=_cparams("arbitrary"),
        name="lru",
    )(x_t, g_t, *lru_w)


def _merge_kernel(x_ref, sh_ref, sc_ref, gt_ref, ya_ref, yb_ref, yc_ref, yd_ref, wmg_ref, wb_ref, wo_ref,
                  g_ref, b_ref, o_ref, *, alpha, parts):
    ks = range(parts)
    xs, shs, scs, gts = (_row_parts(r, parts) for r in (x_ref, sh_ref, sc_ref, gt_ref))
    hs = [(_ln(xs[k]) * (1.0 + scs[k]) + shs[k]).astype(BF16) for k in ks]
    accs = [jnp.zeros(xs[k].shape, F32) for k in ks]
    for n, y_ref in enumerate((ya_ref, yb_ref, yc_ref, yd_ref)):
        ys = _row_parts(y_ref, parts)
        logits = [jnp.dot(hs[k], wmg_ref[:, n * D_MODEL:(n + 1) * D_MODEL], preferred_element_type=F32)
                  for k in ks]
        accs = [accs[k] + _sigmoid(logits[k]) * _bdot(ys[k], wb_ref[n]) for k in ks]
    outs = [_bdot(accs[k], wo_ref[...]) for k in ks]
    hm = x_ref.shape[0] // parts
    for k in ks:
        z = alpha * xs[k] + gts[k] * outs[k]
        o_ref[k * hm:(k + 1) * hm, :] = _ln(z) * g_ref[...] + b_ref[...]


def _merge(x, mod, ya, yb, yc, yd, w_mg, w_branch, w_out, ln_g, ln_b, alpha, tm):
    n = x.shape[0]
    l = mod.layer
    row = lambda w: pl.BlockSpec((tm, w), lambda i: (i, 0))
    return pl.pallas_call(
        functools.partial(_merge_kernel, alpha=alpha, parts=2 if tm >= 512 else 1),
        grid=(n // tm,),
        in_specs=[
            row(D_MODEL), mod.spec(1, 0, tm), mod.spec(1, 1, tm), mod.spec(1, 2, tm),
            row(BRANCH_W), row(BRANCH_W), row(BRANCH_W), row(BRANCH_W),
            _resident((None, D_MODEL, N_BRANCH * D_MODEL), lambda i: (l, 0, 0)),
            _resident((None, N_BRANCH, BRANCH_W, D_MODEL), lambda i: (l, 0, 0, 0)),
            _resident((None, D_MODEL, D_MODEL), lambda i: (l, 0, 0)),
            pl.BlockSpec((None, None, 1, D_MODEL), lambda i: (l, 1, 0, 0)),
            pl.BlockSpec((None, None, 1, D_MODEL), lambda i: (l, 1, 0, 0)),
        ],
        out_specs=row(D_MODEL),
        out_shape=jax.ShapeDtypeStruct((n, D_MODEL), F32),
        compiler_params=_cparams("parallel"),
        name="merge",
    )(x, mod.arr, mod.arr, mod.arr, ya, yb, yc, yd, w_mg, w_branch, w_out, ln_g, ln_b)


def _smix1_kernel(ret_ref, gqkv_ref, gab_ref, su_ref, lx_ref, cos_ref, sin_ref,
                  gcv_ref, sre_ref, sim_ref, lru_ref, lcv_ref,
                  gcw_ref, alog_ref, dtb_ref,
                  are_ref, aim_ref, bre_ref, bim_ref, cre_ref, cim_ref, d_ref, gw_ref, gb_ref,
                  lcw_ref, lcb_ref, wa_ref, wx_ref, ba_ref, bx_ref, lam_ref,
                  qkt_ref, gv_ref, gx_ref, yc_ref, lh_ref, gcv_out, sre_out, sim_out, lcv_out):
    ret = ret_ref[...]
    rq = _rotate(ret[:, 0:RET_QK], cos_ref[...], sin_ref[...])
    rk = _rotate(ret[:, RET_QK:2 * RET_QK], cos_ref[...], sin_ref[...]) * (RET_DK ** -0.5)

    x = gqkv_ref[...]
    conv = gcw_ref[3:4, :] * x
    for j in range(CONV_W - 1):
        conv = conv + gcw_ref[j:j + 1, :] * gcv_ref[j]
    gcv_out[0] = gcv_ref[1]
    gcv_out[1] = gcv_ref[2]
    gcv_out[2] = x
    cq = _silu(conv)
    gq = jnp.concatenate([_l2n(cq[:, h * GDN_DK:(h + 1) * GDN_DK]) * (GDN_DK ** -0.5)
                          for h in range(GDN_HEADS)], axis=1)
    gk = jnp.concatenate([_l2n(cq[:, GDN_K + h * GDN_DK:GDN_K + (h + 1) * GDN_DK])
                          for h in range(GDN_HEADS)], axis=1)
    gv_ref[...] = cq[:, 2 * GDN_K:]
    gl = _gate_lanes(gab_ref[...], alog_ref[...], dtb_ref[...])
    gx = jnp.concatenate([jnp.broadcast_to(gl[:, r:r + 1], (gl.shape[0], LANE))
                          for r in range(2 * GDN_HEADS)], axis=1)
    gx_ref[...] = jnp.where(_iota(gx.shape, 1) < GDN_HEADS * LANE, jnp.exp(gx), gx)
    qkt_ref[...] = jnp.concatenate([rq, rk, gq, gk], axis=1).T

    u = su_ref[...]
    h_re, h_im = [], []
    for j in range(SSM_BLK):
        sl = slice(j * SSM_BLK_STATE, (j + 1) * SSM_BLK_STATE)
        uj = u[:, j * LANE:(j + 1) * LANE]
        a_re, a_im = are_ref[:, sl], aim_ref[:, sl]
        p_re, p_im = sre_ref[:, sl], sim_ref[:, sl]
        h_re.append(a_re * p_re - a_im * p_im + _bdot(uj, bre_ref[j]))
        h_im.append(a_re * p_im + a_im * p_re + _bdot(uj, bim_ref[j]))
    h_re = jnp.concatenate(h_re, axis=1)
    h_im = jnp.concatenate(h_im, axis=1)
    sre_out[...] = h_re
    sim_out[...] = h_im
    yc_ref[...] = _s5_readout(u, h_re, h_im, cre_ref, cim_ref, d_ref[...], gw_ref, gb_ref[...])

    lx = lx_ref[...]
    conv = lcw_ref[3:4, :] * lx
    for j in range(CONV_W - 1):
        conv = conv + lcw_ref[j:j + 1, :] * lcv_ref[j]
    lcv_out[0] = lcv_ref[1]
    lcv_out[1] = lcv_ref[2]
    lcv_out[2] = lx
    a, b = _lru_gates(conv + lcb_ref[...], wa_ref, wx_ref, ba_ref[...], bx_ref[...], lam_ref[...])
    lh_ref[...] = a * lru_ref[...] + b


def _smix1(pieces, cos_row, sin_row, states, gdn_w, s5_w, lru_w, layer):
    rows = pieces["ret"].shape[0]
    acts = [pieces["ret"], pieces["gqkv"], pieces["gab"], pieces["su"], pieces["lx"], cos_row, sin_row]
    stacked = [*states, *gdn_w, *s5_w, *lru_w]
    ins = acts + stacked
    full = lambda a: pl.BlockSpec(a.shape, lambda i: (0,) * a.ndim)
    sds = lambda *s: jax.ShapeDtypeStruct(s, F32)
    out_shape = [sds(QKT_ROWS, rows), sds(rows, GDN_HEADS * GDN_DV), sds(rows, 8 * LANE),
                 sds(rows, SSM_WIDTH), sds(rows, LRU_WIDTH),
                 sds(CONV_W - 1, rows, GDN_QKV), sds(rows, SSM_STATE), sds(rows, SSM_STATE),
                 sds(CONV_W - 1, rows, LRU_WIDTH)]
    return pl.pallas_call(
        _smix1_kernel,
        grid=(1,),
        in_specs=[full(a) for a in acts] + [_layer_spec(a, layer) for a in stacked],
        out_specs=[pl.BlockSpec(s.shape, lambda i, nd=len(s.shape): (0,) * nd) for s in out_shape],
        out_shape=out_shape,
        compiler_params=_cparams("arbitrary"),
        name="sample_mix",
    )(*ins)


def _smix2_kernel(qkt_ref, rv_ref, gv_ref, gx_ref, rg_ref, gz_ref, lh_ref, lg_ref, nw_ref, sret_ref, sgdn_ref,
                  ya_ref, yb_ref, yd_ref, nret_ref, ngdn_ref, o_ret, o_gdn):
    i = pl.program_id(0)
    bt = rv_ref.shape[0]
    rows = qkt_ref.shape[1]
    gammas = _ret_gammas()
    q_hi, q_lo = _split(qkt_ref[...])
    qk_split = jnp.concatenate([q_hi, q_lo], axis=1)

    for j in range(bt):
        b = i * bt + j
        onehot = jnp.where(_iota((2 * rows, LANE), 0) % rows == b, 1.0, 0.0).astype(BF16)
        cols = jnp.dot(qk_split, onehot, preferred_element_type=F32)
        r = slice(j, j + 1)
        for h in range(RET_HEADS):
            q_c = cols[h * RET_DK:(h + 1) * RET_DK]
            k_c = cols[RET_QK + h * RET_DK:RET_QK + (h + 1) * RET_DK]
            v_r = rv_ref[r, h * RET_DV:(h + 1) * RET_DV]
            s_new = gammas[h] * sret_ref[j, h] + k_c * v_r
            nret_ref[j, h] = s_new
            o_ret[r, h * RET_DV:(h + 1) * RET_DV] = jnp.sum(q_c * s_new, axis=0, keepdims=True)
        base = 2 * RET_QK
        for h in range(GDN_HEADS):
            q_c = cols[base + h * GDN_DK:base + (h + 1) * GDN_DK]
            k_c = cols[base + GDN_K + h * GDN_DK:base + GDN_K + (h + 1) * GDN_DK]
            v_r = gv_ref[r, h * GDN_DV:(h + 1) * GDN_DV]
            e_g = gx_ref[r, h * LANE:(h + 1) * LANE]
            beta = gx_ref[r, (GDN_HEADS + h) * LANE:(GDN_HEADS + h + 1) * LANE]
            s = sgdn_ref[j, h]
            v_new = beta * (v_r - e_g * jnp.sum(k_c * s, axis=0, keepdims=True))
            s_new = e_g * s + k_c * v_new
            ngdn_ref[j, h] = s_new
            o_gdn[r, h * GDN_DV:(h + 1) * GDN_DV] = jnp.sum(q_c * s_new, axis=0, keepdims=True)
    rg, gz = rg_ref[...], gz_ref[...]
    o_r, o_g = o_ret[...], o_gdn[...]
    ya_ref[...] = jnp.concatenate(
        [_ln(o_r[:, h * RET_DV:(h + 1) * RET_DV]) for h in range(RET_HEADS)], axis=1) * _silu(rg)
    yb_ref[...] = jnp.concatenate(
        [_rms(o_g[:, h * GDN_DV:(h + 1) * GDN_DV]) * nw_ref[...] for h in range(GDN_HEADS)], axis=1) * _silu(gz)
    yd_ref[...] = lh_ref[...] * jax.nn.gelu(lg_ref[...])


def _smix2(qkt, rv, gv, gx, rg, gz, lh, lg, norm_w, s_ret, s_gdn, layer, bt):
    rows = rv.shape[0]
    row = lambda w: pl.BlockSpec((bt, w), lambda i: (i, 0))
    sds = lambda *s: jax.ShapeDtypeStruct(s, F32)
    ret_blk = pl.BlockSpec((None, bt, RET_HEADS, RET_DK, RET_DV), lambda i: (layer, i, 0, 0, 0))
    gdn_blk = pl.BlockSpec((None, bt, GDN_HEADS, GDN_DK, GDN_DV), lambda i: (layer, i, 0, 0, 0))
    return pl.pallas_call(
        _smix2_kernel,
        grid=(rows // bt,),
        in_specs=[pl.BlockSpec(qkt.shape, lambda i: (0, 0)),
                  pl.BlockSpec((bt, RET_V), lambda i: (i, 1)),
                  row(GDN_HEADS * GDN_DV), row(8 * LANE),
                  pl.BlockSpec((bt, RET_V), lambda i: (i, 2)),
                  row(GDN_HEADS * GDN_DV), row(LRU_WIDTH), row(LRU_WIDTH),
                  _layer_spec(norm_w, layer), ret_blk, gdn_blk],
        out_specs=[row(RET_V), row(GDN_HEADS * GDN_DV), row(LRU_WIDTH),
                   pl.BlockSpec((bt, RET_HEADS, RET_DK, RET_DV), lambda i: (i, 0, 0, 0)),
                   pl.BlockSpec((bt, GDN_HEADS, GDN_DK, GDN_DV), lambda i: (i, 0, 0, 0))],
        out_shape=[sds(rows, RET_V), sds(rows, GDN_HEADS * GDN_DV), sds(rows, LRU_WIDTH),
                   sds(rows, RET_HEADS, RET_DK, RET_DV), sds(rows, GDN_HEADS, GDN_DK, GDN_DV)],
        scratch_shapes=[pltpu.VMEM((bt, RET_V), F32), pltpu.VMEM((bt, GDN_HEADS * GDN_DV), F32)],
        compiler_params=_cparams("parallel"),
        name="sample_state",
    )(qkt, rv, gv, gx, rg, gz, lh, lg, norm_w, s_ret, s_gdn)


def _block_diag(w):
    nb, bs, _ = w.shape
    return jnp.einsum("nij,nm->nimj", w, jnp.eye(nb, dtype=w.dtype)).reshape(nb * bs, nb * bs)


def kernel(x_prompt, x_sample, c_prompt, c_sample, state_ret, state_gdn, state_gdn_conv, state_ssm_re, state_ssm_im, state_lru, state_lru_conv, w_ada, b_ada, ln_g, ln_b, w_ffn_up, w_ffn_down, w_in, gdn_conv_w, gdn_a_log, gdn_dt_bias, gdn_norm_w, ssm_lam_re, ssm_lam_im, ssm_log_step, ssm_b_re, ssm_b_im, ssm_c_re, ssm_c_im, ssm_d, ssm_glu_w, ssm_glu_b, lru_conv_w, lru_conv_b, lru_wa, lru_ba, lru_wx, lru_bx, lru_lam, w_branch, w_out):
    nb, t, _ = x_prompt.shape
    ns = x_sample.shape[0]
    depth = w_ada.shape[0]
    assert t % RET_CHUNK == 0 and x_sample.shape[1] == 1
    alpha = (2 * depth) ** 0.25
    tm = 256 if t % 256 == 0 else RET_CHUNK
    tm_ffn = 512 if t % 512 == 0 else tm
    tc = 64

    wup = w_ffn_up.astype(BF16)
    wdn = w_ffn_down.astype(BF16)
    wbr = w_branch.astype(BF16)
    wout = w_out.astype(BF16)
    in_sizes = (RET_QK, RET_QK, RET_V, RET_V, GDN_QKV, GDN_HEADS, GDN_HEADS, GDN_HEADS * GDN_DV,
                SSM_WIDTH, LRU_WIDTH, LRU_WIDTH, N_BRANCH * D_MODEL)
    offs = np.concatenate([[0], np.cumsum(in_sizes)])
    seg = lambda a, b: w_in[:, :, offs[a]:offs[b]]
    w_main = jnp.concatenate([seg(0, 5), seg(7, 11)], axis=-1).astype(BF16)
    w_mg = seg(11, 12).astype(BF16)
    w_small = jnp.pad(seg(5, 7), ((0, 0), (0, 0), (0, LANE - 2 * GDN_HEADS))).astype(BF16)
    ln_g4 = ln_g.reshape(depth, N_SUB, 1, D_MODEL)
    ln_b4 = ln_b.reshape(depth, N_SUB, 1, D_MODEL)

    mod_all = _ada(jnp.concatenate([c_prompt, c_sample], axis=0), w_ada, b_ada)
    mod_p = mod_all[:, :nb].reshape(depth, nb, 3 * N_SUB, 1, D_MODEL).transpose(0, 2, 1, 3, 4)
    mod_s = mod_all[:, nb:]

    row3 = lambda a: a.reshape(depth, 1, -1)
    lane_rows = lambda v: jnp.pad(v, ((0, 0), (0, LANE - v.shape[1]))).reshape(depth, 1, LANE)
    alog_rows, dtb_rows = lane_rows(gdn_a_log), lane_rows(gdn_dt_bias)
    norm_w = row3(gdn_norm_w)
    s5_w = (*jax.vmap(_s5_params)(ssm_lam_re, ssm_lam_im, ssm_log_step, ssm_b_re, ssm_b_im, ssm_c_re, ssm_c_im),
            row3(ssm_d), ssm_glu_w.astype(BF16), row3(ssm_glu_b))
    lru_w = (lru_conv_w, row3(lru_conv_b),
             jax.vmap(_block_diag)(lru_wa).astype(BF16), jax.vmap(_block_diag)(lru_wx).astype(BF16),
             row3(lru_ba), row3(lru_bx), row3(lru_lam))
    s_states = (state_gdn_conv.transpose(0, 2, 1, 3),
                state_ssm_re.reshape(depth, ns, SSM_STATE), state_ssm_im.reshape(depth, ns, SSM_STATE),
                state_lru, state_lru_conv.transpose(0, 2, 1, 3))

    cos_s, sin_s = _rope_tables(jnp.full((1,), PAST_LEN))
    xp = x_prompt.reshape(nb * t, D_MODEL)
    xs = x_sample.reshape(ns, D_MODEL)
    new_p, new_s = [], []
    for l in range(depth):
        mp = _Mod(mod_p, l, False, t)
        ms = _Mod(mod_s, l, True)

        xp = _ffn(xp, mp, 0, 0, wup, wdn, ln_g4, ln_b4, alpha, tm_ffn)
        pc = _inproj(xp, mp, w_main, w_small, tm)
        ya, ret_p = _retention_prompt(pc["ret"], nb, t)
        yb, gdn_p, gcv_p = _gdn_prompt(pc["gqkv"], pc["gab"], pc["gz"], gdn_conv_w,
                                       alog_rows, dtb_rows, norm_w, l, nb, t)
        to_t = lambda a: a.reshape(nb, t, -1)
        from_t = lambda a: a.reshape(nb * t, -1)
        yc_t, sre_p, sim_p = _s5_prompt(to_t(pc["su"]), s5_w, l, tc)
        yd_t, lru_p, lcv_p = _lru_prompt(to_t(pc["lx"]), to_t(pc["lg"]), lru_w, l, tc)
        xp = _merge(xp, mp, ya, yb, from_t(yc_t), from_t(yd_t), w_mg, wbr, wout, ln_g4, ln_b4, alpha, tm_ffn)
        xp = _ffn(xp, mp, 2, 1, wup, wdn, ln_g4, ln_b4, alpha, tm_ffn)
        new_p.append((ret_p, gdn_p, gcv_p,
                      sre_p.reshape(nb, SSM_GROUPS, SSM_P), sim_p.reshape(nb, SSM_GROUPS, SSM_P),
                      lru_p, lcv_p.transpose(1, 0, 2)))

        xs = _ffn(xs, ms, 0, 0, wup, wdn, ln_g4, ln_b4, alpha, ns)
        sc = _inproj(xs, ms, w_main, w_small, ns)
        (qkt, gv, gx, yc, lh, gcv_s, sre_s, sim_s, lcv_s) = _smix1(
            sc, cos_s, sin_s, s_states, (gdn_conv_w, alog_rows, dtb_rows), s5_w, lru_w, l)
        ya, yb, yd, ret_s, gdn_s = _smix2(qkt, sc["ret"], gv, gx, sc["ret"], sc["gz"], lh, sc["lg"],
                                          norm_w, state_ret, state_gdn, l, SUBLANE)
        xs = _merge(xs, ms, ya, yb, yc, yd, w_mg, wbr, wout, ln_g4, ln_b4, alpha, ns)
        xs = _ffn(xs, ms, 2, 1, wup, wdn, ln_g4, ln_b4, alpha, ns)
        new_s.append((ret_s, gdn_s, gcv_s.transpose(1, 0, 2),
                      sre_s.reshape(ns, SSM_GROUPS, SSM_P), sim_s.reshape(ns, SSM_GROUPS, SSM_P),
                      lh, lcv_s.transpose(1, 0, 2)))

    ret_p, gdn_p, gcv_p, sre_p, sim_p, lru_p, lcv_p = [jnp.stack(z) for z in zip(*new_p)]
    ret_s, gdn_s, gcv_s, sre_s, sim_s, lru_s, lcv_s = [jnp.stack(z) for z in zip(*new_s)]
    return (xp.reshape(nb, t, D_MODEL), xs.reshape(ns, 1, D_MODEL),
            ret_p, ret_s, gdn_p, gdn_s, gcv_p, gcv_s, sre_p, sre_s, sim_p, sim_s,
            lru_p, lru_s, lcv_p, lcv_s)
```

```python
import functools
import math

import numpy as np
import jax
import jax.numpy as jnp
from jax import lax
from jax.experimental import pallas as pl
from jax.experimental.pallas import tpu as pltpu

F32 = jnp.float32
BF16 = jnp.bfloat16
HI = lax.Precision.HIGHEST

D_MODEL = 1024
RET_HEADS, RET_DK, RET_DV, RET_CHUNK = 4, 64, 128, 128
RET_STEP_CHUNKS = 4
ROPE_BASE = 10000.0
GDN_HEADS, GDN_DK, GDN_DV, GDN_CHUNK = 4, 128, 128, 64
GDN_STEP_CHUNKS = 4
GDN_QKV = 2 * GDN_HEADS * GDN_DK + GDN_HEADS * GDN_DV
CONV_W = 4
SSM_GROUP, SSM_GROUPS, SSM_P = 16, 32, 64
SSM_WIDTH = SSM_GROUP * SSM_GROUPS
SSM_STATE = SSM_GROUPS * SSM_P
LRU_WIDTH, LRU_BLOCKS = 512, 8
LRU_C = 8.0
N_BRANCH, BRANCH_W = 4, 512
D_FF = 2816
N_SUB = 3
LN_EPS = 1e-5
PAST_LEN = 16384

RET_QK = RET_HEADS * RET_DK
RET_V = RET_HEADS * RET_DV
RET_W = 2 * RET_QK + 2 * RET_V
GDN_K = GDN_HEADS * GDN_DK
QKT_ROWS = 2 * RET_QK + 2 * GDN_K
LANE = 128
SUBLANE = 8
FF_CHUNKS = ((0, 768), (768, 1024), (1792, 1024))
VMEM_LIMIT = 56 * 1024 * 1024


def _cparams(*sem):
    return pltpu.CompilerParams(dimension_semantics=sem, vmem_limit_bytes=VMEM_LIMIT)


def _layer_spec(a, layer):
    nd = a.ndim
    return pl.BlockSpec((None,) + a.shape[1:], lambda *_: (layer,) + (0,) * (nd - 1))


def _resident(shape, index_map):
    return pl.BlockSpec(shape, index_map, pipeline_mode=pl.Buffered(1))


def _ln(x):
    mu = jnp.mean(x, axis=-1, keepdims=True)
    xc = x - mu
    return xc * lax.rsqrt(jnp.mean(xc * xc, axis=-1, keepdims=True) + LN_EPS)


def _sigmoid(x):
    return 0.5 * jnp.tanh(0.5 * x) + 0.5


def _silu(x):
    return x * _sigmoid(x)


def _softplus(x):
    return jnp.maximum(x, 0.0) + jnp.log1p(jnp.exp(-jnp.abs(x)))


def _bdot(a, w):
    return jnp.dot(a.astype(BF16), w.astype(BF16), preferred_element_type=F32)


def _split(x):
    hi = x.astype(BF16)
    return hi, (x - hi.astype(F32)).astype(BF16)


def _hdot(a, b):
    return jnp.dot(a, b, preferred_element_type=F32, precision=HI)


def _dot_nt(a, b, precision=None):
    return lax.dot_general(a, b, (((1,), (1,)), ((), ())),
                           preferred_element_type=F32, precision=precision)


def _iota(shape, dim):
    return lax.broadcasted_iota(jnp.int32, shape, dim)


def _to_time_major(x):
    nb, tc, width = x.shape
    return jnp.swapaxes(x, 0, 1).reshape(tc * nb, width)


def _from_time_major(y, nb):
    rows, width = y.shape
    return jnp.swapaxes(y.reshape(rows // nb, nb, width), 0, 1)


def _ada_kernel(c_ref, w_ref, b_ref, o_ref):
    o_ref[...] = _bdot(_silu(c_ref[...]), w_ref[...]) + b_ref[...]


def _ada(c_all, w_ada, b_ada):
    depth, _, n_out = w_ada.shape
    rows = c_all.shape[0]
    tn = 1152
    return pl.pallas_call(
        _ada_kernel,
        grid=(depth, n_out // tn),
        in_specs=[
            pl.BlockSpec((rows, D_MODEL), lambda l, j: (0, 0)),
            pl.BlockSpec((None, D_MODEL, tn), lambda l, j: (l, 0, j)),
            pl.BlockSpec((None, 1, tn), lambda l, j: (l, 0, j)),
        ],
        out_specs=pl.BlockSpec((None, rows, tn), lambda l, j: (l, 0, j)),
        out_shape=jax.ShapeDtypeStruct((depth, rows, n_out), F32),
        compiler_params=_cparams("parallel", "parallel"),
        name="ada",
    )(c_all, w_ada, b_ada.reshape(depth, 1, n_out))


class _Mod:
    def __init__(self, arr, layer, per_row, rows_per_batch=None):
        self.arr, self.layer, self.per_row, self.rpb = arr, layer, per_row, rows_per_batch

    def spec(self, sub, which, tm):
        k, l = 3 * sub + which, self.layer
        if self.per_row:
            return pl.BlockSpec((None, tm, D_MODEL), lambda i: (l, i, k))
        tiles = self.rpb // tm
        return pl.BlockSpec((None, None, None, 1, D_MODEL), lambda i: (l, k, i // tiles, 0, 0))


def _row_parts(ref, parts):
    if ref.shape[0] == 1:
        return [ref[...]] * parts
    hm = ref.shape[0] // parts
    return [ref[k * hm:(k + 1) * hm, :] for k in range(parts)]


def _ffn_kernel(x_ref, sh_ref, sc_ref, gt_ref, wup_ref, wdn_ref, g_ref, b_ref, o_ref, *, alpha, parts):
    ks = range(parts)
    xs, shs, scs, gts = (_row_parts(r, parts) for r in (x_ref, sh_ref, sc_ref, gt_ref))
    hs = [(_ln(xs[k]) * (1.0 + scs[k]) + shs[k]).astype(BF16) for k in ks]
    accs = [jnp.zeros(xs[k].shape, F32) for k in ks]
    for lo, width in FF_CHUNKS:
        a = [jnp.dot(hs[k], wup_ref[:, lo:lo + width], preferred_element_type=F32) for k in ks]
        b = [jnp.dot(hs[k], wup_ref[:, D_FF + lo:D_FF + lo + width], preferred_element_type=F32) for k in ks]
        act = [(_silu(a[k]) * b[k]).astype(BF16) for k in ks]
        accs = [accs[k] + jnp.dot(act[k], wdn_ref[lo:lo + width, :], preferred_element_type=F32) for k in ks]
    hm = x_ref.shape[0] // parts
    for k in ks:
        z = alpha * xs[k] + 0.5 * gts[k] * accs[k]
        o_ref[k * hm:(k + 1) * hm, :] = _ln(z) * g_ref[...] + b_ref[...]


def _ffn(x, mod, sub, which, wup, wdn, ln_g, ln_b, alpha, tm):
    n = x.shape[0]
    l = mod.layer
    row = pl.BlockSpec((tm, D_MODEL), lambda i: (i, 0))
    return pl.pallas_call(
        functools.partial(_ffn_kernel, alpha=alpha, parts=2 if tm >= 512 else 1),
        grid=(n // tm,),
        in_specs=[
            row, mod.spec(sub, 0, tm), mod.spec(sub, 1, tm), mod.spec(sub, 2, tm),
            _resident((None, None, D_MODEL, 2 * D_FF), lambda i: (l, which, 0, 0)),
            _resident((None, None, D_FF, D_MODEL), lambda i: (l, which, 0, 0)),
            pl.BlockSpec((None, None, 1, D_MODEL), lambda i: (l, sub, 0, 0)),
            pl.BlockSpec((None, None, 1, D_MODEL), lambda i: (l, sub, 0, 0)),
        ],
        out_specs=row,
        out_shape=jax.ShapeDtypeStruct((n, D_MODEL), F32),
        compiler_params=_cparams("parallel"),
        name="ffn",
    )(x, mod.arr, mod.arr, mod.arr, wup, wdn, ln_g, ln_b)


_IN_PIECES = (("ret", RET_W), ("gqkv", GDN_QKV), ("gab", LANE), ("gz", 512), ("su", SSM_WIDTH),
              ("lx", LRU_WIDTH), ("lg", LRU_WIDTH))
IN_MAIN = sum(w for _, w in _IN_PIECES)
MG_WIDTH = N_BRANCH * D_MODEL
MG_BLOCK = -(-IN_MAIN // MG_WIDTH)
IN_PACKED = (MG_BLOCK + 1) * MG_WIDTH


def _pack_w_in(w_in):
    gate_lo = RET_W + GDN_QKV
    gate_hi = gate_lo + 2 * GDN_HEADS
    mg_lo = w_in.shape[-1] - MG_WIDTH
    zeros = lambda w: jnp.zeros(w_in.shape[:-1] + (w,), w_in.dtype)
    return jnp.concatenate([w_in[..., :gate_hi], zeros(LANE - 2 * GDN_HEADS), w_in[..., gate_hi:mg_lo],
                            zeros(MG_BLOCK * MG_WIDTH - IN_MAIN), w_in[..., mg_lo:]], axis=-1).astype(BF16)


def _l2n(x):
    return x * lax.rsqrt(jnp.sum(x * x, axis=-1, keepdims=True) + 1e-6)


def _gdn_activate(conv):
    cq = _silu(conv)
    q = [_l2n(cq[:, h * GDN_DK:(h + 1) * GDN_DK]) * (GDN_DK ** -0.5) for h in range(GDN_HEADS)]
    k = [_l2n(cq[:, GDN_K + h * GDN_DK:GDN_K + (h + 1) * GDN_DK]) for h in range(GDN_HEADS)]
    return jnp.concatenate(q + k + [cq[:, 2 * GDN_K:]], axis=1)


def _inproj_kernel(x_ref, sh_ref, sc_ref, w_ref, *refs, seq_tiles):
    h = (_ln(x_ref[...]) * (1.0 + sc_ref[...]) + sh_ref[...]).astype(BF16)
    n_out = len(_IN_PIECES)
    if seq_tiles is None:
        o_refs = refs
    else:
        cw_ref, o_refs, cv_ref, xbuf = refs[0], refs[1:1 + n_out], refs[1 + n_out], refs[2 + n_out]

        @pl.when(pl.program_id(0) % seq_tiles == 0)
        def _():
            xbuf[0:SUBLANE, :] = jnp.zeros((SUBLANE, GDN_QKV), F32)

    offsets = np.cumsum([0] + [w for _, w in _IN_PIECES])
    order = sorted(range(n_out), key=lambda p: _IN_PIECES[p][0] != "gqkv")
    for p in order:
        (name, width), o_ref, lo = _IN_PIECES[p], o_refs[p], int(offsets[p])
        y = jnp.dot(h, w_ref[:, lo:lo + width], preferred_element_type=F32)
        if name == "gqkv" and seq_tiles is not None:
            tm = y.shape[0]
            xbuf[SUBLANE:SUBLANE + tm, :] = y
            conv = cw_ref[CONV_W - 1:CONV_W, :] * y
            for j in range(CONV_W - 1):
                r0 = SUBLANE - (CONV_W - 1) + j
                conv = conv + cw_ref[j:j + 1, :] * xbuf[r0:r0 + tm, :]
            cv_ref[0] = xbuf[SUBLANE + tm - (CONV_W - 1):SUBLANE + tm, :]
            xbuf[0:SUBLANE, :] = y[tm - SUBLANE:, :]
            y = _gdn_activate(conv)
        o_ref[...] = y


def _inproj(x, mod, w_packed, tm, conv=None):
    n = x.shape[0]
    l = mod.layer
    widths = [w for _, w in _IN_PIECES]
    ins = [x, mod.arr, mod.arr, w_packed]
    in_specs = [pl.BlockSpec((tm, D_MODEL), lambda i: (i, 0)),
                mod.spec(1, 0, tm), mod.spec(1, 1, tm),
                _resident((None, D_MODEL, IN_MAIN), lambda i: (l, 0, 0))]
    out_specs = [pl.BlockSpec((tm, w), lambda i: (i, 0)) for w in widths]
    out_shape = [jax.ShapeDtypeStruct((n, w), F32) for w in widths]
    scratch, seq_tiles = [], None
    if conv is not None:
        conv_w, nb, t = conv
        seq_tiles = t // tm
        ins.append(conv_w)
        in_specs.append(_layer_spec(conv_w, l))
        out_specs.append(pl.BlockSpec((1, CONV_W - 1, GDN_QKV), lambda i: (i // seq_tiles, 0, 0)))
        out_shape.append(jax.ShapeDtypeStruct((nb, CONV_W - 1, GDN_QKV), F32))
        scratch.append(pltpu.VMEM((SUBLANE + tm, GDN_QKV), F32))
    outs = pl.pallas_call(
        functools.partial(_inproj_kernel, seq_tiles=seq_tiles),
        grid=(n // tm,),
        in_specs=in_specs,
        out_specs=out_specs,
        out_shape=out_shape,
        scratch_shapes=scratch,
        compiler_params=_cparams("arbitrary"),
        name="inproj",
    )(*ins)
    return dict(zip([nm for nm, _ in _IN_PIECES] + ["gdn_conv"], outs))


def _rope_tables(pos):
    half = RET_DK // 2
    inv = ROPE_BASE ** (-jnp.arange(half, dtype=F32) / half)
    ang = pos.astype(F32)[:, None] * inv[None, :]
    cos, sin = jnp.cos(ang), jnp.sin(ang)
    cos_t = jnp.tile(jnp.concatenate([cos, cos], axis=-1), (1, RET_HEADS))
    sin_t = jnp.tile(jnp.concatenate([-sin, sin], axis=-1), (1, RET_HEADS))
    return cos_t, sin_t


def _rotate(z, cos, sin):
    half = RET_DK // 2
    first = (_iota(z.shape, 1) % RET_DK) < half
    swapped = jnp.where(first, pltpu.roll(z, RET_QK - half, 1), pltpu.roll(z, half, 1))
    return z * cos + swapped * sin


def _ret_gammas():
    return [1.0 - 2.0 ** (-5.0 - h) for h in range(RET_HEADS)]


def _ret_consts(chunk, n_sub):
    log_g = np.log(np.array(_ret_gammas(), np.float64))
    idx = np.arange(chunk, dtype=np.float64)
    rel = idx[:, None] - idx[None, :]
    dmask = np.where(rel[None] >= 0, np.exp(log_g[:, None, None] * np.maximum(rel, 0.0)[None]), 0.0)
    qdec = np.repeat(np.exp(log_g[None, :] * (idx[:, None] + 1.0)), RET_DK, axis=1)
    kdec = np.repeat(np.exp(log_g[None, :] * (chunk - 1.0 - idx[:, None])), RET_DK, axis=1)
    qdec, kdec = np.tile(qdec, (n_sub, 1)), np.tile(kdec, (n_sub, 1))
    head_r = np.arange(RET_QK) // RET_DK
    head_c = np.arange(RET_V) // RET_DV
    bd = (head_r[:, None] == head_c[None, :]).astype(np.float64)
    cd = bd * np.exp(log_g * chunk)[head_r][:, None]
    f = lambda a: jnp.asarray(a, F32)
    return f(dmask), f(qdec), f(kdec), f(cd), f(bd)


def _ret_kernel(x_ref, cos_ref, sin_ref, dmask_ref, qdec_ref, kdec_ref, cd_ref, bd_ref,
                y_ref, st_ref, s_scr):
    c = pl.program_id(1)
    ch = RET_CHUNK
    n_sub = x_ref.shape[0] // ch

    @pl.when(c == 0)
    def _():
        s_scr[...] = jnp.zeros(s_scr.shape, F32)

    x = x_ref[...]
    q = _rotate(x[:, 0:RET_QK], cos_ref[...], sin_ref[...])
    k = _rotate(x[:, RET_QK:2 * RET_QK], cos_ref[...], sin_ref[...]) * (RET_DK ** -0.5)
    v = x[:, 2 * RET_QK:2 * RET_QK + RET_V].astype(BF16)
    gate = x[:, 2 * RET_QK + RET_V:]
    q_dec = (q * qdec_ref[...]).astype(BF16)
    k_dec = k * kdec_ref[...]
    kb = k.astype(BF16)
    head = _iota((ch, RET_QK), 1) // RET_DK
    intra, upd = {}, {}
    for n in range(n_sub):
        sl = slice(n * ch, (n + 1) * ch)
        for h in range(RET_HEADS):
            qh = jnp.where(head == h, q[sl], 0.0).astype(BF16)
            inner = _dot_nt(qh, kb[sl]) * dmask_ref[h]
            intra[n, h] = _bdot(inner, v[sl, h * RET_DV:(h + 1) * RET_DV])
        upd[n] = jnp.dot(k_dec[sl].T.astype(BF16), v[sl], preferred_element_type=F32) * bd_ref[...]
    s = s_scr[...]
    cross = {}
    for n in range(n_sub):
        cross[n] = jnp.dot(q_dec[n * ch:(n + 1) * ch], s.astype(BF16), preferred_element_type=F32)
        s = s * cd_ref[...] + upd[n]
    s_scr[...] = s
    for n in range(n_sub):
        sl = slice(n * ch, (n + 1) * ch)
        outs = [_ln(intra[n, h] + cross[n][:, h * RET_DV:(h + 1) * RET_DV]) for h in range(RET_HEADS)]
        y_ref[sl, :] = jnp.concatenate(outs, axis=1) * _silu(gate[sl])

    @pl.when(c == pl.num_programs(1) - 1)
    def _():
        for h in range(RET_HEADS):
            st_ref[0, h] = s[h * RET_DK:(h + 1) * RET_DK, h * RET_DV:(h + 1) * RET_DV]


def _retention_prompt(ret, nb, t):
    n_sub = RET_STEP_CHUNKS if t % (RET_STEP_CHUNKS * RET_CHUNK) == 0 else 1
    chunk = n_sub * RET_CHUNK
    nc = t // chunk
    cos_t, sin_t = _rope_tables(jnp.arange(t))
    consts = _ret_consts(RET_CHUNK, n_sub)
    full = lambda a: pl.BlockSpec(a.shape, lambda b, c: (0,) * a.ndim)
    return pl.pallas_call(
        _ret_kernel,
        grid=(nb, nc),
        in_specs=[
            pl.BlockSpec((chunk, RET_W), lambda b, c: (b * nc + c, 0)),
            pl.BlockSpec((chunk, RET_QK), lambda b, c: (c, 0)),
            pl.BlockSpec((chunk, RET_QK), lambda b, c: (c, 0)),
        ] + [full(a) for a in consts],
        out_specs=[
            pl.BlockSpec((chunk, RET_V), lambda b, c: (b * nc + c, 0)),
            pl.BlockSpec((1, RET_HEADS, RET_DK, RET_DV), lambda b, c: (b, 0, 0, 0)),
        ],
        out_shape=[
            jax.ShapeDtypeStruct((nb * t, RET_V), F32),
            jax.ShapeDtypeStruct((nb, RET_HEADS, RET_DK, RET_DV), F32),
        ],
        scratch_shapes=[pltpu.VMEM((RET_QK, RET_V), F32)],
        compiler_params=_cparams("parallel", "arbitrary"),
        name="retention",
    )(ret, cos_t, sin_t, *consts)


def _rms(x):
    return x * lax.rsqrt(jnp.mean(x * x, axis=-1, keepdims=True) + 1e-6)


def _gate_lanes(gab, alog_row, dtb_row):
    g = -jnp.exp(alog_row) * _softplus(gab + dtb_row)
    return jnp.where(_iota(gab.shape, 1) < GDN_HEADS, g, _sigmoid(gab))


def _gdn_kernel(x_ref, gab_ref, gz_ref, alog_ref, dtb_ref, nw_ref, y_ref, st_ref, s_scr):
    c = pl.program_id(1)
    ch = GDN_CHUNK

    @pl.when(c == 0)
    def _():
        s_scr[...] = jnp.zeros(s_scr.shape, F32)

    rows = x_ref.shape[0]
    cq = x_ref[...]

    gl = _gate_lanes(gab_ref[...], alog_ref[...], dtb_ref[...])
    col = _iota((ch, 2 * ch), 1) % ch
    tril = _iota((ch, 2 * ch), 0) >= col
    strict = _iota((ch, 2 * ch), 0) > col
    is_g = _iota(gl.shape, 1) < GDN_HEADS
    r_i, c_i = _iota((rows, rows), 0), _iota((rows, rows), 1)
    chunk_tril = ((r_i >= c_i) & (r_i // ch == c_i // ch)).astype(F32)
    csum = _hdot(chunk_tril, jnp.where(is_g, gl, 0.0))
    gsel = jnp.where(is_g, csum, gl)
    n_sub = rows // ch
    heads = range(GDN_HEADS)
    probs = [(h, n) for h in heads for n in range(n_sub)]

    q_all, k_all, g_all, qe_all, rhs_all, kbeta_all = {}, {}, {}, {}, {}, {}
    for h in heads:
        q_all[h] = cq[:, h * GDN_DK:(h + 1) * GDN_DK]
        k_all[h] = cq[:, GDN_K + h * GDN_DK:GDN_K + (h + 1) * GDN_DK]
        v_h = cq[:, 2 * GDN_K + h * GDN_DV:2 * GDN_K + (h + 1) * GDN_DV]
        g_all[h] = jnp.broadcast_to(gsel[:, h:h + 1], (rows, LANE))
        b_h = jnp.broadcast_to(gsel[:, GDN_HEADS + h:GDN_HEADS + h + 1], (rows, LANE))
        e_h = jnp.exp(g_all[h])
        kbeta_all[h] = k_all[h] * b_h
        rhs_all[h] = jnp.concatenate([v_h * b_h, kbeta_all[h] * e_h], axis=1)
        qe_all[h] = q_all[h] * e_h

    decay, pp, uw, attn, kd_t, e_last = {}, {}, {}, {}, {}, {}
    for h, n in probs:
        sl = slice(n * ch, (n + 1) * ch)
        g_b = g_all[h][sl]
        g_cols = jnp.concatenate([g_b, g_b], axis=0).T[:ch, :]
        diff = g_b - g_cols
        decay[h, n] = jnp.where(tril, jnp.exp(jnp.where(tril, diff, 0.0)), 0.0)
        g_last = g_b[ch - 1:ch, :]
        e_last[h, n] = jnp.exp(g_last)
        kd_t[h, n] = (k_all[h][sl] * jnp.exp(g_last - g_b)).T.astype(BF16)
    for h, n in probs:
        sl = slice(n * ch, (n + 1) * ch)
        kb_hi, kb_lo = _split(kbeta_all[h][sl])
        k_hi, k_lo = _split(jnp.concatenate([k_all[h][sl], k_all[h][sl]], axis=0))
        kk = _dot_nt(jnp.concatenate([kb_hi, kb_hi, kb_lo, kb_lo], axis=1),
                     jnp.concatenate([k_hi, k_lo, k_hi, k_lo], axis=1))
        pp[h, n] = -jnp.where(strict, kk * decay[h, n], 0.0)
        attn[h, n] = _dot_nt(q_all[h][sl].astype(BF16), k_all[h][sl].astype(BF16)) * decay[h, n][:, :ch]
    for stage in range(int(math.log2(ch))):
        for h, n in probs:
            p_hi, p_lo = _split(pp[h, n])
            lhs = jnp.concatenate([p_hi, p_lo], axis=1)
            y = rhs_all[h][n * ch:(n + 1) * ch] if stage == 0 else uw[h, n]
            y_hi, y_lo = _split(y)
            uw[h, n] = y + jnp.dot(lhs, jnp.concatenate([y_hi, y_lo, y_hi, y_lo], axis=0),
                                   preferred_element_type=F32)
            if stage + 1 < int(math.log2(ch)):
                pp[h, n] = jnp.dot(lhs, jnp.concatenate([p_hi, p_lo, p_hi, p_lo], axis=0),
                                   preferred_element_type=F32)
    s = {h: s_scr[h] for h in heads}
    o_parts = {h: [] for h in heads}
    for n in range(n_sub):
        for h in heads:
            sl = slice(n * ch, (n + 1) * ch)
            u, w = uw[h, n][:, :GDN_DV], uw[h, n][:, GDN_DV:]
            ws_qs = _bdot(jnp.concatenate([w, qe_all[h][sl]], axis=0), s[h])
            v_new = u - ws_qs[:ch]
            o_parts[h].append(ws_qs[ch:] + _bdot(attn[h, n], v_new))
            s[h] = s[h] * e_last[h, n] + jnp.dot(kd_t[h, n], v_new.astype(BF16), preferred_element_type=F32)
    norm_w = nw_ref[...]
    gz = gz_ref[...]
    outs = []
    for h in heads:
        s_scr[h] = s[h]
        o = jnp.concatenate(o_parts[h], axis=0)
        outs.append(_rms(o) * norm_w * _silu(gz[:, h * GDN_DV:(h + 1) * GDN_DV]))
    y_ref[...] = jnp.concatenate(outs, axis=1)

    @pl.when(c == pl.num_programs(1) - 1)
    def _():
        st_ref[0] = s_scr[...]


def _gdn_prompt(gqkv, gab, gz, alog_row, dtb_row, norm_w, layer, nb, t):
    ch = GDN_STEP_CHUNKS * GDN_CHUNK if t % (GDN_STEP_CHUNKS * GDN_CHUNK) == 0 else GDN_CHUNK
    nc = t // ch
    row = lambda w: pl.BlockSpec((ch, w), lambda b, c: (b * nc + c, 0))
    return pl.pallas_call(
        _gdn_kernel,
        grid=(nb, nc),
        in_specs=[row(GDN_QKV), row(LANE), row(GDN_HEADS * GDN_DV)]
        + [_layer_spec(a, layer) for a in (alog_row, dtb_row, norm_w)],
        out_specs=[
            row(GDN_HEADS * GDN_DV),
            pl.BlockSpec((1, GDN_HEADS, GDN_DK, GDN_DV), lambda b, c: (b, 0, 0, 0)),
        ],
        out_shape=[
            jax.ShapeDtypeStruct((nb * t, GDN_HEADS * GDN_DV), F32),
            jax.ShapeDtypeStruct((nb, GDN_HEADS, GDN_DK, GDN_DV), F32),
        ],
        scratch_shapes=[pltpu.VMEM((GDN_HEADS, GDN_DK, GDN_DV), F32)],
        compiler_params=_cparams("parallel", "arbitrary"),
        name="gdn",
    )(gqkv, gab, gz, alog_row, dtb_row, norm_w)


SSM_BLK = 4
SSM_BLK_STATE = SSM_STATE // SSM_BLK


def _s5_params(lam_re, lam_im, log_step, b_re, b_im, c_re, c_im):
    step = jnp.exp(log_step.astype(F32))[:, None]
    mag = jnp.exp(lam_re * step)
    ab_re = mag * jnp.cos(lam_im * step)
    ab_im = mag * jnp.sin(lam_im * step)
    den = lam_re * lam_re + lam_im * lam_im
    nr = ab_re - 1.0
    f_re = (nr * lam_re + ab_im * lam_im) / den
    f_im = (ab_im * lam_re - nr * lam_im) / den
    bb_re = f_re[..., None] * b_re - f_im[..., None] * b_im
    bb_im = f_re[..., None] * b_im + f_im[..., None] * b_re
    gpb = SSM_GROUPS // SSM_BLK
    eye = jnp.eye(gpb, dtype=F32)

    def in_mat(bb):
        bb = bb.reshape(SSM_BLK, gpb, SSM_P, SSM_GROUP)
        m = jnp.einsum("jgpc,gk->jgckp", bb, eye)
        return m.reshape(SSM_BLK, gpb * SSM_GROUP, gpb * SSM_P).astype(BF16)

    def out_mat(cc):
        cc = cc.reshape(SSM_BLK, gpb, SSM_GROUP, SSM_P)
        m = jnp.einsum("jgcp,gk->jgpkc", cc, eye)
        return m.reshape(SSM_BLK, gpb * SSM_P, gpb * SSM_GROUP).astype(BF16)

    return (ab_re.reshape(1, SSM_STATE), ab_im.reshape(1, SSM_STATE),
            in_mat(bb_re), in_mat(bb_im), out_mat(c_re), out_mat(c_im))


def _s5_readout(u, h_re, h_im, cre_ref, cim_ref, d_row, gw_ref, gb_row):
    ys = []
    for j in range(SSM_BLK):
        sl = slice(j * SSM_BLK_STATE, (j + 1) * SSM_BLK_STATE)
        ys.append(_bdot(h_re[:, sl], cre_ref[j]) - _bdot(h_im[:, sl], cim_ref[j]))
    y = jax.nn.gelu(jnp.concatenate(ys, axis=1) + d_row * u)
    return y * _sigmoid(_bdot(y, gw_ref[...]) + gb_row)


def _s5_kernel(u_ref, are_ref, aim_ref, bre_ref, bim_ref, cre_ref, cim_ref, d_ref, gw_ref, gb_ref,
               y_ref, hre_ref, him_ref, bu_re, bu_im, h_re, h_im, *, nb):
    i = pl.program_id(0)
    tc = u_ref.shape[1]
    rows = tc * nb

    @pl.when(i == 0)
    def _():
        h_re[...] = jnp.zeros(h_re.shape, F32)
        h_im[...] = jnp.zeros(h_im.shape, F32)

    u = _to_time_major(u_ref[...])
    for j in range(SSM_BLK):
        uj = u[:, j * LANE:(j + 1) * LANE]
        sl = slice(j * SSM_BLK_STATE, (j + 1) * SSM_BLK_STATE)
        bu_re[:, sl] = _bdot(uj, bre_ref[j])
        bu_im[:, sl] = _bdot(uj, bim_ref[j])
    for j in range(SSM_BLK):
        sl = slice(j * SSM_BLK_STATE, (j + 1) * SSM_BLK_STATE)
        a_re = jnp.broadcast_to(are_ref[:, sl], (nb, SSM_BLK_STATE))
        a_im = jnp.broadcast_to(aim_ref[:, sl], (nb, SSM_BLK_STATE))

        def body(t, carry):
            hr, hi = carry
            r = pl.ds(pl.multiple_of(t * nb, nb), nb)
            nr = a_re * hr - a_im * hi + bu_re[r, sl]
            ni = a_re * hi + a_im * hr + bu_im[r, sl]
            bu_re[r, sl] = nr
            bu_im[r, sl] = ni
            return nr, ni

        hr, hi = lax.fori_loop(0, tc, body, (h_re[:, sl], h_im[:, sl]), unroll=4)
        h_re[:, sl] = hr
        h_im[:, sl] = hi
    y = _s5_readout(u, bu_re[...], bu_im[...], cre_ref, cim_ref, d_ref[...], gw_ref, gb_ref[...])
    y_ref[...] = _from_time_major(y, nb)
    hre_ref[...] = h_re[...]
    him_ref[...] = h_im[...]


def _s5_prompt(u_t, s5_w, layer, tc):
    nb, t, _ = u_t.shape
    blk = pl.BlockSpec((nb, tc, SSM_WIDTH), lambda i: (0, i, 0))
    st = pl.BlockSpec((nb, SSM_STATE), lambda i: (0, 0))
    return pl.pallas_call(
        functools.partial(_s5_kernel, nb=nb),
        grid=(t // tc,),
        in_specs=[blk] + [_layer_spec(a, layer) for a in s5_w],
        out_specs=[blk, st, st],
        out_shape=[jax.ShapeDtypeStruct((nb, t, SSM_WIDTH), F32),
                   jax.ShapeDtypeStruct((nb, SSM_STATE), F32),
                   jax.ShapeDtypeStruct((nb, SSM_STATE), F32)],
        scratch_shapes=[pltpu.VMEM((tc * nb, SSM_STATE), F32), pltpu.VMEM((tc * nb, SSM_STATE), F32),
                        pltpu.VMEM((nb, SSM_STATE), F32), pltpu.VMEM((nb, SSM_STATE), F32)],
        compiler_params=_cparams("arbitrary"),
        name="s5",
    )(u_t, *s5_w)


def _lru_gates(cx, wa_ref, wx_ref, ba_row, bx_row, lam_row):
    r = _sigmoid(_bdot(cx, wa_ref[...]) + ba_row)
    i = _sigmoid(_bdot(cx, wx_ref[...]) + bx_row)
    log_a = -LRU_C * r * _softplus(-lam_row)
    a = jnp.exp(log_a)
    th = jnp.tanh(log_a)
    b = jnp.sqrt(-2.0 * th / (1.0 - th)) * (i * cx)
    return a, b


def _lru_kernel(x_ref, g_ref, cw_ref, cb_ref, wa_ref, wx_ref, ba_ref, bx_ref, lam_ref,
                y_ref, h_out, cv_out, xbuf, a_buf, b_buf, h_scr, *, nb):
    i = pl.program_id(0)
    tc = x_ref.shape[1]
    rows = tc * nb
    tail = (CONV_W - 1) * nb

    @pl.when(i == 0)
    def _():
        xbuf[0:tail, :] = jnp.zeros((tail, LRU_WIDTH), F32)
        h_scr[...] = jnp.zeros(h_scr.shape, F32)

    x = _to_time_major(x_ref[...])
    xbuf[tail:tail + rows, :] = x
    conv = cw_ref[3:4, :] * x
    for j in range(CONV_W - 1):
        conv = conv + cw_ref[j:j + 1, :] * xbuf[j * nb:j * nb + rows, :]
    cv_out[...] = xbuf[rows:rows + tail, :].reshape(CONV_W - 1, nb, LRU_WIDTH)
    xbuf[0:tail, :] = x[rows - tail:, :]
    a, b = _lru_gates(conv + cb_ref[...], wa_ref, wx_ref, ba_ref[...], bx_ref[...], lam_ref[...])
    a_buf[...] = a
    b_buf[...] = b

    def body(t, h):
        r = pl.ds(pl.multiple_of(t * nb, nb), nb)
        h = a_buf[r, :] * h + b_buf[r, :]
        b_buf[r, :] = h
        return h

    h = lax.fori_loop(0, tc, body, h_scr[...], unroll=8)
    h_scr[...] = h
    h_out[...] = h
    y = b_buf[...] * jax.nn.gelu(_to_time_major(g_ref[...]))
    y_ref[...] = _from_time_major(y, nb)


def _lru_prompt(x_t, g_t, lru_w, layer, tc):
    nb, t, _ = x_t.shape
    blk = pl.BlockSpec((nb, tc, LRU_WIDTH), lambda i: (0, i, 0))
    rows = tc * nb
    return pl.pallas_call(
        functools.partial(_lru_kernel, nb=nb),
        grid=(t // tc,),
        in_specs=[blk, blk] + [_layer_spec(a, layer) for a in lru_w],
        out_specs=[blk,
                   pl.BlockSpec((nb, LRU_WIDTH), lambda i: (0, 0)),
                   pl.BlockSpec((CONV_W - 1, nb, LRU_WIDTH), lambda i: (0, 0, 0))],
        out_shape=[jax.ShapeDtypeStruct((nb, t, LRU_WIDTH), F32),
                   jax.ShapeDtypeStruct((nb, LRU_WIDTH), F32),
                   jax.ShapeDtypeStruct((CONV_W - 1, nb, LRU_WIDTH), F32)],
        scratch_shapes=[pltpu.VMEM(((CONV_W - 1) * nb + rows, LRU_WIDTH), F32),
                        pltpu.VMEM((rows, LRU_WIDTH), F32), pltpu.VMEM((rows, LRU_WIDTH), F32),
                        pltpu.VMEM((nb, LRU_WIDTH), F32)],
        compiler_params=_cparams("arbitrary"),
        name="lru",
    )(x_t, g_t, *lru_w)


def _merge_kernel(x_ref, sh_ref, sc_ref, gt_ref, ya_ref, yb_ref, yc_ref, yd_ref, wmg_ref, wb_ref, wo_ref,
                  g_ref, b_ref, o_ref, *, alpha, parts):
    ks = range(parts)
    xs, shs, scs, gts = (_row_parts(r, parts) for r in (x_ref, sh_ref, sc_ref, gt_ref))
    hs = [(_ln(xs[k]) * (1.0 + scs[k]) + shs[k]).astype(BF16) for k in ks]
    accs = [jnp.zeros(xs[k].shape, F32) for k in ks]
    for n, y_ref in enumerate((ya_ref, yb_ref, yc_ref, yd_ref)):
        ys = _row_parts(y_ref, parts)
        logits = [jnp.dot(hs[k], wmg_ref[:, n * D_MODEL:(n + 1) * D_MODEL], preferred_element_type=F32)
                  for k in ks]
        accs = [accs[k] + _sigmoid(logits[k]) * _bdot(ys[k], wb_ref[n]) for k in ks]
    outs = [_bdot(accs[k], wo_ref[...]) for k in ks]
    hm = x_ref.shape[0] // parts
    for k in ks:
        z = alpha * xs[k] + gts[k] * outs[k]
        o_ref[k * hm:(k + 1) * hm, :] = _ln(z) * g_ref[...] + b_ref[...]


def _merge(x, mod, ya, yb, yc, yd, w_mg, w_branch, w_out, ln_g, ln_b, alpha, tm):
    n = x.shape[0]
    l = mod.layer
    row = lambda w: pl.BlockSpec((tm, w), lambda i: (i, 0))
    return pl.pallas_call(
        functools.partial(_merge_kernel, alpha=alpha, parts=2 if tm >= 512 else 1),
        grid=(n // tm,),
        in_specs=[
            row(D_MODEL), mod.spec(1, 0, tm), mod.spec(1, 1, tm), mod.spec(1, 2, tm),
            row(BRANCH_W), row(BRANCH_W), row(BRANCH_W), row(BRANCH_W),
            _resident((None, D_MODEL, MG_WIDTH), lambda i: (l, 0, MG_BLOCK)),
            _resident((None, N_BRANCH, BRANCH_W, D_MODEL), lambda i: (l, 0, 0, 0)),
            _resident((None, D_MODEL, D_MODEL), lambda i: (l, 0, 0)),
            pl.BlockSpec((None, None, 1, D_MODEL), lambda i: (l, 1, 0, 0)),
            pl.BlockSpec((None, None, 1, D_MODEL), lambda i: (l, 1, 0, 0)),
        ],
        out_specs=row(D_MODEL),
        out_shape=jax.ShapeDtypeStruct((n, D_MODEL), F32),
        compiler_params=_cparams("parallel"),
        name="merge",
    )(x, mod.arr, mod.arr, mod.arr, ya, yb, yc, yd, w_mg, w_branch, w_out, ln_g, ln_b)


def _smix1_kernel(ret_ref, gqkv_ref, gab_ref, su_ref, lx_ref, cos_ref, sin_ref,
                  gcv_ref, sre_ref, sim_ref, lru_ref, lcv_ref,
                  gcw_ref, alog_ref, dtb_ref,
                  are_ref, aim_ref, bre_ref, bim_ref, cre_ref, cim_ref, d_ref, gw_ref, gb_ref,
                  lcw_ref, lcb_ref, wa_ref, wx_ref, ba_ref, bx_ref, lam_ref,
                  qkt_ref, gv_ref, gx_ref, yc_ref, lh_ref, gcv_out, sre_out, sim_out, lcv_out):
    ret = ret_ref[...]
    rq = _rotate(ret[:, 0:RET_QK], cos_ref[...], sin_ref[...])
    rk = _rotate(ret[:, RET_QK:2 * RET_QK], cos_ref[...], sin_ref[...]) * (RET_DK ** -0.5)

    x = gqkv_ref[...]
    conv = gcw_ref[3:4, :] * x
    for j in range(CONV_W - 1):
        conv = conv + gcw_ref[j:j + 1, :] * gcv_ref[j]
    gcv_out[0] = gcv_ref[1]
    gcv_out[1] = gcv_ref[2]
    gcv_out[2] = x
    cq = _gdn_activate(conv)
    gq, gk = cq[:, :GDN_K], cq[:, GDN_K:2 * GDN_K]
    gv_ref[...] = cq[:, 2 * GDN_K:]
    gl = _gate_lanes(gab_ref[...], alog_ref[...], dtb_ref[...])
    gx = jnp.concatenate([jnp.broadcast_to(gl[:, r:r + 1], (gl.shape[0], LANE))
                          for r in range(2 * GDN_HEADS)], axis=1)
    gx_ref[...] = jnp.where(_iota(gx.shape, 1) < GDN_HEADS * LANE, jnp.exp(gx), gx)
    qkt_ref[...] = jnp.concatenate([rq, rk, gq, gk], axis=1).T

    u = su_ref[...]
    h_re, h_im = [], []
    for j in range(SSM_BLK):
        sl = slice(j * SSM_BLK_STATE, (j + 1) * SSM_BLK_STATE)
        uj = u[:, j * LANE:(j + 1) * LANE]
        a_re, a_im = are_ref[:, sl], aim_ref[:, sl]
        p_re, p_im = sre_ref[:, sl], sim_ref[:, sl]
        h_re.append(a_re * p_re - a_im * p_im + _bdot(uj, bre_ref[j]))
        h_im.append(a_re * p_im + a_im * p_re + _bdot(uj, bim_ref[j]))
    h_re = jnp.concatenate(h_re, axis=1)
    h_im = jnp.concatenate(h_im, axis=1)
    sre_out[...] = h_re
    sim_out[...] = h_im
    yc_ref[...] = _s5_readout(u, h_re, h_im, cre_ref, cim_ref, d_ref[...], gw_ref, gb_ref[...])

    lx = lx_ref[...]
    conv = lcw_ref[3:4, :] * lx
    for j in range(CONV_W - 1):
        conv = conv + lcw_ref[j:j + 1, :] * lcv_ref[j]
    lcv_out[0] = lcv_ref[1]
    lcv_out[1] = lcv_ref[2]
    lcv_out[2] = lx
    a, b = _lru_gates(conv + lcb_ref[...], wa_ref, wx_ref, ba_ref[...], bx_ref[...], lam_ref[...])
    lh_ref[...] = a * lru_ref[...] + b


def _smix1(pieces, cos_row, sin_row, states, gdn_w, s5_w, lru_w, layer):
    rows = pieces["ret"].shape[0]
    acts = [pieces["ret"], pieces["gqkv"], pieces["gab"], pieces["su"], pieces["lx"], cos_row, sin_row]
    stacked = [*states, *gdn_w, *s5_w, *lru_w]
    ins = acts + stacked
    full = lambda a: pl.BlockSpec(a.shape, lambda i: (0,) * a.ndim)
    sds = lambda *s: jax.ShapeDtypeStruct(s, F32)
    out_shape = [sds(QKT_ROWS, rows), sds(rows, GDN_HEADS * GDN_DV), sds(rows, 8 * LANE),
                 sds(rows, SSM_WIDTH), sds(rows, LRU_WIDTH),
                 sds(CONV_W - 1, rows, GDN_QKV), sds(rows, SSM_STATE), sds(rows, SSM_STATE),
                 sds(CONV_W - 1, rows, LRU_WIDTH)]
    return pl.pallas_call(
        _smix1_kernel,
        grid=(1,),
        in_specs=[full(a) for a in acts] + [_layer_spec(a, layer) for a in stacked],
        out_specs=[pl.BlockSpec(s.shape, lambda i, nd=len(s.shape): (0,) * nd) for s in out_shape],
        out_shape=out_shape,
        compiler_params=_cparams("arbitrary"),
        name="sample_mix",
    )(*ins)


def _smix2_kernel(qkt_ref, rv_ref, gv_ref, gx_ref, rg_ref, gz_ref, lh_ref, lg_ref, nw_ref, sret_ref, sgdn_ref,
                  *refs):
    ya_ref, yb_ref, yd_ref, nret_ref, ngdn_ref, o_ret, o_gdn = refs[-7:]
    i = pl.program_id(0)
    bt = rv_ref.shape[0]
    rows = qkt_ref.shape[1]
    gammas = _ret_gammas()
    q_hi, q_lo = _split(qkt_ref[...])
    qk_split = jnp.concatenate([q_hi, q_lo], axis=1)

    for j in range(bt):
        b = i * bt + j
        onehot = jnp.where(_iota((2 * rows, LANE), 0) % rows == b, 1.0, 0.0).astype(BF16)
        cols = jnp.dot(qk_split, onehot, preferred_element_type=F32)
        r = slice(j, j + 1)
        for h in range(RET_HEADS):
            q_c = cols[h * RET_DK:(h + 1) * RET_DK]
            k_c = cols[RET_QK + h * RET_DK:RET_QK + (h + 1) * RET_DK]
            v_r = rv_ref[r, h * RET_DV:(h + 1) * RET_DV]
            s_new = gammas[h] * sret_ref[j, h] + k_c * v_r
            nret_ref[j, h] = s_new
            o_ret[r, h * RET_DV:(h + 1) * RET_DV] = jnp.sum(q_c * s_new, axis=0, keepdims=True)
        base = 2 * RET_QK
        for h in range(GDN_HEADS):
            q_c = cols[base + h * GDN_DK:base + (h + 1) * GDN_DK]
            k_c = cols[base + GDN_K + h * GDN_DK:base + GDN_K + (h + 1) * GDN_DK]
            v_r = gv_ref[r, h * GDN_DV:(h + 1) * GDN_DV]
            e_g = gx_ref[r, h * LANE:(h + 1) * LANE]
            beta = gx_ref[r, (GDN_HEADS + h) * LANE:(GDN_HEADS + h + 1) * LANE]
            s = sgdn_ref[j, h]
            v_new = beta * (v_r - e_g * jnp.sum(k_c * s, axis=0, keepdims=True))
            s_new = e_g * s + k_c * v_new
            ngdn_ref[j, h] = s_new
            o_gdn[r, h * GDN_DV:(h + 1) * GDN_DV] = jnp.sum(q_c * s_new, axis=0, keepdims=True)
    rg, gz = rg_ref[...], gz_ref[...]
    o_r, o_g = o_ret[...], o_gdn[...]
    ya_ref[...] = jnp.concatenate(
        [_ln(o_r[:, h * RET_DV:(h + 1) * RET_DV]) for h in range(RET_HEADS)], axis=1) * _silu(rg)
    yb_ref[...] = jnp.concatenate(
        [_rms(o_g[:, h * GDN_DV:(h + 1) * GDN_DV]) * nw_ref[...] for h in range(GDN_HEADS)], axis=1) * _silu(gz)
    yd_ref[...] = lh_ref[...] * jax.nn.gelu(lg_ref[...])


def _smix2(qkt, rv, gv, gx, rg, gz, lh, lg, norm_w, s_ret, s_gdn, layer, bt, prev):
    rows = rv.shape[0]
    row = lambda w: pl.BlockSpec((bt, w), lambda i: (i, 0))
    sds = lambda *s: jax.ShapeDtypeStruct(s, F32)
    ret_blk = pl.BlockSpec((None, bt, RET_HEADS, RET_DK, RET_DV), lambda i: (layer, i, 0, 0, 0))
    gdn_blk = pl.BlockSpec((None, bt, GDN_HEADS, GDN_DK, GDN_DV), lambda i: (layer, i, 0, 0, 0))
    ins = [qkt, rv, gv, gx, rg, gz, lh, lg, norm_w, s_ret, s_gdn]
    in_specs = [pl.BlockSpec(qkt.shape, lambda i: (0, 0)),
                pl.BlockSpec((bt, RET_V), lambda i: (i, 1)),
                row(GDN_HEADS * GDN_DV), row(8 * LANE),
                pl.BlockSpec((bt, RET_V), lambda i: (i, 2)),
                row(GDN_HEADS * GDN_DV), row(LRU_WIDTH), row(LRU_WIDTH),
                _layer_spec(norm_w, layer), ret_blk, gdn_blk]
    aliases = {}
    if prev is not None:
        aliases = {len(ins): 3, len(ins) + 1: 4}
        ins += list(prev)
        in_specs += [pl.BlockSpec(memory_space=pl.ANY)] * 2
    return pl.pallas_call(
        _smix2_kernel,
        grid=(rows // bt,),
        in_specs=in_specs,
        out_specs=[row(RET_V), row(GDN_HEADS * GDN_DV), row(LRU_WIDTH), ret_blk, gdn_blk],
        out_shape=[sds(rows, RET_V), sds(rows, GDN_HEADS * GDN_DV), sds(rows, LRU_WIDTH),
                   sds(*s_ret.shape), sds(*s_gdn.shape)],
        input_output_aliases=aliases,
        scratch_shapes=[pltpu.VMEM((bt, RET_V), F32), pltpu.VMEM((bt, GDN_HEADS * GDN_DV), F32)],
        compiler_params=_cparams("parallel"),
        name="sample_state",
    )(*ins)


def _block_diag(w):
    nb, bs, _ = w.shape
    return jnp.einsum("nij,nm->nimj", w, jnp.eye(nb, dtype=w.dtype)).reshape(nb * bs, nb * bs)


def kernel(x_prompt, x_sample, c_prompt, c_sample, state_ret, state_gdn, state_gdn_conv, state_ssm_re, state_ssm_im, state_lru, state_lru_conv, w_ada, b_ada, ln_g, ln_b, w_ffn_up, w_ffn_down, w_in, gdn_conv_w, gdn_a_log, gdn_dt_bias, gdn_norm_w, ssm_lam_re, ssm_lam_im, ssm_log_step, ssm_b_re, ssm_b_im, ssm_c_re, ssm_c_im, ssm_d, ssm_glu_w, ssm_glu_b, lru_conv_w, lru_conv_b, lru_wa, lru_ba, lru_wx, lru_bx, lru_lam, w_branch, w_out):
    nb, t, _ = x_prompt.shape
    ns = x_sample.shape[0]
    depth = w_ada.shape[0]
    assert t % RET_CHUNK == 0 and x_sample.shape[1] == 1
    alpha = (2 * depth) ** 0.25
    tm = 256 if t % 256 == 0 else RET_CHUNK
    tm_ffn = 512 if t % 512 == 0 else tm
    tc = 64

    wup = w_ffn_up.astype(BF16)
    wdn = w_ffn_down.astype(BF16)
    wbr = w_branch.astype(BF16)
    wout = w_out.astype(BF16)
    w_packed = _pack_w_in(w_in)
    ln_g4 = ln_g.reshape(depth, N_SUB, 1, D_MODEL)
    ln_b4 = ln_b.reshape(depth, N_SUB, 1, D_MODEL)

    mod_all = _ada(jnp.concatenate([c_prompt, c_sample], axis=0), w_ada, b_ada)
    mod_p = mod_all[:, :nb].reshape(depth, nb, 3 * N_SUB, 1, D_MODEL).transpose(0, 2, 1, 3, 4)
    mod_s = mod_all[:, nb:]

    row3 = lambda a: a.reshape(depth, 1, -1)
    lane_rows = lambda v: jnp.pad(v, ((0, 0), (0, LANE - v.shape[1]))).reshape(depth, 1, LANE)
    alog_rows, dtb_rows = lane_rows(gdn_a_log), lane_rows(gdn_dt_bias)
    norm_w = row3(gdn_norm_w)
    s5_w = (*jax.vmap(_s5_params)(ssm_lam_re, ssm_lam_im, ssm_log_step, ssm_b_re, ssm_b_im, ssm_c_re, ssm_c_im),
            row3(ssm_d), ssm_glu_w.astype(BF16), row3(ssm_glu_b))
    lru_w = (lru_conv_w, row3(lru_conv_b),
             jax.vmap(_block_diag)(lru_wa).astype(BF16), jax.vmap(_block_diag)(lru_wx).astype(BF16),
             row3(lru_ba), row3(lru_bx), row3(lru_lam))
    s_states = (state_gdn_conv.transpose(0, 2, 1, 3),
                state_ssm_re.reshape(depth, ns, SSM_STATE), state_ssm_im.reshape(depth, ns, SSM_STATE),
                state_lru, state_lru_conv.transpose(0, 2, 1, 3))

    cos_s, sin_s = _rope_tables(jnp.full((1,), PAST_LEN))
    xp = x_prompt.reshape(nb * t, D_MODEL)
    xs = x_sample.reshape(ns, D_MODEL)
    new_p, new_s = [], []
    big_s = None
    for l in range(depth):
        mp = _Mod(mod_p, l, False, t)
        ms = _Mod(mod_s, l, True)

        xp = _ffn(xp, mp, 0, 0, wup, wdn, ln_g4, ln_b4, alpha, tm_ffn)
        pc = _inproj(xp, mp, w_packed, tm, conv=(gdn_conv_w, nb, t))
        gcv_p = pc["gdn_conv"]
        ya, ret_p = _retention_prompt(pc["ret"], nb, t)
        yb, gdn_p = _gdn_prompt(pc["gqkv"], pc["gab"], pc["gz"], alog_rows, dtb_rows, norm_w, l, nb, t)
        to_t = lambda a: a.reshape(nb, t, -1)
        from_t = lambda a: a.reshape(nb * t, -1)
        yc_t, sre_p, sim_p = _s5_prompt(to_t(pc["su"]), s5_w, l, tc)
        yd_t, lru_p, lcv_p = _lru_prompt(to_t(pc["lx"]), to_t(pc["lg"]), lru_w, l, tc)
        xp = _merge(xp, mp, ya, yb, from_t(yc_t), from_t(yd_t), w_packed, wbr, wout, ln_g4, ln_b4, alpha, tm_ffn)
        xp = _ffn(xp, mp, 2, 1, wup, wdn, ln_g4, ln_b4, alpha, tm_ffn)
        new_p.append((ret_p, gdn_p, gcv_p,
                      sre_p.reshape(nb, SSM_GROUPS, SSM_P), sim_p.reshape(nb, SSM_GROUPS, SSM_P),
                      lru_p, lcv_p.transpose(1, 0, 2)))

        xs = _ffn(xs, ms, 0, 0, wup, wdn, ln_g4, ln_b4, alpha, ns)
        sc = _inproj(xs, ms, w_packed, ns)
        (qkt, gv, gx, yc, lh, gcv_s, sre_s, sim_s, lcv_s) = _smix1(
            sc, cos_s, sin_s, s_states, (gdn_conv_w, alog_rows, dtb_rows), s5_w, lru_w, l)
        ya, yb, yd, *big_s = _smix2(qkt, sc["ret"], gv, gx, sc["ret"], sc["gz"], lh, sc["lg"],
                                    norm_w, state_ret, state_gdn, l, SUBLANE, big_s)
        xs = _merge(xs, ms, ya, yb, yc, yd, w_packed, wbr, wout, ln_g4, ln_b4, alpha, ns)
        xs = _ffn(xs, ms, 2, 1, wup, wdn, ln_g4, ln_b4, alpha, ns)
        new_s.append((gcv_s.transpose(1, 0, 2),
                      sre_s.reshape(ns, SSM_GROUPS, SSM_P), sim_s.reshape(ns, SSM_GROUPS, SSM_P),
                      lh, lcv_s.transpose(1, 0, 2)))

    ret_p, gdn_p, gcv_p, sre_p, sim_p, lru_p, lcv_p = [jnp.stack(z) for z in zip(*new_p)]
    gcv_s, sre_s, sim_s, lru_s, lcv_s = [jnp.stack(z) for z in zip(*new_s)]
    ret_s, gdn_s = big_s
    return (xp.reshape(nb, t, D_MODEL), xs.reshape(ns, 1, D_MODEL),
            ret_p, ret_s, gdn_p, gdn_s, gcv_p, gcv_s, sre_p, sre_s, sim_p, sim_s,
            lru_p, lru_s, lcv_p, lcv_s)
```

```python
import functools
import math

import numpy as np
import jax
import jax.numpy as jnp
from jax import lax
from jax.experimental import pallas as pl
from jax.experimental.pallas import tpu as pltpu

F32 = jnp.float32
BF16 = jnp.bfloat16
HI = lax.Precision.HIGHEST

D_MODEL = 1024
RET_HEADS, RET_DK, RET_DV, RET_CHUNK = 4, 64, 128, 128
RET_STEP_CHUNKS = 4
ROPE_BASE = 10000.0
GDN_HEADS, GDN_DK, GDN_DV, GDN_CHUNK = 4, 128, 128, 64
GDN_STEP_CHUNKS = 4
GDN_QKV = 2 * GDN_HEADS * GDN_DK + GDN_HEADS * GDN_DV
CONV_W = 4
SSM_GROUP, SSM_GROUPS, SSM_P = 16, 32, 64
SSM_WIDTH = SSM_GROUP * SSM_GROUPS
SSM_STATE = SSM_GROUPS * SSM_P
LRU_WIDTH, LRU_BLOCKS = 512, 8
LRU_C = 8.0
N_BRANCH, BRANCH_W = 4, 512
D_FF = 2816
N_SUB = 3
LN_EPS = 1e-5
PAST_LEN = 16384

RET_QK = RET_HEADS * RET_DK
RET_V = RET_HEADS * RET_DV
RET_W = 2 * RET_QK + 2 * RET_V
GDN_K = GDN_HEADS * GDN_DK
QKT_ROWS = 2 * RET_QK + 2 * GDN_K
LANE = 128
SUBLANE = 8
FF_CHUNKS = ((0, 768), (768, 1024), (1792, 1024))
VMEM_LIMIT = 56 * 1024 * 1024


def _cparams(*sem):
    return pltpu.CompilerParams(dimension_semantics=sem, vmem_limit_bytes=VMEM_LIMIT)


def _layer_spec(a, layer):
    nd = a.ndim
    return pl.BlockSpec((None,) + a.shape[1:], lambda *_: (layer,) + (0,) * (nd - 1))


def _resident(shape, index_map):
    return pl.BlockSpec(shape, index_map, pipeline_mode=pl.Buffered(1))


def _ln(x):
    mu = jnp.mean(x, axis=-1, keepdims=True)
    xc = x - mu
    return xc * lax.rsqrt(jnp.mean(xc * xc, axis=-1, keepdims=True) + LN_EPS)


def _sigmoid(x):
    return 0.5 * jnp.tanh(0.5 * x) + 0.5


def _silu(x):
    return x * _sigmoid(x)


def _softplus(x):
    return jnp.maximum(x, 0.0) + jnp.log1p(jnp.exp(-jnp.abs(x)))


def _bdot(a, w):
    return jnp.dot(a.astype(BF16), w.astype(BF16), preferred_element_type=F32)


def _split(x):
    hi = x.astype(BF16)
    return hi, (x - hi.astype(F32)).astype(BF16)


def _hdot(a, b):
    return jnp.dot(a, b, preferred_element_type=F32, precision=HI)


def _dot_nt(a, b, precision=None):
    return lax.dot_general(a, b, (((1,), (1,)), ((), ())),
                           preferred_element_type=F32, precision=precision)


def _iota(shape, dim):
    return lax.broadcasted_iota(jnp.int32, shape, dim)


def _to_time_major(x):
    nb, tc, width = x.shape
    return jnp.swapaxes(x, 0, 1).reshape(tc * nb, width)


def _from_time_major(y, nb):
    rows, width = y.shape
    return jnp.swapaxes(y.reshape(rows // nb, nb, width), 0, 1)


def _ada_kernel(c_ref, w_ref, b_ref, o_ref):
    o_ref[...] = _bdot(_silu(c_ref[...]), w_ref[...]) + b_ref[...]


def _ada(c_all, w_ada, b_ada):
    depth, _, n_out = w_ada.shape
    rows = c_all.shape[0]
    tn = 1152
    return pl.pallas_call(
        _ada_kernel,
        grid=(depth, n_out // tn),
        in_specs=[
            pl.BlockSpec((rows, D_MODEL), lambda l, j: (0, 0)),
            pl.BlockSpec((None, D_MODEL, tn), lambda l, j: (l, 0, j)),
            pl.BlockSpec((None, 1, tn), lambda l, j: (l, 0, j)),
        ],
        out_specs=pl.BlockSpec((None, rows, tn), lambda l, j: (l, 0, j)),
        out_shape=jax.ShapeDtypeStruct((depth, rows, n_out), F32),
        compiler_params=_cparams("parallel", "parallel"),
        name="ada",
    )(c_all, w_ada, b_ada.reshape(depth, 1, n_out))


class _Mod:
    def __init__(self, arr, layer, per_row, rows_per_batch=None):
        self.arr, self.layer, self.per_row, self.rpb = arr, layer, per_row, rows_per_batch

    def spec(self, sub, which, tm):
        k, l = 3 * sub + which, self.layer
        if self.per_row:
            return pl.BlockSpec((None, tm, D_MODEL), lambda i: (l, i, k))
        tiles = self.rpb // tm
        return pl.BlockSpec((None, None, None, 1, D_MODEL), lambda i: (l, k, i // tiles, 0, 0))


def _row_parts(ref, parts):
    if ref.shape[0] == 1:
        return [ref[...]] * parts
    hm = ref.shape[0] // parts
    return [ref[k * hm:(k + 1) * hm, :] for k in range(parts)]


def _ffn_kernel(x_ref, sh_ref, sc_ref, gt_ref, wup_ref, wdn_ref, g_ref, b_ref, o_ref, *, alpha, parts):
    ks = range(parts)
    xs, shs, scs, gts = (_row_parts(r, parts) for r in (x_ref, sh_ref, sc_ref, gt_ref))
    hs = [(_ln(xs[k]) * (1.0 + scs[k]) + shs[k]).astype(BF16) for k in ks]
    accs = [jnp.zeros(xs[k].shape, F32) for k in ks]
    for lo, width in FF_CHUNKS:
        a = [jnp.dot(hs[k], wup_ref[:, lo:lo + width], preferred_element_type=F32) for k in ks]
        b = [jnp.dot(hs[k], wup_ref[:, D_FF + lo:D_FF + lo + width], preferred_element_type=F32) for k in ks]
        act = [(_silu(a[k]) * b[k]).astype(BF16) for k in ks]
        accs = [accs[k] + jnp.dot(act[k], wdn_ref[lo:lo + width, :], preferred_element_type=F32) for k in ks]
    hm = x_ref.shape[0] // parts
    for k in ks:
        z = alpha * xs[k] + 0.5 * gts[k] * accs[k]
        o_ref[k * hm:(k + 1) * hm, :] = _ln(z) * g_ref[...] + b_ref[...]


def _ffn(x, mod, sub, which, wup, wdn, ln_g, ln_b, alpha, tm):
    n = x.shape[0]
    l = mod.layer
    row = pl.BlockSpec((tm, D_MODEL), lambda i: (i, 0))
    return pl.pallas_call(
        functools.partial(_ffn_kernel, alpha=alpha, parts=2 if tm >= 512 else 1),
        grid=(n // tm,),
        in_specs=[
            row, mod.spec(sub, 0, tm), mod.spec(sub, 1, tm), mod.spec(sub, 2, tm),
            _resident((None, None, D_MODEL, 2 * D_FF), lambda i: (l, which, 0, 0)),
            _resident((None, None, D_FF, D_MODEL), lambda i: (l, which, 0, 0)),
            pl.BlockSpec((None, None, 1, D_MODEL), lambda i: (l, sub, 0, 0)),
            pl.BlockSpec((None, None, 1, D_MODEL), lambda i: (l, sub, 0, 0)),
        ],
        out_specs=row,
        out_shape=jax.ShapeDtypeStruct((n, D_MODEL), F32),
        compiler_params=_cparams("parallel"),
        name="ffn",
    )(x, mod.arr, mod.arr, mod.arr, wup, wdn, ln_g, ln_b)


_IN_PIECES = (("ret", RET_W), ("gqkv", GDN_QKV), ("gab", LANE), ("gz", 512), ("su", SSM_WIDTH),
              ("lx", LRU_WIDTH), ("lg", LRU_WIDTH))
IN_MAIN = sum(w for _, w in _IN_PIECES)
MG_WIDTH = N_BRANCH * D_MODEL


PACK_ROWS = 512


def _pack_kernel(wt_ref, main_ref, mg_ref):
    gate_lo = RET_W + GDN_QKV
    gate_hi = gate_lo + 2 * GDN_HEADS
    mg_lo = wt_ref.shape[0] - MG_WIDTH

    def move(dst_ref, dst, src, width):
        for c in range(0, width, PACK_ROWS):
            dst_ref[:, dst + c:dst + c + PACK_ROWS] = wt_ref[src + c:src + c + PACK_ROWS, :].T.astype(BF16)

    move(main_ref, 0, 0, gate_lo)
    g = wt_ref[gate_lo:gate_lo + LANE, :].T
    main_ref[:, gate_lo:gate_lo + LANE] = jnp.where(_iota(g.shape, 1) < 2 * GDN_HEADS, g, 0.0).astype(BF16)
    move(main_ref, gate_lo + LANE, gate_hi, mg_lo - gate_hi)
    move(mg_ref, 0, mg_lo, MG_WIDTH)


def _pack_w_in(w_in):
    depth, d_in, d_out = w_in.shape
    tk = 256
    blk = lambda w: pl.BlockSpec((None, tk, w), lambda l, i: (l, i, 0))
    return pl.pallas_call(
        _pack_kernel,
        grid=(depth, d_in // tk),
        in_specs=[pl.BlockSpec((None, d_out, tk), lambda l, i: (l, 0, i))],
        out_specs=[blk(IN_MAIN), blk(MG_WIDTH)],
        out_shape=[jax.ShapeDtypeStruct((depth, d_in, IN_MAIN), BF16),
                   jax.ShapeDtypeStruct((depth, d_in, MG_WIDTH), BF16)],
        compiler_params=_cparams("parallel", "parallel"),
        name="pack_w_in",
    )(jnp.swapaxes(w_in, 1, 2))


def _l2n(x):
    return x * lax.rsqrt(jnp.sum(x * x, axis=-1, keepdims=True) + 1e-6)


def _gdn_activate(conv):
    cq = _silu(conv)
    q = [_l2n(cq[:, h * GDN_DK:(h + 1) * GDN_DK]) * (GDN_DK ** -0.5) for h in range(GDN_HEADS)]
    k = [_l2n(cq[:, GDN_K + h * GDN_DK:GDN_K + (h + 1) * GDN_DK]) for h in range(GDN_HEADS)]
    return jnp.concatenate(q + k + [cq[:, 2 * GDN_K:]], axis=1)


def _inproj_kernel(x_ref, sh_ref, sc_ref, w_ref, *refs, seq_tiles):
    h = (_ln(x_ref[...]) * (1.0 + sc_ref[...]) + sh_ref[...]).astype(BF16)
    n_out = len(_IN_PIECES)
    if seq_tiles is None:
        o_refs = refs
    else:
        cw_ref, o_refs, cv_ref, xbuf = refs[0], refs[1:1 + n_out], refs[1 + n_out], refs[2 + n_out]

        @pl.when(pl.program_id(0) % seq_tiles == 0)
        def _():
            xbuf[0:SUBLANE, :] = jnp.zeros((SUBLANE, GDN_QKV), F32)

    offsets = np.cumsum([0] + [w for _, w in _IN_PIECES])
    order = sorted(range(n_out), key=lambda p: _IN_PIECES[p][0] != "gqkv")
    for p in order:
        (name, width), o_ref, lo = _IN_PIECES[p], o_refs[p], int(offsets[p])
        y = jnp.dot(h, w_ref[:, lo:lo + width], preferred_element_type=F32)
        if name == "gqkv" and seq_tiles is not None:
            tm = y.shape[0]
            xbuf[SUBLANE:SUBLANE + tm, :] = y
            conv = cw_ref[CONV_W - 1:CONV_W, :] * y
            for j in range(CONV_W - 1):
                r0 = SUBLANE - (CONV_W - 1) + j
                conv = conv + cw_ref[j:j + 1, :] * xbuf[r0:r0 + tm, :]
            cv_ref[0] = xbuf[SUBLANE + tm - (CONV_W - 1):SUBLANE + tm, :]
            xbuf[0:SUBLANE, :] = y[tm - SUBLANE:, :]
            y = _gdn_activate(conv)
        o_ref[...] = y


def _inproj(x, mod, w_packed, tm, conv=None):
    n = x.shape[0]
    l = mod.layer
    widths = [w for _, w in _IN_PIECES]
    ins = [x, mod.arr, mod.arr, w_packed]
    in_specs = [pl.BlockSpec((tm, D_MODEL), lambda i: (i, 0)),
                mod.spec(1, 0, tm), mod.spec(1, 1, tm),
                _resident((None, D_MODEL, IN_MAIN), lambda i: (l, 0, 0))]
    out_specs = [pl.BlockSpec((tm, w), lambda i: (i, 0)) for w in widths]
    out_shape = [jax.ShapeDtypeStruct((n, w), F32) for w in widths]
    scratch, seq_tiles = [], None
    if conv is not None:
        conv_w, nb, t = conv
        seq_tiles = t // tm
        ins.append(conv_w)
        in_specs.append(_layer_spec(conv_w, l))
        out_specs.append(pl.BlockSpec((1, CONV_W - 1, GDN_QKV), lambda i: (i // seq_tiles, 0, 0)))
        out_shape.append(jax.ShapeDtypeStruct((nb, CONV_W - 1, GDN_QKV), F32))
        scratch.append(pltpu.VMEM((SUBLANE + tm, GDN_QKV), F32))
    outs = pl.pallas_call(
        functools.partial(_inproj_kernel, seq_tiles=seq_tiles),
        grid=(n // tm,),
        in_specs=in_specs,
        out_specs=out_specs,
        out_shape=out_shape,
        scratch_shapes=scratch,
        compiler_params=_cparams("arbitrary"),
        name="inproj",
    )(*ins)
    return dict(zip([nm for nm, _ in _IN_PIECES] + ["gdn_conv"], outs))


def _rope_tables(pos):
    half = RET_DK // 2
    inv = ROPE_BASE ** (-jnp.arange(half, dtype=F32) / half)
    ang = pos.astype(F32)[:, None] * inv[None, :]
    cos, sin = jnp.cos(ang), jnp.sin(ang)
    cos_t = jnp.tile(jnp.concatenate([cos, cos], axis=-1), (1, RET_HEADS))
    sin_t = jnp.tile(jnp.concatenate([-sin, sin], axis=-1), (1, RET_HEADS))
    return cos_t, sin_t


def _rotate(z, cos, sin):
    half = RET_DK // 2
    first = (_iota(z.shape, 1) % RET_DK) < half
    swapped = jnp.where(first, pltpu.roll(z, RET_QK - half, 1), pltpu.roll(z, half, 1))
    return z * cos + swapped * sin


def _ret_gammas():
    return [1.0 - 2.0 ** (-5.0 - h) for h in range(RET_HEADS)]


def _ret_consts(chunk, n_sub):
    log_g = np.log(np.array(_ret_gammas(), np.float64))
    idx = np.arange(chunk, dtype=np.float64)
    rel = idx[:, None] - idx[None, :]
    dmask = np.where(rel[None] >= 0, np.exp(log_g[:, None, None] * np.maximum(rel, 0.0)[None]), 0.0)
    qdec = np.repeat(np.exp(log_g[None, :] * (idx[:, None] + 1.0)), RET_DK, axis=1)
    kdec = np.repeat(np.exp(log_g[None, :] * (chunk - 1.0 - idx[:, None])), RET_DK, axis=1)
    qdec, kdec = np.tile(qdec, (n_sub, 1)), np.tile(kdec, (n_sub, 1))
    head_r = np.arange(RET_QK) // RET_DK
    head_c = np.arange(RET_V) // RET_DV
    bd = (head_r[:, None] == head_c[None, :]).astype(np.float64)
    cd = bd * np.exp(log_g * chunk)[head_r][:, None]
    f = lambda a: jnp.asarray(a, F32)
    return f(dmask), f(qdec), f(kdec), f(cd), f(bd)


def _ret_kernel(x_ref, cos_ref, sin_ref, dmask_ref, qdec_ref, kdec_ref, cd_ref, bd_ref,
                y_ref, st_ref, s_scr):
    c = pl.program_id(1)
    ch = RET_CHUNK
    n_sub = x_ref.shape[0] // ch

    @pl.when(c == 0)
    def _():
        s_scr[...] = jnp.zeros(s_scr.shape, F32)

    x = x_ref[...]
    q = _rotate(x[:, 0:RET_QK], cos_ref[...], sin_ref[...])
    k = _rotate(x[:, RET_QK:2 * RET_QK], cos_ref[...], sin_ref[...]) * (RET_DK ** -0.5)
    v = x[:, 2 * RET_QK:2 * RET_QK + RET_V].astype(BF16)
    gate = x[:, 2 * RET_QK + RET_V:]
    q_dec = (q * qdec_ref[...]).astype(BF16)
    k_dec = k * kdec_ref[...]
    kb = k.astype(BF16)
    head = _iota((ch, RET_QK), 1) // RET_DK
    intra, upd = {}, {}
    for n in range(n_sub):
        sl = slice(n * ch, (n + 1) * ch)
        for h in range(RET_HEADS):
            qh = jnp.where(head == h, q[sl], 0.0).astype(BF16)
            inner = _dot_nt(qh, kb[sl]) * dmask_ref[h]
            intra[n, h] = _bdot(inner, v[sl, h * RET_DV:(h + 1) * RET_DV])
        upd[n] = jnp.dot(k_dec[sl].T.astype(BF16), v[sl], preferred_element_type=F32) * bd_ref[...]
    s = s_scr[...]
    cross = {}
    for n in range(n_sub):
        cross[n] = jnp.dot(q_dec[n * ch:(n + 1) * ch], s.astype(BF16), preferred_element_type=F32)
        s = s * cd_ref[...] + upd[n]
    s_scr[...] = s
    for n in range(n_sub):
        sl = slice(n * ch, (n + 1) * ch)
        outs = [_ln(intra[n, h] + cross[n][:, h * RET_DV:(h + 1) * RET_DV]) for h in range(RET_HEADS)]
        y_ref[sl, :] = jnp.concatenate(outs, axis=1) * _silu(gate[sl])

    @pl.when(c == pl.num_programs(1) - 1)
    def _():
        for h in range(RET_HEADS):
            st_ref[0, h] = s[h * RET_DK:(h + 1) * RET_DK, h * RET_DV:(h + 1) * RET_DV]


def _retention_prompt(ret, nb, t):
    n_sub = RET_STEP_CHUNKS if t % (RET_STEP_CHUNKS * RET_CHUNK) == 0 else 1
    chunk = n_sub * RET_CHUNK
    nc = t // chunk
    cos_t, sin_t = _rope_tables(jnp.arange(t))
    consts = _ret_consts(RET_CHUNK, n_sub)
    full = lambda a: pl.BlockSpec(a.shape, lambda b, c: (0,) * a.ndim)
    return pl.pallas_call(
        _ret_kernel,
        grid=(nb, nc),
        in_specs=[
            pl.BlockSpec((chunk, RET_W), lambda b, c: (b * nc + c, 0)),
            pl.BlockSpec((chunk, RET_QK), lambda b, c: (c, 0)),
            pl.BlockSpec((chunk, RET_QK), lambda b, c: (c, 0)),
        ] + [full(a) for a in consts],
        out_specs=[
            pl.BlockSpec((chunk, RET_V), lambda b, c: (b * nc + c, 0)),
            pl.BlockSpec((1, RET_HEADS, RET_DK, RET_DV), lambda b, c: (b, 0, 0, 0)),
        ],
        out_shape=[
            jax.ShapeDtypeStruct((nb * t, RET_V), F32),
            jax.ShapeDtypeStruct((nb, RET_HEADS, RET_DK, RET_DV), F32),
        ],
        scratch_shapes=[pltpu.VMEM((RET_QK, RET_V), F32)],
        compiler_params=_cparams("parallel", "arbitrary"),
        name="retention",
    )(ret, cos_t, sin_t, *consts)


def _rms(x):
    return x * lax.rsqrt(jnp.mean(x * x, axis=-1, keepdims=True) + 1e-6)


def _gate_lanes(gab, alog_row, dtb_row):
    g = -jnp.exp(alog_row) * _softplus(gab + dtb_row)
    return jnp.where(_iota(gab.shape, 1) < GDN_HEADS, g, _sigmoid(gab))


def _gdn_kernel(x_ref, gab_ref, gz_ref, alog_ref, dtb_ref, nw_ref, y_ref, st_ref, s_scr):
    c = pl.program_id(1)
    ch = GDN_CHUNK

    @pl.when(c == 0)
    def _():
        s_scr[...] = jnp.zeros(s_scr.shape, F32)

    rows = x_ref.shape[0]
    cq = x_ref[...]

    gl = _gate_lanes(gab_ref[...], alog_ref[...], dtb_ref[...])
    col = _iota((ch, 2 * ch), 1) % ch
    tril = _iota((ch, 2 * ch), 0) >= col
    strict = _iota((ch, 2 * ch), 0) > col
    is_g = _iota(gl.shape, 1) < GDN_HEADS
    r_i, c_i = _iota((rows, rows), 0), _iota((rows, rows), 1)
    chunk_tril = ((r_i >= c_i) & (r_i // ch == c_i // ch)).astype(F32)
    csum = _hdot(chunk_tril, jnp.where(is_g, gl, 0.0))
    gsel = jnp.where(is_g, csum, gl)
    n_sub = rows // ch
    heads = range(GDN_HEADS)
    probs = [(h, n) for h in heads for n in range(n_sub)]

    q_all, k_all, g_all, qe_all, rhs_all, kbeta_all = {}, {}, {}, {}, {}, {}
    for h in heads:
        q_all[h] = cq[:, h * GDN_DK:(h + 1) * GDN_DK]
        k_all[h] = cq[:, GDN_K + h * GDN_DK:GDN_K + (h + 1) * GDN_DK]
        v_h = cq[:, 2 * GDN_K + h * GDN_DV:2 * GDN_K + (h + 1) * GDN_DV]
        g_all[h] = jnp.broadcast_to(gsel[:, h:h + 1], (rows, LANE))
        b_h = jnp.broadcast_to(gsel[:, GDN_HEADS + h:GDN_HEADS + h + 1], (rows, LANE))
        e_h = jnp.exp(g_all[h])
        kbeta_all[h] = k_all[h] * b_h
        rhs_all[h] = jnp.concatenate([v_h * b_h, kbeta_all[h] * e_h], axis=1)
        qe_all[h] = q_all[h] * e_h

    decay, pp, uw, attn, kd_t, e_last = {}, {}, {}, {}, {}, {}
    for h, n in probs:
        sl = slice(n * ch, (n + 1) * ch)
        g_b = g_all[h][sl]
        g_cols = jnp.concatenate([g_b, g_b], axis=0).T[:ch, :]
        diff = g_b - g_cols
        decay[h, n] = jnp.where(tril, jnp.exp(jnp.where(tril, diff, 0.0)), 0.0)
        g_last = g_b[ch - 1:ch, :]
        e_last[h, n] = jnp.exp(g_last)
        kd_t[h, n] = (k_all[h][sl] * jnp.exp(g_last - g_b)).T.astype(BF16)
    for h, n in probs:
        sl = slice(n * ch, (n + 1) * ch)
        kb_hi, kb_lo = _split(kbeta_all[h][sl])
        k_hi, k_lo = _split(jnp.concatenate([k_all[h][sl], k_all[h][sl]], axis=0))
        kk = _dot_nt(jnp.concatenate([kb_hi, kb_hi, kb_lo, kb_lo], axis=1),
                     jnp.concatenate([k_hi, k_lo, k_hi, k_lo], axis=1))
        pp[h, n] = -jnp.where(strict, kk * decay[h, n], 0.0)
        attn[h, n] = _dot_nt(q_all[h][sl].astype(BF16), k_all[h][sl].astype(BF16)) * decay[h, n][:, :ch]
    for stage in range(int(math.log2(ch))):
        for h, n in probs:
            p_hi, p_lo = _split(pp[h, n])
            lhs = jnp.concatenate([p_hi, p_lo], axis=1)
            y = rhs_all[h][n * ch:(n + 1) * ch] if stage == 0 else uw[h, n]
            y_hi, y_lo = _split(y)
            uw[h, n] = y + jnp.dot(lhs, jnp.concatenate([y_hi, y_lo, y_hi, y_lo], axis=0),
                                   preferred_element_type=F32)
            if stage + 1 < int(math.log2(ch)):
                pp[h, n] = jnp.dot(lhs, jnp.concatenate([p_hi, p_lo, p_hi, p_lo], axis=0),
                                   preferred_element_type=F32)
    s = {h: s_scr[h] for h in heads}
    o_parts = {h: [] for h in heads}
    for n in range(n_sub):
        for h in heads:
            sl = slice(n * ch, (n + 1) * ch)
            u, w = uw[h, n][:, :GDN_DV], uw[h, n][:, GDN_DV:]
            ws_qs = _bdot(jnp.concatenate([w, qe_all[h][sl]], axis=0), s[h])
            v_new = u - ws_qs[:ch]
            o_parts[h].append(ws_qs[ch:] + _bdot(attn[h, n], v_new))
            s[h] = s[h] * e_last[h, n] + jnp.dot(kd_t[h, n], v_new.astype(BF16), preferred_element_type=F32)
    norm_w = nw_ref[...]
    gz = gz_ref[...]
    outs = []
    for h in heads:
        s_scr[h] = s[h]
        o = jnp.concatenate(o_parts[h], axis=0)
        outs.append(_rms(o) * norm_w * _silu(gz[:, h * GDN_DV:(h + 1) * GDN_DV]))
    y_ref[...] = jnp.concatenate(outs, axis=1)

    @pl.when(c == pl.num_programs(1) - 1)
    def _():
        st_ref[0] = s_scr[...]


def _gdn_prompt(gqkv, gab, gz, alog_row, dtb_row, norm_w, layer, nb, t):
    ch = GDN_STEP_CHUNKS * GDN_CHUNK if t % (GDN_STEP_CHUNKS * GDN_CHUNK) == 0 else GDN_CHUNK
    nc = t // ch
    row = lambda w: pl.BlockSpec((ch, w), lambda b, c: (b * nc + c, 0))
    return pl.pallas_call(
        _gdn_kernel,
        grid=(nb, nc),
        in_specs=[row(GDN_QKV), row(LANE), row(GDN_HEADS * GDN_DV)]
        + [_layer_spec(a, layer) for a in (alog_row, dtb_row, norm_w)],
        out_specs=[
            row(GDN_HEADS * GDN_DV),
            pl.BlockSpec((1, GDN_HEADS, GDN_DK, GDN_DV), lambda b, c: (b, 0, 0, 0)),
        ],
        out_shape=[
            jax.ShapeDtypeStruct((nb * t, GDN_HEADS * GDN_DV), F32),
            jax.ShapeDtypeStruct((nb, GDN_HEADS, GDN_DK, GDN_DV), F32),
        ],
        scratch_shapes=[pltpu.VMEM((GDN_HEADS, GDN_DK, GDN_DV), F32)],
        compiler_params=_cparams("parallel", "arbitrary"),
        name="gdn",
    )(gqkv, gab, gz, alog_row, dtb_row, norm_w)


SSM_BLK = 4
SSM_BLK_STATE = SSM_STATE // SSM_BLK


def _s5_params(lam_re, lam_im, log_step, b_re, b_im, c_re, c_im):
    step = jnp.exp(log_step.astype(F32))[:, None]
    mag = jnp.exp(lam_re * step)
    ab_re = mag * jnp.cos(lam_im * step)
    ab_im = mag * jnp.sin(lam_im * step)
    den = lam_re * lam_re + lam_im * lam_im
    nr = ab_re - 1.0
    f_re = (nr * lam_re + ab_im * lam_im) / den
    f_im = (ab_im * lam_re - nr * lam_im) / den
    bb_re = f_re[..., None] * b_re - f_im[..., None] * b_im
    bb_im = f_re[..., None] * b_im + f_im[..., None] * b_re
    gpb = SSM_GROUPS // SSM_BLK
    eye = jnp.eye(gpb, dtype=F32)

    def in_mat(bb):
        bb = bb.reshape(SSM_BLK, gpb, SSM_P, SSM_GROUP)
        m = jnp.einsum("jgpc,gk->jgckp", bb, eye)
        return m.reshape(SSM_BLK, gpb * SSM_GROUP, gpb * SSM_P).astype(BF16)

    def out_mat(cc):
        cc = cc.reshape(SSM_BLK, gpb, SSM_GROUP, SSM_P)
        m = jnp.einsum("jgcp,gk->jgpkc", cc, eye)
        return m.reshape(SSM_BLK, gpb * SSM_P, gpb * SSM_GROUP).astype(BF16)

    return (ab_re.reshape(1, SSM_STATE), ab_im.reshape(1, SSM_STATE),
            in_mat(bb_re), in_mat(bb_im), out_mat(c_re), out_mat(c_im))


def _s5_readout(u, h_re, h_im, cre_ref, cim_ref, d_row, gw_ref, gb_row):
    ys = []
    for j in range(SSM_BLK):
        sl = slice(j * SSM_BLK_STATE, (j + 1) * SSM_BLK_STATE)
        ys.append(_bdot(h_re[:, sl], cre_ref[j]) - _bdot(h_im[:, sl], cim_ref[j]))
    y = jax.nn.gelu(jnp.concatenate(ys, axis=1) + d_row * u)
    return y * _sigmoid(_bdot(y, gw_ref[...]) + gb_row)


def _scan_mixers_kernel(u_ref, x_ref, g_ref,
                        are_ref, aim_ref, bre_ref, bim_ref, cre_ref, cim_ref, d_ref, gw_ref, gb_ref,
                        cw_ref, cb_ref, wa_ref, wx_ref, ba_ref, bx_ref, lam_ref,
                        y_ref, hre_ref, him_ref, yd_ref, h_out, cv_out,
                        bu_re, bu_im, h_re, h_im, xbuf, a_buf, b_buf, h_scr, *, nb):
    i = pl.program_id(0)
    tc = u_ref.shape[1]
    rows = tc * nb
    tail = (CONV_W - 1) * nb

    @pl.when(i == 0)
    def _():
        h_re[...] = jnp.zeros(h_re.shape, F32)
        h_im[...] = jnp.zeros(h_im.shape, F32)
        xbuf[0:tail, :] = jnp.zeros((tail, LRU_WIDTH), F32)
        h_scr[...] = jnp.zeros(h_scr.shape, F32)

    u = _to_time_major(u_ref[...])
    for j in range(SSM_BLK):
        uj = u[:, j * LANE:(j + 1) * LANE]
        sl = slice(j * SSM_BLK_STATE, (j + 1) * SSM_BLK_STATE)
        bu_re[:, sl] = _bdot(uj, bre_ref[j])
        bu_im[:, sl] = _bdot(uj, bim_ref[j])
    x = _to_time_major(x_ref[...])
    xbuf[tail:tail + rows, :] = x
    conv = cw_ref[CONV_W - 1:CONV_W, :] * x
    for j in range(CONV_W - 1):
        conv = conv + cw_ref[j:j + 1, :] * xbuf[j * nb:j * nb + rows, :]
    cv_out[...] = xbuf[rows:rows + tail, :].reshape(CONV_W - 1, nb, LRU_WIDTH)
    xbuf[0:tail, :] = x[rows - tail:, :]
    a, b = _lru_gates(conv + cb_ref[...], wa_ref, wx_ref, ba_ref[...], bx_ref[...], lam_ref[...])
    a_buf[...] = a
    b_buf[...] = b

    for j in range(SSM_BLK):
        sl = slice(j * SSM_BLK_STATE, (j + 1) * SSM_BLK_STATE)
        a_re = jnp.broadcast_to(are_ref[:, sl], (nb, SSM_BLK_STATE))
        a_im = jnp.broadcast_to(aim_ref[:, sl], (nb, SSM_BLK_STATE))

        def body(t, carry):
            hr, hi = carry
            r = pl.ds(pl.multiple_of(t * nb, nb), nb)
            nr = a_re * hr - a_im * hi + bu_re[r, sl]
            ni = a_re * hi + a_im * hr + bu_im[r, sl]
            bu_re[r, sl] = nr
            bu_im[r, sl] = ni
            return nr, ni

        hr, hi = lax.fori_loop(0, tc, body, (h_re[:, sl], h_im[:, sl]), unroll=4)
        h_re[:, sl] = hr
        h_im[:, sl] = hi

    def lru_body(t, h):
        r = pl.ds(pl.multiple_of(t * nb, nb), nb)
        h = a_buf[r, :] * h + b_buf[r, :]
        b_buf[r, :] = h
        return h

    h = lax.fori_loop(0, tc, lru_body, h_scr[...], unroll=8)
    h_scr[...] = h
    h_out[...] = h

    y = _s5_readout(u, bu_re[...], bu_im[...], cre_ref, cim_ref, d_ref[...], gw_ref, gb_ref[...])
    y_ref[...] = _from_time_major(y, nb)
    hre_ref[...] = h_re[...]
    him_ref[...] = h_im[...]
    yd = b_buf[...] * jax.nn.gelu(_to_time_major(g_ref[...]))
    yd_ref[...] = _from_time_major(yd, nb)


def _scan_mixers(u_t, x_t, g_t, s5_w, lru_w, layer, tc):
    nb, t, _ = u_t.shape
    rows = tc * nb
    blk = pl.BlockSpec((nb, tc, SSM_WIDTH), lambda i: (0, i, 0))
    st = pl.BlockSpec((nb, SSM_STATE), lambda i: (0, 0))
    sds = lambda *s: jax.ShapeDtypeStruct(s, F32)
    return pl.pallas_call(
        functools.partial(_scan_mixers_kernel, nb=nb),
        grid=(t // tc,),
        in_specs=[blk, blk, blk] + [_layer_spec(a, layer) for a in (*s5_w, *lru_w)],
        out_specs=[blk, st, st, blk,
                   pl.BlockSpec((nb, LRU_WIDTH), lambda i: (0, 0)),
                   pl.BlockSpec((CONV_W - 1, nb, LRU_WIDTH), lambda i: (0, 0, 0))],
        out_shape=[sds(nb, t, SSM_WIDTH), sds(nb, SSM_STATE), sds(nb, SSM_STATE),
                   sds(nb, t, LRU_WIDTH), sds(nb, LRU_WIDTH), sds(CONV_W - 1, nb, LRU_WIDTH)],
        scratch_shapes=[pltpu.VMEM((rows, SSM_STATE), F32), pltpu.VMEM((rows, SSM_STATE), F32),
                        pltpu.VMEM((nb, SSM_STATE), F32), pltpu.VMEM((nb, SSM_STATE), F32),
                        pltpu.VMEM(((CONV_W - 1) * nb + rows, LRU_WIDTH), F32),
                        pltpu.VMEM((rows, LRU_WIDTH), F32), pltpu.VMEM((rows, LRU_WIDTH), F32),
                        pltpu.VMEM((nb, LRU_WIDTH), F32)],
        compiler_params=_cparams("arbitrary"),
        name="scan_mixers",
    )(u_t, x_t, g_t, *s5_w, *lru_w)


def _lru_gates(cx, wa_ref, wx_ref, ba_row, bx_row, lam_row):
    r = _sigmoid(_bdot(cx, wa_ref[...]) + ba_row)
    i = _sigmoid(_bdot(cx, wx_ref[...]) + bx_row)
    log_a = -LRU_C * r * _softplus(-lam_row)
    a = jnp.exp(log_a)
    th = jnp.tanh(log_a)
    b = jnp.sqrt(-2.0 * th / (1.0 - th)) * (i * cx)
    return a, b


def _merge_kernel(x_ref, sh_ref, sc_ref, gt_ref, ya_ref, yb_ref, yc_ref, yd_ref, wmg_ref, wb_ref, wo_ref,
                  g_ref, b_ref, o_ref, *, alpha, parts):
    ks = range(parts)
    xs, shs, scs, gts = (_row_parts(r, parts) for r in (x_ref, sh_ref, sc_ref, gt_ref))
    hs = [(_ln(xs[k]) * (1.0 + scs[k]) + shs[k]).astype(BF16) for k in ks]
    accs = [jnp.zeros(xs[k].shape, F32) for k in ks]
    for n, y_ref in enumerate((ya_ref, yb_ref, yc_ref, yd_ref)):
        ys = _row_parts(y_ref, parts)
        logits = [jnp.dot(hs[k], wmg_ref[:, n * D_MODEL:(n + 1) * D_MODEL], preferred_element_type=F32)
                  for k in ks]
        accs = [accs[k] + _sigmoid(logits[k]) * _bdot(ys[k], wb_ref[n]) for k in ks]
    outs = [_bdot(accs[k], wo_ref[...]) for k in ks]
    hm = x_ref.shape[0] // parts
    for k in ks:
        z = alpha * xs[k] + gts[k] * outs[k]
        o_ref[k * hm:(k + 1) * hm, :] = _ln(z) * g_ref[...] + b_ref[...]


def _merge(x, mod, ya, yb, yc, yd, w_mg, w_branch, w_out, ln_g, ln_b, alpha, tm):
    n = x.shape[0]
    l = mod.layer
    row = lambda w: pl.BlockSpec((tm, w), lambda i: (i, 0))
    return pl.pallas_call(
        functools.partial(_merge_kernel, alpha=alpha, parts=2 if tm >= 512 else 1),
        grid=(n // tm,),
        in_specs=[
            row(D_MODEL), mod.spec(1, 0, tm), mod.spec(1, 1, tm), mod.spec(1, 2, tm),
            row(BRANCH_W), row(BRANCH_W), row(BRANCH_W), row(BRANCH_W),
            _resident((None, D_MODEL, MG_WIDTH), lambda i: (l, 0, 0)),
            _resident((None, N_BRANCH, BRANCH_W, D_MODEL), lambda i: (l, 0, 0, 0)),
            _resident((None, D_MODEL, D_MODEL), lambda i: (l, 0, 0)),
            pl.BlockSpec((None, None, 1, D_MODEL), lambda i: (l, 1, 0, 0)),
            pl.BlockSpec((None, None, 1, D_MODEL), lambda i: (l, 1, 0, 0)),
        ],
        out_specs=row(D_MODEL),
        out_shape=jax.ShapeDtypeStruct((n, D_MODEL), F32),
        compiler_params=_cparams("parallel"),
        name="merge",
    )(x, mod.arr, mod.arr, mod.arr, ya, yb, yc, yd, w_mg, w_branch, w_out, ln_g, ln_b)


def _smix1_kernel(ret_ref, gqkv_ref, gab_ref, su_ref, lx_ref, cos_ref, sin_ref,
                  gcv_ref, sre_ref, sim_ref, lru_ref, lcv_ref,
                  gcw_ref, alog_ref, dtb_ref,
                  are_ref, aim_ref, bre_ref, bim_ref, cre_ref, cim_ref, d_ref, gw_ref, gb_ref,
                  lcw_ref, lcb_ref, wa_ref, wx_ref, ba_ref, bx_ref, lam_ref,
                  qkt_ref, gv_ref, gx_ref, yc_ref, lh_ref, gcv_out, sre_out, sim_out, lcv_out):
    ret = ret_ref[...]
    rq = _rotate(ret[:, 0:RET_QK], cos_ref[...], sin_ref[...])
    rk = _rotate(ret[:, RET_QK:2 * RET_QK], cos_ref[...], sin_ref[...]) * (RET_DK ** -0.5)

    x = gqkv_ref[...]
    conv = gcw_ref[3:4, :] * x
    for j in range(CONV_W - 1):
        conv = conv + gcw_ref[j:j + 1, :] * gcv_ref[j]
    gcv_out[0] = gcv_ref[1]
    gcv_out[1] = gcv_ref[2]
    gcv_out[2] = x
    cq = _gdn_activate(conv)
    gq, gk = cq[:, :GDN_K], cq[:, GDN_K:2 * GDN_K]
    gv_ref[...] = cq[:, 2 * GDN_K:]
    gl = _gate_lanes(gab_ref[...], alog_ref[...], dtb_ref[...])
    gx = jnp.concatenate([jnp.broadcast_to(gl[:, r:r + 1], (gl.shape[0], LANE))
                          for r in range(2 * GDN_HEADS)], axis=1)
    gx_ref[...] = jnp.where(_iota(gx.shape, 1) < GDN_HEADS * LANE, jnp.exp(gx), gx)
    qkt_ref[...] = jnp.concatenate([rq, rk, gq, gk], axis=1).T

    u = su_ref[...]
    h_re, h_im = [], []
    for j in range(SSM_BLK):
        sl = slice(j * SSM_BLK_STATE, (j + 1) * SSM_BLK_STATE)
        uj = u[:, j * LANE:(j + 1) * LANE]
        a_re, a_im = are_ref[:, sl], aim_ref[:, sl]
        p_re, p_im = sre_ref[:, sl], sim_ref[:, sl]
        h_re.append(a_re * p_re - a_im * p_im + _bdot(uj, bre_ref[j]))
        h_im.append(a_re * p_im + a_im * p_re + _bdot(uj, bim_ref[j]))
    h_re = jnp.concatenate(h_re, axis=1)
    h_im = jnp.concatenate(h_im, axis=1)
    sre_out[...] = h_re
    sim_out[...] = h_im
    yc_ref[...] = _s5_readout(u, h_re, h_im, cre_ref, cim_ref, d_ref[...], gw_ref, gb_ref[...])

    lx = lx_ref[...]
    conv = lcw_ref[3:4, :] * lx
    for j in range(CONV_W - 1):
        conv = conv + lcw_ref[j:j + 1, :] * lcv_ref[j]
    lcv_out[0] = lcv_ref[1]
    lcv_out[1] = lcv_ref[2]
    lcv_out[2] = lx
    a, b = _lru_gates(conv + lcb_ref[...], wa_ref, wx_ref, ba_ref[...], bx_ref[...], lam_ref[...])
    lh_ref[...] = a * lru_ref[...] + b


def _smix1(pieces, cos_row, sin_row, states, gdn_w, s5_w, lru_w, layer):
    rows = pieces["ret"].shape[0]
    acts = [pieces["ret"], pieces["gqkv"], pieces["gab"], pieces["su"], pieces["lx"], cos_row, sin_row]
    stacked = [*states, *gdn_w, *s5_w, *lru_w]
    ins = acts + stacked
    full = lambda a: pl.BlockSpec(a.shape, lambda i: (0,) * a.ndim)
    sds = lambda *s: jax.ShapeDtypeStruct(s, F32)
    out_shape = [sds(QKT_ROWS, rows), sds(rows, GDN_HEADS * GDN_DV), sds(rows, 8 * LANE),
                 sds(rows, SSM_WIDTH), sds(rows, LRU_WIDTH),
                 sds(CONV_W - 1, rows, GDN_QKV), sds(rows, SSM_STATE), sds(rows, SSM_STATE),
                 sds(CONV_W - 1, rows, LRU_WIDTH)]
    return pl.pallas_call(
        _smix1_kernel,
        grid=(1,),
        in_specs=[full(a) for a in acts] + [_layer_spec(a, layer) for a in stacked],
        out_specs=[pl.BlockSpec(s.shape, lambda i, nd=len(s.shape): (0,) * nd) for s in out_shape],
        out_shape=out_shape,
        compiler_params=_cparams("arbitrary"),
        name="sample_mix",
    )(*ins)


def _smix2_kernel(qkt_ref, rv_ref, gv_ref, gx_ref, rg_ref, gz_ref, lh_ref, lg_ref, nw_ref, sret_ref, sgdn_ref,
                  *refs):
    ya_ref, yb_ref, yd_ref, nret_ref, ngdn_ref, o_ret, o_gdn = refs[-7:]
    i = pl.program_id(0)
    bt = rv_ref.shape[0]
    rows = qkt_ref.shape[1]
    gammas = _ret_gammas()
    q_hi, q_lo = _split(qkt_ref[...])
    qk_split = jnp.concatenate([q_hi, q_lo], axis=1)

    for j in range(bt):
        b = i * bt + j
        onehot = jnp.where(_iota((2 * rows, LANE), 0) % rows == b, 1.0, 0.0).astype(BF16)
        cols = jnp.dot(qk_split, onehot, preferred_element_type=F32)
        r = slice(j, j + 1)
        for h in range(RET_HEADS):
            q_c = cols[h * RET_DK:(h + 1) * RET_DK]
            k_c = cols[RET_QK + h * RET_DK:RET_QK + (h + 1) * RET_DK]
            v_r = rv_ref[r, h * RET_DV:(h + 1) * RET_DV]
            s_new = gammas[h] * sret_ref[j, h] + k_c * v_r
            nret_ref[j, h] = s_new
            o_ret[r, h * RET_DV:(h + 1) * RET_DV] = jnp.sum(q_c * s_new, axis=0, keepdims=True)
        base = 2 * RET_QK
        for h in range(GDN_HEADS):
            q_c = cols[base + h * GDN_DK:base + (h + 1) * GDN_DK]
            k_c = cols[base + GDN_K + h * GDN_DK:base + GDN_K + (h + 1) * GDN_DK]
            v_r = gv_ref[r, h * GDN_DV:(h + 1) * GDN_DV]
            e_g = gx_ref[r, h * LANE:(h + 1) * LANE]
            beta = gx_ref[r, (GDN_HEADS + h) * LANE:(GDN_HEADS + h + 1) * LANE]
            s = sgdn_ref[j, h]
            v_new = beta * (v_r - e_g * jnp.sum(k_c * s, axis=0, keepdims=True))
            s_new = e_g * s + k_c * v_new
            ngdn_ref[j, h] = s_new
            o_gdn[r, h * GDN_DV:(h + 1) * GDN_DV] = jnp.sum(q_c * s_new, axis=0, keepdims=True)
    rg, gz = rg_ref[...], gz_ref[...]
    o_r, o_g = o_ret[...], o_gdn[...]
    ya_ref[...] = jnp.concatenate(
        [_ln(o_r[:, h * RET_DV:(h + 1) * RET_DV]) for h in range(RET_HEADS)], axis=1) * _silu(rg)
    yb_ref[...] = jnp.concatenate(
        [_rms(o_g[:, h * GDN_DV:(h + 1) * GDN_DV]) * nw_ref[...] for h in range(GDN_HEADS)], axis=1) * _silu(gz)
    yd_ref[...] = lh_ref[...] * jax.nn.gelu(lg_ref[...])


def _smix2(qkt, rv, gv, gx, rg, gz, lh, lg, norm_w, s_ret, s_gdn, layer, bt, prev):
    rows = rv.shape[0]
    row = lambda w: pl.BlockSpec((bt, w), lambda i: (i, 0))
    sds = lambda *s: jax.ShapeDtypeStruct(s, F32)
    ret_blk = pl.BlockSpec((None, bt, RET_HEADS, RET_DK, RET_DV), lambda i: (layer, i, 0, 0, 0))
    gdn_blk = pl.BlockSpec((None, bt, GDN_HEADS, GDN_DK, GDN_DV), lambda i: (layer, i, 0, 0, 0))
    ins = [qkt, rv, gv, gx, rg, gz, lh, lg, norm_w, s_ret, s_gdn]
    in_specs = [pl.BlockSpec(qkt.shape, lambda i: (0, 0)),
                pl.BlockSpec((bt, RET_V), lambda i: (i, 1)),
                row(GDN_HEADS * GDN_DV), row(8 * LANE),
                pl.BlockSpec((bt, RET_V), lambda i: (i, 2)),
                row(GDN_HEADS * GDN_DV), row(LRU_WIDTH), row(LRU_WIDTH),
                _layer_spec(norm_w, layer), ret_blk, gdn_blk]
    aliases = {}
    if prev is not None:
        aliases = {len(ins): 3, len(ins) + 1: 4}
        ins += list(prev)
        in_specs += [pl.BlockSpec(memory_space=pl.ANY)] * 2
    return pl.pallas_call(
        _smix2_kernel,
        grid=(rows // bt,),
        in_specs=in_specs,
        out_specs=[row(RET_V), row(GDN_HEADS * GDN_DV), row(LRU_WIDTH), ret_blk, gdn_blk],
        out_shape=[sds(rows, RET_V), sds(rows, GDN_HEADS * GDN_DV), sds(rows, LRU_WIDTH),
                   sds(*s_ret.shape), sds(*s_gdn.shape)],
        input_output_aliases=aliases,
        scratch_shapes=[pltpu.VMEM((bt, RET_V), F32), pltpu.VMEM((bt, GDN_HEADS * GDN_DV), F32)],
        compiler_params=_cparams("parallel"),
        name="sample_state",
    )(*ins)


def _block_diag(w):
    nb, bs, _ = w.shape
    return jnp.einsum("nij,nm->nimj", w, jnp.eye(nb, dtype=w.dtype)).reshape(nb * bs, nb * bs)


def kernel(x_prompt, x_sample, c_prompt, c_sample, state_ret, state_gdn, state_gdn_conv, state_ssm_re, state_ssm_im, state_lru, state_lru_conv, w_ada, b_ada, ln_g, ln_b, w_ffn_up, w_ffn_down, w_in, gdn_conv_w, gdn_a_log, gdn_dt_bias, gdn_norm_w, ssm_lam_re, ssm_lam_im, ssm_log_step, ssm_b_re, ssm_b_im, ssm_c_re, ssm_c_im, ssm_d, ssm_glu_w, ssm_glu_b, lru_conv_w, lru_conv_b, lru_wa, lru_ba, lru_wx, lru_bx, lru_lam, w_branch, w_out):
    nb, t, _ = x_prompt.shape
    ns = x_sample.shape[0]
    depth = w_ada.shape[0]
    assert t % RET_CHUNK == 0 and x_sample.shape[1] == 1
    alpha = (2 * depth) ** 0.25
    tm = 256 if t % 256 == 0 else RET_CHUNK
    tm_ffn = 512 if t % 512 == 0 else tm
    tc = 64

    wup = w_ffn_up.astype(BF16)
    wdn = w_ffn_down.astype(BF16)
    wbr = w_branch.astype(BF16)
    wout = w_out.astype(BF16)
    w_packed, w_mg = _pack_w_in(w_in)
    ln_g4 = ln_g.reshape(depth, N_SUB, 1, D_MODEL)
    ln_b4 = ln_b.reshape(depth, N_SUB, 1, D_MODEL)

    mod_all = _ada(jnp.concatenate([c_prompt, c_sample], axis=0), w_ada, b_ada)
    mod_p = mod_all[:, :nb].reshape(depth, nb, 3 * N_SUB, 1, D_MODEL).transpose(0, 2, 1, 3, 4)
    mod_s = mod_all[:, nb:]

    row3 = lambda a: a.reshape(depth, 1, -1)
    lane_rows = lambda v: jnp.pad(v, ((0, 0), (0, LANE - v.shape[1]))).reshape(depth, 1, LANE)
    alog_rows, dtb_rows = lane_rows(gdn_a_log), lane_rows(gdn_dt_bias)
    norm_w = row3(gdn_norm_w)
    s5_w = (*jax.vmap(_s5_params)(ssm_lam_re, ssm_lam_im, ssm_log_step, ssm_b_re, ssm_b_im, ssm_c_re, ssm_c_im),
            row3(ssm_d), ssm_glu_w.astype(BF16), row3(ssm_glu_b))
    lru_w = (lru_conv_w, row3(lru_conv_b),
             jax.vmap(_block_diag)(lru_wa).astype(BF16), jax.vmap(_block_diag)(lru_wx).astype(BF16),
             row3(lru_ba), row3(lru_bx), row3(lru_lam))
    s_states = (state_gdn_conv.transpose(0, 2, 1, 3),
                state_ssm_re.reshape(depth, ns, SSM_STATE), state_ssm_im.reshape(depth, ns, SSM_STATE),
                state_lru, state_lru_conv.transpose(0, 2, 1, 3))

    cos_s, sin_s = _rope_tables(jnp.full((1,), PAST_LEN))
    xp = x_prompt.reshape(nb * t, D_MODEL)
    xs = x_sample.reshape(ns, D_MODEL)
    new_p, new_s = [], []
    big_s = None
    for l in range(depth):
        mp = _Mod(mod_p, l, False, t)
        ms = _Mod(mod_s, l, True)

        xp = _ffn(xp, mp, 0, 0, wup, wdn, ln_g4, ln_b4, alpha, tm_ffn)
        pc = _inproj(xp, mp, w_packed, tm, conv=(gdn_conv_w, nb, t))
        gcv_p = pc["gdn_conv"]
        ya, ret_p = _retention_prompt(pc["ret"], nb, t)
        yb, gdn_p = _gdn_prompt(pc["gqkv"], pc["gab"], pc["gz"], alog_rows, dtb_rows, norm_w, l, nb, t)
        to_t = lambda a: a.reshape(nb, t, -1)
        from_t = lambda a: a.reshape(nb * t, -1)
        yc_t, sre_p, sim_p, yd_t, lru_p, lcv_p = _scan_mixers(
            to_t(pc["su"]), to_t(pc["lx"]), to_t(pc["lg"]), s5_w, lru_w, l, tc)
        xp = _merge(xp, mp, ya, yb, from_t(yc_t), from_t(yd_t), w_mg, wbr, wout, ln_g4, ln_b4, alpha, tm_ffn)
        xp = _ffn(xp, mp, 2, 1, wup, wdn, ln_g4, ln_b4, alpha, tm_ffn)
        new_p.append((ret_p, gdn_p, gcv_p,
                      sre_p.reshape(nb, SSM_GROUPS, SSM_P), sim_p.reshape(nb, SSM_GROUPS, SSM_P),
                      lru_p, lcv_p.transpose(1, 0, 2)))

        xs = _ffn(xs, ms, 0, 0, wup, wdn, ln_g4, ln_b4, alpha, ns)
        sc = _inproj(xs, ms, w_packed, ns)
        (qkt, gv, gx, yc, lh, gcv_s, sre_s, sim_s, lcv_s) = _smix1(
            sc, cos_s, sin_s, s_states, (gdn_conv_w, alog_rows, dtb_rows), s5_w, lru_w, l)
        ya, yb, yd, *big_s = _smix2(qkt, sc["ret"], gv, gx, sc["ret"], sc["gz"], lh, sc["lg"],
                                    norm_w, state_ret, state_gdn, l, SUBLANE, big_s)
        xs = _merge(xs, ms, ya, yb, yc, yd, w_mg, wbr, wout, ln_g4, ln_b4, alpha, ns)
        xs = _ffn(xs, ms, 2, 1, wup, wdn, ln_g4, ln_b4, alpha, ns)
        new_s.append((gcv_s.transpose(1, 0, 2),
                      sre_s.reshape(ns, SSM_GROUPS, SSM_P), sim_s.reshape(ns, SSM_GROUPS, SSM_P),
                      lh, lcv_s.transpose(1, 0, 2)))

    ret_p, gdn_p, gcv_p, sre_p, sim_p, lru_p, lcv_p = [jnp.stack(z) for z in zip(*new_p)]
    gcv_s, sre_s, sim_s, lru_s, lcv_s = [jnp.stack(z) for z in zip(*new_s)]
    ret_s, gdn_s = big_s
    return (xp.reshape(nb, t, D_MODEL), xs.reshape(ns, 1, D_MODEL),
            ret_p, ret_s, gdn_p, gdn_s, gcv_p, gcv_s, sre_p, sre_s, sim_p, sim_s,
            lru_p, lru_s, lcv_p, lcv_s)
```

```python
import functools
import math

import numpy as np
import jax
import jax.numpy as jnp
from jax import lax
from jax.experimental import pallas as pl
from jax.experimental.pallas import tpu as pltpu

F32 = jnp.float32
BF16 = jnp.bfloat16

D_MODEL = 1024
RET_HEADS, RET_DK, RET_DV, RET_CHUNK = 4, 64, 128, 128
RET_STEP_CHUNKS = 4
ROPE_BASE = 10000.0
GDN_HEADS, GDN_DK, GDN_DV, GDN_CHUNK = 4, 128, 128, 64
GDN_STEP_CHUNKS = 4
GDN_QKV = 2 * GDN_HEADS * GDN_DK + GDN_HEADS * GDN_DV
CONV_W = 4
SSM_GROUP, SSM_GROUPS, SSM_P = 16, 32, 64
SSM_WIDTH = SSM_GROUP * SSM_GROUPS
SSM_STATE = SSM_GROUPS * SSM_P
LRU_WIDTH, LRU_BLOCKS = 512, 8
LRU_C = 8.0
N_BRANCH, BRANCH_W = 4, 512
D_FF = 2816
N_SUB = 3
LN_EPS = 1e-5
PAST_LEN = 16384

RET_QK = RET_HEADS * RET_DK
RET_V = RET_HEADS * RET_DV
RET_W = 2 * RET_QK + 2 * RET_V
GDN_K = GDN_HEADS * GDN_DK
QKT_ROWS = 2 * RET_QK + 2 * GDN_K
LANE = 128
SUBLANE = 8
FF_CHUNKS = ((0, 768), (768, 1024), (1792, 1024))
VMEM_LIMIT = 56 * 1024 * 1024


def _cparams(*sem):
    return pltpu.CompilerParams(dimension_semantics=sem, vmem_limit_bytes=VMEM_LIMIT)


def _layer_spec(a, layer):
    nd = a.ndim
    return pl.BlockSpec((None,) + a.shape[1:], lambda *_: (layer,) + (0,) * (nd - 1))


def _resident(shape, index_map):
    return pl.BlockSpec(shape, index_map, pipeline_mode=pl.Buffered(1))


def _ln(x):
    mu = jnp.mean(x, axis=-1, keepdims=True)
    xc = x - mu
    return xc * lax.rsqrt(jnp.mean(xc * xc, axis=-1, keepdims=True) + LN_EPS)


def _sigmoid(x):
    return 0.5 * jnp.tanh(0.5 * x) + 0.5


def _silu(x):
    return x * _sigmoid(x)


def _softplus(x):
    return jnp.maximum(x, 0.0) + jnp.log1p(jnp.exp(-jnp.abs(x)))


def _bdot(a, w):
    return jnp.dot(a.astype(BF16), w.astype(BF16), preferred_element_type=F32)


def _split(x):
    hi = x.astype(BF16)
    return hi, (x - hi.astype(F32)).astype(BF16)


def _dot_nt(a, b):
    return lax.dot_general(a, b, (((1,), (1,)), ((), ())), preferred_element_type=F32)


def _iota(shape, dim):
    return lax.broadcasted_iota(jnp.int32, shape, dim)


def _to_time_major(x):
    nb, tc, width = x.shape
    return jnp.swapaxes(x, 0, 1).reshape(tc * nb, width)


def _from_time_major(y, nb):
    rows, width = y.shape
    return jnp.swapaxes(y.reshape(rows // nb, nb, width), 0, 1)


def _ada_kernel(cp_ref, cs_ref, w_ref, b_ref, op_ref, os_ref):
    w = w_ref[...].astype(BF16)
    op_ref[...] = _bdot(_silu(cp_ref[...]), w) + b_ref[...]
    os_ref[...] = _bdot(_silu(cs_ref[...]), w) + b_ref[...]


def _ada(c_prompt, c_sample, w_ada, b_ada):
    depth, _, n_out = w_ada.shape
    tn = 1152
    cond = lambda c: pl.BlockSpec(c.shape, lambda l, j: (0, 0))
    out = lambda c: pl.BlockSpec((None, c.shape[0], tn), lambda l, j: (l, 0, j))
    return pl.pallas_call(
        _ada_kernel,
        grid=(depth, n_out // tn),
        in_specs=[
            cond(c_prompt), cond(c_sample),
            pl.BlockSpec((None, D_MODEL, tn), lambda l, j: (l, 0, j)),
            pl.BlockSpec((None, 1, tn), lambda l, j: (l, 0, j)),
        ],
        out_specs=[out(c_prompt), out(c_sample)],
        out_shape=[jax.ShapeDtypeStruct((depth, c.shape[0], n_out), F32) for c in (c_prompt, c_sample)],
        compiler_params=_cparams("parallel", "parallel"),
        name="ada",
    )(c_prompt, c_sample, w_ada, b_ada.reshape(depth, 1, n_out))


class _Mod:
    def __init__(self, arr, layer, per_row, rows_per_batch=None):
        self.arr, self.layer, self.per_row, self.rpb = arr, layer, per_row, rows_per_batch

    def spec(self, sub, which, tm):
        k, l = 3 * sub + which, self.layer
        if self.per_row:
            return pl.BlockSpec((None, tm, D_MODEL), lambda i: (l, i, k))
        tiles = self.rpb // tm
        return pl.BlockSpec((None, None, None, 1, D_MODEL), lambda i: (l, k, i // tiles, 0, 0))


def _row_parts(ref, parts):
    if ref.shape[0] == 1:
        return [ref[...]] * parts
    hm = ref.shape[0] // parts
    return [ref[k * hm:(k + 1) * hm, :] for k in range(parts)]


def _ffn_kernel(x_ref, sh_ref, sc_ref, gt_ref, wup_ref, wdn_ref, g_ref, b_ref, o_ref, *, alpha, parts):
    ks = range(parts)
    xs, shs, scs, gts = (_row_parts(r, parts) for r in (x_ref, sh_ref, sc_ref, gt_ref))
    hs = [(_ln(xs[k]) * (1.0 + scs[k]) + shs[k]).astype(BF16) for k in ks]
    accs = [jnp.zeros(xs[k].shape, F32) for k in ks]
    for lo, width in FF_CHUNKS:
        a = [jnp.dot(hs[k], wup_ref[:, lo:lo + width], preferred_element_type=F32) for k in ks]
        b = [jnp.dot(hs[k], wup_ref[:, D_FF + lo:D_FF + lo + width], preferred_element_type=F32) for k in ks]
        act = [(_silu(a[k]) * b[k]).astype(BF16) for k in ks]
        accs = [accs[k] + jnp.dot(act[k], wdn_ref[lo:lo + width, :], preferred_element_type=F32) for k in ks]
    hm = x_ref.shape[0] // parts
    for k in ks:
        z = alpha * xs[k] + 0.5 * gts[k] * accs[k]
        o_ref[k * hm:(k + 1) * hm, :] = _ln(z) * g_ref[...] + b_ref[...]


def _ffn(x, mod, sub, which, wup, wdn, ln_g, ln_b, alpha, tm):
    n = x.shape[0]
    l = mod.layer
    row = pl.BlockSpec((tm, D_MODEL), lambda i: (i, 0))
    return pl.pallas_call(
        functools.partial(_ffn_kernel, alpha=alpha, parts=2 if tm >= 512 else 1),
        grid=(n // tm,),
        in_specs=[
            row, mod.spec(sub, 0, tm), mod.spec(sub, 1, tm), mod.spec(sub, 2, tm),
            _resident((None, None, D_MODEL, 2 * D_FF), lambda i: (l, which, 0, 0)),
            _resident((None, None, D_FF, D_MODEL), lambda i: (l, which, 0, 0)),
            pl.BlockSpec((None, None, 1, D_MODEL), lambda i: (l, sub, 0, 0)),
            pl.BlockSpec((None, None, 1, D_MODEL), lambda i: (l, sub, 0, 0)),
        ],
        out_specs=row,
        out_shape=jax.ShapeDtypeStruct((n, D_MODEL), F32),
        compiler_params=_cparams("parallel"),
        name="ffn",
    )(x, mod.arr, mod.arr, mod.arr, wup, wdn, ln_g, ln_b)


_IN_PIECES = (("ret", RET_W), ("gqkv", GDN_QKV), ("gab", LANE), ("gz", 512), ("su", SSM_WIDTH),
              ("lx", LRU_WIDTH), ("lg", LRU_WIDTH))
IN_MAIN = sum(w for _, w in _IN_PIECES)
MG_WIDTH = N_BRANCH * D_MODEL


PACK_ROWS = 512


def _pack_kernel(wt_ref, main_ref, mg_ref):
    gate_lo = RET_W + GDN_QKV
    gate_hi = gate_lo + 2 * GDN_HEADS
    mg_lo = wt_ref.shape[0] - MG_WIDTH

    def move(dst_ref, dst, src, width):
        for c in range(0, width, PACK_ROWS):
            dst_ref[:, dst + c:dst + c + PACK_ROWS] = wt_ref[src + c:src + c + PACK_ROWS, :].T.astype(BF16)

    move(main_ref, 0, 0, gate_lo)
    g = wt_ref[gate_lo:gate_lo + LANE, :].T
    main_ref[:, gate_lo:gate_lo + LANE] = jnp.where(_iota(g.shape, 1) < 2 * GDN_HEADS, g, 0.0).astype(BF16)
    move(main_ref, gate_lo + LANE, gate_hi, mg_lo - gate_hi)
    move(mg_ref, 0, mg_lo, MG_WIDTH)


def _pack_w_in(w_in):
    depth, d_in, d_out = w_in.shape
    tk = 256
    blk = lambda w: pl.BlockSpec((None, tk, w), lambda l, i: (l, i, 0))
    return pl.pallas_call(
        _pack_kernel,
        grid=(depth, d_in // tk),
        in_specs=[pl.BlockSpec((None, d_out, tk), lambda l, i: (l, 0, i))],
        out_specs=[blk(IN_MAIN), blk(MG_WIDTH)],
        out_shape=[jax.ShapeDtypeStruct((depth, d_in, IN_MAIN), BF16),
                   jax.ShapeDtypeStruct((depth, d_in, MG_WIDTH), BF16)],
        compiler_params=_cparams("parallel", "parallel"),
        name="pack_w_in",
    )(jnp.swapaxes(w_in, 1, 2))


def _l2n(x):
    return x * lax.rsqrt(jnp.sum(x * x, axis=-1, keepdims=True) + 1e-6)


def _gdn_activate(conv):
    cq = _silu(conv)
    q = [_l2n(cq[:, h * GDN_DK:(h + 1) * GDN_DK]) * (GDN_DK ** -0.5) for h in range(GDN_HEADS)]
    k = [_l2n(cq[:, GDN_K + h * GDN_DK:GDN_K + (h + 1) * GDN_DK]) for h in range(GDN_HEADS)]
    return jnp.concatenate(q + k + [cq[:, 2 * GDN_K:]], axis=1)


def _inproj_kernel(x_ref, sh_ref, sc_ref, w_ref, *refs, seq_tiles):
    h = (_ln(x_ref[...]) * (1.0 + sc_ref[...]) + sh_ref[...]).astype(BF16)
    n_out = len(_IN_PIECES)
    if seq_tiles is None:
        o_refs = refs
    else:
        cw_ref, o_refs, cv_ref, xbuf = refs[0], refs[1:1 + n_out], refs[1 + n_out], refs[2 + n_out]

        @pl.when(pl.program_id(0) % seq_tiles == 0)
        def _():
            xbuf[0:SUBLANE, :] = jnp.zeros((SUBLANE, GDN_QKV), F32)

    offsets = np.cumsum([0] + [w for _, w in _IN_PIECES])
    order = sorted(range(n_out), key=lambda p: _IN_PIECES[p][0] != "gqkv")
    for p in order:
        (name, width), o_ref, lo = _IN_PIECES[p], o_refs[p], int(offsets[p])
        y = jnp.dot(h, w_ref[:, lo:lo + width], preferred_element_type=F32)
        if name == "gqkv" and seq_tiles is not None:
            tm = y.shape[0]
            xbuf[SUBLANE:SUBLANE + tm, :] = y
            conv = cw_ref[CONV_W - 1:CONV_W, :] * y
            for j in range(CONV_W - 1):
                r0 = SUBLANE - (CONV_W - 1) + j
                conv = conv + cw_ref[j:j + 1, :] * xbuf[r0:r0 + tm, :]
            cv_ref[0] = xbuf[SUBLANE + tm - (CONV_W - 1):SUBLANE + tm, :]
            xbuf[0:SUBLANE, :] = y[tm - SUBLANE:, :]
            y = _gdn_activate(conv)
        o_ref[...] = y


def _inproj(x, mod, w_packed, tm, conv=None):
    n = x.shape[0]
    l = mod.layer
    widths = [w for _, w in _IN_PIECES]
    ins = [x, mod.arr, mod.arr, w_packed]
    in_specs = [pl.BlockSpec((tm, D_MODEL), lambda i: (i, 0)),
                mod.spec(1, 0, tm), mod.spec(1, 1, tm),
                _resident((None, D_MODEL, IN_MAIN), lambda i: (l, 0, 0))]
    out_specs = [pl.BlockSpec((tm, w), lambda i: (i, 0)) for w in widths]
    out_shape = [jax.ShapeDtypeStruct((n, w), F32) for w in widths]
    scratch, seq_tiles = [], None
    if conv is not None:
        conv_w, nb, t = conv
        seq_tiles = t // tm
        ins.append(conv_w)
        in_specs.append(_layer_spec(conv_w, l))
        out_specs.append(pl.BlockSpec((1, CONV_W - 1, GDN_QKV), lambda i: (i // seq_tiles, 0, 0)))
        out_shape.append(jax.ShapeDtypeStruct((nb, CONV_W - 1, GDN_QKV), F32))
        scratch.append(pltpu.VMEM((SUBLANE + tm, GDN_QKV), F32))
    outs = pl.pallas_call(
        functools.partial(_inproj_kernel, seq_tiles=seq_tiles),
        grid=(n // tm,),
        in_specs=in_specs,
        out_specs=out_specs,
        out_shape=out_shape,
        scratch_shapes=scratch,
        compiler_params=_cparams("arbitrary"),
        name="inproj",
    )(*ins)
    return dict(zip([nm for nm, _ in _IN_PIECES] + ["gdn_conv"], outs))


def _rope_tables(pos):
    half = RET_DK // 2
    inv = ROPE_BASE ** (-jnp.arange(half, dtype=F32) / half)
    ang = pos.astype(F32)[:, None] * inv[None, :]
    cos, sin = jnp.cos(ang), jnp.sin(ang)
    cos_t = jnp.tile(jnp.concatenate([cos, cos], axis=-1), (1, RET_HEADS))
    sin_t = jnp.tile(jnp.concatenate([-sin, sin], axis=-1), (1, RET_HEADS))
    return cos_t, sin_t


def _rotate(z, cos, sin):
    half = RET_DK // 2
    first = (_iota(z.shape, 1) % RET_DK) < half
    swapped = jnp.where(first, pltpu.roll(z, RET_QK - half, 1), pltpu.roll(z, half, 1))
    return z * cos + swapped * sin


def _ret_gammas():
    return [1.0 - 2.0 ** (-5.0 - h) for h in range(RET_HEADS)]


def _ret_consts(chunk, n_sub):
    log_g = np.log(np.array(_ret_gammas(), np.float64))
    idx = np.arange(chunk, dtype=np.float64)
    rel = idx[:, None] - idx[None, :]
    dmask = np.where(rel[None] >= 0, np.exp(log_g[:, None, None] * np.maximum(rel, 0.0)[None]), 0.0)
    qdec = np.repeat(np.exp(log_g[None, :] * (idx[:, None] + 1.0)), RET_DK, axis=1)
    kdec = np.repeat(np.exp(log_g[None, :] * (chunk - 1.0 - idx[:, None])), RET_DK, axis=1)
    qdec, kdec = np.tile(qdec, (n_sub, 1)), np.tile(kdec, (n_sub, 1))
    head_r = np.arange(RET_QK) // RET_DK
    head_c = np.arange(RET_V) // RET_DV
    bd = (head_r[:, None] == head_c[None, :]).astype(np.float64)
    cd = bd * np.exp(log_g * chunk)[head_r][:, None]
    f = lambda a: jnp.asarray(a, F32)
    return f(dmask), f(qdec), f(kdec), f(cd), f(bd)


def _ret_kernel(x_ref, cos_ref, sin_ref, dmask_ref, qdec_ref, kdec_ref, cd_ref, bd_ref,
                y_ref, st_ref, s_scr):
    c = pl.program_id(1)
    ch = RET_CHUNK
    n_sub = x_ref.shape[0] // ch

    @pl.when(c == 0)
    def _():
        s_scr[...] = jnp.zeros(s_scr.shape, F32)

    x = x_ref[...]
    q = _rotate(x[:, 0:RET_QK], cos_ref[...], sin_ref[...])
    k = _rotate(x[:, RET_QK:2 * RET_QK], cos_ref[...], sin_ref[...]) * (RET_DK ** -0.5)
    v = x[:, 2 * RET_QK:2 * RET_QK + RET_V].astype(BF16)
    gate = x[:, 2 * RET_QK + RET_V:]
    q_dec = (q * qdec_ref[...]).astype(BF16)
    k_dec = k * kdec_ref[...]
    kb = k.astype(BF16)
    head = _iota((ch, RET_QK), 1) // RET_DK
    intra, upd = {}, {}
    for n in range(n_sub):
        sl = slice(n * ch, (n + 1) * ch)
        for h in range(RET_HEADS):
            qh = jnp.where(head == h, q[sl], 0.0).astype(BF16)
            inner = _dot_nt(qh, kb[sl]) * dmask_ref[h]
            intra[n, h] = _bdot(inner, v[sl, h * RET_DV:(h + 1) * RET_DV])
        upd[n] = jnp.dot(k_dec[sl].T.astype(BF16), v[sl], preferred_element_type=F32) * bd_ref[...]
    s = s_scr[...]
    cross = {}
    for n in range(n_sub):
        cross[n] = jnp.dot(q_dec[n * ch:(n + 1) * ch], s.astype(BF16), preferred_element_type=F32)
        s = s * cd_ref[...] + upd[n]
    s_scr[...] = s
    for n in range(n_sub):
        sl = slice(n * ch, (n + 1) * ch)
        outs = [_ln(intra[n, h] + cross[n][:, h * RET_DV:(h + 1) * RET_DV]) for h in range(RET_HEADS)]
        y_ref[sl, :] = jnp.concatenate(outs, axis=1) * _silu(gate[sl])

    @pl.when(c == pl.num_programs(1) - 1)
    def _():
        for h in range(RET_HEADS):
            st_ref[0, h] = s[h * RET_DK:(h + 1) * RET_DK, h * RET_DV:(h + 1) * RET_DV]


def _retention_prompt(ret, nb, t):
    n_sub = RET_STEP_CHUNKS if t % (RET_STEP_CHUNKS * RET_CHUNK) == 0 else 1
    chunk = n_sub * RET_CHUNK
    nc = t // chunk
    cos_t, sin_t = _rope_tables(jnp.arange(t))
    consts = _ret_consts(RET_CHUNK, n_sub)
    full = lambda a: pl.BlockSpec(a.shape, lambda b, c: (0,) * a.ndim)
    return pl.pallas_call(
        _ret_kernel,
        grid=(nb, nc),
        in_specs=[
            pl.BlockSpec((chunk, RET_W), lambda b, c: (b * nc + c, 0)),
            pl.BlockSpec((chunk, RET_QK), lambda b, c: (c, 0)),
            pl.BlockSpec((chunk, RET_QK), lambda b, c: (c, 0)),
        ] + [full(a) for a in consts],
        out_specs=[
            pl.BlockSpec((chunk, RET_V), lambda b, c: (b * nc + c, 0)),
            pl.BlockSpec((1, RET_HEADS, RET_DK, RET_DV), lambda b, c: (b, 0, 0, 0)),
        ],
        out_shape=[
            jax.ShapeDtypeStruct((nb * t, RET_V), F32),
            jax.ShapeDtypeStruct((nb, RET_HEADS, RET_DK, RET_DV), F32),
        ],
        scratch_shapes=[pltpu.VMEM((RET_QK, RET_V), F32)],
        compiler_params=_cparams("parallel", "arbitrary"),
        name="retention",
    )(ret, cos_t, sin_t, *consts)


def _rms(x):
    return x * lax.rsqrt(jnp.mean(x * x, axis=-1, keepdims=True) + 1e-6)


def _gate_lanes(gab, alog_row, dtb_row):
    g = -jnp.exp(alog_row) * _softplus(gab + dtb_row)
    return jnp.where(_iota(gab.shape, 1) < GDN_HEADS, g, _sigmoid(gab))


def _gdn_kernel(x_ref, gab_ref, gz_ref, alog_ref, dtb_ref, nw_ref, y_ref, st_ref, s_scr):
    c = pl.program_id(1)
    ch = GDN_CHUNK

    @pl.when(c == 0)
    def _():
        s_scr[...] = jnp.zeros(s_scr.shape, F32)

    rows = x_ref.shape[0]
    cq = x_ref[...]

    gl = _gate_lanes(gab_ref[...], alog_ref[...], dtb_ref[...])
    col = _iota((ch, 2 * ch), 1) % ch
    tril = _iota((ch, 2 * ch), 0) >= col
    strict = _iota((ch, 2 * ch), 0) > col
    is_g = _iota(gl.shape, 1) < GDN_HEADS
    r_i, c_i = _iota((rows, rows), 0), _iota((rows, rows), 1)
    chunk_tril = jnp.where((r_i >= c_i) & (r_i // ch == c_i // ch), 1.0, 0.0).astype(BF16)
    g_f32 = jnp.where(is_g, gl, 0.0)
    g_hi = g_f32.astype(BF16)
    g_rest = g_f32 - g_hi.astype(F32)
    g_mid, g_lo = _split(g_rest)
    csum = jnp.dot(jnp.concatenate([chunk_tril] * 3, axis=1), jnp.concatenate([g_hi, g_mid, g_lo], axis=0),
                   preferred_element_type=F32)
    gsel = jnp.where(is_g, csum, gl)
    n_sub = rows // ch
    heads = range(GDN_HEADS)
    probs = [(h, n) for h in heads for n in range(n_sub)]

    q_all, k_all, g_all, qe_all, rhs_all, kbeta_all = {}, {}, {}, {}, {}, {}
    for h in heads:
        q_all[h] = cq[:, h * GDN_DK:(h + 1) * GDN_DK]
        k_all[h] = cq[:, GDN_K + h * GDN_DK:GDN_K + (h + 1) * GDN_DK]
        v_h = cq[:, 2 * GDN_K + h * GDN_DV:2 * GDN_K + (h + 1) * GDN_DV]
        g_all[h] = jnp.broadcast_to(gsel[:, h:h + 1], (rows, LANE))
        b_h = jnp.broadcast_to(gsel[:, GDN_HEADS + h:GDN_HEADS + h + 1], (rows, LANE))
        e_h = jnp.exp(g_all[h])
        kbeta_all[h] = k_all[h] * b_h
        rhs_all[h] = jnp.concatenate([v_h * b_h, kbeta_all[h] * e_h], axis=1)
        qe_all[h] = q_all[h] * e_h

    decay, pp, uw, attn, kd_t, e_last = {}, {}, {}, {}, {}, {}
    for h, n in probs:
        sl = slice(n * ch, (n + 1) * ch)
        g_b = g_all[h][sl]
        g_cols = jnp.concatenate([g_b, g_b], axis=0).T[:ch, :]
        diff = g_b - g_cols
        decay[h, n] = jnp.where(tril, jnp.exp(jnp.where(tril, diff, 0.0)), 0.0)
        g_last = g_b[ch - 1:ch, :]
        e_last[h, n] = jnp.exp(g_last)
        kd_t[h, n] = (k_all[h][sl] * jnp.exp(g_last - g_b)).T.astype(BF16)
    for h, n in probs:
        sl = slice(n * ch, (n + 1) * ch)
        kb_hi, kb_lo = _split(kbeta_all[h][sl])
        k_hi, k_lo = _split(jnp.concatenate([k_all[h][sl], k_all[h][sl]], axis=0))
        kk = _dot_nt(jnp.concatenate([kb_hi, kb_hi, kb_lo, kb_lo], axis=1),
                     jnp.concatenate([k_hi, k_lo, k_hi, k_lo], axis=1))
        pp[h, n] = -jnp.where(strict, kk * decay[h, n], 0.0)
        attn[h, n] = _dot_nt(q_all[h][sl].astype(BF16), k_all[h][sl].astype(BF16)) * decay[h, n][:, :ch]
    for stage in range(int(math.log2(ch))):
        for h, n in probs:
            p_hi, p_lo = _split(pp[h, n])
            lhs = jnp.concatenate([p_hi, p_lo], axis=1)
            y = rhs_all[h][n * ch:(n + 1) * ch] if stage == 0 else uw[h, n]
            y_hi, y_lo = _split(y)
            uw[h, n] = y + jnp.dot(lhs, jnp.concatenate([y_hi, y_lo, y_hi, y_lo], axis=0),
                                   preferred_element_type=F32)
            if stage + 1 < int(math.log2(ch)):
                pp[h, n] = jnp.dot(lhs, jnp.concatenate([p_hi, p_lo, p_hi, p_lo], axis=0),
                                   preferred_element_type=F32)
    s = {h: s_scr[h] for h in heads}
    o_parts = {h: [] for h in heads}
    for n in range(n_sub):
        for h in heads:
            sl = slice(n * ch, (n + 1) * ch)
            u, w = uw[h, n][:, :GDN_DV], uw[h, n][:, GDN_DV:]
            ws_qs = _bdot(jnp.concatenate([w, qe_all[h][sl]], axis=0), s[h])
            v_new = u - ws_qs[:ch]
            o_parts[h].append(ws_qs[ch:] + _bdot(attn[h, n], v_new))
            s[h] = s[h] * e_last[h, n] + jnp.dot(kd_t[h, n], v_new.astype(BF16), preferred_element_type=F32)
    norm_w = nw_ref[...]
    gz = gz_ref[...]
    outs = []
    for h in heads:
        s_scr[h] = s[h]
        o = jnp.concatenate(o_parts[h], axis=0)
        outs.append(_rms(o) * norm_w * _silu(gz[:, h * GDN_DV:(h + 1) * GDN_DV]))
    y_ref[...] = jnp.concatenate(outs, axis=1)

    @pl.when(c == pl.num_programs(1) - 1)
    def _():
        st_ref[0] = s_scr[...]


def _gdn_prompt(gqkv, gab, gz, alog_row, dtb_row, norm_w, layer, nb, t):
    ch = GDN_STEP_CHUNKS * GDN_CHUNK if t % (GDN_STEP_CHUNKS * GDN_CHUNK) == 0 else GDN_CHUNK
    nc = t // ch
    row = lambda w: pl.BlockSpec((ch, w), lambda b, c: (b * nc + c, 0))
    return pl.pallas_call(
        _gdn_kernel,
        grid=(nb, nc),
        in_specs=[row(GDN_QKV), row(LANE), row(GDN_HEADS * GDN_DV)]
        + [_layer_spec(a, layer) for a in (alog_row, dtb_row, norm_w)],
        out_specs=[
            row(GDN_HEADS * GDN_DV),
            pl.BlockSpec((1, GDN_HEADS, GDN_DK, GDN_DV), lambda b, c: (b, 0, 0, 0)),
        ],
        out_shape=[
            jax.ShapeDtypeStruct((nb * t, GDN_HEADS * GDN_DV), F32),
            jax.ShapeDtypeStruct((nb, GDN_HEADS, GDN_DK, GDN_DV), F32),
        ],
        scratch_shapes=[pltpu.VMEM((GDN_HEADS, GDN_DK, GDN_DV), F32)],
        compiler_params=_cparams("parallel", "arbitrary"),
        name="gdn",
    )(gqkv, gab, gz, alog_row, dtb_row, norm_w)


SSM_BLK = 4
SSM_BLK_STATE = SSM_STATE // SSM_BLK


def _s5_params(lam_re, lam_im, log_step, b_re, b_im, c_re, c_im):
    step = jnp.exp(log_step.astype(F32))[:, None]
    mag = jnp.exp(lam_re * step)
    ab_re = mag * jnp.cos(lam_im * step)
    ab_im = mag * jnp.sin(lam_im * step)
    den = lam_re * lam_re + lam_im * lam_im
    nr = ab_re - 1.0
    f_re = (nr * lam_re + ab_im * lam_im) / den
    f_im = (ab_im * lam_re - nr * lam_im) / den
    bb_re = f_re[..., None] * b_re - f_im[..., None] * b_im
    bb_im = f_re[..., None] * b_im + f_im[..., None] * b_re
    gpb = SSM_GROUPS // SSM_BLK
    eye = jnp.eye(gpb, dtype=F32)

    def in_mat(bb):
        bb = bb.reshape(SSM_BLK, gpb, SSM_P, SSM_GROUP)
        m = jnp.einsum("jgpc,gk->jgckp", bb, eye)
        return m.reshape(SSM_BLK, gpb * SSM_GROUP, gpb * SSM_P).astype(BF16)

    def out_mat(cc):
        cc = cc.reshape(SSM_BLK, gpb, SSM_GROUP, SSM_P)
        m = jnp.einsum("jgcp,gk->jgpkc", cc, eye)
        return m.reshape(SSM_BLK, gpb * SSM_P, gpb * SSM_GROUP).astype(BF16)

    return (ab_re.reshape(1, SSM_STATE), ab_im.reshape(1, SSM_STATE),
            in_mat(bb_re), in_mat(bb_im), out_mat(c_re), out_mat(c_im))


def _s5_readout(u, h_re, h_im, cre_ref, cim_ref, d_row, gw_ref, gb_row):
    ys = []
    for j in range(SSM_BLK):
        sl = slice(j * SSM_BLK_STATE, (j + 1) * SSM_BLK_STATE)
        ys.append(_bdot(h_re[:, sl], cre_ref[j]) - _bdot(h_im[:, sl], cim_ref[j]))
    y = jax.nn.gelu(jnp.concatenate(ys, axis=1) + d_row * u)
    return y * _sigmoid(_bdot(y, gw_ref[...]) + gb_row)


def _scan_mixers_kernel(u_ref, x_ref, g_ref,
                        are_ref, aim_ref, bre_ref, bim_ref, cre_ref, cim_ref, d_ref, gw_ref, gb_ref,
                        cw_ref, cb_ref, wa_ref, wx_ref, ba_ref, bx_ref, lam_ref,
                        y_ref, hre_ref, him_ref, yd_ref, h_out, cv_out,
                        bu_re, bu_im, h_re, h_im, xbuf, a_buf, b_buf, h_scr, *, nb):
    i = pl.program_id(0)
    tc = u_ref.shape[1]
    rows = tc * nb
    tail = (CONV_W - 1) * nb

    @pl.when(i == 0)
    def _():
        h_re[...] = jnp.zeros(h_re.shape, F32)
        h_im[...] = jnp.zeros(h_im.shape, F32)
        xbuf[0:tail, :] = jnp.zeros((tail, LRU_WIDTH), F32)
        h_scr[...] = jnp.zeros(h_scr.shape, F32)

    u = _to_time_major(u_ref[...])
    for j in range(SSM_BLK):
        uj = u[:, j * LANE:(j + 1) * LANE]
        sl = slice(j * SSM_BLK_STATE, (j + 1) * SSM_BLK_STATE)
        bu_re[:, sl] = _bdot(uj, bre_ref[j])
        bu_im[:, sl] = _bdot(uj, bim_ref[j])
    x = _to_time_major(x_ref[...])
    xbuf[tail:tail + rows, :] = x
    conv = cw_ref[CONV_W - 1:CONV_W, :] * x
    for j in range(CONV_W - 1):
        conv = conv + cw_ref[j:j + 1, :] * xbuf[j * nb:j * nb + rows, :]
    cv_out[...] = xbuf[rows:rows + tail, :].reshape(CONV_W - 1, nb, LRU_WIDTH)
    xbuf[0:tail, :] = x[rows - tail:, :]
    a, b = _lru_gates(conv + cb_ref[...], wa_ref, wx_ref, ba_ref[...], bx_ref[...], lam_ref[...])
    a_buf[...] = a
    b_buf[...] = b

    for j in range(SSM_BLK):
        sl = slice(j * SSM_BLK_STATE, (j + 1) * SSM_BLK_STATE)
        a_re = jnp.broadcast_to(are_ref[:, sl], (nb, SSM_BLK_STATE))
        a_im = jnp.broadcast_to(aim_ref[:, sl], (nb, SSM_BLK_STATE))

        def body(t, carry):
            hr, hi = carry
            r = pl.ds(pl.multiple_of(t * nb, nb), nb)
            nr = a_re * hr - a_im * hi + bu_re[r, sl]
            ni = a_re * hi + a_im * hr + bu_im[r, sl]
            bu_re[r, sl] = nr
            bu_im[r, sl] = ni
            return nr, ni

        hr, hi = lax.fori_loop(0, tc, body, (h_re[:, sl], h_im[:, sl]), unroll=4)
        h_re[:, sl] = hr
        h_im[:, sl] = hi

    def lru_body(t, h):
        r = pl.ds(pl.multiple_of(t * nb, nb), nb)
        h = a_buf[r, :] * h + b_buf[r, :]
        b_buf[r, :] = h
        return h

    h = lax.fori_loop(0, tc, lru_body, h_scr[...], unroll=8)
    h_scr[...] = h
    h_out[...] = h

    y = _s5_readout(u, bu_re[...], bu_im[...], cre_ref, cim_ref, d_ref[...], gw_ref, gb_ref[...])
    y_ref[...] = _from_time_major(y, nb)
    hre_ref[...] = h_re[...]
    him_ref[...] = h_im[...]
    yd = b_buf[...] * jax.nn.gelu(_to_time_major(g_ref[...]))
    yd_ref[...] = _from_time_major(yd, nb)


def _scan_mixers(u_t, x_t, g_t, s5_w, lru_w, layer, tc):
    nb, t, _ = u_t.shape
    rows = tc * nb
    blk = pl.BlockSpec((nb, tc, SSM_WIDTH), lambda i: (0, i, 0))
    st = pl.BlockSpec((nb, SSM_STATE), lambda i: (0, 0))
    sds = lambda *s: jax.ShapeDtypeStruct(s, F32)
    return pl.pallas_call(
        functools.partial(_scan_mixers_kernel, nb=nb),
        grid=(t // tc,),
        in_specs=[blk, blk, blk] + [_layer_spec(a, layer) for a in (*s5_w, *lru_w)],
        out_specs=[blk, st, st, blk,
                   pl.BlockSpec((nb, LRU_WIDTH), lambda i: (0, 0)),
                   pl.BlockSpec((CONV_W - 1, nb, LRU_WIDTH), lambda i: (0, 0, 0))],
        out_shape=[sds(nb, t, SSM_WIDTH), sds(nb, SSM_STATE), sds(nb, SSM_STATE),
                   sds(nb, t, LRU_WIDTH), sds(nb, LRU_WIDTH), sds(CONV_W - 1, nb, LRU_WIDTH)],
        scratch_shapes=[pltpu.VMEM((rows, SSM_STATE), F32), pltpu.VMEM((rows, SSM_STATE), F32),
                        pltpu.VMEM((nb, SSM_STATE), F32), pltpu.VMEM((nb, SSM_STATE), F32),
                        pltpu.VMEM(((CONV_W - 1) * nb + rows, LRU_WIDTH), F32),
                        pltpu.VMEM((rows, LRU_WIDTH), F32), pltpu.VMEM((rows, LRU_WIDTH), F32),
                        pltpu.VMEM((nb, LRU_WIDTH), F32)],
        compiler_params=_cparams("arbitrary"),
        name="scan_mixers",
    )(u_t, x_t, g_t, *s5_w, *lru_w)


def _lru_gates(cx, wa_ref, wx_ref, ba_row, bx_row, lam_row):
    r = _sigmoid(_bdot(cx, wa_ref[...]) + ba_row)
    i = _sigmoid(_bdot(cx, wx_ref[...]) + bx_row)
    log_a = -LRU_C * r * _softplus(-lam_row)
    a = jnp.exp(log_a)
    th = jnp.tanh(log_a)
    b = jnp.sqrt(-2.0 * th / (1.0 - th)) * (i * cx)
    return a, b


def _merge_kernel(x_ref, sh_ref, sc_ref, gt_ref, ya_ref, yb_ref, yc_ref, yd_ref, wmg_ref, wb_ref, wo_ref,
                  g_ref, b_ref, o_ref, *, alpha, parts):
    ks = range(parts)
    xs, shs, scs, gts = (_row_parts(r, parts) for r in (x_ref, sh_ref, sc_ref, gt_ref))
    hs = [(_ln(xs[k]) * (1.0 + scs[k]) + shs[k]).astype(BF16) for k in ks]
    accs = [jnp.zeros(xs[k].shape, F32) for k in ks]
    for n, y_ref in enumerate((ya_ref, yb_ref, yc_ref, yd_ref)):
        ys = _row_parts(y_ref, parts)
        logits = [jnp.dot(hs[k], wmg_ref[:, n * D_MODEL:(n + 1) * D_MODEL], preferred_element_type=F32)
                  for k in ks]
        accs = [accs[k] + _sigmoid(logits[k]) * _bdot(ys[k], wb_ref[n]) for k in ks]
    outs = [_bdot(accs[k], wo_ref[...]) for k in ks]
    hm = x_ref.shape[0] // parts
    for k in ks:
        z = alpha * xs[k] + gts[k] * outs[k]
        o_ref[k * hm:(k + 1) * hm, :] = _ln(z) * g_ref[...] + b_ref[...]


def _merge(x, mod, ya, yb, yc, yd, w_mg, w_branch, w_out, ln_g, ln_b, alpha, tm):
    n = x.shape[0]
    l = mod.layer
    row = lambda w: pl.BlockSpec((tm, w), lambda i: (i, 0))
    return pl.pallas_call(
        functools.partial(_merge_kernel, alpha=alpha, parts=2 if tm >= 512 else 1),
        grid=(n // tm,),
        in_specs=[
            row(D_MODEL), mod.spec(1, 0, tm), mod.spec(1, 1, tm), mod.spec(1, 2, tm),
            row(BRANCH_W), row(BRANCH_W), row(BRANCH_W), row(BRANCH_W),
            _resident((None, D_MODEL, MG_WIDTH), lambda i: (l, 0, 0)),
            _resident((None, N_BRANCH, BRANCH_W, D_MODEL), lambda i: (l, 0, 0, 0)),
            _resident((None, D_MODEL, D_MODEL), lambda i: (l, 0, 0)),
            pl.BlockSpec((None, None, 1, D_MODEL), lambda i: (l, 1, 0, 0)),
            pl.BlockSpec((None, None, 1, D_MODEL), lambda i: (l, 1, 0, 0)),
        ],
        out_specs=row(D_MODEL),
        out_shape=jax.ShapeDtypeStruct((n, D_MODEL), F32),
        compiler_params=_cparams("parallel"),
        name="merge",
    )(x, mod.arr, mod.arr, mod.arr, ya, yb, yc, yd, w_mg, w_branch, w_out, ln_g, ln_b)


def _smix1_kernel(ret_ref, gqkv_ref, gab_ref, su_ref, lx_ref, cos_ref, sin_ref,
                  gcv_ref, sre_ref, sim_ref, lru_ref, lcv_ref,
                  gcw_ref, alog_ref, dtb_ref,
                  are_ref, aim_ref, bre_ref, bim_ref, cre_ref, cim_ref, d_ref, gw_ref, gb_ref,
                  lcw_ref, lcb_ref, wa_ref, wx_ref, ba_ref, bx_ref, lam_ref,
                  qkt_ref, gv_ref, gx_ref, yc_ref, lh_ref, gcv_out, sre_out, sim_out, lcv_out):
    ret = ret_ref[...]
    rq = _rotate(ret[:, 0:RET_QK], cos_ref[...], sin_ref[...])
    rk = _rotate(ret[:, RET_QK:2 * RET_QK], cos_ref[...], sin_ref[...]) * (RET_DK ** -0.5)

    x = gqkv_ref[...]
    conv = gcw_ref[3:4, :] * x
    for j in range(CONV_W - 1):
        conv = conv + gcw_ref[j:j + 1, :] * gcv_ref[j]
    gcv_out[0] = gcv_ref[1]
    gcv_out[1] = gcv_ref[2]
    gcv_out[2] = x
    cq = _gdn_activate(conv)
    gq, gk = cq[:, :GDN_K], cq[:, GDN_K:2 * GDN_K]
    gv_ref[...] = cq[:, 2 * GDN_K:]
    gl = _gate_lanes(gab_ref[...], alog_ref[...], dtb_ref[...])
    gx = jnp.concatenate([jnp.broadcast_to(gl[:, r:r + 1], (gl.shape[0], LANE))
                          for r in range(2 * GDN_HEADS)], axis=1)
    gx_ref[...] = jnp.where(_iota(gx.shape, 1) < GDN_HEADS * LANE, jnp.exp(gx), gx)
    qkt_ref[...] = jnp.concatenate([rq, rk, gq, gk], axis=1).T

    u = su_ref[...]
    h_re, h_im = [], []
    for j in range(SSM_BLK):
        sl = slice(j * SSM_BLK_STATE, (j + 1) * SSM_BLK_STATE)
        uj = u[:, j * LANE:(j + 1) * LANE]
        a_re, a_im = are_ref[:, sl], aim_ref[:, sl]
        p_re, p_im = sre_ref[:, sl], sim_ref[:, sl]
        h_re.append(a_re * p_re - a_im * p_im + _bdot(uj, bre_ref[j]))
        h_im.append(a_re * p_im + a_im * p_re + _bdot(uj, bim_ref[j]))
    h_re = jnp.concatenate(h_re, axis=1)
    h_im = jnp.concatenate(h_im, axis=1)
    sre_out[...] = h_re
    sim_out[...] = h_im
    yc_ref[...] = _s5_readout(u, h_re, h_im, cre_ref, cim_ref, d_ref[...], gw_ref, gb_ref[...])

    lx = lx_ref[...]
    conv = lcw_ref[3:4, :] * lx
    for j in range(CONV_W - 1):
        conv = conv + lcw_ref[j:j + 1, :] * lcv_ref[j]
    lcv_out[0] = lcv_ref[1]
    lcv_out[1] = lcv_ref[2]
    lcv_out[2] = lx
    a, b = _lru_gates(conv + lcb_ref[...], wa_ref, wx_ref, ba_ref[...], bx_ref[...], lam_ref[...])
    lh_ref[...] = a * lru_ref[...] + b


def _smix1(pieces, cos_row, sin_row, states, gdn_w, s5_w, lru_w, layer):
    rows = pieces["ret"].shape[0]
    acts = [pieces["ret"], pieces["gqkv"], pieces["gab"], pieces["su"], pieces["lx"], cos_row, sin_row]
    stacked = [*states, *gdn_w, *s5_w, *lru_w]
    ins = acts + stacked
    full = lambda a: pl.BlockSpec(a.shape, lambda i: (0,) * a.ndim)
    sds = lambda *s: jax.ShapeDtypeStruct(s, F32)
    out_shape = [sds(QKT_ROWS, rows), sds(rows, GDN_HEADS * GDN_DV), sds(rows, 8 * LANE),
                 sds(rows, SSM_WIDTH), sds(rows, LRU_WIDTH),
                 sds(CONV_W - 1, rows, GDN_QKV), sds(rows, SSM_STATE), sds(rows, SSM_STATE),
                 sds(CONV_W - 1, rows, LRU_WIDTH)]
    return pl.pallas_call(
        _smix1_kernel,
        grid=(1,),
        in_specs=[full(a) for a in acts] + [_layer_spec(a, layer) for a in stacked],
        out_specs=[pl.BlockSpec(s.shape, lambda i, nd=len(s.shape): (0,) * nd) for s in out_shape],
        out_shape=out_shape,
        compiler_params=_cparams("arbitrary"),
        name="sample_mix",
    )(*ins)


def _smix2_kernel(qkt_ref, rv_ref, gv_ref, gx_ref, rg_ref, gz_ref, lh_ref, lg_ref, nw_ref, sret_ref, sgdn_ref,
                  *refs, first):
    ya_ref, yb_ref, yd_ref, nret_ref, ngdn_ref, o_ret, o_gdn = refs[-7:]
    if first:
        for l in range(1, nret_ref.shape[0]):
            nret_ref[l] = jnp.zeros(nret_ref.shape[1:], F32)
            ngdn_ref[l] = jnp.zeros(ngdn_ref.shape[1:], F32)
        nret_ref, ngdn_ref = nret_ref.at[0], ngdn_ref.at[0]
    i = pl.program_id(0)
    bt = rv_ref.shape[0]
    rows = qkt_ref.shape[1]
    gammas = _ret_gammas()
    q_hi, q_lo = _split(qkt_ref[...])
    qk_split = jnp.concatenate([q_hi, q_lo], axis=1)

    for j in range(bt):
        b = i * bt + j
        onehot = jnp.where(_iota((2 * rows, LANE), 0) % rows == b, 1.0, 0.0).astype(BF16)
        cols = jnp.dot(qk_split, onehot, preferred_element_type=F32)
        r = slice(j, j + 1)
        for h in range(RET_HEADS):
            q_c = cols[h * RET_DK:(h + 1) * RET_DK]
            k_c = cols[RET_QK + h * RET_DK:RET_QK + (h + 1) * RET_DK]
            v_r = rv_ref[r, h * RET_DV:(h + 1) * RET_DV]
            s_new = gammas[h] * sret_ref[j, h] + k_c * v_r
            nret_ref[j, h] = s_new
            o_ret[r, h * RET_DV:(h + 1) * RET_DV] = jnp.sum(q_c * s_new, axis=0, keepdims=True)
        base = 2 * RET_QK
        for h in range(GDN_HEADS):
            q_c = cols[base + h * GDN_DK:base + (h + 1) * GDN_DK]
            k_c = cols[base + GDN_K + h * GDN_DK:base + GDN_K + (h + 1) * GDN_DK]
            v_r = gv_ref[r, h * GDN_DV:(h + 1) * GDN_DV]
            e_g = gx_ref[r, h * LANE:(h + 1) * LANE]
            beta = gx_ref[r, (GDN_HEADS + h) * LANE:(GDN_HEADS + h + 1) * LANE]
            s = sgdn_ref[j, h]
            v_new = beta * (v_r - e_g * jnp.sum(k_c * s, axis=0, keepdims=True))
            s_new = e_g * s + k_c * v_new
            ngdn_ref[j, h] = s_new
            o_gdn[r, h * GDN_DV:(h + 1) * GDN_DV] = jnp.sum(q_c * s_new, axis=0, keepdims=True)
    rg, gz = rg_ref[...], gz_ref[...]
    o_r, o_g = o_ret[...], o_gdn[...]
    ya_ref[...] = jnp.concatenate(
        [_ln(o_r[:, h * RET_DV:(h + 1) * RET_DV]) for h in range(RET_HEADS)], axis=1) * _silu(rg)
    yb_ref[...] = jnp.concatenate(
        [_rms(o_g[:, h * GDN_DV:(h + 1) * GDN_DV]) * nw_ref[...] for h in range(GDN_HEADS)], axis=1) * _silu(gz)
    yd_ref[...] = lh_ref[...] * jax.nn.gelu(lg_ref[...])


def _smix2(qkt, rv, gv, gx, rg, gz, lh, lg, norm_w, s_ret, s_gdn, layer, bt, prev):
    rows = rv.shape[0]
    row = lambda w: pl.BlockSpec((bt, w), lambda i: (i, 0))
    sds = lambda *s: jax.ShapeDtypeStruct(s, F32)
    ret_blk = pl.BlockSpec((None, bt, RET_HEADS, RET_DK, RET_DV), lambda i: (layer, i, 0, 0, 0))
    gdn_blk = pl.BlockSpec((None, bt, GDN_HEADS, GDN_DK, GDN_DV), lambda i: (layer, i, 0, 0, 0))
    ins = [qkt, rv, gv, gx, rg, gz, lh, lg, norm_w, s_ret, s_gdn]
    in_specs = [pl.BlockSpec(qkt.shape, lambda i: (0, 0)),
                pl.BlockSpec((bt, RET_V), lambda i: (i, 1)),
                row(GDN_HEADS * GDN_DV), row(8 * LANE),
                pl.BlockSpec((bt, RET_V), lambda i: (i, 2)),
                row(GDN_HEADS * GDN_DV), row(LRU_WIDTH), row(LRU_WIDTH),
                _layer_spec(norm_w, layer), ret_blk, gdn_blk]
    aliases = {}
    if prev is None:
        depth = s_ret.shape[0]
        out_ret = pl.BlockSpec((depth, bt, RET_HEADS, RET_DK, RET_DV), lambda i: (0, i, 0, 0, 0))
        out_gdn = pl.BlockSpec((depth, bt, GDN_HEADS, GDN_DK, GDN_DV), lambda i: (0, i, 0, 0, 0))
    else:
        out_ret, out_gdn = ret_blk, gdn_blk
        aliases = {len(ins): 3, len(ins) + 1: 4}
        ins += list(prev)
        in_specs += [pl.BlockSpec(memory_space=pl.ANY)] * 2
    return pl.pallas_call(
        functools.partial(_smix2_kernel, first=prev is None),
        grid=(rows // bt,),
        in_specs=in_specs,
        out_specs=[row(RET_V), row(GDN_HEADS * GDN_DV), row(LRU_WIDTH), out_ret, out_gdn],
        out_shape=[sds(rows, RET_V), sds(rows, GDN_HEADS * GDN_DV), sds(rows, LRU_WIDTH),
                   sds(*s_ret.shape), sds(*s_gdn.shape)],
        input_output_aliases=aliases,
        scratch_shapes=[pltpu.VMEM((bt, RET_V), F32), pltpu.VMEM((bt, GDN_HEADS * GDN_DV), F32)],
        compiler_params=_cparams("parallel"),
        name="sample_state",
    )(*ins)


def _block_diag(w):
    nb, bs, _ = w.shape
    return jnp.einsum("nij,nm->nimj", w, jnp.eye(nb, dtype=w.dtype)).reshape(nb * bs, nb * bs)


def kernel(x_prompt, x_sample, c_prompt, c_sample, state_ret, state_gdn, state_gdn_conv, state_ssm_re, state_ssm_im, state_lru, state_lru_conv, w_ada, b_ada, ln_g, ln_b, w_ffn_up, w_ffn_down, w_in, gdn_conv_w, gdn_a_log, gdn_dt_bias, gdn_norm_w, ssm_lam_re, ssm_lam_im, ssm_log_step, ssm_b_re, ssm_b_im, ssm_c_re, ssm_c_im, ssm_d, ssm_glu_w, ssm_glu_b, lru_conv_w, lru_conv_b, lru_wa, lru_ba, lru_wx, lru_bx, lru_lam, w_branch, w_out):
    nb, t, _ = x_prompt.shape
    ns = x_sample.shape[0]
    depth = w_ada.shape[0]
    assert t % RET_CHUNK == 0 and x_sample.shape[1] == 1
    alpha = (2 * depth) ** 0.25
    tm = 256 if t % 256 == 0 else RET_CHUNK
    tm_ffn = 512 if t % 512 == 0 else tm
    tc = 64

    wup = w_ffn_up.astype(BF16)
    wdn = w_ffn_down.astype(BF16)
    wbr = w_branch.astype(BF16)
    wout = w_out.astype(BF16)
    w_packed, w_mg = _pack_w_in(w_in)
    ln_g4 = ln_g.reshape(depth, N_SUB, 1, D_MODEL)
    ln_b4 = ln_b.reshape(depth, N_SUB, 1, D_MODEL)

    mod_p, mod_s = _ada(c_prompt, c_sample, w_ada, b_ada)
    mod_p = mod_p.reshape(depth, nb, 3 * N_SUB, 1, D_MODEL).transpose(0, 2, 1, 3, 4)

    row3 = lambda a: a.reshape(depth, 1, -1)
    lane_rows = lambda v: jnp.pad(v, ((0, 0), (0, LANE - v.shape[1]))).reshape(depth, 1, LANE)
    alog_rows, dtb_rows = lane_rows(gdn_a_log), lane_rows(gdn_dt_bias)
    norm_w = row3(gdn_norm_w)
    s5_w = (*jax.vmap(_s5_params)(ssm_lam_re, ssm_lam_im, ssm_log_step, ssm_b_re, ssm_b_im, ssm_c_re, ssm_c_im),
            row3(ssm_d), ssm_glu_w.astype(BF16), row3(ssm_glu_b))
    lru_w = (lru_conv_w, row3(lru_conv_b),
             jax.vmap(_block_diag)(lru_wa).astype(BF16), jax.vmap(_block_diag)(lru_wx).astype(BF16),
             row3(lru_ba), row3(lru_bx), row3(lru_lam))
    s_states = (state_gdn_conv.transpose(0, 2, 1, 3),
                state_ssm_re.reshape(depth, ns, SSM_STATE), state_ssm_im.reshape(depth, ns, SSM_STATE),
                state_lru, state_lru_conv.transpose(0, 2, 1, 3))

    cos_s, sin_s = _rope_tables(jnp.full((1,), PAST_LEN))
    xp = x_prompt.reshape(nb * t, D_MODEL)
    xs = x_sample.reshape(ns, D_MODEL)
    new_p, new_s = [], []
    big_s = None
    for l in range(depth):
        mp = _Mod(mod_p, l, False, t)
        ms = _Mod(mod_s, l, True)

        xp = _ffn(xp, mp, 0, 0, wup, wdn, ln_g4, ln_b4, alpha, tm_ffn)
        pc = _inproj(xp, mp, w_packed, tm, conv=(gdn_conv_w, nb, t))
        gcv_p = pc["gdn_conv"]
        ya, ret_p = _retention_prompt(pc["ret"], nb, t)
        yb, gdn_p = _gdn_prompt(pc["gqkv"], pc["gab"], pc["gz"], alog_rows, dtb_rows, norm_w, l, nb, t)
        to_t = lambda a: a.reshape(nb, t, -1)
        from_t = lambda a: a.reshape(nb * t, -1)
        yc_t, sre_p, sim_p, yd_t, lru_p, lcv_p = _scan_mixers(
            to_t(pc["su"]), to_t(pc["lx"]), to_t(pc["lg"]), s5_w, lru_w, l, tc)
        xp = _merge(xp, mp, ya, yb, from_t(yc_t), from_t(yd_t), w_mg, wbr, wout, ln_g4, ln_b4, alpha, tm_ffn)
        xp = _ffn(xp, mp, 2, 1, wup, wdn, ln_g4, ln_b4, alpha, tm_ffn)
        new_p.append((ret_p, gdn_p, gcv_p,
                      sre_p.reshape(nb, SSM_GROUPS, SSM_P), sim_p.reshape(nb, SSM_GROUPS, SSM_P),
                      lru_p, lcv_p.transpose(1, 0, 2)))

        xs = _ffn(xs, ms, 0, 0, wup, wdn, ln_g4, ln_b4, alpha, ns)
        sc = _inproj(xs, ms, w_packed, ns)
        (qkt, gv, gx, yc, lh, gcv_s, sre_s, sim_s, lcv_s) = _smix1(
            sc, cos_s, sin_s, s_states, (gdn_conv_w, alog_rows, dtb_rows), s5_w, lru_w, l)
        ya, yb, yd, *big_s = _smix2(qkt, sc["ret"], gv, gx, sc["ret"], sc["gz"], lh, sc["lg"],
                                    norm_w, state_ret, state_gdn, l, SUBLANE, big_s)
        xs = _merge(xs, ms, ya, yb, yc, yd, w_mg, wbr, wout, ln_g4, ln_b4, alpha, ns)
        xs = _ffn(xs, ms, 2, 1, wup, wdn, ln_g4, ln_b4, alpha, ns)
        new_s.append((gcv_s.transpose(1, 0, 2),
                      sre_s.reshape(ns, SSM_GROUPS, SSM_P), sim_s.reshape(ns, SSM_GROUPS, SSM_P),
                      lh, lcv_s.transpose(1, 0, 2)))

    ret_p, gdn_p, gcv_p, sre_p, sim_p, lru_p, lcv_p = [jnp.stack(z) for z in zip(*new_p)]
    gcv_s, sre_s, sim_s, lru_s, lcv_s = [jnp.stack(z) for z in zip(*new_s)]
    ret_s, gdn_s = big_s
    return (xp.reshape(nb, t, D_MODEL), xs.reshape(ns, 1, D_MODEL),
            ret_p, ret_s, gdn_p, gdn_s, gcv_p, gcv_s, sre_p, sre_s, sim_p, sim_s,
            lru_p, lru_s, lcv_p, lcv_s)
```

```python
import functools
import math

import numpy as np
import jax
import jax.numpy as jnp
from jax import lax
from jax.experimental import pallas as pl
from jax.experimental.pallas import tpu as pltpu

F32 = jnp.float32
BF16 = jnp.bfloat16

D_MODEL = 1024
RET_HEADS, RET_DK, RET_DV, RET_CHUNK = 4, 64, 128, 128
RET_STEP_CHUNKS = 4
ROPE_BASE = 10000.0
GDN_HEADS, GDN_DK, GDN_DV, GDN_CHUNK = 4, 128, 128, 64
GDN_STEP_CHUNKS = 4
GDN_QKV = 2 * GDN_HEADS * GDN_DK + GDN_HEADS * GDN_DV
CONV_W = 4
SSM_GROUP, SSM_GROUPS, SSM_P = 16, 32, 64
SSM_WIDTH = SSM_GROUP * SSM_GROUPS
SSM_STATE = SSM_GROUPS * SSM_P
LRU_WIDTH, LRU_BLOCKS = 512, 8
LRU_C = 8.0
N_BRANCH, BRANCH_W = 4, 512
D_FF = 2816
N_SUB = 3
LN_EPS = 1e-5
PAST_LEN = 16384

RET_QK = RET_HEADS * RET_DK
RET_V = RET_HEADS * RET_DV
RET_W = 2 * RET_QK + 2 * RET_V
GDN_K = GDN_HEADS * GDN_DK
QKT_ROWS = 2 * RET_QK + 2 * GDN_K
LANE = 128
SUBLANE = 8
FF_CHUNKS = ((0, 768), (768, 1024), (1792, 1024))
VMEM_LIMIT = 56 * 1024 * 1024


def _cparams(*sem):
    return pltpu.CompilerParams(dimension_semantics=sem, vmem_limit_bytes=VMEM_LIMIT)


def _layer_spec(a, layer):
    nd = a.ndim
    return pl.BlockSpec((None,) + a.shape[1:], lambda *_: (layer,) + (0,) * (nd - 1))


def _resident(shape, index_map):
    return pl.BlockSpec(shape, index_map, pipeline_mode=pl.Buffered(1))


def _ln(x):
    mu = jnp.mean(x, axis=-1, keepdims=True)
    xc = x - mu
    return xc * lax.rsqrt(jnp.mean(xc * xc, axis=-1, keepdims=True) + LN_EPS)


def _sigmoid(x):
    return 0.5 * jnp.tanh(0.5 * x) + 0.5


def _silu(x):
    return x * _sigmoid(x)


def _softplus(x):
    return jnp.maximum(x, 0.0) + jnp.log1p(jnp.exp(-jnp.abs(x)))


def _bdot(a, w):
    return jnp.dot(a.astype(BF16), w.astype(BF16), preferred_element_type=F32)


def _split(x):
    hi = x.astype(BF16)
    return hi, (x - hi.astype(F32)).astype(BF16)


def _dot_nt(a, b):
    return lax.dot_general(a, b, (((1,), (1,)), ((), ())), preferred_element_type=F32)


def _iota(shape, dim):
    return lax.broadcasted_iota(jnp.int32, shape, dim)


def _to_time_major(x):
    nb, tc, width = x.shape
    return jnp.swapaxes(x, 0, 1).reshape(tc * nb, width)


def _from_time_major(y, nb):
    rows, width = y.shape
    return jnp.swapaxes(y.reshape(rows // nb, nb, width), 0, 1)


def _ada_kernel(cp_ref, cs_ref, w_ref, b_ref, op_ref, os_ref):
    w = w_ref[...].astype(BF16)
    op_ref[...] = _bdot(_silu(cp_ref[...]), w) + b_ref[...]
    os_ref[...] = _bdot(_silu(cs_ref[...]), w) + b_ref[...]


def _ada(c_prompt, c_sample, w_ada, b_ada):
    depth, _, n_out = w_ada.shape
    tn = 1152
    cond = lambda c: pl.BlockSpec(c.shape, lambda l, j: (0, 0))
    out = lambda c: pl.BlockSpec((None, c.shape[0], tn), lambda l, j: (l, 0, j))
    return pl.pallas_call(
        _ada_kernel,
        grid=(depth, n_out // tn),
        in_specs=[
            cond(c_prompt), cond(c_sample),
            pl.BlockSpec((None, D_MODEL, tn), lambda l, j: (l, 0, j)),
            pl.BlockSpec((None, 1, tn), lambda l, j: (l, 0, j)),
        ],
        out_specs=[out(c_prompt), out(c_sample)],
        out_shape=[jax.ShapeDtypeStruct((depth, c.shape[0], n_out), F32) for c in (c_prompt, c_sample)],
        compiler_params=_cparams("parallel", "parallel"),
        name="ada",
    )(c_prompt, c_sample, w_ada, b_ada.reshape(depth, 1, n_out))


class _Mod:
    def __init__(self, arr, layer, per_row, rows_per_batch=None):
        self.arr, self.layer, self.per_row, self.rpb = arr, layer, per_row, rows_per_batch

    def spec(self, sub, which, tm):
        k, l = 3 * sub + which, self.layer
        if self.per_row:
            return pl.BlockSpec((None, tm, D_MODEL), lambda i: (l, i, k))
        tiles = self.rpb // tm
        return pl.BlockSpec((None, None, None, 1, D_MODEL), lambda i: (l, k, i // tiles, 0, 0))


def _row_parts(ref, parts):
    if ref.shape[0] == 1:
        return [ref[...]] * parts
    hm = ref.shape[0] // parts
    return [ref[k * hm:(k + 1) * hm, :] for k in range(parts)]


def _ffn_kernel(x_ref, sh_ref, sc_ref, gt_ref, wup_ref, wdn_ref, g_ref, b_ref, o_ref, *, alpha, parts):
    ks = range(parts)
    xs, shs, scs, gts = (_row_parts(r, parts) for r in (x_ref, sh_ref, sc_ref, gt_ref))
    hs = [(_ln(xs[k]) * (1.0 + scs[k]) + shs[k]).astype(BF16) for k in ks]
    accs = [jnp.zeros(xs[k].shape, F32) for k in ks]
    for lo, width in FF_CHUNKS:
        a = [jnp.dot(hs[k], wup_ref[:, lo:lo + width], preferred_element_type=F32) for k in ks]
        b = [jnp.dot(hs[k], wup_ref[:, D_FF + lo:D_FF + lo + width], preferred_element_type=F32) for k in ks]
        act = [(_silu(a[k]) * b[k]).astype(BF16) for k in ks]
        accs = [accs[k] + jnp.dot(act[k], wdn_ref[lo:lo + width, :], preferred_element_type=F32) for k in ks]
    hm = x_ref.shape[0] // parts
    for k in ks:
        z = alpha * xs[k] + 0.5 * gts[k] * accs[k]
        o_ref[k * hm:(k + 1) * hm, :] = _ln(z) * g_ref[...] + b_ref[...]


def _ffn(x, mod, sub, which, wup, wdn, ln_g, ln_b, alpha, tm):
    n = x.shape[0]
    l = mod.layer
    row = pl.BlockSpec((tm, D_MODEL), lambda i: (i, 0))
    return pl.pallas_call(
        functools.partial(_ffn_kernel, alpha=alpha, parts=2 if tm >= 512 else 1),
        grid=(n // tm,),
        in_specs=[
            row, mod.spec(sub, 0, tm), mod.spec(sub, 1, tm), mod.spec(sub, 2, tm),
            _resident((None, None, D_MODEL, 2 * D_FF), lambda i: (l, which, 0, 0)),
            _resident((None, None, D_FF, D_MODEL), lambda i: (l, which, 0, 0)),
            pl.BlockSpec((None, None, 1, D_MODEL), lambda i: (l, sub, 0, 0)),
            pl.BlockSpec((None, None, 1, D_MODEL), lambda i: (l, sub, 0, 0)),
        ],
        out_specs=row,
        out_shape=jax.ShapeDtypeStruct((n, D_MODEL), F32),
        compiler_params=_cparams("parallel"),
        name="ffn",
    )(x, mod.arr, mod.arr, mod.arr, wup, wdn, ln_g, ln_b)


_IN_PIECES = (("ret", RET_W), ("gqkv", GDN_QKV), ("gab", LANE), ("gz", 512), ("su", SSM_WIDTH),
              ("lx", LRU_WIDTH), ("lg", LRU_WIDTH))
IN_MAIN = sum(w for _, w in _IN_PIECES)
MG_WIDTH = N_BRANCH * D_MODEL


PACK_ROWS = 512


def _pack_kernel(wt_ref, main_ref, mg_ref):
    gate_lo = RET_W + GDN_QKV
    gate_hi = gate_lo + 2 * GDN_HEADS
    mg_lo = wt_ref.shape[0] - MG_WIDTH

    def move(dst_ref, dst, src, width):
        for c in range(0, width, PACK_ROWS):
            dst_ref[:, dst + c:dst + c + PACK_ROWS] = wt_ref[src + c:src + c + PACK_ROWS, :].T.astype(BF16)

    move(main_ref, 0, 0, gate_lo)
    g = wt_ref[gate_lo:gate_lo + LANE, :].T
    main_ref[:, gate_lo:gate_lo + LANE] = jnp.where(_iota(g.shape, 1) < 2 * GDN_HEADS, g, 0.0).astype(BF16)
    move(main_ref, gate_lo + LANE, gate_hi, mg_lo - gate_hi)
    move(mg_ref, 0, mg_lo, MG_WIDTH)


def _pack_w_in(w_in):
    depth, d_in, d_out = w_in.shape
    tk = 256
    blk = lambda w: pl.BlockSpec((None, tk, w), lambda l, i: (l, i, 0))
    return pl.pallas_call(
        _pack_kernel,
        grid=(depth, d_in // tk),
        in_specs=[pl.BlockSpec((None, d_out, tk), lambda l, i: (l, 0, i))],
        out_specs=[blk(IN_MAIN), blk(MG_WIDTH)],
        out_shape=[jax.ShapeDtypeStruct((depth, d_in, IN_MAIN), BF16),
                   jax.ShapeDtypeStruct((depth, d_in, MG_WIDTH), BF16)],
        compiler_params=_cparams("parallel", "parallel"),
        name="pack_w_in",
    )(jnp.swapaxes(w_in, 1, 2))


def _l2n(x):
    return x * lax.rsqrt(jnp.sum(x * x, axis=-1, keepdims=True) + 1e-6)


def _gdn_activate(conv):
    cq = _silu(conv)
    q = [_l2n(cq[:, h * GDN_DK:(h + 1) * GDN_DK]) * (GDN_DK ** -0.5) for h in range(GDN_HEADS)]
    k = [_l2n(cq[:, GDN_K + h * GDN_DK:GDN_K + (h + 1) * GDN_DK]) for h in range(GDN_HEADS)]
    return jnp.concatenate(q + k + [cq[:, 2 * GDN_K:]], axis=1)


def _inproj_kernel(x_ref, sh_ref, sc_ref, w_ref, *refs, seq_tiles):
    h = (_ln(x_ref[...]) * (1.0 + sc_ref[...]) + sh_ref[...]).astype(BF16)
    n_out = len(_IN_PIECES)
    if seq_tiles is None:
        o_refs = refs
    else:
        cw_ref, o_refs, cv_ref, xbuf = refs[0], refs[1:1 + n_out], refs[1 + n_out], refs[2 + n_out]

        @pl.when(pl.program_id(0) % seq_tiles == 0)
        def _():
            xbuf[0:SUBLANE, :] = jnp.zeros((SUBLANE, GDN_QKV), F32)

    offsets = np.cumsum([0] + [w for _, w in _IN_PIECES])
    order = sorted(range(n_out), key=lambda p: _IN_PIECES[p][0] != "gqkv")
    for p in order:
        (name, width), o_ref, lo = _IN_PIECES[p], o_refs[p], int(offsets[p])
        y = jnp.dot(h, w_ref[:, lo:lo + width], preferred_element_type=F32)
        if name == "gqkv" and seq_tiles is not None:
            tm = y.shape[0]
            xbuf[SUBLANE:SUBLANE + tm, :] = y
            conv = cw_ref[CONV_W - 1:CONV_W, :] * y
            for j in range(CONV_W - 1):
                r0 = SUBLANE - (CONV_W - 1) + j
                conv = conv + cw_ref[j:j + 1, :] * xbuf[r0:r0 + tm, :]
            cv_ref[0] = xbuf[SUBLANE + tm - (CONV_W - 1):SUBLANE + tm, :]
            xbuf[0:SUBLANE, :] = y[tm - SUBLANE:, :]
            y = _gdn_activate(conv)
        o_ref[...] = y


def _inproj(x, mod, w_packed, tm, conv=None):
    n = x.shape[0]
    l = mod.layer
    widths = [w for _, w in _IN_PIECES]
    ins = [x, mod.arr, mod.arr, w_packed]
    in_specs = [pl.BlockSpec((tm, D_MODEL), lambda i: (i, 0)),
                mod.spec(1, 0, tm), mod.spec(1, 1, tm),
                _resident((None, D_MODEL, IN_MAIN), lambda i: (l, 0, 0))]
    out_specs = [pl.BlockSpec((tm, w), lambda i: (i, 0)) for w in widths]
    out_shape = [jax.ShapeDtypeStruct((n, w), F32) for w in widths]
    scratch, seq_tiles = [], None
    if conv is not None:
        conv_w, nb, t = conv
        seq_tiles = t // tm
        ins.append(conv_w)
        in_specs.append(_layer_spec(conv_w, l))
        out_specs.append(pl.BlockSpec((1, CONV_W - 1, GDN_QKV), lambda i: (i // seq_tiles, 0, 0)))
        out_shape.append(jax.ShapeDtypeStruct((nb, CONV_W - 1, GDN_QKV), F32))
        scratch.append(pltpu.VMEM((SUBLANE + tm, GDN_QKV), F32))
    outs = pl.pallas_call(
        functools.partial(_inproj_kernel, seq_tiles=seq_tiles),
        grid=(n // tm,),
        in_specs=in_specs,
        out_specs=out_specs,
        out_shape=out_shape,
        scratch_shapes=scratch,
        compiler_params=_cparams("arbitrary"),
        name="inproj",
    )(*ins)
    return dict(zip([nm for nm, _ in _IN_PIECES] + ["gdn_conv"], outs))


def _rope_tables(pos):
    half = RET_DK // 2
    inv = ROPE_BASE ** (-jnp.arange(half, dtype=F32) / half)
    ang = pos.astype(F32)[:, None] * inv[None, :]
    cos, sin = jnp.cos(ang), jnp.sin(ang)
    cos_t = jnp.tile(jnp.concatenate([cos, cos], axis=-1), (1, RET_HEADS))
    sin_t = jnp.tile(jnp.concatenate([-sin, sin], axis=-1), (1, RET_HEADS))
    return cos_t, sin_t


def _rotate(z, cos, sin):
    half = RET_DK // 2
    first = (_iota(z.shape, 1) % RET_DK) < half
    swapped = jnp.where(first, pltpu.roll(z, RET_QK - half, 1), pltpu.roll(z, half, 1))
    return z * cos + swapped * sin


def _ret_gammas():
    return [1.0 - 2.0 ** (-5.0 - h) for h in range(RET_HEADS)]


def _ret_consts(chunk, n_sub):
    log_g = np.log(np.array(_ret_gammas(), np.float64))
    idx = np.arange(chunk, dtype=np.float64)
    rel = idx[:, None] - idx[None, :]
    dmask = np.where(rel[None] >= 0, np.exp(log_g[:, None, None] * np.maximum(rel, 0.0)[None]), 0.0)
    qdec = np.repeat(np.exp(log_g[None, :] * (idx[:, None] + 1.0)), RET_DK, axis=1)
    kdec = np.repeat(np.exp(log_g[None, :] * (chunk - 1.0 - idx[:, None])), RET_DK, axis=1)
    qdec, kdec = np.tile(qdec, (n_sub, 1)), np.tile(kdec, (n_sub, 1))
    head_r = np.arange(RET_QK) // RET_DK
    head_c = np.arange(RET_V) // RET_DV
    bd = (head_r[:, None] == head_c[None, :]).astype(np.float64)
    cd = bd * np.exp(log_g * chunk)[head_r][:, None]
    f = lambda a: jnp.asarray(a, F32)
    return f(dmask), f(qdec), f(kdec), f(cd), f(bd)


def _ret_kernel(x_ref, cos_ref, sin_ref, dmask_ref, qdec_ref, kdec_ref, cd_ref, bd_ref,
                y_ref, st_ref, s_scr):
    c = pl.program_id(1)
    ch = RET_CHUNK
    n_sub = x_ref.shape[0] // ch

    @pl.when(c == 0)
    def _():
        s_scr[...] = jnp.zeros(s_scr.shape, F32)

    x = x_ref[...]
    q = _rotate(x[:, 0:RET_QK], cos_ref[...], sin_ref[...])
    k = _rotate(x[:, RET_QK:2 * RET_QK], cos_ref[...], sin_ref[...]) * (RET_DK ** -0.5)
    v = x[:, 2 * RET_QK:2 * RET_QK + RET_V].astype(BF16)
    gate = x[:, 2 * RET_QK + RET_V:]
    q_dec = (q * qdec_ref[...]).astype(BF16)
    k_dec = k * kdec_ref[...]
    kb = k.astype(BF16)
    head = _iota((ch, RET_QK), 1) // RET_DK
    intra, upd = {}, {}
    for n in range(n_sub):
        sl = slice(n * ch, (n + 1) * ch)
        for h in range(RET_HEADS):
            qh = jnp.where(head == h, q[sl], 0.0).astype(BF16)
            inner = _dot_nt(qh, kb[sl]) * dmask_ref[h]
            intra[n, h] = _bdot(inner, v[sl, h * RET_DV:(h + 1) * RET_DV])
        upd[n] = jnp.dot(k_dec[sl].T.astype(BF16), v[sl], preferred_element_type=F32) * bd_ref[...]
    s = s_scr[...]
    cross = {}
    for n in range(n_sub):
        cross[n] = jnp.dot(q_dec[n * ch:(n + 1) * ch], s.astype(BF16), preferred_element_type=F32)
        s = s * cd_ref[...] + upd[n]
    s_scr[...] = s
    for n in range(n_sub):
        sl = slice(n * ch, (n + 1) * ch)
        outs = [_ln(intra[n, h] + cross[n][:, h * RET_DV:(h + 1) * RET_DV]) for h in range(RET_HEADS)]
        y_ref[sl, :] = jnp.concatenate(outs, axis=1) * _silu(gate[sl])

    @pl.when(c == pl.num_programs(1) - 1)
    def _():
        for h in range(RET_HEADS):
            st_ref[0, h] = s[h * RET_DK:(h + 1) * RET_DK, h * RET_DV:(h + 1) * RET_DV]


def _retention_prompt(ret, nb, t):
    n_sub = RET_STEP_CHUNKS if t % (RET_STEP_CHUNKS * RET_CHUNK) == 0 else 1
    chunk = n_sub * RET_CHUNK
    nc = t // chunk
    cos_t, sin_t = _rope_tables(jnp.arange(t))
    consts = _ret_consts(RET_CHUNK, n_sub)
    full = lambda a: pl.BlockSpec(a.shape, lambda b, c: (0,) * a.ndim)
    return pl.pallas_call(
        _ret_kernel,
        grid=(nb, nc),
        in_specs=[
            pl.BlockSpec((chunk, RET_W), lambda b, c: (b * nc + c, 0)),
            pl.BlockSpec((chunk, RET_QK), lambda b, c: (c, 0)),
            pl.BlockSpec((chunk, RET_QK), lambda b, c: (c, 0)),
        ] + [full(a) for a in consts],
        out_specs=[
            pl.BlockSpec((chunk, RET_V), lambda b, c: (b * nc + c, 0)),
            pl.BlockSpec((1, RET_HEADS, RET_DK, RET_DV), lambda b, c: (b, 0, 0, 0)),
        ],
        out_shape=[
            jax.ShapeDtypeStruct((nb * t, RET_V), F32),
            jax.ShapeDtypeStruct((nb, RET_HEADS, RET_DK, RET_DV), F32),
        ],
        scratch_shapes=[pltpu.VMEM((RET_QK, RET_V), F32)],
        compiler_params=_cparams("parallel", "arbitrary"),
        name="retention",
    )(ret, cos_t, sin_t, *consts)


def _rms(x):
    return x * lax.rsqrt(jnp.mean(x * x, axis=-1, keepdims=True) + 1e-6)


def _gate_lanes(gab, alog_row, dtb_row):
    g = -jnp.exp(alog_row) * _softplus(gab + dtb_row)
    return jnp.where(_iota(gab.shape, 1) < GDN_HEADS, g, _sigmoid(gab))


def _gdn_kernel(x_ref, gab_ref, gz_ref, alog_ref, dtb_ref, nw_ref, y_ref, st_ref, s_scr):
    c = pl.program_id(1)
    ch = GDN_CHUNK

    @pl.when(c == 0)
    def _():
        s_scr[...] = jnp.zeros(s_scr.shape, F32)

    rows = x_ref.shape[0]
    cq = x_ref[...]

    gl = _gate_lanes(gab_ref[...], alog_ref[...], dtb_ref[...])
    col = _iota((ch, 2 * ch), 1) % ch
    tril = _iota((ch, 2 * ch), 0) >= col
    strict = _iota((ch, 2 * ch), 0) > col
    is_g = _iota(gl.shape, 1) < GDN_HEADS
    r_i, c_i = _iota((rows, rows), 0), _iota((rows, rows), 1)
    chunk_tril = ((r_i >= c_i) & (r_i // ch == c_i // ch)).astype(F32)
    csum = jnp.dot(chunk_tril, jnp.where(is_g, gl, 0.0), preferred_element_type=F32,
                   precision=lax.Precision.HIGHEST)
    gsel = jnp.where(is_g, csum, gl)
    n_sub = rows // ch
    heads = range(GDN_HEADS)
    probs = [(h, n) for h in heads for n in range(n_sub)]

    q_all, k_all, g_all, qe_all, rhs_all, kbeta_all = {}, {}, {}, {}, {}, {}
    for h in heads:
        q_all[h] = cq[:, h * GDN_DK:(h + 1) * GDN_DK]
        k_all[h] = cq[:, GDN_K + h * GDN_DK:GDN_K + (h + 1) * GDN_DK]
        v_h = cq[:, 2 * GDN_K + h * GDN_DV:2 * GDN_K + (h + 1) * GDN_DV]
        g_all[h] = jnp.broadcast_to(gsel[:, h:h + 1], (rows, LANE))
        b_h = jnp.broadcast_to(gsel[:, GDN_HEADS + h:GDN_HEADS + h + 1], (rows, LANE))
        e_h = jnp.exp(g_all[h])
        kbeta_all[h] = k_all[h] * b_h
        rhs_all[h] = jnp.concatenate([v_h * b_h, kbeta_all[h] * e_h], axis=1)
        qe_all[h] = q_all[h] * e_h

    decay, pp, uw, attn, kd_t, e_last = {}, {}, {}, {}, {}, {}
    for h, n in probs:
        sl = slice(n * ch, (n + 1) * ch)
        g_b = g_all[h][sl]
        g_cols = jnp.concatenate([g_b, g_b], axis=0).T[:ch, :]
        diff = g_b - g_cols
        decay[h, n] = jnp.where(tril, jnp.exp(jnp.where(tril, diff, 0.0)), 0.0)
        g_last = g_b[ch - 1:ch, :]
        e_last[h, n] = jnp.exp(g_last)
        kd_t[h, n] = (k_all[h][sl] * jnp.exp(g_last - g_b)).T.astype(BF16)
    for h, n in probs:
        sl = slice(n * ch, (n + 1) * ch)
        kb_hi, kb_lo = _split(kbeta_all[h][sl])
        k_hi, k_lo = _split(jnp.concatenate([k_all[h][sl], k_all[h][sl]], axis=0))
        kk = _dot_nt(jnp.concatenate([kb_hi, kb_hi, kb_lo, kb_lo], axis=1),
                     jnp.concatenate([k_hi, k_lo, k_hi, k_lo], axis=1))
        pp[h, n] = -jnp.where(strict, kk * decay[h, n], 0.0)
        attn[h, n] = _dot_nt(q_all[h][sl].astype(BF16), k_all[h][sl].astype(BF16)) * decay[h, n][:, :ch]
    for stage in range(int(math.log2(ch))):
        for h, n in probs:
            p_hi, p_lo = _split(pp[h, n])
            lhs = jnp.concatenate([p_hi, p_lo], axis=1)
            y = rhs_all[h][n * ch:(n + 1) * ch] if stage == 0 else uw[h, n]
            y_hi, y_lo = _split(y)
            uw[h, n] = y + jnp.dot(lhs, jnp.concatenate([y_hi, y_lo, y_hi, y_lo], axis=0),
                                   preferred_element_type=F32)
            if stage + 1 < int(math.log2(ch)):
                pp[h, n] = jnp.dot(lhs, jnp.concatenate([p_hi, p_lo, p_hi, p_lo], axis=0),
                                   preferred_element_type=F32)
    s = {h: s_scr[h] for h in heads}
    o_parts = {h: [] for h in heads}
    for n in range(n_sub):
        for h in heads:
            sl = slice(n * ch, (n + 1) * ch)
            u, w = uw[h, n][:, :GDN_DV], uw[h, n][:, GDN_DV:]
            ws_qs = _bdot(jnp.concatenate([w, qe_all[h][sl]], axis=0), s[h])
            v_new = u - ws_qs[:ch]
            o_parts[h].append(ws_qs[ch:] + _bdot(attn[h, n], v_new))
            s[h] = s[h] * e_last[h, n] + jnp.dot(kd_t[h, n], v_new.astype(BF16), preferred_element_type=F32)
    norm_w = nw_ref[...]
    gz = gz_ref[...]
    outs = []
    for h in heads:
        s_scr[h] = s[h]
        o = jnp.concatenate(o_parts[h], axis=0)
        outs.append(_rms(o) * norm_w * _silu(gz[:, h * GDN_DV:(h + 1) * GDN_DV]))
    y_ref[...] = jnp.concatenate(outs, axis=1)

    @pl.when(c == pl.num_programs(1) - 1)
    def _():
        st_ref[0] = s_scr[...]


def _gdn_prompt(gqkv, gab, gz, alog_row, dtb_row, norm_w, layer, nb, t):
    ch = GDN_STEP_CHUNKS * GDN_CHUNK if t % (GDN_STEP_CHUNKS * GDN_CHUNK) == 0 else GDN_CHUNK
    nc = t // ch
    row = lambda w: pl.BlockSpec((ch, w), lambda b, c: (b * nc + c, 0))
    return pl.pallas_call(
        _gdn_kernel,
        grid=(nb, nc),
        in_specs=[row(GDN_QKV), row(LANE), row(GDN_HEADS * GDN_DV)]
        + [_layer_spec(a, layer) for a in (alog_row, dtb_row, norm_w)],
        out_specs=[
            row(GDN_HEADS * GDN_DV),
            pl.BlockSpec((1, GDN_HEADS, GDN_DK, GDN_DV), lambda b, c: (b, 0, 0, 0)),
        ],
        out_shape=[
            jax.ShapeDtypeStruct((nb * t, GDN_HEADS * GDN_DV), F32),
            jax.ShapeDtypeStruct((nb, GDN_HEADS, GDN_DK, GDN_DV), F32),
        ],
        scratch_shapes=[pltpu.VMEM((GDN_HEADS, GDN_DK, GDN_DV), F32)],
        compiler_params=_cparams("parallel", "arbitrary"),
        name="gdn",
    )(gqkv, gab, gz, alog_row, dtb_row, norm_w)


SSM_BLK = 4
SSM_BLK_STATE = SSM_STATE // SSM_BLK


def _s5_params(lam_re, lam_im, log_step, b_re, b_im, c_re, c_im):
    step = jnp.exp(log_step.astype(F32))[:, None]
    mag = jnp.exp(lam_re * step)
    ab_re = mag * jnp.cos(lam_im * step)
    ab_im = mag * jnp.sin(lam_im * step)
    den = lam_re * lam_re + lam_im * lam_im
    nr = ab_re - 1.0
    f_re = (nr * lam_re + ab_im * lam_im) / den
    f_im = (ab_im * lam_re - nr * lam_im) / den
    bb_re = f_re[..., None] * b_re - f_im[..., None] * b_im
    bb_im = f_re[..., None] * b_im + f_im[..., None] * b_re
    gpb = SSM_GROUPS // SSM_BLK
    eye = jnp.eye(gpb, dtype=F32)

    def in_mat(bb):
        bb = bb.reshape(SSM_BLK, gpb, SSM_P, SSM_GROUP)
        m = jnp.einsum("jgpc,gk->jgckp", bb, eye)
        return m.reshape(SSM_BLK, gpb * SSM_GROUP, gpb * SSM_P).astype(BF16)

    def out_mat(cc):
        cc = cc.reshape(SSM_BLK, gpb, SSM_GROUP, SSM_P)
        m = jnp.einsum("jgcp,gk->jgpkc", cc, eye)
        return m.reshape(SSM_BLK, gpb * SSM_P, gpb * SSM_GROUP).astype(BF16)

    return (ab_re.reshape(1, SSM_STATE), ab_im.reshape(1, SSM_STATE),
            in_mat(bb_re), in_mat(bb_im), out_mat(c_re), out_mat(c_im))


def _s5_readout(u, h_re, h_im, cre_ref, cim_ref, d_row, gw_ref, gb_row):
    ys = []
    for j in range(SSM_BLK):
        sl = slice(j * SSM_BLK_STATE, (j + 1) * SSM_BLK_STATE)
        ys.append(_bdot(h_re[:, sl], cre_ref[j]) - _bdot(h_im[:, sl], cim_ref[j]))
    y = jax.nn.gelu(jnp.concatenate(ys, axis=1) + d_row * u)
    return y * _sigmoid(_bdot(y, gw_ref[...]) + gb_row)


def _scan_mixers_kernel(u_ref, x_ref, g_ref,
                        are_ref, aim_ref, bre_ref, bim_ref, cre_ref, cim_ref, d_ref, gw_ref, gb_ref,
                        cw_ref, cb_ref, wa_ref, wx_ref, ba_ref, bx_ref, lam_ref,
                        y_ref, hre_ref, him_ref, yd_ref, h_out, cv_out,
                        bu_re, bu_im, h_re, h_im, xbuf, a_buf, b_buf, h_scr, *, nb):
    i = pl.program_id(0)
    tc = u_ref.shape[1]
    rows = tc * nb
    tail = (CONV_W - 1) * nb

    @pl.when(i == 0)
    def _():
        h_re[...] = jnp.zeros(h_re.shape, F32)
        h_im[...] = jnp.zeros(h_im.shape, F32)
        xbuf[0:tail, :] = jnp.zeros((tail, LRU_WIDTH), F32)
        h_scr[...] = jnp.zeros(h_scr.shape, F32)

    u = _to_time_major(u_ref[...])
    for j in range(SSM_BLK):
        uj = u[:, j * LANE:(j + 1) * LANE]
        sl = slice(j * SSM_BLK_STATE, (j + 1) * SSM_BLK_STATE)
        bu_re[:, sl] = _bdot(uj, bre_ref[j])
        bu_im[:, sl] = _bdot(uj, bim_ref[j])
    x = _to_time_major(x_ref[...])
    xbuf[tail:tail + rows, :] = x
    conv = cw_ref[CONV_W - 1:CONV_W, :] * x
    for j in range(CONV_W - 1):
        conv = conv + cw_ref[j:j + 1, :] * xbuf[j * nb:j * nb + rows, :]
    cv_out[...] = xbuf[rows:rows + tail, :].reshape(CONV_W - 1, nb, LRU_WIDTH)
    xbuf[0:tail, :] = x[rows - tail:, :]
    a, b = _lru_gates(conv + cb_ref[...], wa_ref, wx_ref, ba_ref[...], bx_ref[...], lam_ref[...])
    a_buf[...] = a
    b_buf[...] = b

    for j in range(SSM_BLK):
        sl = slice(j * SSM_BLK_STATE, (j + 1) * SSM_BLK_STATE)
        a_re = jnp.broadcast_to(are_ref[:, sl], (nb, SSM_BLK_STATE))
        a_im = jnp.broadcast_to(aim_ref[:, sl], (nb, SSM_BLK_STATE))

        def body(t, carry):
            hr, hi = carry
            r = pl.ds(pl.multiple_of(t * nb, nb), nb)
            nr = a_re * hr - a_im * hi + bu_re[r, sl]
            ni = a_re * hi + a_im * hr + bu_im[r, sl]
            bu_re[r, sl] = nr
            bu_im[r, sl] = ni
            return nr, ni

        hr, hi = lax.fori_loop(0, tc, body, (h_re[:, sl], h_im[:, sl]), unroll=4)
        h_re[:, sl] = hr
        h_im[:, sl] = hi

    def lru_body(t, h):
        r = pl.ds(pl.multiple_of(t * nb, nb), nb)
        h = a_buf[r, :] * h + b_buf[r, :]
        b_buf[r, :] = h
        return h

    h = lax.fori_loop(0, tc, lru_body, h_scr[...], unroll=8)
    h_scr[...] = h
    h_out[...] = h

    y = _s5_readout(u, bu_re[...], bu_im[...], cre_ref, cim_ref, d_ref[...], gw_ref, gb_ref[...])
    y_ref[...] = _from_time_major(y, nb)
    hre_ref[...] = h_re[...]
    him_ref[...] = h_im[...]
    yd = b_buf[...] * jax.nn.gelu(_to_time_major(g_ref[...]))
    yd_ref[...] = _from_time_major(yd, nb)


def _scan_mixers(u_t, x_t, g_t, s5_w, lru_w, layer, tc):
    nb, t, _ = u_t.shape
    rows = tc * nb
    blk = pl.BlockSpec((nb, tc, SSM_WIDTH), lambda i: (0, i, 0))
    st = pl.BlockSpec((nb, SSM_STATE), lambda i: (0, 0))
    sds = lambda *s: jax.ShapeDtypeStruct(s, F32)
    return pl.pallas_call(
        functools.partial(_scan_mixers_kernel, nb=nb),
        grid=(t // tc,),
        in_specs=[blk, blk, blk] + [_layer_spec(a, layer) for a in (*s5_w, *lru_w)],
        out_specs=[blk, st, st, blk,
                   pl.BlockSpec((nb, LRU_WIDTH), lambda i: (0, 0)),
                   pl.BlockSpec((CONV_W - 1, nb, LRU_WIDTH), lambda i: (0, 0, 0))],
        out_shape=[sds(nb, t, SSM_WIDTH), sds(nb, SSM_STATE), sds(nb, SSM_STATE),
                   sds(nb, t, LRU_WIDTH), sds(nb, LRU_WIDTH), sds(CONV_W - 1, nb, LRU_WIDTH)],
        scratch_shapes=[pltpu.VMEM((rows, SSM_STATE), F32), pltpu.VMEM((rows, SSM_STATE), F32),
                        pltpu.VMEM((nb, SSM_STATE), F32), pltpu.VMEM((nb, SSM_STATE), F32),
                        pltpu.VMEM(((CONV_W - 1) * nb + rows, LRU_WIDTH), F32),
                        pltpu.VMEM((rows, LRU_WIDTH), F32), pltpu.VMEM((rows, LRU_WIDTH), F32),
                        pltpu.VMEM((nb, LRU_WIDTH), F32)],
        compiler_params=_cparams("arbitrary"),
        name="scan_mixers",
    )(u_t, x_t, g_t, *s5_w, *lru_w)


def _lru_gates(cx, wa_ref, wx_ref, ba_row, bx_row, lam_row):
    r = _sigmoid(_bdot(cx, wa_ref[...]) + ba_row)
    i = _sigmoid(_bdot(cx, wx_ref[...]) + bx_row)
    log_a = -LRU_C * r * _softplus(-lam_row)
    a = jnp.exp(log_a)
    th = jnp.tanh(log_a)
    b = jnp.sqrt(-2.0 * th / (1.0 - th)) * (i * cx)
    return a, b


def _merge_kernel(x_ref, sh_ref, sc_ref, gt_ref, ya_ref, yb_ref, yc_ref, yd_ref, wmg_ref, wb_ref, wo_ref,
                  g_ref, b_ref, o_ref, *, alpha, parts):
    ks = range(parts)
    xs, shs, scs, gts = (_row_parts(r, parts) for r in (x_ref, sh_ref, sc_ref, gt_ref))
    hs = [(_ln(xs[k]) * (1.0 + scs[k]) + shs[k]).astype(BF16) for k in ks]
    accs = [jnp.zeros(xs[k].shape, F32) for k in ks]
    for n, y_ref in enumerate((ya_ref, yb_ref, yc_ref, yd_ref)):
        ys = _row_parts(y_ref, parts)
        logits = [jnp.dot(hs[k], wmg_ref[:, n * D_MODEL:(n + 1) * D_MODEL], preferred_element_type=F32)
                  for k in ks]
        accs = [accs[k] + _sigmoid(logits[k]) * _bdot(ys[k], wb_ref[n]) for k in ks]
    outs = [_bdot(accs[k], wo_ref[...]) for k in ks]
    hm = x_ref.shape[0] // parts
    for k in ks:
        z = alpha * xs[k] + gts[k] * outs[k]
        o_ref[k * hm:(k + 1) * hm, :] = _ln(z) * g_ref[...] + b_ref[...]


def _merge(x, mod, ya, yb, yc, yd, w_mg, w_branch, w_out, ln_g, ln_b, alpha, tm):
    n = x.shape[0]
    l = mod.layer
    row = lambda w: pl.BlockSpec((tm, w), lambda i: (i, 0))
    return pl.pallas_call(
        functools.partial(_merge_kernel, alpha=alpha, parts=2 if tm >= 512 else 1),
        grid=(n // tm,),
        in_specs=[
            row(D_MODEL), mod.spec(1, 0, tm), mod.spec(1, 1, tm), mod.spec(1, 2, tm),
            row(BRANCH_W), row(BRANCH_W), row(BRANCH_W), row(BRANCH_W),
            _resident((None, D_MODEL, MG_WIDTH), lambda i: (l, 0, 0)),
            _resident((None, N_BRANCH, BRANCH_W, D_MODEL), lambda i: (l, 0, 0, 0)),
            _resident((None, D_MODEL, D_MODEL), lambda i: (l, 0, 0)),
            pl.BlockSpec((None, None, 1, D_MODEL), lambda i: (l, 1, 0, 0)),
            pl.BlockSpec((None, None, 1, D_MODEL), lambda i: (l, 1, 0, 0)),
        ],
        out_specs=row(D_MODEL),
        out_shape=jax.ShapeDtypeStruct((n, D_MODEL), F32),
        compiler_params=_cparams("parallel"),
        name="merge",
    )(x, mod.arr, mod.arr, mod.arr, ya, yb, yc, yd, w_mg, w_branch, w_out, ln_g, ln_b)


def _smix1_kernel(ret_ref, gqkv_ref, gab_ref, su_ref, lx_ref, cos_ref, sin_ref,
                  gcv_ref, sre_ref, sim_ref, lru_ref, lcv_ref,
                  gcw_ref, alog_ref, dtb_ref,
                  are_ref, aim_ref, bre_ref, bim_ref, cre_ref, cim_ref, d_ref, gw_ref, gb_ref,
                  lcw_ref, lcb_ref, wa_ref, wx_ref, ba_ref, bx_ref, lam_ref,
                  qkt_ref, gv_ref, gx_ref, yc_ref, lh_ref, gcv_out, sre_out, sim_out, lcv_out):
    ret = ret_ref[...]
    rq = _rotate(ret[:, 0:RET_QK], cos_ref[...], sin_ref[...])
    rk = _rotate(ret[:, RET_QK:2 * RET_QK], cos_ref[...], sin_ref[...]) * (RET_DK ** -0.5)

    x = gqkv_ref[...]
    conv = gcw_ref[3:4, :] * x
    for j in range(CONV_W - 1):
        conv = conv + gcw_ref[j:j + 1, :] * gcv_ref[j]
    gcv_out[0] = gcv_ref[1]
    gcv_out[1] = gcv_ref[2]
    gcv_out[2] = x
    cq = _gdn_activate(conv)
    gq, gk = cq[:, :GDN_K], cq[:, GDN_K:2 * GDN_K]
    gv_ref[...] = cq[:, 2 * GDN_K:]
    gl = _gate_lanes(gab_ref[...], alog_ref[...], dtb_ref[...])
    gx = jnp.concatenate([jnp.broadcast_to(gl[:, r:r + 1], (gl.shape[0], LANE))
                          for r in range(2 * GDN_HEADS)], axis=1)
    gx_ref[...] = jnp.where(_iota(gx.shape, 1) < GDN_HEADS * LANE, jnp.exp(gx), gx)
    qkt_ref[...] = jnp.concatenate([rq, rk, gq, gk], axis=1).T

    u = su_ref[...]
    h_re, h_im = [], []
    for j in range(SSM_BLK):
        sl = slice(j * SSM_BLK_STATE, (j + 1) * SSM_BLK_STATE)
        uj = u[:, j * LANE:(j + 1) * LANE]
        a_re, a_im = are_ref[:, sl], aim_ref[:, sl]
        p_re, p_im = sre_ref[:, sl], sim_ref[:, sl]
        h_re.append(a_re * p_re - a_im * p_im + _bdot(uj, bre_ref[j]))
        h_im.append(a_re * p_im + a_im * p_re + _bdot(uj, bim_ref[j]))
    h_re = jnp.concatenate(h_re, axis=1)
    h_im = jnp.concatenate(h_im, axis=1)
    sre_out[...] = h_re
    sim_out[...] = h_im
    yc_ref[...] = _s5_readout(u, h_re, h_im, cre_ref, cim_ref, d_ref[...], gw_ref, gb_ref[...])

    lx = lx_ref[...]
    conv = lcw_ref[3:4, :] * lx
    for j in range(CONV_W - 1):
        conv = conv + lcw_ref[j:j + 1, :] * lcv_ref[j]
    lcv_out[0] = lcv_ref[1]
    lcv_out[1] = lcv_ref[2]
    lcv_out[2] = lx
    a, b = _lru_gates(conv + lcb_ref[...], wa_ref, wx_ref, ba_ref[...], bx_ref[...], lam_ref[...])
    lh_ref[...] = a * lru_ref[...] + b


def _smix1(pieces, cos_row, sin_row, states, gdn_w, s5_w, lru_w, layer):
    rows = pieces["ret"].shape[0]
    acts = [pieces["ret"], pieces["gqkv"], pieces["gab"], pieces["su"], pieces["lx"], cos_row, sin_row]
    stacked = [*states, *gdn_w, *s5_w, *lru_w]
    ins = acts + stacked
    full = lambda a: pl.BlockSpec(a.shape, lambda i: (0,) * a.ndim)
    sds = lambda *s: jax.ShapeDtypeStruct(s, F32)
    out_shape = [sds(QKT_ROWS, rows), sds(rows, GDN_HEADS * GDN_DV), sds(rows, 8 * LANE),
                 sds(rows, SSM_WIDTH), sds(rows, LRU_WIDTH),
                 sds(CONV_W - 1, rows, GDN_QKV), sds(rows, SSM_STATE), sds(rows, SSM_STATE),
                 sds(CONV_W - 1, rows, LRU_WIDTH)]
    return pl.pallas_call(
        _smix1_kernel,
        grid=(1,),
        in_specs=[full(a) for a in acts] + [_layer_spec(a, layer) for a in stacked],
        out_specs=[pl.BlockSpec(s.shape, lambda i, nd=len(s.shape): (0,) * nd) for s in out_shape],
        out_shape=out_shape,
        compiler_params=_cparams("arbitrary"),
        name="sample_mix",
    )(*ins)


def _smix2_kernel(qkt_ref, rv_ref, gv_ref, gx_ref, rg_ref, gz_ref, lh_ref, lg_ref, nw_ref, sret_ref, sgdn_ref,
                  *refs, first):
    ya_ref, yb_ref, yd_ref, nret_ref, ngdn_ref, o_ret, o_gdn = refs[-7:]
    if first:
        for l in range(1, nret_ref.shape[0]):
            nret_ref[l] = jnp.zeros(nret_ref.shape[1:], F32)
            ngdn_ref[l] = jnp.zeros(ngdn_ref.shape[1:], F32)
        nret_ref, ngdn_ref = nret_ref.at[0], ngdn_ref.at[0]
    i = pl.program_id(0)
    bt = rv_ref.shape[0]
    rows = qkt_ref.shape[1]
    gammas = _ret_gammas()
    q_hi, q_lo = _split(qkt_ref[...])
    qk_split = jnp.concatenate([q_hi, q_lo], axis=1)

    for j in range(bt):
        b = i * bt + j
        onehot = jnp.where(_iota((2 * rows, LANE), 0) % rows == b, 1.0, 0.0).astype(BF16)
        cols = jnp.dot(qk_split, onehot, preferred_element_type=F32)
        r = slice(j, j + 1)
        for h in range(RET_HEADS):
            q_c = cols[h * RET_DK:(h + 1) * RET_DK]
            k_c = cols[RET_QK + h * RET_DK:RET_QK + (h + 1) * RET_DK]
            v_r = rv_ref[r, h * RET_DV:(h + 1) * RET_DV]
            s_new = gammas[h] * sret_ref[j, h] + k_c * v_r
            nret_ref[j, h] = s_new
            o_ret[r, h * RET_DV:(h + 1) * RET_DV] = jnp.sum(q_c * s_new, axis=0, keepdims=True)
        base = 2 * RET_QK
        for h in range(GDN_HEADS):
            q_c = cols[base + h * GDN_DK:base + (h + 1) * GDN_DK]
            k_c = cols[base + GDN_K + h * GDN_DK:base + GDN_K + (h + 1) * GDN_DK]
            v_r = gv_ref[r, h * GDN_DV:(h + 1) * GDN_DV]
            e_g = gx_ref[r, h * LANE:(h + 1) * LANE]
            beta = gx_ref[r, (GDN_HEADS + h) * LANE:(GDN_HEADS + h + 1) * LANE]
            s = sgdn_ref[j, h]
            v_new = beta * (v_r - e_g * jnp.sum(k_c * s, axis=0, keepdims=True))
            s_new = e_g * s + k_c * v_new
            ngdn_ref[j, h] = s_new
            o_gdn[r, h * GDN_DV:(h + 1) * GDN_DV] = jnp.sum(q_c * s_new, axis=0, keepdims=True)
    rg, gz = rg_ref[...], gz_ref[...]
    o_r, o_g = o_ret[...], o_gdn[...]
    ya_ref[...] = jnp.concatenate(
        [_ln(o_r[:, h * RET_DV:(h + 1) * RET_DV]) for h in range(RET_HEADS)], axis=1) * _silu(rg)
    yb_ref[...] = jnp.concatenate(
        [_rms(o_g[:, h * GDN_DV:(h + 1) * GDN_DV]) * nw_ref[...] for h in range(GDN_HEADS)], axis=1) * _silu(gz)
    yd_ref[...] = lh_ref[...] * jax.nn.gelu(lg_ref[...])


def _smix2(qkt, rv, gv, gx, rg, gz, lh, lg, norm_w, s_ret, s_gdn, layer, bt, prev):
    rows = rv.shape[0]
    row = lambda w: pl.BlockSpec((bt, w), lambda i: (i, 0))
    sds = lambda *s: jax.ShapeDtypeStruct(s, F32)
    ret_blk = pl.BlockSpec((None, bt, RET_HEADS, RET_DK, RET_DV), lambda i: (layer, i, 0, 0, 0))
    gdn_blk = pl.BlockSpec((None, bt, GDN_HEADS, GDN_DK, GDN_DV), lambda i: (layer, i, 0, 0, 0))
    ins = [qkt, rv, gv, gx, rg, gz, lh, lg, norm_w, s_ret, s_gdn]
    in_specs = [pl.BlockSpec(qkt.shape, lambda i: (0, 0)),
                pl.BlockSpec((bt, RET_V), lambda i: (i, 1)),
                row(GDN_HEADS * GDN_DV), row(8 * LANE),
                pl.BlockSpec((bt, RET_V), lambda i: (i, 2)),
                row(GDN_HEADS * GDN_DV), row(LRU_WIDTH), row(LRU_WIDTH),
                _layer_spec(norm_w, layer), ret_blk, gdn_blk]
    aliases = {}
    if prev is None:
        depth = s_ret.shape[0]
        out_ret = pl.BlockSpec((depth, bt, RET_HEADS, RET_DK, RET_DV), lambda i: (0, i, 0, 0, 0))
        out_gdn = pl.BlockSpec((depth, bt, GDN_HEADS, GDN_DK, GDN_DV), lambda i: (0, i, 0, 0, 0))
    else:
        out_ret, out_gdn = ret_blk, gdn_blk
        aliases = {len(ins): 3, len(ins) + 1: 4}
        ins += list(prev)
        in_specs += [pl.BlockSpec(memory_space=pl.ANY)] * 2
    return pl.pallas_call(
        functools.partial(_smix2_kernel, first=prev is None),
        grid=(rows // bt,),
        in_specs=in_specs,
        out_specs=[row(RET_V), row(GDN_HEADS * GDN_DV), row(LRU_WIDTH), out_ret, out_gdn],
        out_shape=[sds(rows, RET_V), sds(rows, GDN_HEADS * GDN_DV), sds(rows, LRU_WIDTH),
                   sds(*s_ret.shape), sds(*s_gdn.shape)],
        input_output_aliases=aliases,
        scratch_shapes=[pltpu.VMEM((bt, RET_V), F32), pltpu.VMEM((bt, GDN_HEADS * GDN_DV), F32)],
        compiler_params=_cparams("parallel"),
        name="sample_state",
    )(*ins)


def _block_diag(w):
    nb, bs, _ = w.shape
    return jnp.einsum("nij,nm->nimj", w, jnp.eye(nb, dtype=w.dtype)).reshape(nb * bs, nb * bs)


def kernel(x_prompt, x_sample, c_prompt, c_sample, state_ret, state_gdn, state_gdn_conv, state_ssm_re, state_ssm_im, state_lru, state_lru_conv, w_ada, b_ada, ln_g, ln_b, w_ffn_up, w_ffn_down, w_in, gdn_conv_w, gdn_a_log, gdn_dt_bias, gdn_norm_w, ssm_lam_re, ssm_lam_im, ssm_log_step, ssm_b_re, ssm_b_im, ssm_c_re, ssm_c_im, ssm_d, ssm_glu_w, ssm_glu_b, lru_conv_w, lru_conv_b, lru_wa, lru_ba, lru_wx, lru_bx, lru_lam, w_branch, w_out):
    nb, t, _ = x_prompt.shape
    ns = x_sample.shape[0]
    depth = w_ada.shape[0]
    assert t % RET_CHUNK == 0 and x_sample.shape[1] == 1
    alpha = (2 * depth) ** 0.25
    tm = 256 if t % 256 == 0 else RET_CHUNK
    tm_ffn = 512 if t % 512 == 0 else tm
    tc = 64

    wup = w_ffn_up.astype(BF16)
    wdn = w_ffn_down.astype(BF16)
    wbr = w_branch.astype(BF16)
    wout = w_out.astype(BF16)
    w_packed, w_mg = _pack_w_in(w_in)
    ln_g4 = ln_g.reshape(depth, N_SUB, 1, D_MODEL)
    ln_b4 = ln_b.reshape(depth, N_SUB, 1, D_MODEL)

    mod_p, mod_s = _ada(c_prompt, c_sample, w_ada, b_ada)
    mod_p = mod_p.reshape(depth, nb, 3 * N_SUB, 1, D_MODEL).transpose(0, 2, 1, 3, 4)

    row3 = lambda a: a.reshape(depth, 1, -1)
    lane_rows = lambda v: jnp.pad(v, ((0, 0), (0, LANE - v.shape[1]))).reshape(depth, 1, LANE)
    alog_rows, dtb_rows = lane_rows(gdn_a_log), lane_rows(gdn_dt_bias)
    norm_w = row3(gdn_norm_w)
    s5_w = (*jax.vmap(_s5_params)(ssm_lam_re, ssm_lam_im, ssm_log_step, ssm_b_re, ssm_b_im, ssm_c_re, ssm_c_im),
            row3(ssm_d), ssm_glu_w.astype(BF16), row3(ssm_glu_b))
    lru_w = (lru_conv_w, row3(lru_conv_b),
             jax.vmap(_block_diag)(lru_wa).astype(BF16), jax.vmap(_block_diag)(lru_wx).astype(BF16),
             row3(lru_ba), row3(lru_bx), row3(lru_lam))
    s_states = (state_gdn_conv.transpose(0, 2, 1, 3),
                state_ssm_re.reshape(depth, ns, SSM_STATE), state_ssm_im.reshape(depth, ns, SSM_STATE),
                state_lru, state_lru_conv.transpose(0, 2, 1, 3))

    cos_s, sin_s = _rope_tables(jnp.full((1,), PAST_LEN))
    xp = x_prompt.reshape(nb * t, D_MODEL)
    xs = x_sample.reshape(ns, D_MODEL)
    new_p, new_s = [], []
    big_s = None
    for l in range(depth):
        mp = _Mod(mod_p, l, False, t)
        ms = _Mod(mod_s, l, True)

        xp = _ffn(xp, mp, 0, 0, wup, wdn, ln_g4, ln_b4, alpha, tm_ffn)
        pc = _inproj(xp, mp, w_packed, tm, conv=(gdn_conv_w, nb, t))
        gcv_p = pc["gdn_conv"]
        ya, ret_p = _retention_prompt(pc["ret"], nb, t)
        yb, gdn_p = _gdn_prompt(pc["gqkv"], pc["gab"], pc["gz"], alog_rows, dtb_rows, norm_w, l, nb, t)
        to_t = lambda a: a.reshape(nb, t, -1)
        from_t = lambda a: a.reshape(nb * t, -1)
        yc_t, sre_p, sim_p, yd_t, lru_p, lcv_p = _scan_mixers(
            to_t(pc["su"]), to_t(pc["lx"]), to_t(pc["lg"]), s5_w, lru_w, l, tc)
        xp = _merge(xp, mp, ya, yb, from_t(yc_t), from_t(yd_t), w_mg, wbr, wout, ln_g4, ln_b4, alpha, tm_ffn)
        xp = _ffn(xp, mp, 2, 1, wup, wdn, ln_g4, ln_b4, alpha, tm_ffn)
        new_p.append((ret_p, gdn_p, gcv_p,
                      sre_p.reshape(nb, SSM_GROUPS, SSM_P), sim_p.reshape(nb, SSM_GROUPS, SSM_P),
                      lru_p, lcv_p.transpose(1, 0, 2)))

        xs = _ffn(xs, ms, 0, 0, wup, wdn, ln_g4, ln_b4, alpha, ns)
        sc = _inproj(xs, ms, w_packed, ns)
        (qkt, gv, gx, yc, lh, gcv_s, sre_s, sim_s, lcv_s) = _smix1(
            sc, cos_s, sin_s, s_states, (gdn_conv_w, alog_rows, dtb_rows), s5_w, lru_w, l)
        ya, yb, yd, *big_s = _smix2(qkt, sc["ret"], gv, gx, sc["ret"], sc["gz"], lh, sc["lg"],
                                    norm_w, state_ret, state_gdn, l, SUBLANE, big_s)
        xs = _merge(xs, ms, ya, yb, yc, yd, w_mg, wbr, wout, ln_g4, ln_b4, alpha, ns)
        xs = _ffn(xs, ms, 2, 1, wup, wdn, ln_g4, ln_b4, alpha, ns)
        new_s.append((gcv_s.transpose(1, 0, 2),
                      sre_s.reshape(ns, SSM_GROUPS, SSM_P), sim_s.reshape(ns, SSM_GROUPS, SSM_P),
                      lh, lcv_s.transpose(1, 0, 2)))

    ret_p, gdn_p, gcv_p, sre_p, sim_p, lru_p, lcv_p = [jnp.stack(z) for z in zip(*new_p)]
    gcv_s, sre_s, sim_s, lru_s, lcv_s = [jnp.stack(z) for z in zip(*new_s)]
    ret_s, gdn_s = big_s
    return (xp.reshape(nb, t, D_MODEL), xs.reshape(ns, 1, D_MODEL),
            ret_p, ret_s, gdn_p, gdn_s, gcv_p, gcv_s, sre_p, sre_s, sim_p, sim_s,
            lru_p, lru_s, lcv_p, lcv_s)
```

```python
import functools
import math

import numpy as np
import jax
import jax.numpy as jnp
from jax import lax
from jax.experimental import pallas as pl
from jax.experimental.pallas import tpu as pltpu

F32 = jnp.float32
BF16 = jnp.bfloat16

D_MODEL = 1024
RET_HEADS, RET_DK, RET_DV, RET_CHUNK = 4, 64, 128, 128
RET_STEP_CHUNKS = 4
ROPE_BASE = 10000.0
GDN_HEADS, GDN_DK, GDN_DV, GDN_CHUNK = 4, 128, 128, 64
GDN_STEP_CHUNKS = 2
GDN_QKV = 2 * GDN_HEADS * GDN_DK + GDN_HEADS * GDN_DV
CONV_W = 4
SSM_GROUP, SSM_GROUPS, SSM_P = 16, 32, 64
SSM_WIDTH = SSM_GROUP * SSM_GROUPS
SSM_STATE = SSM_GROUPS * SSM_P
LRU_WIDTH = 512
LRU_C = 8.0
N_BRANCH, BRANCH_W = 4, 512
D_FF = 2816
N_SUB = 3
LN_EPS = 1e-5
PAST_LEN = 16384

RET_QK = RET_HEADS * RET_DK
RET_V = RET_HEADS * RET_DV
RET_W = 2 * RET_QK + 2 * RET_V
GDN_K = GDN_HEADS * GDN_DK
QKT_ROWS = 2 * RET_QK + 2 * GDN_K
LANE = 128
SUBLANE = 8
FF_CHUNKS = ((0, 768), (768, 1024), (1792, 1024))
VMEM_LIMIT = 56 * 1024 * 1024


def _cparams(*sem):
    return pltpu.CompilerParams(dimension_semantics=sem, vmem_limit_bytes=VMEM_LIMIT)


def _layer_spec(a, layer):
    nd = a.ndim
    return pl.BlockSpec((None,) + a.shape[1:], lambda *_: (layer,) + (0,) * (nd - 1))


def _resident(shape, index_map):
    return pl.BlockSpec(shape, index_map, pipeline_mode=pl.Buffered(1))


def _ln(x):
    mu = jnp.mean(x, axis=-1, keepdims=True)
    xc = x - mu
    return xc * lax.rsqrt(jnp.mean(xc * xc, axis=-1, keepdims=True) + LN_EPS)


def _sigmoid(x):
    return 0.5 * jnp.tanh(0.5 * x) + 0.5


def _silu(x):
    return x * _sigmoid(x)


def _softplus(x):
    return jnp.maximum(x, 0.0) + jnp.log1p(jnp.exp(-jnp.abs(x)))


def _bdot(a, w):
    return jnp.dot(a.astype(BF16), w.astype(BF16), preferred_element_type=F32)


def _split(x):
    hi = x.astype(BF16)
    return hi, (x - hi.astype(F32)).astype(BF16)


def _dot_nt(a, b):
    return lax.dot_general(a, b, (((1,), (1,)), ((), ())), preferred_element_type=F32)


def _iota(shape, dim):
    return lax.broadcasted_iota(jnp.int32, shape, dim)


def _to_time_major(x):
    nb, tc, width = x.shape
    return jnp.swapaxes(x, 0, 1).reshape(tc * nb, width)


def _from_time_major(y, nb):
    rows, width = y.shape
    return jnp.swapaxes(y.reshape(rows // nb, nb, width), 0, 1)


def _ada_kernel(cp_ref, cs_ref, w_ref, b_ref, op_ref, os_ref):
    w = w_ref[...].astype(BF16)
    op_ref[...] = _bdot(_silu(cp_ref[...]), w) + b_ref[...]
    os_ref[...] = _bdot(_silu(cs_ref[...]), w) + b_ref[...]


def _ada(c_prompt, c_sample, w_ada, b_ada):
    depth, _, n_out = w_ada.shape
    tn = 1152
    cond = lambda c: pl.BlockSpec(c.shape, lambda l, j: (0, 0))
    out = lambda c: pl.BlockSpec((None, c.shape[0], tn), lambda l, j: (l, 0, j))
    return pl.pallas_call(
        _ada_kernel,
        grid=(depth, n_out // tn),
        in_specs=[
            cond(c_prompt), cond(c_sample),
            pl.BlockSpec((None, D_MODEL, tn), lambda l, j: (l, 0, j)),
            pl.BlockSpec((None, 1, tn), lambda l, j: (l, 0, j)),
        ],
        out_specs=[out(c_prompt), out(c_sample)],
        out_shape=[jax.ShapeDtypeStruct((depth, c.shape[0], n_out), F32) for c in (c_prompt, c_sample)],
        compiler_params=_cparams("parallel", "parallel"),
        name="ada",
    )(c_prompt, c_sample, w_ada, b_ada.reshape(depth, 1, n_out))


class _Mod:
    def __init__(self, arr, layer, per_row, rows_per_batch=None):
        self.arr, self.layer, self.per_row, self.rpb = arr, layer, per_row, rows_per_batch

    def spec(self, sub, which, tm):
        k, l = 3 * sub + which, self.layer
        if self.per_row:
            return pl.BlockSpec((None, tm, D_MODEL), lambda i: (l, i, k))
        tiles = self.rpb // tm
        return pl.BlockSpec((None, None, None, 1, D_MODEL), lambda i: (l, k, i // tiles, 0, 0))


def _row_parts(ref, parts):
    if ref.shape[0] == 1:
        return [ref[...]] * parts
    hm = ref.shape[0] // parts
    return [ref[k * hm:(k + 1) * hm, :] for k in range(parts)]


def _ffn_kernel(x_ref, sh_ref, sc_ref, gt_ref, wup_ref, wdn_ref, g_ref, b_ref, o_ref, *, alpha, parts):
    ks = range(parts)
    xs, shs, scs, gts = (_row_parts(r, parts) for r in (x_ref, sh_ref, sc_ref, gt_ref))
    hs = [(_ln(xs[k]) * (1.0 + scs[k]) + shs[k]).astype(BF16) for k in ks]
    accs = [jnp.zeros(xs[k].shape, F32) for k in ks]
    for lo, width in FF_CHUNKS:
        a = [jnp.dot(hs[k], wup_ref[:, lo:lo + width], preferred_element_type=F32) for k in ks]
        b = [jnp.dot(hs[k], wup_ref[:, D_FF + lo:D_FF + lo + width], preferred_element_type=F32) for k in ks]
        act = [(_silu(a[k]) * b[k]).astype(BF16) for k in ks]
        accs = [accs[k] + jnp.dot(act[k], wdn_ref[lo:lo + width, :], preferred_element_type=F32) for k in ks]
    hm = x_ref.shape[0] // parts
    for k in ks:
        z = alpha * xs[k] + 0.5 * gts[k] * accs[k]
        o_ref[k * hm:(k + 1) * hm, :] = _ln(z) * g_ref[...] + b_ref[...]


def _ffn(x, mod, sub, which, wup, wdn, ln_g, ln_b, alpha, tm):
    n = x.shape[0]
    l = mod.layer
    row = pl.BlockSpec((tm, D_MODEL), lambda i: (i, 0))
    return pl.pallas_call(
        functools.partial(_ffn_kernel, alpha=alpha, parts=2 if tm >= 512 else 1),
        grid=(n // tm,),
        in_specs=[
            row, mod.spec(sub, 0, tm), mod.spec(sub, 1, tm), mod.spec(sub, 2, tm),
            _resident((None, None, D_MODEL, 2 * D_FF), lambda i: (l, which, 0, 0)),
            _resident((None, None, D_FF, D_MODEL), lambda i: (l, which, 0, 0)),
            pl.BlockSpec((None, None, 1, D_MODEL), lambda i: (l, sub, 0, 0)),
            pl.BlockSpec((None, None, 1, D_MODEL), lambda i: (l, sub, 0, 0)),
        ],
        out_specs=row,
        out_shape=jax.ShapeDtypeStruct((n, D_MODEL), F32),
        compiler_params=_cparams("parallel"),
        name="ffn",
    )(x, mod.arr, mod.arr, mod.arr, wup, wdn, ln_g, ln_b)


_IN_PIECES = (("ret", RET_W), ("gqkv", GDN_QKV), ("gab", LANE), ("gz", 512), ("su", SSM_WIDTH),
              ("lx", LRU_WIDTH), ("lg", LRU_WIDTH))
IN_MAIN = sum(w for _, w in _IN_PIECES)
MG_WIDTH = N_BRANCH * D_MODEL


PACK_ROWS = 512


def _pack_kernel(wt_ref, main_ref, mg_ref):
    gate_lo = RET_W + GDN_QKV
    gate_hi = gate_lo + 2 * GDN_HEADS
    mg_lo = wt_ref.shape[0] - MG_WIDTH

    def move(dst_ref, dst, src, width):
        for c in range(0, width, PACK_ROWS):
            dst_ref[:, dst + c:dst + c + PACK_ROWS] = wt_ref[src + c:src + c + PACK_ROWS, :].T.astype(BF16)

    move(main_ref, 0, 0, gate_lo)
    g = wt_ref[gate_lo:gate_lo + LANE, :].T
    main_ref[:, gate_lo:gate_lo + LANE] = jnp.where(_iota(g.shape, 1) < 2 * GDN_HEADS, g, 0.0).astype(BF16)
    move(main_ref, gate_lo + LANE, gate_hi, mg_lo - gate_hi)
    move(mg_ref, 0, mg_lo, MG_WIDTH)


def _pack_w_in(w_in):
    depth, d_in, d_out = w_in.shape
    tk = 256
    blk = lambda w: pl.BlockSpec((None, tk, w), lambda l, i: (l, i, 0))
    return pl.pallas_call(
        _pack_kernel,
        grid=(depth, d_in // tk),
        in_specs=[pl.BlockSpec((None, d_out, tk), lambda l, i: (l, 0, i))],
        out_specs=[blk(IN_MAIN), blk(MG_WIDTH)],
        out_shape=[jax.ShapeDtypeStruct((depth, d_in, IN_MAIN), BF16),
                   jax.ShapeDtypeStruct((depth, d_in, MG_WIDTH), BF16)],
        compiler_params=_cparams("parallel", "parallel"),
        name="pack_w_in",
    )(jnp.swapaxes(w_in, 1, 2))


def _l2n(x):
    return x * lax.rsqrt(jnp.sum(x * x, axis=-1, keepdims=True) + 1e-6)


def _gdn_activate(conv):
    cq = _silu(conv)
    q = [_l2n(cq[:, h * GDN_DK:(h + 1) * GDN_DK]) * (GDN_DK ** -0.5) for h in range(GDN_HEADS)]
    k = [_l2n(cq[:, GDN_K + h * GDN_DK:GDN_K + (h + 1) * GDN_DK]) for h in range(GDN_HEADS)]
    return jnp.concatenate(q + k + [cq[:, 2 * GDN_K:]], axis=1)


def _inproj_kernel(x_ref, sh_ref, sc_ref, w_ref, *refs, seq_tiles):
    parts = 2 if x_ref.shape[0] >= 512 else 1
    xs, shs, scs = (_row_parts(r, parts) for r in (x_ref, sh_ref, sc_ref))
    hs = [(_ln(xs[k]) * (1.0 + scs[k]) + shs[k]).astype(BF16) for k in range(parts)]
    n_out = len(_IN_PIECES)
    if seq_tiles is None:
        o_refs = refs
    else:
        cw_ref, o_refs, cv_ref, xbuf = refs[0], refs[1:1 + n_out], refs[1 + n_out], refs[2 + n_out]

        @pl.when(pl.program_id(0) % seq_tiles == 0)
        def _():
            xbuf[0:SUBLANE, :] = jnp.zeros((SUBLANE, GDN_QKV), F32)

    offsets = np.cumsum([0] + [w for _, w in _IN_PIECES])
    order = sorted(range(n_out), key=lambda p: _IN_PIECES[p][0] != "gqkv")
    for p in order:
        (name, width), o_ref, lo = _IN_PIECES[p], o_refs[p], int(offsets[p])
        y = jnp.concatenate([jnp.dot(h, w_ref[:, lo:lo + width], preferred_element_type=F32) for h in hs], axis=0)
        if name == "gqkv" and seq_tiles is not None:
            tm = y.shape[0]
            xbuf[SUBLANE:SUBLANE + tm, :] = y
            conv = cw_ref[CONV_W - 1:CONV_W, :] * y
            for j in range(CONV_W - 1):
                r0 = SUBLANE - (CONV_W - 1) + j
                conv = conv + cw_ref[j:j + 1, :] * xbuf[r0:r0 + tm, :]
            cv_ref[0] = xbuf[SUBLANE + tm - (CONV_W - 1):SUBLANE + tm, :]
            xbuf[0:SUBLANE, :] = y[tm - SUBLANE:, :]
            y = _gdn_activate(conv)
        o_ref[...] = y


def _inproj(x, mod, w_packed, tm, conv=None):
    n = x.shape[0]
    l = mod.layer
    widths = [w for _, w in _IN_PIECES]
    ins = [x, mod.arr, mod.arr, w_packed]
    in_specs = [pl.BlockSpec((tm, D_MODEL), lambda i: (i, 0)),
                mod.spec(1, 0, tm), mod.spec(1, 1, tm),
                _resident((None, D_MODEL, IN_MAIN), lambda i: (l, 0, 0))]
    out_specs = [pl.BlockSpec((tm, w), lambda i: (i, 0)) for w in widths]
    out_shape = [jax.ShapeDtypeStruct((n, w), F32) for w in widths]
    scratch, seq_tiles = [], None
    if conv is not None:
        conv_w, nb, t = conv
        seq_tiles = t // tm
        ins.append(conv_w)
        in_specs.append(_layer_spec(conv_w, l))
        out_specs.append(pl.BlockSpec((1, CONV_W - 1, GDN_QKV), lambda i: (i // seq_tiles, 0, 0)))
        out_shape.append(jax.ShapeDtypeStruct((nb, CONV_W - 1, GDN_QKV), F32))
        scratch.append(pltpu.VMEM((SUBLANE + tm, GDN_QKV), F32))
    outs = pl.pallas_call(
        functools.partial(_inproj_kernel, seq_tiles=seq_tiles),
        grid=(n // tm,),
        in_specs=in_specs,
        out_specs=out_specs,
        out_shape=out_shape,
        scratch_shapes=scratch,
        compiler_params=_cparams("arbitrary"),
        name="inproj",
    )(*ins)
    return dict(zip([nm for nm, _ in _IN_PIECES] + ["gdn_conv"], outs))


def _rope_tables(pos):
    half = RET_DK // 2
    inv = ROPE_BASE ** (-jnp.arange(half, dtype=F32) / half)
    ang = pos.astype(F32)[:, None] * inv[None, :]
    cos, sin = jnp.cos(ang), jnp.sin(ang)
    cos_t = jnp.tile(jnp.concatenate([cos, cos], axis=-1), (1, RET_HEADS))
    sin_t = jnp.tile(jnp.concatenate([-sin, sin], axis=-1), (1, RET_HEADS))
    return cos_t, sin_t


def _rotate(z, cos, sin):
    half = RET_DK // 2
    first = (_iota(z.shape, 1) % RET_DK) < half
    swapped = jnp.where(first, pltpu.roll(z, RET_QK - half, 1), pltpu.roll(z, half, 1))
    return z * cos + swapped * sin


def _ret_gammas():
    return [1.0 - 2.0 ** (-5.0 - h) for h in range(RET_HEADS)]


def _ret_consts(chunk, n_sub):
    log_g = np.log(np.array(_ret_gammas(), np.float64))
    idx = np.arange(chunk, dtype=np.float64)
    rel = idx[:, None] - idx[None, :]
    dmask = np.where(rel[None] >= 0, np.exp(log_g[:, None, None] * np.maximum(rel, 0.0)[None]), 0.0)
    qdec = np.repeat(np.exp(log_g[None, :] * (idx[:, None] + 1.0)), RET_DK, axis=1)
    kdec = np.repeat(np.exp(log_g[None, :] * (chunk - 1.0 - idx[:, None])), RET_DK, axis=1)
    qdec, kdec = np.tile(qdec, (n_sub, 1)), np.tile(kdec, (n_sub, 1))
    head_r = np.arange(RET_QK) // RET_DK
    head_c = np.arange(RET_V) // RET_DV
    bd = (head_r[:, None] == head_c[None, :]).astype(np.float64)
    cd = bd * np.exp(log_g * chunk)[head_r][:, None]
    f = lambda a: jnp.asarray(a, F32)
    return f(dmask), f(qdec), f(kdec), f(cd), f(bd)


def _ret_kernel(x_ref, cos_ref, sin_ref, dmask_ref, qdec_ref, kdec_ref, cd_ref, bd_ref,
                y_ref, st_ref, s_scr):
    c = pl.program_id(1)
    ch = RET_CHUNK
    n_sub = x_ref.shape[0] // ch

    @pl.when(c == 0)
    def _():
        s_scr[...] = jnp.zeros(s_scr.shape, F32)

    x = x_ref[...]
    q = _rotate(x[:, 0:RET_QK], cos_ref[...], sin_ref[...])
    k = _rotate(x[:, RET_QK:2 * RET_QK], cos_ref[...], sin_ref[...]) * (RET_DK ** -0.5)
    v = x[:, 2 * RET_QK:2 * RET_QK + RET_V].astype(BF16)
    gate = x[:, 2 * RET_QK + RET_V:]
    q_dec = (q * qdec_ref[...]).astype(BF16)
    k_dec = k * kdec_ref[...]
    kb = k.astype(BF16)
    head = _iota((ch, RET_QK), 1) // RET_DK
    intra, upd = {}, {}
    for n in range(n_sub):
        sl = slice(n * ch, (n + 1) * ch)
        q_heads = jnp.concatenate([jnp.where(head == h, q[sl], 0.0) for h in range(RET_HEADS)], axis=0)
        inner_all = _dot_nt(q_heads.astype(BF16), kb[sl])
        for h in range(RET_HEADS):
            inner = inner_all[h * ch:(h + 1) * ch] * dmask_ref[h]
            intra[n, h] = _bdot(inner, v[sl, h * RET_DV:(h + 1) * RET_DV])
        upd[n] = jnp.dot(k_dec[sl].T.astype(BF16), v[sl], preferred_element_type=F32) * bd_ref[...]
    s = s_scr[...]
    cross = {}
    for n in range(n_sub):
        cross[n] = jnp.dot(q_dec[n * ch:(n + 1) * ch], s.astype(BF16), preferred_element_type=F32)
        s = s * cd_ref[...] + upd[n]
    s_scr[...] = s
    for n in range(n_sub):
        sl = slice(n * ch, (n + 1) * ch)
        outs = [_ln(intra[n, h] + cross[n][:, h * RET_DV:(h + 1) * RET_DV]) for h in range(RET_HEADS)]
        y_ref[sl, :] = jnp.concatenate(outs, axis=1) * _silu(gate[sl])

    @pl.when(c == pl.num_programs(1) - 1)
    def _():
        for h in range(RET_HEADS):
            st_ref[0, h] = s[h * RET_DK:(h + 1) * RET_DK, h * RET_DV:(h + 1) * RET_DV]


def _retention_prompt(ret, nb, t):
    n_sub = RET_STEP_CHUNKS if t % (RET_STEP_CHUNKS * RET_CHUNK) == 0 else 1
    chunk = n_sub * RET_CHUNK
    nc = t // chunk
    cos_t, sin_t = _rope_tables(jnp.arange(t))
    consts = _ret_consts(RET_CHUNK, n_sub)
    full = lambda a: pl.BlockSpec(a.shape, lambda b, c: (0,) * a.ndim)
    return pl.pallas_call(
        _ret_kernel,
        grid=(nb, nc),
        in_specs=[
            pl.BlockSpec((chunk, RET_W), lambda b, c: (b * nc + c, 0)),
            pl.BlockSpec((chunk, RET_QK), lambda b, c: (c, 0)),
            pl.BlockSpec((chunk, RET_QK), lambda b, c: (c, 0)),
        ] + [full(a) for a in consts],
        out_specs=[
            pl.BlockSpec((chunk, RET_V), lambda b, c: (b * nc + c, 0)),
            pl.BlockSpec((1, RET_HEADS, RET_DK, RET_DV), lambda b, c: (b, 0, 0, 0)),
        ],
        out_shape=[
            jax.ShapeDtypeStruct((nb * t, RET_V), F32),
            jax.ShapeDtypeStruct((nb, RET_HEADS, RET_DK, RET_DV), F32),
        ],
        scratch_shapes=[pltpu.VMEM((RET_QK, RET_V), F32)],
        compiler_params=_cparams("parallel", "arbitrary"),
        name="retention",
    )(ret, cos_t, sin_t, *consts)


def _rms(x):
    return x * lax.rsqrt(jnp.mean(x * x, axis=-1, keepdims=True) + 1e-6)


def _gate_lanes(gab, alog_row, dtb_row):
    g = -jnp.exp(alog_row) * _softplus(gab + dtb_row)
    return jnp.where(_iota(gab.shape, 1) < GDN_HEADS, g, _sigmoid(gab))


def _gdn_kernel(x_ref, gab_ref, gz_ref, alog_ref, dtb_ref, nw_ref, y_ref, st_ref, s_scr):
    c = pl.program_id(1)
    ch = GDN_CHUNK

    @pl.when(c == 0)
    def _():
        s_scr[...] = jnp.zeros(s_scr.shape, F32)

    rows = x_ref.shape[0]
    cq = x_ref[...]

    gl = _gate_lanes(gab_ref[...], alog_ref[...], dtb_ref[...])
    col = _iota((ch, 2 * ch), 1) % ch
    tril = _iota((ch, 2 * ch), 0) >= col
    strict = _iota((ch, 2 * ch), 0) > col
    is_g = _iota(gl.shape, 1) < GDN_HEADS
    r_i, c_i = _iota((rows, rows), 0), _iota((rows, rows), 1)
    chunk_tril = ((r_i >= c_i) & (r_i // ch == c_i // ch)).astype(F32)
    csum = jnp.dot(chunk_tril, jnp.where(is_g, gl, 0.0), preferred_element_type=F32,
                   precision=lax.Precision.HIGHEST)
    gsel = jnp.where(is_g, csum, gl)
    n_sub = rows // ch
    heads = range(GDN_HEADS)
    probs = [(h, n) for h in heads for n in range(n_sub)]

    q_all, k_all, g_all, qe_all, rhs_all, kbeta_all = {}, {}, {}, {}, {}, {}
    for h in heads:
        q_all[h] = cq[:, h * GDN_DK:(h + 1) * GDN_DK]
        k_all[h] = cq[:, GDN_K + h * GDN_DK:GDN_K + (h + 1) * GDN_DK]
        v_h = cq[:, 2 * GDN_K + h * GDN_DV:2 * GDN_K + (h + 1) * GDN_DV]
        g_all[h] = jnp.broadcast_to(gsel[:, h:h + 1], (rows, LANE))
        b_h = jnp.broadcast_to(gsel[:, GDN_HEADS + h:GDN_HEADS + h + 1], (rows, LANE))
        e_h = jnp.exp(g_all[h])
        kbeta_all[h] = k_all[h] * b_h
        rhs_all[h] = jnp.concatenate([v_h * b_h, kbeta_all[h] * e_h], axis=1)
        qe_all[h] = q_all[h] * e_h

    decay, pp, uw, attn, kd_t, e_last = {}, {}, {}, {}, {}, {}
    for h, n in probs:
        sl = slice(n * ch, (n + 1) * ch)
        g_b = g_all[h][sl]
        g_cols = jnp.concatenate([g_b, g_b], axis=0).T[:ch, :]
        diff = g_b - g_cols
        decay[h, n] = jnp.where(tril, jnp.exp(jnp.where(tril, diff, 0.0)), 0.0)
        g_last = g_b[ch - 1:ch, :]
        e_last[h, n] = jnp.exp(g_last)
        kd_t[h, n] = (k_all[h][sl] * jnp.exp(g_last - g_b)).T.astype(BF16)
    for h, n in probs:
        sl = slice(n * ch, (n + 1) * ch)
        kb_hi, kb_lo = _split(kbeta_all[h][sl])
        k_hi, k_lo = _split(jnp.concatenate([k_all[h][sl], k_all[h][sl]], axis=0))
        kk = _dot_nt(jnp.concatenate([kb_hi, kb_hi, kb_lo, kb_lo], axis=1),
                     jnp.concatenate([k_hi, k_lo, k_hi, k_lo], axis=1))
        pp[h, n] = -jnp.where(strict, kk * decay[h, n], 0.0)
        attn[h, n] = _dot_nt(q_all[h][sl].astype(BF16), k_all[h][sl].astype(BF16)) * decay[h, n][:, :ch]
    for stage in range(int(math.log2(ch))):
        for h, n in probs:
            p_hi, p_lo = _split(pp[h, n])
            lhs = jnp.concatenate([p_hi, p_lo], axis=1)
            y = rhs_all[h][n * ch:(n + 1) * ch] if stage == 0 else uw[h, n]
            y_hi, y_lo = _split(y)
            uw[h, n] = y + jnp.dot(lhs, jnp.concatenate([y_hi, y_lo, y_hi, y_lo], axis=0),
                                   preferred_element_type=F32)
            if stage + 1 < int(math.log2(ch)):
                pp[h, n] = jnp.dot(lhs, jnp.concatenate([p_hi, p_lo, p_hi, p_lo], axis=0),
                                   preferred_element_type=F32)
    s = {h: s_scr[h] for h in heads}
    o_parts = {h: [] for h in heads}
    for n in range(n_sub):
        for h in heads:
            sl = slice(n * ch, (n + 1) * ch)
            u, w = uw[h, n][:, :GDN_DV], uw[h, n][:, GDN_DV:]
            ws_qs = _bdot(jnp.concatenate([w, qe_all[h][sl]], axis=0), s[h])
            v_new = u - ws_qs[:ch]
            o_parts[h].append(ws_qs[ch:] + _bdot(attn[h, n], v_new))
            s[h] = s[h] * e_last[h, n] + jnp.dot(kd_t[h, n], v_new.astype(BF16), preferred_element_type=F32)
    norm_w = nw_ref[...]
    gz = gz_ref[...]
    outs = []
    for h in heads:
        s_scr[h] = s[h]
        o = jnp.concatenate(o_parts[h], axis=0)
        outs.append(_rms(o) * norm_w * _silu(gz[:, h * GDN_DV:(h + 1) * GDN_DV]))
    y_ref[...] = jnp.concatenate(outs, axis=1)

    @pl.when(c == pl.num_programs(1) - 1)
    def _():
        st_ref[0] = s_scr[...]


def _gdn_prompt(gqkv, gab, gz, alog_row, dtb_row, norm_w, layer, nb, t):
    ch = GDN_STEP_CHUNKS * GDN_CHUNK if t % (GDN_STEP_CHUNKS * GDN_CHUNK) == 0 else GDN_CHUNK
    nc = t // ch
    row = lambda w: pl.BlockSpec((ch, w), lambda b, c: (b * nc + c, 0))
    return pl.pallas_call(
        _gdn_kernel,
        grid=(nb, nc),
        in_specs=[row(GDN_QKV), row(LANE), row(GDN_HEADS * GDN_DV)]
        + [_layer_spec(a, layer) for a in (alog_row, dtb_row, norm_w)],
        out_specs=[
            row(GDN_HEADS * GDN_DV),
            pl.BlockSpec((1, GDN_HEADS, GDN_DK, GDN_DV), lambda b, c: (b, 0, 0, 0)),
        ],
        out_shape=[
            jax.ShapeDtypeStruct((nb * t, GDN_HEADS * GDN_DV), F32),
            jax.ShapeDtypeStruct((nb, GDN_HEADS, GDN_DK, GDN_DV), F32),
        ],
        scratch_shapes=[pltpu.VMEM((GDN_HEADS, GDN_DK, GDN_DV), F32)],
        compiler_params=_cparams("parallel", "arbitrary"),
        name="gdn",
    )(gqkv, gab, gz, alog_row, dtb_row, norm_w)


SSM_BLK = 4
SSM_BLK_STATE = SSM_STATE // SSM_BLK


def _s5_params(lam_re, lam_im, log_step, b_re, b_im, c_re, c_im):
    step = jnp.exp(log_step.astype(F32))[:, None]
    mag = jnp.exp(lam_re * step)
    ab_re = mag * jnp.cos(lam_im * step)
    ab_im = mag * jnp.sin(lam_im * step)
    den = lam_re * lam_re + lam_im * lam_im
    nr = ab_re - 1.0
    f_re = (nr * lam_re + ab_im * lam_im) / den
    f_im = (ab_im * lam_re - nr * lam_im) / den
    bb_re = f_re[..., None] * b_re - f_im[..., None] * b_im
    bb_im = f_re[..., None] * b_im + f_im[..., None] * b_re
    gpb = SSM_GROUPS // SSM_BLK
    eye = jnp.eye(gpb, dtype=F32)

    def in_mat(bb):
        bb = bb.reshape(SSM_BLK, gpb, SSM_P, SSM_GROUP)
        m = jnp.einsum("jgpc,gk->jgckp", bb, eye)
        return m.reshape(SSM_BLK, gpb * SSM_GROUP, gpb * SSM_P).astype(BF16)

    def out_mat(cc):
        cc = cc.reshape(SSM_BLK, gpb, SSM_GROUP, SSM_P)
        m = jnp.einsum("jgcp,gk->jgpkc", cc, eye)
        return m.reshape(SSM_BLK, gpb * SSM_P, gpb * SSM_GROUP).astype(BF16)

    return (ab_re.reshape(1, SSM_STATE), ab_im.reshape(1, SSM_STATE),
            in_mat(bb_re), in_mat(bb_im), out_mat(c_re), out_mat(c_im))


def _s5_readout(u, h_re, h_im, cre_ref, cim_ref, d_row, gw_ref, gb_row):
    ys = []
    for j in range(SSM_BLK):
        sl = slice(j * SSM_BLK_STATE, (j + 1) * SSM_BLK_STATE)
        ys.append(_bdot(h_re[:, sl], cre_ref[j]) - _bdot(h_im[:, sl], cim_ref[j]))
    y = jax.nn.gelu(jnp.concatenate(ys, axis=1) + d_row * u)
    return y * _sigmoid(_bdot(y, gw_ref[...]) + gb_row)


def _scan_mixers_kernel(u_ref, x_ref, g_ref,
                        are_ref, aim_ref, bre_ref, bim_ref, cre_ref, cim_ref, d_ref, gw_ref, gb_ref,
                        cw_ref, cb_ref, wa_ref, wx_ref, ba_ref, bx_ref, lam_ref,
                        y_ref, hre_ref, him_ref, yd_ref, h_out, cv_out,
                        bu_re, bu_im, h_re, h_im, xbuf, a_buf, b_buf, h_scr, *, nb):
    i = pl.program_id(0)
    tc = u_ref.shape[1]
    rows = tc * nb
    tail = (CONV_W - 1) * nb

    @pl.when(i == 0)
    def _():
        h_re[...] = jnp.zeros(h_re.shape, F32)
        h_im[...] = jnp.zeros(h_im.shape, F32)
        xbuf[0:tail, :] = jnp.zeros((tail, LRU_WIDTH), F32)
        h_scr[...] = jnp.zeros(h_scr.shape, F32)

    u = _to_time_major(u_ref[...])
    for j in range(SSM_BLK):
        uj = u[:, j * LANE:(j + 1) * LANE]
        sl = slice(j * SSM_BLK_STATE, (j + 1) * SSM_BLK_STATE)
        bu_re[:, sl] = _bdot(uj, bre_ref[j])
        bu_im[:, sl] = _bdot(uj, bim_ref[j])
    x = _to_time_major(x_ref[...])
    xbuf[tail:tail + rows, :] = x
    conv = cw_ref[CONV_W - 1:CONV_W, :] * x
    for j in range(CONV_W - 1):
        conv = conv + cw_ref[j:j + 1, :] * xbuf[j * nb:j * nb + rows, :]
    cv_out[...] = xbuf[rows:rows + tail, :].reshape(CONV_W - 1, nb, LRU_WIDTH)
    xbuf[0:tail, :] = x[rows - tail:, :]
    a, b = _lru_gates(conv + cb_ref[...], wa_ref, wx_ref, ba_ref[...], bx_ref[...], lam_ref[...])
    a_buf[...] = a
    b_buf[...] = b

    for j in range(SSM_BLK):
        sl = slice(j * SSM_BLK_STATE, (j + 1) * SSM_BLK_STATE)
        a_re = jnp.broadcast_to(are_ref[:, sl], (nb, SSM_BLK_STATE))
        a_im = jnp.broadcast_to(aim_ref[:, sl], (nb, SSM_BLK_STATE))

        def body(t, carry):
            hr, hi = carry
            r = pl.ds(pl.multiple_of(t * nb, nb), nb)
            nr = a_re * hr - a_im * hi + bu_re[r, sl]
            ni = a_re * hi + a_im * hr + bu_im[r, sl]
            bu_re[r, sl] = nr
            bu_im[r, sl] = ni
            return nr, ni

        hr, hi = lax.fori_loop(0, tc, body, (h_re[:, sl], h_im[:, sl]), unroll=4)
        h_re[:, sl] = hr
        h_im[:, sl] = hi

    def lru_body(t, h):
        r = pl.ds(pl.multiple_of(t * nb, nb), nb)
        h = a_buf[r, :] * h + b_buf[r, :]
        b_buf[r, :] = h
        return h

    h = lax.fori_loop(0, tc, lru_body, h_scr[...], unroll=8)
    h_scr[...] = h
    h_out[...] = h

    y = _s5_readout(u, bu_re[...], bu_im[...], cre_ref, cim_ref, d_ref[...], gw_ref, gb_ref[...])
    y_ref[...] = _from_time_major(y, nb)
    hre_ref[...] = h_re[...]
    him_ref[...] = h_im[...]
    yd = b_buf[...] * jax.nn.gelu(_to_time_major(g_ref[...]))
    yd_ref[...] = _from_time_major(yd, nb)


def _scan_mixers(u_t, x_t, g_t, s5_w, lru_w, layer, tc):
    nb, t, _ = u_t.shape
    rows = tc * nb
    blk = pl.BlockSpec((nb, tc, SSM_WIDTH), lambda i: (0, i, 0))
    st = pl.BlockSpec((nb, SSM_STATE), lambda i: (0, 0))
    sds = lambda *s: jax.ShapeDtypeStruct(s, F32)
    return pl.pallas_call(
        functools.partial(_scan_mixers_kernel, nb=nb),
        grid=(t // tc,),
        in_specs=[blk, blk, blk] + [_layer_spec(a, layer) for a in (*s5_w, *lru_w)],
        out_specs=[blk, st, st, blk,
                   pl.BlockSpec((nb, LRU_WIDTH), lambda i: (0, 0)),
                   pl.BlockSpec((CONV_W - 1, nb, LRU_WIDTH), lambda i: (0, 0, 0))],
        out_shape=[sds(nb, t, SSM_WIDTH), sds(nb, SSM_STATE), sds(nb, SSM_STATE),
                   sds(nb, t, LRU_WIDTH), sds(nb, LRU_WIDTH), sds(CONV_W - 1, nb, LRU_WIDTH)],
        scratch_shapes=[pltpu.VMEM((rows, SSM_STATE), F32), pltpu.VMEM((rows, SSM_STATE), F32),
                        pltpu.VMEM((nb, SSM_STATE), F32), pltpu.VMEM((nb, SSM_STATE), F32),
                        pltpu.VMEM(((CONV_W - 1) * nb + rows, LRU_WIDTH), F32),
                        pltpu.VMEM((rows, LRU_WIDTH), F32), pltpu.VMEM((rows, LRU_WIDTH), F32),
                        pltpu.VMEM((nb, LRU_WIDTH), F32)],
        compiler_params=_cparams("arbitrary"),
        name="scan_mixers",
    )(u_t, x_t, g_t, *s5_w, *lru_w)


def _lru_gates(cx, wa_ref, wx_ref, ba_row, bx_row, lam_row):
    r = _sigmoid(_bdot(cx, wa_ref[...]) + ba_row)
    i = _sigmoid(_bdot(cx, wx_ref[...]) + bx_row)
    log_a = -LRU_C * r * _softplus(-lam_row)
    a = jnp.exp(log_a)
    th = jnp.tanh(log_a)
    b = jnp.sqrt(-2.0 * th / (1.0 - th)) * (i * cx)
    return a, b


def _merge_kernel(x_ref, sh_ref, sc_ref, gt_ref, ya_ref, yb_ref, yc_ref, yd_ref, wmg_ref, wb_ref, wo_ref,
                  g_ref, b_ref, o_ref, *, alpha, parts):
    ks = range(parts)
    xs, shs, scs, gts = (_row_parts(r, parts) for r in (x_ref, sh_ref, sc_ref, gt_ref))
    hs = [(_ln(xs[k]) * (1.0 + scs[k]) + shs[k]).astype(BF16) for k in ks]
    accs = [jnp.zeros(xs[k].shape, F32) for k in ks]
    for n, y_ref in enumerate((ya_ref, yb_ref, yc_ref, yd_ref)):
        ys = _row_parts(y_ref, parts)
        logits = [jnp.dot(hs[k], wmg_ref[:, n * D_MODEL:(n + 1) * D_MODEL], preferred_element_type=F32)
                  for k in ks]
        accs = [accs[k] + _sigmoid(logits[k]) * _bdot(ys[k], wb_ref[n]) for k in ks]
    outs = [_bdot(accs[k], wo_ref[...]) for k in ks]
    hm = x_ref.shape[0] // parts
    for k in ks:
        z = alpha * xs[k] + gts[k] * outs[k]
        o_ref[k * hm:(k + 1) * hm, :] = _ln(z) * g_ref[...] + b_ref[...]


def _merge(x, mod, ya, yb, yc, yd, w_mg, w_branch, w_out, ln_g, ln_b, alpha, tm):
    n = x.shape[0]
    l = mod.layer
    row = lambda w: pl.BlockSpec((tm, w), lambda i: (i, 0))
    return pl.pallas_call(
        functools.partial(_merge_kernel, alpha=alpha, parts=2 if tm >= 512 else 1),
        grid=(n // tm,),
        in_specs=[
            row(D_MODEL), mod.spec(1, 0, tm), mod.spec(1, 1, tm), mod.spec(1, 2, tm),
            row(BRANCH_W), row(BRANCH_W), row(BRANCH_W), row(BRANCH_W),
            _resident((None, D_MODEL, MG_WIDTH), lambda i: (l, 0, 0)),
            _resident((None, N_BRANCH, BRANCH_W, D_MODEL), lambda i: (l, 0, 0, 0)),
            _resident((None, D_MODEL, D_MODEL), lambda i: (l, 0, 0)),
            pl.BlockSpec((None, None, 1, D_MODEL), lambda i: (l, 1, 0, 0)),
            pl.BlockSpec((None, None, 1, D_MODEL), lambda i: (l, 1, 0, 0)),
        ],
        out_specs=row(D_MODEL),
        out_shape=jax.ShapeDtypeStruct((n, D_MODEL), F32),
        compiler_params=_cparams("parallel"),
        name="merge",
    )(x, mod.arr, mod.arr, mod.arr, ya, yb, yc, yd, w_mg, w_branch, w_out, ln_g, ln_b)


def _smix1_kernel(ret_ref, gqkv_ref, gab_ref, su_ref, lx_ref, cos_ref, sin_ref,
                  gcv_ref, sre_ref, sim_ref, lru_ref, lcv_ref,
                  gcw_ref, alog_ref, dtb_ref,
                  are_ref, aim_ref, bre_ref, bim_ref, cre_ref, cim_ref, d_ref, gw_ref, gb_ref,
                  lcw_ref, lcb_ref, wa_ref, wx_ref, ba_ref, bx_ref, lam_ref,
                  qkt_ref, gv_ref, gx_ref, yc_ref, lh_ref, gcv_out, sre_out, sim_out, lcv_out):
    ret = ret_ref[...]
    rq = _rotate(ret[:, 0:RET_QK], cos_ref[...], sin_ref[...])
    rk = _rotate(ret[:, RET_QK:2 * RET_QK], cos_ref[...], sin_ref[...]) * (RET_DK ** -0.5)

    x = gqkv_ref[...]
    conv = gcw_ref[3:4, :] * x
    for j in range(CONV_W - 1):
        conv = conv + gcw_ref[j:j + 1, :] * gcv_ref[j]
    gcv_out[0] = gcv_ref[1]
    gcv_out[1] = gcv_ref[2]
    gcv_out[2] = x
    cq = _gdn_activate(conv)
    gq, gk = cq[:, :GDN_K], cq[:, GDN_K:2 * GDN_K]
    gv_ref[...] = cq[:, 2 * GDN_K:]
    gl = _gate_lanes(gab_ref[...], alog_ref[...], dtb_ref[...])
    gx = jnp.concatenate([jnp.broadcast_to(gl[:, r:r + 1], (gl.shape[0], LANE))
                          for r in range(2 * GDN_HEADS)], axis=1)
    gx_ref[...] = jnp.where(_iota(gx.shape, 1) < GDN_HEADS * LANE, jnp.exp(gx), gx)
    qkt_ref[...] = jnp.concatenate([rq, rk, gq, gk], axis=1).T

    u = su_ref[...]
    h_re, h_im = [], []
    for j in range(SSM_BLK):
        sl = slice(j * SSM_BLK_STATE, (j + 1) * SSM_BLK_STATE)
        uj = u[:, j * LANE:(j + 1) * LANE]
        a_re, a_im = are_ref[:, sl], aim_ref[:, sl]
        p_re, p_im = sre_ref[:, sl], sim_ref[:, sl]
        h_re.append(a_re * p_re - a_im * p_im + _bdot(uj, bre_ref[j]))
        h_im.append(a_re * p_im + a_im * p_re + _bdot(uj, bim_ref[j]))
    h_re = jnp.concatenate(h_re, axis=1)
    h_im = jnp.concatenate(h_im, axis=1)
    sre_out[...] = h_re
    sim_out[...] = h_im
    yc_ref[...] = _s5_readout(u, h_re, h_im, cre_ref, cim_ref, d_ref[...], gw_ref, gb_ref[...])

    lx = lx_ref[...]
    conv = lcw_ref[3:4, :] * lx
    for j in range(CONV_W - 1):
        conv = conv + lcw_ref[j:j + 1, :] * lcv_ref[j]
    lcv_out[0] = lcv_ref[1]
    lcv_out[1] = lcv_ref[2]
    lcv_out[2] = lx
    a, b = _lru_gates(conv + lcb_ref[...], wa_ref, wx_ref, ba_ref[...], bx_ref[...], lam_ref[...])
    lh_ref[...] = a * lru_ref[...] + b


def _smix1(pieces, cos_row, sin_row, states, gdn_w, s5_w, lru_w, layer):
    rows = pieces["ret"].shape[0]
    acts = [pieces["ret"], pieces["gqkv"], pieces["gab"], pieces["su"], pieces["lx"], cos_row, sin_row]
    stacked = [*states, *gdn_w, *s5_w, *lru_w]
    ins = acts + stacked
    full = lambda a: pl.BlockSpec(a.shape, lambda i: (0,) * a.ndim)
    sds = lambda *s: jax.ShapeDtypeStruct(s, F32)
    out_shape = [sds(QKT_ROWS, rows), sds(rows, GDN_HEADS * GDN_DV), sds(rows, 8 * LANE),
                 sds(rows, SSM_WIDTH), sds(rows, LRU_WIDTH),
                 sds(CONV_W - 1, rows, GDN_QKV), sds(rows, SSM_STATE), sds(rows, SSM_STATE),
                 sds(CONV_W - 1, rows, LRU_WIDTH)]
    return pl.pallas_call(
        _smix1_kernel,
        grid=(1,),
        in_specs=[full(a) for a in acts] + [_layer_spec(a, layer) for a in stacked],
        out_specs=[pl.BlockSpec(s.shape, lambda i, nd=len(s.shape): (0,) * nd) for s in out_shape],
        out_shape=out_shape,
        compiler_params=_cparams("arbitrary"),
        name="sample_mix",
    )(*ins)


def _smix2_kernel(qkt_ref, rv_ref, gv_ref, gx_ref, rg_ref, gz_ref, lh_ref, lg_ref, nw_ref, sret_ref, sgdn_ref,
                  *refs, first):
    ya_ref, yb_ref, yd_ref, nret_ref, ngdn_ref, o_ret, o_gdn = refs[-7:]
    if first:
        for l in range(1, nret_ref.shape[0]):
            nret_ref[l] = jnp.zeros(nret_ref.shape[1:], F32)
            ngdn_ref[l] = jnp.zeros(ngdn_ref.shape[1:], F32)
        nret_ref, ngdn_ref = nret_ref.at[0], ngdn_ref.at[0]
    i = pl.program_id(0)
    bt = rv_ref.shape[0]
    rows = qkt_ref.shape[1]
    gammas = _ret_gammas()
    q_hi, q_lo = _split(qkt_ref[...])
    qk_split = jnp.concatenate([q_hi, q_lo], axis=1)

    for j in range(bt):
        b = i * bt + j
        onehot = jnp.where(_iota((2 * rows, LANE), 0) % rows == b, 1.0, 0.0).astype(BF16)
        cols = jnp.dot(qk_split, onehot, preferred_element_type=F32)
        r = slice(j, j + 1)
        for h in range(RET_HEADS):
            q_c = cols[h * RET_DK:(h + 1) * RET_DK]
            k_c = cols[RET_QK + h * RET_DK:RET_QK + (h + 1) * RET_DK]
            v_r = rv_ref[r, h * RET_DV:(h + 1) * RET_DV]
            s_new = gammas[h] * sret_ref[j, h] + k_c * v_r
            nret_ref[j, h] = s_new
            o_ret[r, h * RET_DV:(h + 1) * RET_DV] = jnp.sum(q_c * s_new, axis=0, keepdims=True)
        base = 2 * RET_QK
        for h in range(GDN_HEADS):
            q_c = cols[base + h * GDN_DK:base + (h + 1) * GDN_DK]
            k_c = cols[base + GDN_K + h * GDN_DK:base + GDN_K + (h + 1) * GDN_DK]
            v_r = gv_ref[r, h * GDN_DV:(h + 1) * GDN_DV]
            e_g = gx_ref[r, h * LANE:(h + 1) * LANE]
            beta = gx_ref[r, (GDN_HEADS + h) * LANE:(GDN_HEADS + h + 1) * LANE]
            s = sgdn_ref[j, h]
            v_new = beta * (v_r - e_g * jnp.sum(k_c * s, axis=0, keepdims=True))
            s_new = e_g * s + k_c * v_new
            ngdn_ref[j, h] = s_new
            o_gdn[r, h * GDN_DV:(h + 1) * GDN_DV] = jnp.sum(q_c * s_new, axis=0, keepdims=True)
    rg, gz = rg_ref[...], gz_ref[...]
    o_r, o_g = o_ret[...], o_gdn[...]
    ya_ref[...] = jnp.concatenate(
        [_ln(o_r[:, h * RET_DV:(h + 1) * RET_DV]) for h in range(RET_HEADS)], axis=1) * _silu(rg)
    yb_ref[...] = jnp.concatenate(
        [_rms(o_g[:, h * GDN_DV:(h + 1) * GDN_DV]) * nw_ref[...] for h in range(GDN_HEADS)], axis=1) * _silu(gz)
    yd_ref[...] = lh_ref[...] * jax.nn.gelu(lg_ref[...])


def _smix2(qkt, rv, gv, gx, rg, gz, lh, lg, norm_w, s_ret, s_gdn, layer, bt, prev):
    rows = rv.shape[0]
    row = lambda w: pl.BlockSpec((bt, w), lambda i: (i, 0))
    sds = lambda *s: jax.ShapeDtypeStruct(s, F32)
    ret_blk = pl.BlockSpec((None, bt, RET_HEADS, RET_DK, RET_DV), lambda i: (layer, i, 0, 0, 0))
    gdn_blk = pl.BlockSpec((None, bt, GDN_HEADS, GDN_DK, GDN_DV), lambda i: (layer, i, 0, 0, 0))
    ins = [qkt, rv, gv, gx, rg, gz, lh, lg, norm_w, s_ret, s_gdn]
    in_specs = [pl.BlockSpec(qkt.shape, lambda i: (0, 0)),
                pl.BlockSpec((bt, RET_V), lambda i: (i, 1)),
                row(GDN_HEADS * GDN_DV), row(8 * LANE),
                pl.BlockSpec((bt, RET_V), lambda i: (i, 2)),
                row(GDN_HEADS * GDN_DV), row(LRU_WIDTH), row(LRU_WIDTH),
                _layer_spec(norm_w, layer), ret_blk, gdn_blk]
    aliases = {}
    if prev is None:
        depth = s_ret.shape[0]
        out_ret = pl.BlockSpec((depth, bt, RET_HEADS, RET_DK, RET_DV), lambda i: (0, i, 0, 0, 0))
        out_gdn = pl.BlockSpec((depth, bt, GDN_HEADS, GDN_DK, GDN_DV), lambda i: (0, i, 0, 0, 0))
    else:
        out_ret, out_gdn = ret_blk, gdn_blk
        aliases = {len(ins): 3, len(ins) + 1: 4}
        ins += list(prev)
        in_specs += [pl.BlockSpec(memory_space=pl.ANY)] * 2
    return pl.pallas_call(
        functools.partial(_smix2_kernel, first=prev is None),
        grid=(rows // bt,),
        in_specs=in_specs,
        out_specs=[row(RET_V), row(GDN_HEADS * GDN_DV), row(LRU_WIDTH), out_ret, out_gdn],
        out_shape=[sds(rows, RET_V), sds(rows, GDN_HEADS * GDN_DV), sds(rows, LRU_WIDTH),
                   sds(*s_ret.shape), sds(*s_gdn.shape)],
        input_output_aliases=aliases,
        scratch_shapes=[pltpu.VMEM((bt, RET_V), F32), pltpu.VMEM((bt, GDN_HEADS * GDN_DV), F32)],
        compiler_params=_cparams("parallel"),
        name="sample_state",
    )(*ins)


def _block_diag(w):
    nb, bs, _ = w.shape
    return jnp.einsum("nij,nm->nimj", w, jnp.eye(nb, dtype=w.dtype)).reshape(nb * bs, nb * bs)


def kernel(x_prompt, x_sample, c_prompt, c_sample, state_ret, state_gdn, state_gdn_conv, state_ssm_re, state_ssm_im, state_lru, state_lru_conv, w_ada, b_ada, ln_g, ln_b, w_ffn_up, w_ffn_down, w_in, gdn_conv_w, gdn_a_log, gdn_dt_bias, gdn_norm_w, ssm_lam_re, ssm_lam_im, ssm_log_step, ssm_b_re, ssm_b_im, ssm_c_re, ssm_c_im, ssm_d, ssm_glu_w, ssm_glu_b, lru_conv_w, lru_conv_b, lru_wa, lru_ba, lru_wx, lru_bx, lru_lam, w_branch, w_out):
    nb, t, _ = x_prompt.shape
    ns = x_sample.shape[0]
    depth = w_ada.shape[0]
    assert t % RET_CHUNK == 0 and x_sample.shape[1] == 1
    alpha = (2 * depth) ** 0.25
    tm = 256 if t % 256 == 0 else RET_CHUNK
    tm_ffn = 512 if t % 512 == 0 else tm
    tc = 128 if t % 128 == 0 else 64

    wup = w_ffn_up.astype(BF16)
    wdn = w_ffn_down.astype(BF16)
    wbr = w_branch.astype(BF16)
    wout = w_out.astype(BF16)
    w_packed, w_mg = _pack_w_in(w_in)
    ln_g4 = ln_g.reshape(depth, N_SUB, 1, D_MODEL)
    ln_b4 = ln_b.reshape(depth, N_SUB, 1, D_MODEL)

    mod_p, mod_s = _ada(c_prompt, c_sample, w_ada, b_ada)
    mod_p = mod_p.reshape(depth, nb, 3 * N_SUB, 1, D_MODEL).transpose(0, 2, 1, 3, 4)

    row3 = lambda a: a.reshape(depth, 1, -1)
    lane_rows = lambda v: jnp.pad(v, ((0, 0), (0, LANE - v.shape[1]))).reshape(depth, 1, LANE)
    alog_rows, dtb_rows = lane_rows(gdn_a_log), lane_rows(gdn_dt_bias)
    norm_w = row3(gdn_norm_w)
    s5_w = (*jax.vmap(_s5_params)(ssm_lam_re, ssm_lam_im, ssm_log_step, ssm_b_re, ssm_b_im, ssm_c_re, ssm_c_im),
            row3(ssm_d), ssm_glu_w.astype(BF16), row3(ssm_glu_b))
    lru_w = (lru_conv_w, row3(lru_conv_b),
             jax.vmap(_block_diag)(lru_wa).astype(BF16), jax.vmap(_block_diag)(lru_wx).astype(BF16),
             row3(lru_ba), row3(lru_bx), row3(lru_lam))
    s_states = (state_gdn_conv.transpose(0, 2, 1, 3),
                state_ssm_re.reshape(depth, ns, SSM_STATE), state_ssm_im.reshape(depth, ns, SSM_STATE),
                state_lru, state_lru_conv.transpose(0, 2, 1, 3))

    cos_s, sin_s = _rope_tables(jnp.full((1,), PAST_LEN))
    xp = x_prompt.reshape(nb * t, D_MODEL)
    xs = x_sample.reshape(ns, D_MODEL)
    new_p, new_s = [], []
    big_s = None
    for l in range(depth):
        mp = _Mod(mod_p, l, False, t)
        ms = _Mod(mod_s, l, True)

        xp = _ffn(xp, mp, 0, 0, wup, wdn, ln_g4, ln_b4, alpha, tm_ffn)
        pc = _inproj(xp, mp, w_packed, tm_ffn, conv=(gdn_conv_w, nb, t))
        gcv_p = pc["gdn_conv"]
        ya, ret_p = _retention_prompt(pc["ret"], nb, t)
        yb, gdn_p = _gdn_prompt(pc["gqkv"], pc["gab"], pc["gz"], alog_rows, dtb_rows, norm_w, l, nb, t)
        to_t = lambda a: a.reshape(nb, t, -1)
        from_t = lambda a: a.reshape(nb * t, -1)
        yc_t, sre_p, sim_p, yd_t, lru_p, lcv_p = _scan_mixers(
            to_t(pc["su"]), to_t(pc["lx"]), to_t(pc["lg"]), s5_w, lru_w, l, tc)
        xp = _merge(xp, mp, ya, yb, from_t(yc_t), from_t(yd_t), w_mg, wbr, wout, ln_g4, ln_b4, alpha, tm_ffn)
        xp = _ffn(xp, mp, 2, 1, wup, wdn, ln_g4, ln_b4, alpha, tm_ffn)
        new_p.append((ret_p, gdn_p, gcv_p,
                      sre_p.reshape(nb, SSM_GROUPS, SSM_P), sim_p.reshape(nb, SSM_GROUPS, SSM_P),
                      lru_p, lcv_p.transpose(1, 0, 2)))

        xs = _ffn(xs, ms, 0, 0, wup, wdn, ln_g4, ln_b4, alpha, ns)
        sc = _inproj(xs, ms, w_packed, ns)
        (qkt, gv, gx, yc, lh, gcv_s, sre_s, sim_s, lcv_s) = _smix1(
            sc, cos_s, sin_s, s_states, (gdn_conv_w, alog_rows, dtb_rows), s5_w, lru_w, l)
        ya, yb, yd, *big_s = _smix2(qkt, sc["ret"], gv, gx, sc["ret"], sc["gz"], lh, sc["lg"],
                                    norm_w, state_ret, state_gdn, l, SUBLANE, big_s)
        xs = _merge(xs, ms, ya, yb, yc, yd, w_mg, wbr, wout, ln_g4, ln_b4, alpha, ns)
        xs = _ffn(xs, ms, 2, 1, wup, wdn, ln_g4, ln_b4, alpha, ns)
        new_s.append((gcv_s.transpose(1, 0, 2),
                      sre_s.reshape(ns, SSM_GROUPS, SSM_P), sim_s.reshape(ns, SSM_GROUPS, SSM_P),
                      lh, lcv_s.transpose(1, 0, 2)))

    ret_p, gdn_p, gcv_p, sre_p, sim_p, lru_p, lcv_p = [jnp.stack(z) for z in zip(*new_p)]
    gcv_s, sre_s, sim_s, lru_s, lcv_s = [jnp.stack(z) for z in zip(*new_s)]
    ret_s, gdn_s = big_s
    return (xp.reshape(nb, t, D_MODEL), xs.reshape(ns, 1, D_MODEL),
            ret_p, ret_s, gdn_p, gdn_s, gcv_p, gcv_s, sre_p, sre_s, sim_p, sim_s,
            lru_p, lru_s, lcv_p, lcv_s)
```

```python
import functools
import math

import numpy as np
import jax
import jax.numpy as jnp
from jax import lax
from jax.experimental import pallas as pl
from jax.experimental.pallas import tpu as pltpu

F32 = jnp.float32
BF16 = jnp.bfloat16

D_MODEL = 1024
RET_HEADS, RET_DK, RET_DV, RET_CHUNK = 4, 64, 128, 128
RET_STEP_CHUNKS = 4
ROPE_BASE = 10000.0
GDN_HEADS, GDN_DK, GDN_DV, GDN_CHUNK = 4, 128, 128, 64
GDN_STEP_CHUNKS = 4
GDN_QKV = 2 * GDN_HEADS * GDN_DK + GDN_HEADS * GDN_DV
CONV_W = 4
SSM_GROUP, SSM_GROUPS, SSM_P = 16, 32, 64
SSM_WIDTH = SSM_GROUP * SSM_GROUPS
SSM_STATE = SSM_GROUPS * SSM_P
LRU_WIDTH = 512
LRU_C = 8.0
N_BRANCH, BRANCH_W = 4, 512
D_FF = 2816
N_SUB = 3
LN_EPS = 1e-5
PAST_LEN = 16384

RET_QK = RET_HEADS * RET_DK
RET_V = RET_HEADS * RET_DV
RET_W = 2 * RET_QK + 2 * RET_V
GDN_K = GDN_HEADS * GDN_DK
QKT_ROWS = 2 * RET_QK + 2 * GDN_K
LANE = 128
SUBLANE = 8
FF_CHUNKS = ((0, 768), (768, 1024), (1792, 1024))
VMEM_LIMIT = 56 * 1024 * 1024


def _cparams(*sem):
    return pltpu.CompilerParams(dimension_semantics=sem, vmem_limit_bytes=VMEM_LIMIT)


def _layer_spec(a, layer):
    nd = a.ndim
    return pl.BlockSpec((None,) + a.shape[1:], lambda *_: (layer,) + (0,) * (nd - 1))


def _resident(shape, index_map):
    return pl.BlockSpec(shape, index_map, pipeline_mode=pl.Buffered(1))


def _ln(x):
    mu = jnp.mean(x, axis=-1, keepdims=True)
    xc = x - mu
    return xc * lax.rsqrt(jnp.mean(xc * xc, axis=-1, keepdims=True) + LN_EPS)


def _sigmoid(x):
    return 0.5 * jnp.tanh(0.5 * x) + 0.5


def _silu(x):
    return x * _sigmoid(x)


def _softplus(x):
    return jnp.maximum(x, 0.0) + jnp.log1p(jnp.exp(-jnp.abs(x)))


def _bdot(a, w):
    return jnp.dot(a.astype(BF16), w.astype(BF16), preferred_element_type=F32)


def _split(x):
    hi = x.astype(BF16)
    return hi, (x - hi.astype(F32)).astype(BF16)


def _dot_nt(a, b):
    return lax.dot_general(a, b, (((1,), (1,)), ((), ())), preferred_element_type=F32)


def _iota(shape, dim):
    return lax.broadcasted_iota(jnp.int32, shape, dim)


def _to_time_major(x):
    nb, tc, width = x.shape
    return jnp.swapaxes(x, 0, 1).reshape(tc * nb, width)


def _from_time_major(y, nb):
    rows, width = y.shape
    return jnp.swapaxes(y.reshape(rows // nb, nb, width), 0, 1)


def _ada_kernel(cp_ref, cs_ref, w_ref, b_ref, op_ref, os_ref):
    w = w_ref[...].astype(BF16)
    op_ref[...] = _bdot(_silu(cp_ref[...]), w) + b_ref[...]
    os_ref[...] = _bdot(_silu(cs_ref[...]), w) + b_ref[...]


def _ada(c_prompt, c_sample, w_ada, b_ada):
    depth, _, n_out = w_ada.shape
    tn = 1152
    cond = lambda c: pl.BlockSpec(c.shape, lambda l, j: (0, 0))
    out = lambda c: pl.BlockSpec((None, c.shape[0], tn), lambda l, j: (l, 0, j))
    return pl.pallas_call(
        _ada_kernel,
        grid=(depth, n_out // tn),
        in_specs=[
            cond(c_prompt), cond(c_sample),
            pl.BlockSpec((None, D_MODEL, tn), lambda l, j: (l, 0, j)),
            pl.BlockSpec((None, 1, tn), lambda l, j: (l, 0, j)),
        ],
        out_specs=[out(c_prompt), out(c_sample)],
        out_shape=[jax.ShapeDtypeStruct((depth, c.shape[0], n_out), F32) for c in (c_prompt, c_sample)],
        compiler_params=_cparams("parallel", "parallel"),
        name="ada",
    )(c_prompt, c_sample, w_ada, b_ada.reshape(depth, 1, n_out))


class _Mod:
    def __init__(self, arr, layer, per_row, rows_per_batch=None):
        self.arr, self.layer, self.per_row, self.rpb = arr, layer, per_row, rows_per_batch

    def spec(self, sub, which, tm):
        k, l = 3 * sub + which, self.layer
        if self.per_row:
            return pl.BlockSpec((None, tm, D_MODEL), lambda i: (l, i, k))
        tiles = self.rpb // tm
        return pl.BlockSpec((None, None, None, 1, D_MODEL), lambda i: (l, k, i // tiles, 0, 0))


def _row_parts(ref, parts):
    if ref.shape[0] == 1:
        return [ref[...]] * parts
    hm = ref.shape[0] // parts
    return [ref[k * hm:(k + 1) * hm, :] for k in range(parts)]


def _ffn_kernel(x_ref, sh_ref, sc_ref, gt_ref, wup_ref, wdn_ref, g_ref, b_ref, o_ref, *, alpha, parts):
    ks = range(parts)
    xs, shs, scs, gts = (_row_parts(r, parts) for r in (x_ref, sh_ref, sc_ref, gt_ref))
    hs = [(_ln(xs[k]) * (1.0 + scs[k]) + shs[k]).astype(BF16) for k in ks]
    accs = [jnp.zeros(xs[k].shape, F32) for k in ks]
    for lo, width in FF_CHUNKS:
        a = [jnp.dot(hs[k], wup_ref[:, lo:lo + width], preferred_element_type=F32) for k in ks]
        b = [jnp.dot(hs[k], wup_ref[:, D_FF + lo:D_FF + lo + width], preferred_element_type=F32) for k in ks]
        act = [(_silu(a[k]) * b[k]).astype(BF16) for k in ks]
        accs = [accs[k] + jnp.dot(act[k], wdn_ref[lo:lo + width, :], preferred_element_type=F32) for k in ks]
    hm = x_ref.shape[0] // parts
    for k in ks:
        z = alpha * xs[k] + 0.5 * gts[k] * accs[k]
        o_ref[k * hm:(k + 1) * hm, :] = _ln(z) * g_ref[...] + b_ref[...]


def _ffn(x, mod, sub, which, wup, wdn, ln_g, ln_b, alpha, tm):
    n = x.shape[0]
    l = mod.layer
    row = pl.BlockSpec((tm, D_MODEL), lambda i: (i, 0))
    return pl.pallas_call(
        functools.partial(_ffn_kernel, alpha=alpha, parts=2 if tm >= 512 else 1),
        grid=(n // tm,),
        in_specs=[
            row, mod.spec(sub, 0, tm), mod.spec(sub, 1, tm), mod.spec(sub, 2, tm),
            _resident((None, None, D_MODEL, 2 * D_FF), lambda i: (l, which, 0, 0)),
            _resident((None, None, D_FF, D_MODEL), lambda i: (l, which, 0, 0)),
            pl.BlockSpec((None, None, 1, D_MODEL), lambda i: (l, sub, 0, 0)),
            pl.BlockSpec((None, None, 1, D_MODEL), lambda i: (l, sub, 0, 0)),
        ],
        out_specs=row,
        out_shape=jax.ShapeDtypeStruct((n, D_MODEL), F32),
        compiler_params=_cparams("parallel"),
        name="ffn",
    )(x, mod.arr, mod.arr, mod.arr, wup, wdn, ln_g, ln_b)


_IN_PIECES = (("ret", RET_W), ("gqkv", GDN_QKV), ("gab", LANE), ("gz", 512), ("su", SSM_WIDTH),
              ("lx", LRU_WIDTH), ("lg", LRU_WIDTH))
IN_MAIN = sum(w for _, w in _IN_PIECES)
MG_WIDTH = N_BRANCH * D_MODEL


PACK_ROWS = 512


def _pack_kernel(wt_ref, main_ref, mg_ref):
    gate_lo = RET_W + GDN_QKV
    gate_hi = gate_lo + 2 * GDN_HEADS
    mg_lo = wt_ref.shape[0] - MG_WIDTH

    def move(dst_ref, dst, src, width):
        for c in range(0, width, PACK_ROWS):
            dst_ref[:, dst + c:dst + c + PACK_ROWS] = wt_ref[src + c:src + c + PACK_ROWS, :].T.astype(BF16)

    move(main_ref, 0, 0, gate_lo)
    g = wt_ref[gate_lo:gate_lo + LANE, :].T
    main_ref[:, gate_lo:gate_lo + LANE] = jnp.where(_iota(g.shape, 1) < 2 * GDN_HEADS, g, 0.0).astype(BF16)
    move(main_ref, gate_lo + LANE, gate_hi, mg_lo - gate_hi)
    move(mg_ref, 0, mg_lo, MG_WIDTH)


def _pack_w_in(w_in):
    depth, d_in, d_out = w_in.shape
    tk = 256
    blk = lambda w: pl.BlockSpec((None, tk, w), lambda l, i: (l, i, 0))
    return pl.pallas_call(
        _pack_kernel,
        grid=(depth, d_in // tk),
        in_specs=[pl.BlockSpec((None, d_out, tk), lambda l, i: (l, 0, i))],
        out_specs=[blk(IN_MAIN), blk(MG_WIDTH)],
        out_shape=[jax.ShapeDtypeStruct((depth, d_in, IN_MAIN), BF16),
                   jax.ShapeDtypeStruct((depth, d_in, MG_WIDTH), BF16)],
        compiler_params=_cparams("parallel", "parallel"),
        name="pack_w_in",
    )(jnp.swapaxes(w_in, 1, 2))


def _l2n(x):
    return x * lax.rsqrt(jnp.sum(x * x, axis=-1, keepdims=True) + 1e-6)


def _gdn_activate(conv):
    cq = _silu(conv)
    q = [_l2n(cq[:, h * GDN_DK:(h + 1) * GDN_DK]) * (GDN_DK ** -0.5) for h in range(GDN_HEADS)]
    k = [_l2n(cq[:, GDN_K + h * GDN_DK:GDN_K + (h + 1) * GDN_DK]) for h in range(GDN_HEADS)]
    return jnp.concatenate(q + k + [cq[:, 2 * GDN_K:]], axis=1)


def _inproj_kernel(x_ref, sh_ref, sc_ref, w_ref, *refs, seq_tiles):
    parts = 2 if x_ref.shape[0] >= 512 else 1
    xs, shs, scs = (_row_parts(r, parts) for r in (x_ref, sh_ref, sc_ref))
    hs = [(_ln(xs[k]) * (1.0 + scs[k]) + shs[k]).astype(BF16) for k in range(parts)]
    n_out = len(_IN_PIECES)
    if seq_tiles is None:
        o_refs = refs
    else:
        cw_ref, o_refs, cv_ref, xbuf = refs[0], refs[1:1 + n_out], refs[1 + n_out], refs[2 + n_out]

        @pl.when(pl.program_id(0) % seq_tiles == 0)
        def _():
            xbuf[0:SUBLANE, :] = jnp.zeros((SUBLANE, GDN_QKV), F32)

    offsets = np.cumsum([0] + [w for _, w in _IN_PIECES])
    order = sorted(range(n_out), key=lambda p: _IN_PIECES[p][0] != "gqkv")
    for p in order:
        (name, width), o_ref, lo = _IN_PIECES[p], o_refs[p], int(offsets[p])
        y = jnp.concatenate([jnp.dot(h, w_ref[:, lo:lo + width], preferred_element_type=F32) for h in hs], axis=0)
        if name == "gqkv" and seq_tiles is not None:
            tm = y.shape[0]
            xbuf[SUBLANE:SUBLANE + tm, :] = y
            conv = cw_ref[CONV_W - 1:CONV_W, :] * y
            for j in range(CONV_W - 1):
                r0 = SUBLANE - (CONV_W - 1) + j
                conv = conv + cw_ref[j:j + 1, :] * xbuf[r0:r0 + tm, :]
            cv_ref[0] = xbuf[SUBLANE + tm - (CONV_W - 1):SUBLANE + tm, :]
            xbuf[0:SUBLANE, :] = y[tm - SUBLANE:, :]
            y = _gdn_activate(conv)
        o_ref[...] = y


def _inproj(x, mod, w_packed, tm, conv=None):
    n = x.shape[0]
    l = mod.layer
    widths = [w for _, w in _IN_PIECES]
    ins = [x, mod.arr, mod.arr, w_packed]
    in_specs = [pl.BlockSpec((tm, D_MODEL), lambda i: (i, 0)),
                mod.spec(1, 0, tm), mod.spec(1, 1, tm),
                _resident((None, D_MODEL, IN_MAIN), lambda i: (l, 0, 0))]
    out_specs = [pl.BlockSpec((tm, w), lambda i: (i, 0)) for w in widths]
    out_shape = [jax.ShapeDtypeStruct((n, w), F32) for w in widths]
    scratch, seq_tiles = [], None
    if conv is not None:
        conv_w, nb, t = conv
        seq_tiles = t // tm
        ins.append(conv_w)
        in_specs.append(_layer_spec(conv_w, l))
        out_specs.append(pl.BlockSpec((1, CONV_W - 1, GDN_QKV), lambda i: (i // seq_tiles, 0, 0)))
        out_shape.append(jax.ShapeDtypeStruct((nb, CONV_W - 1, GDN_QKV), F32))
        scratch.append(pltpu.VMEM((SUBLANE + tm, GDN_QKV), F32))
    outs = pl.pallas_call(
        functools.partial(_inproj_kernel, seq_tiles=seq_tiles),
        grid=(n // tm,),
        in_specs=in_specs,
        out_specs=out_specs,
        out_shape=out_shape,
        scratch_shapes=scratch,
        compiler_params=_cparams("arbitrary"),
        name="inproj",
    )(*ins)
    return dict(zip([nm for nm, _ in _IN_PIECES] + ["gdn_conv"], outs))


def _rope_tables(pos):
    half = RET_DK // 2
    inv = ROPE_BASE ** (-jnp.arange(half, dtype=F32) / half)
    ang = pos.astype(F32)[:, None] * inv[None, :]
    cos, sin = jnp.cos(ang), jnp.sin(ang)
    cos_t = jnp.tile(jnp.concatenate([cos, cos], axis=-1), (1, RET_HEADS))
    sin_t = jnp.tile(jnp.concatenate([-sin, sin], axis=-1), (1, RET_HEADS))
    return cos_t, sin_t


def _rotate(z, cos, sin):
    half = RET_DK // 2
    first = (_iota(z.shape, 1) % RET_DK) < half
    swapped = jnp.where(first, pltpu.roll(z, RET_QK - half, 1), pltpu.roll(z, half, 1))
    return z * cos + swapped * sin


def _ret_gammas():
    return [1.0 - 2.0 ** (-5.0 - h) for h in range(RET_HEADS)]


def _ret_consts(chunk, n_sub):
    log_g = np.log(np.array(_ret_gammas(), np.float64))
    idx = np.arange(chunk, dtype=np.float64)
    rel = idx[:, None] - idx[None, :]
    dmask = np.where(rel[None] >= 0, np.exp(log_g[:, None, None] * np.maximum(rel, 0.0)[None]), 0.0)
    qdec = np.repeat(np.exp(log_g[None, :] * (idx[:, None] + 1.0)), RET_DK, axis=1)
    kdec = np.repeat(np.exp(log_g[None, :] * (chunk - 1.0 - idx[:, None])), RET_DK, axis=1)
    qdec, kdec = np.tile(qdec, (n_sub, 1)), np.tile(kdec, (n_sub, 1))
    head_r = np.arange(RET_QK) // RET_DK
    head_c = np.arange(RET_V) // RET_DV
    bd = (head_r[:, None] == head_c[None, :]).astype(np.float64)
    cd = bd * np.exp(log_g * chunk)[head_r][:, None]
    f = lambda a: jnp.asarray(a, F32)
    return f(dmask), f(qdec), f(kdec), f(cd), f(bd)


def _ret_kernel(x_ref, cos_ref, sin_ref, dmask_ref, qdec_ref, kdec_ref, cd_ref, bd_ref,
                y_ref, st_ref, s_scr):
    c = pl.program_id(1)
    ch = RET_CHUNK
    n_sub = x_ref.shape[0] // ch

    @pl.when(c == 0)
    def _():
        s_scr[...] = jnp.zeros(s_scr.shape, F32)

    x = x_ref[...]
    q = _rotate(x[:, 0:RET_QK], cos_ref[...], sin_ref[...])
    k = _rotate(x[:, RET_QK:2 * RET_QK], cos_ref[...], sin_ref[...]) * (RET_DK ** -0.5)
    v = x[:, 2 * RET_QK:2 * RET_QK + RET_V].astype(BF16)
    gate = x[:, 2 * RET_QK + RET_V:]
    q_dec = (q * qdec_ref[...]).astype(BF16)
    k_dec = k * kdec_ref[...]
    kb = k.astype(BF16)
    head = _iota((ch, RET_QK), 1) // RET_DK
    intra, upd = {}, {}
    for n in range(n_sub):
        sl = slice(n * ch, (n + 1) * ch)
        q_heads = jnp.concatenate([jnp.where(head == h, q[sl], 0.0) for h in range(RET_HEADS)], axis=0)
        inner_all = _dot_nt(q_heads.astype(BF16), kb[sl])
        for h in range(RET_HEADS):
            inner = inner_all[h * ch:(h + 1) * ch] * dmask_ref[h]
            intra[n, h] = _bdot(inner, v[sl, h * RET_DV:(h + 1) * RET_DV])
        upd[n] = jnp.dot(k_dec[sl].T.astype(BF16), v[sl], preferred_element_type=F32) * bd_ref[...]
    s = s_scr[...]
    cross = {}
    for n in range(n_sub):
        cross[n] = jnp.dot(q_dec[n * ch:(n + 1) * ch], s.astype(BF16), preferred_element_type=F32)
        s = s * cd_ref[...] + upd[n]
    s_scr[...] = s
    for n in range(n_sub):
        sl = slice(n * ch, (n + 1) * ch)
        outs = [_ln(intra[n, h] + cross[n][:, h * RET_DV:(h + 1) * RET_DV]) for h in range(RET_HEADS)]
        y_ref[sl, :] = jnp.concatenate(outs, axis=1) * _silu(gate[sl])

    @pl.when(c == pl.num_programs(1) - 1)
    def _():
        for h in range(RET_HEADS):
            st_ref[0, h] = s[h * RET_DK:(h + 1) * RET_DK, h * RET_DV:(h + 1) * RET_DV]


def _retention_prompt(ret, nb, t):
    n_sub = RET_STEP_CHUNKS if t % (RET_STEP_CHUNKS * RET_CHUNK) == 0 else 1
    chunk = n_sub * RET_CHUNK
    nc = t // chunk
    cos_t, sin_t = _rope_tables(jnp.arange(t))
    consts = _ret_consts(RET_CHUNK, n_sub)
    full = lambda a: pl.BlockSpec(a.shape, lambda b, c: (0,) * a.ndim)
    return pl.pallas_call(
        _ret_kernel,
        grid=(nb, nc),
        in_specs=[
            pl.BlockSpec((chunk, RET_W), lambda b, c: (b * nc + c, 0)),
            pl.BlockSpec((chunk, RET_QK), lambda b, c: (c, 0)),
            pl.BlockSpec((chunk, RET_QK), lambda b, c: (c, 0)),
        ] + [full(a) for a in consts],
        out_specs=[
            pl.BlockSpec((chunk, RET_V), lambda b, c: (b * nc + c, 0)),
            pl.BlockSpec((1, RET_HEADS, RET_DK, RET_DV), lambda b, c: (b, 0, 0, 0)),
        ],
        out_shape=[
            jax.ShapeDtypeStruct((nb * t, RET_V), F32),
            jax.ShapeDtypeStruct((nb, RET_HEADS, RET_DK, RET_DV), F32),
        ],
        scratch_shapes=[pltpu.VMEM((RET_QK, RET_V), F32)],
        compiler_params=_cparams("parallel", "arbitrary"),
        name="retention",
    )(ret, cos_t, sin_t, *consts)


def _rms(x):
    return x * lax.rsqrt(jnp.mean(x * x, axis=-1, keepdims=True) + 1e-6)


def _gate_lanes(gab, alog_row, dtb_row):
    g = -jnp.exp(alog_row) * _softplus(gab + dtb_row)
    return jnp.where(_iota(gab.shape, 1) < GDN_HEADS, g, _sigmoid(gab))


def _gdn_kernel(x_ref, gab_ref, gz_ref, alog_ref, dtb_ref, nw_ref, y_ref, st_ref, s_scr):
    c = pl.program_id(1)
    ch = GDN_CHUNK

    @pl.when(c == 0)
    def _():
        s_scr[...] = jnp.zeros(s_scr.shape, F32)

    rows = x_ref.shape[0]
    cq = x_ref[...]

    gl = _gate_lanes(gab_ref[...], alog_ref[...], dtb_ref[...])
    col = _iota((ch, 2 * ch), 1) % ch
    tril = _iota((ch, 2 * ch), 0) >= col
    strict = _iota((ch, 2 * ch), 0) > col
    is_g = _iota(gl.shape, 1) < GDN_HEADS
    r_i, c_i = _iota((rows, rows), 0), _iota((rows, rows), 1)
    chunk_tril = ((r_i >= c_i) & (r_i // ch == c_i // ch)).astype(F32)
    csum = jnp.dot(chunk_tril, jnp.where(is_g, gl, 0.0), preferred_element_type=F32,
                   precision=lax.Precision.HIGHEST)
    gsel = jnp.where(is_g, csum, gl)
    n_sub = rows // ch
    heads = range(GDN_HEADS)
    probs = [(h, n) for h in heads for n in range(n_sub)]

    q_all, k_all, g_all, qe_all, rhs_all, kbeta_all = {}, {}, {}, {}, {}, {}
    for h in heads:
        q_all[h] = cq[:, h * GDN_DK:(h + 1) * GDN_DK]
        k_all[h] = cq[:, GDN_K + h * GDN_DK:GDN_K + (h + 1) * GDN_DK]
        v_h = cq[:, 2 * GDN_K + h * GDN_DV:2 * GDN_K + (h + 1) * GDN_DV]
        g_all[h] = jnp.broadcast_to(gsel[:, h:h + 1], (rows, LANE))
        b_h = jnp.broadcast_to(gsel[:, GDN_HEADS + h:GDN_HEADS + h + 1], (rows, LANE))
        e_h = jnp.exp(g_all[h])
        kbeta_all[h] = k_all[h] * b_h
        rhs_all[h] = jnp.concatenate([v_h * b_h, kbeta_all[h] * e_h], axis=1)
        qe_all[h] = q_all[h] * e_h

    decay, pp, uw, attn, kd_t, e_last = {}, {}, {}, {}, {}, {}
    for h, n in probs:
        sl = slice(n * ch, (n + 1) * ch)
        g_b = g_all[h][sl]
        g_cols = jnp.concatenate([g_b, g_b], axis=0).T[:ch, :]
        diff = g_b - g_cols
        decay[h, n] = jnp.where(tril, jnp.exp(jnp.where(tril, diff, 0.0)), 0.0)
        g_last = g_b[ch - 1:ch, :]
        e_last[h, n] = jnp.exp(g_last)
        kd_t[h, n] = (k_all[h][sl] * jnp.exp(g_last - g_b)).T.astype(BF16)
    for h, n in probs:
        sl = slice(n * ch, (n + 1) * ch)
        kb_hi, kb_lo = _split(kbeta_all[h][sl])
        k_hi, k_lo = _split(jnp.concatenate([k_all[h][sl], k_all[h][sl]], axis=0))
        kk = _dot_nt(jnp.concatenate([kb_hi, kb_hi, kb_lo, kb_lo], axis=1),
                     jnp.concatenate([k_hi, k_lo, k_hi, k_lo], axis=1))
        pp[h, n] = -jnp.where(strict, kk * decay[h, n], 0.0)
        attn[h, n] = _dot_nt(q_all[h][sl].astype(BF16), k_all[h][sl].astype(BF16)) * decay[h, n][:, :ch]
    for stage in range(int(math.log2(ch))):
        for h, n in probs:
            p_hi, p_lo = _split(pp[h, n])
            lhs = jnp.concatenate([p_hi, p_lo], axis=1)
            y = rhs_all[h][n * ch:(n + 1) * ch] if stage == 0 else uw[h, n]
            y_hi, y_lo = _split(y)
            uw[h, n] = y + jnp.dot(lhs, jnp.concatenate([y_hi, y_lo, y_hi, y_lo], axis=0),
                                   preferred_element_type=F32)
            if stage + 1 < int(math.log2(ch)):
                pp[h, n] = jnp.dot(lhs, jnp.concatenate([p_hi, p_lo, p_hi, p_lo], axis=0),
                                   preferred_element_type=F32)
    s = {h: s_scr[h] for h in heads}
    o_parts = {h: [] for h in heads}
    for n in range(n_sub):
        for h in heads:
            sl = slice(n * ch, (n + 1) * ch)
            u, w = uw[h, n][:, :GDN_DV], uw[h, n][:, GDN_DV:]
            ws_qs = _bdot(jnp.concatenate([w, qe_all[h][sl]], axis=0), s[h])
            v_new = u - ws_qs[:ch]
            o_parts[h].append(ws_qs[ch:] + _bdot(attn[h, n], v_new))
            s[h] = s[h] * e_last[h, n] + jnp.dot(kd_t[h, n], v_new.astype(BF16), preferred_element_type=F32)
    norm_w = nw_ref[...]
    gz = gz_ref[...]
    outs = []
    for h in heads:
        s_scr[h] = s[h]
        o = jnp.concatenate(o_parts[h], axis=0)
        outs.append(_rms(o) * norm_w * _silu(gz[:, h * GDN_DV:(h + 1) * GDN_DV]))
    y_ref[...] = jnp.concatenate(outs, axis=1)

    @pl.when(c == pl.num_programs(1) - 1)
    def _():
        st_ref[0] = s_scr[...]


def _gdn_prompt(gqkv, gab, gz, alog_row, dtb_row, norm_w, layer, nb, t):
    ch = GDN_STEP_CHUNKS * GDN_CHUNK if t % (GDN_STEP_CHUNKS * GDN_CHUNK) == 0 else GDN_CHUNK
    nc = t // ch
    row = lambda w: pl.BlockSpec((ch, w), lambda b, c: (b * nc + c, 0))
    return pl.pallas_call(
        _gdn_kernel,
        grid=(nb, nc),
        in_specs=[row(GDN_QKV), row(LANE), row(GDN_HEADS * GDN_DV)]
        + [_layer_spec(a, layer) for a in (alog_row, dtb_row, norm_w)],
        out_specs=[
            row(GDN_HEADS * GDN_DV),
            pl.BlockSpec((1, GDN_HEADS, GDN_DK, GDN_DV), lambda b, c: (b, 0, 0, 0)),
        ],
        out_shape=[
            jax.ShapeDtypeStruct((nb * t, GDN_HEADS * GDN_DV), F32),
            jax.ShapeDtypeStruct((nb, GDN_HEADS, GDN_DK, GDN_DV), F32),
        ],
        scratch_shapes=[pltpu.VMEM((GDN_HEADS, GDN_DK, GDN_DV), F32)],
        compiler_params=_cparams("parallel", "arbitrary"),
        name="gdn",
    )(gqkv, gab, gz, alog_row, dtb_row, norm_w)


SSM_BLK = 4
SSM_BLK_STATE = SSM_STATE // SSM_BLK


def _s5_params(lam_re, lam_im, log_step, b_re, b_im, c_re, c_im):
    step = jnp.exp(log_step.astype(F32))[:, None]
    mag = jnp.exp(lam_re * step)
    ab_re = mag * jnp.cos(lam_im * step)
    ab_im = mag * jnp.sin(lam_im * step)
    den = lam_re * lam_re + lam_im * lam_im
    nr = ab_re - 1.0
    f_re = (nr * lam_re + ab_im * lam_im) / den
    f_im = (ab_im * lam_re - nr * lam_im) / den
    bb_re = f_re[..., None] * b_re - f_im[..., None] * b_im
    bb_im = f_re[..., None] * b_im + f_im[..., None] * b_re
    gpb = SSM_GROUPS // SSM_BLK
    eye = jnp.eye(gpb, dtype=F32)

    def in_mat(bb):
        bb = bb.reshape(SSM_BLK, gpb, SSM_P, SSM_GROUP)
        m = jnp.einsum("jgpc,gk->jgckp", bb, eye)
        return m.reshape(SSM_BLK, gpb * SSM_GROUP, gpb * SSM_P).astype(BF16)

    def out_mat(cc):
        cc = cc.reshape(SSM_BLK, gpb, SSM_GROUP, SSM_P)
        m = jnp.einsum("jgcp,gk->jgpkc", cc, eye)
        return m.reshape(SSM_BLK, gpb * SSM_P, gpb * SSM_GROUP).astype(BF16)

    return (ab_re.reshape(1, SSM_STATE), ab_im.reshape(1, SSM_STATE),
            in_mat(bb_re), in_mat(bb_im), out_mat(c_re), out_mat(c_im))


def _s5_readout(u, h_re, h_im, cre_ref, cim_ref, d_row, gw_ref, gb_row):
    ys = []
    for j in range(SSM_BLK):
        sl = slice(j * SSM_BLK_STATE, (j + 1) * SSM_BLK_STATE)
        ys.append(_bdot(h_re[:, sl], cre_ref[j]) - _bdot(h_im[:, sl], cim_ref[j]))
    y = jax.nn.gelu(jnp.concatenate(ys, axis=1) + d_row * u)
    return y * _sigmoid(_bdot(y, gw_ref[...]) + gb_row)


def _scan_mixers_kernel(u_ref, x_ref, g_ref,
                        are_ref, aim_ref, bre_ref, bim_ref, cre_ref, cim_ref, d_ref, gw_ref, gb_ref,
                        cw_ref, cb_ref, wa_ref, wx_ref, ba_ref, bx_ref, lam_ref,
                        y_ref, hre_ref, him_ref, yd_ref, h_out, cv_out,
                        bu_re, bu_im, h_re, h_im, xbuf, a_buf, b_buf, h_scr, *, nb):
    i = pl.program_id(0)
    tc = u_ref.shape[1]
    rows = tc * nb
    tail = (CONV_W - 1) * nb

    @pl.when(i == 0)
    def _():
        h_re[...] = jnp.zeros(h_re.shape, F32)
        h_im[...] = jnp.zeros(h_im.shape, F32)
        xbuf[0:tail, :] = jnp.zeros((tail, LRU_WIDTH), F32)
        h_scr[...] = jnp.zeros(h_scr.shape, F32)

    u = _to_time_major(u_ref[...])
    for j in range(SSM_BLK):
        uj = u[:, j * LANE:(j + 1) * LANE]
        sl = slice(j * SSM_BLK_STATE, (j + 1) * SSM_BLK_STATE)
        bu_re[:, sl] = _bdot(uj, bre_ref[j])
        bu_im[:, sl] = _bdot(uj, bim_ref[j])
    x = _to_time_major(x_ref[...])
    xbuf[tail:tail + rows, :] = x
    conv = cw_ref[CONV_W - 1:CONV_W, :] * x
    for j in range(CONV_W - 1):
        conv = conv + cw_ref[j:j + 1, :] * xbuf[j * nb:j * nb + rows, :]
    cv_out[...] = xbuf[rows:rows + tail, :].reshape(CONV_W - 1, nb, LRU_WIDTH)
    xbuf[0:tail, :] = x[rows - tail:, :]
    a, b = _lru_gates(conv + cb_ref[...], wa_ref, wx_ref, ba_ref[...], bx_ref[...], lam_ref[...])
    a_buf[...] = a
    b_buf[...] = b

    for j in range(SSM_BLK):
        sl = slice(j * SSM_BLK_STATE, (j + 1) * SSM_BLK_STATE)
        a_re = jnp.broadcast_to(are_ref[:, sl], (nb, SSM_BLK_STATE))
        a_im = jnp.broadcast_to(aim_ref[:, sl], (nb, SSM_BLK_STATE))

        def body(t, carry):
            hr, hi = carry
            r = pl.ds(pl.multiple_of(t * nb, nb), nb)
            nr = a_re * hr - a_im * hi + bu_re[r, sl]
            ni = a_re * hi + a_im * hr + bu_im[r, sl]
            bu_re[r, sl] = nr
            bu_im[r, sl] = ni
            return nr, ni

        hr, hi = lax.fori_loop(0, tc, body, (h_re[:, sl], h_im[:, sl]), unroll=4)
        h_re[:, sl] = hr
        h_im[:, sl] = hi

    def lru_body(t, h):
        r = pl.ds(pl.multiple_of(t * nb, nb), nb)
        h = a_buf[r, :] * h + b_buf[r, :]
        b_buf[r, :] = h
        return h

    h = lax.fori_loop(0, tc, lru_body, h_scr[...], unroll=8)
    h_scr[...] = h
    h_out[...] = h

    y = _s5_readout(u, bu_re[...], bu_im[...], cre_ref, cim_ref, d_ref[...], gw_ref, gb_ref[...])
    y_ref[...] = _from_time_major(y, nb)
    hre_ref[...] = h_re[...]
    him_ref[...] = h_im[...]
    yd = b_buf[...] * jax.nn.gelu(_to_time_major(g_ref[...]))
    yd_ref[...] = _from_time_major(yd, nb)


def _scan_mixers(u_t, x_t, g_t, s5_w, lru_w, layer, tc):
    nb, t, _ = u_t.shape
    rows = tc * nb
    blk = pl.BlockSpec((nb, tc, SSM_WIDTH), lambda i: (0, i, 0))
    st = pl.BlockSpec((nb, SSM_STATE), lambda i: (0, 0))
    sds = lambda *s: jax.ShapeDtypeStruct(s, F32)
    return pl.pallas_call(
        functools.partial(_scan_mixers_kernel, nb=nb),
        grid=(t // tc,),
        in_specs=[blk, blk, blk] + [_layer_spec(a, layer) for a in (*s5_w, *lru_w)],
        out_specs=[blk, st, st, blk,
                   pl.BlockSpec((nb, LRU_WIDTH), lambda i: (0, 0)),
                   pl.BlockSpec((CONV_W - 1, nb, LRU_WIDTH), lambda i: (0, 0, 0))],
        out_shape=[sds(nb, t, SSM_WIDTH), sds(nb, SSM_STATE), sds(nb, SSM_STATE),
                   sds(nb, t, LRU_WIDTH), sds(nb, LRU_WIDTH), sds(CONV_W - 1, nb, LRU_WIDTH)],
        scratch_shapes=[pltpu.VMEM((rows, SSM_STATE), F32), pltpu.VMEM((rows, SSM_STATE), F32),
                        pltpu.VMEM((nb, SSM_STATE), F32), pltpu.VMEM((nb, SSM_STATE), F32),
                        pltpu.VMEM(((CONV_W - 1) * nb + rows, LRU_WIDTH), F32),
                        pltpu.VMEM((rows, LRU_WIDTH), F32), pltpu.VMEM((rows, LRU_WIDTH), F32),
                        pltpu.VMEM((nb, LRU_WIDTH), F32)],
        compiler_params=_cparams("arbitrary"),
        name="scan_mixers",
    )(u_t, x_t, g_t, *s5_w, *lru_w)


def _lru_gates(cx, wa_ref, wx_ref, ba_row, bx_row, lam_row):
    r = _sigmoid(_bdot(cx, wa_ref[...]) + ba_row)
    i = _sigmoid(_bdot(cx, wx_ref[...]) + bx_row)
    log_a = -LRU_C * r * _softplus(-lam_row)
    a = jnp.exp(log_a)
    th = jnp.tanh(log_a)
    b = jnp.sqrt(-2.0 * th / (1.0 - th)) * (i * cx)
    return a, b


def _merge_kernel(x_ref, sh_ref, sc_ref, gt_ref, ya_ref, yb_ref, yc_ref, yd_ref, wmg_ref, wb_ref, wo_ref,
                  g_ref, b_ref, o_ref, *, alpha, parts):
    ks = range(parts)
    xs, shs, scs, gts = (_row_parts(r, parts) for r in (x_ref, sh_ref, sc_ref, gt_ref))
    hs = [(_ln(xs[k]) * (1.0 + scs[k]) + shs[k]).astype(BF16) for k in ks]
    accs = [jnp.zeros(xs[k].shape, F32) for k in ks]
    for n, y_ref in enumerate((ya_ref, yb_ref, yc_ref, yd_ref)):
        ys = _row_parts(y_ref, parts)
        logits = [jnp.dot(hs[k], wmg_ref[:, n * D_MODEL:(n + 1) * D_MODEL], preferred_element_type=F32)
                  for k in ks]
        accs = [accs[k] + _sigmoid(logits[k]) * _bdot(ys[k], wb_ref[n]) for k in ks]
    outs = [_bdot(accs[k], wo_ref[...]) for k in ks]
    hm = x_ref.shape[0] // parts
    for k in ks:
        z = alpha * xs[k] + gts[k] * outs[k]
        o_ref[k * hm:(k + 1) * hm, :] = _ln(z) * g_ref[...] + b_ref[...]


def _merge(x, mod, ya, yb, yc, yd, w_mg, w_branch, w_out, ln_g, ln_b, alpha, tm):
    n = x.shape[0]
    l = mod.layer
    row = lambda w: pl.BlockSpec((tm, w), lambda i: (i, 0))
    return pl.pallas_call(
        functools.partial(_merge_kernel, alpha=alpha, parts=2 if tm >= 512 else 1),
        grid=(n // tm,),
        in_specs=[
            row(D_MODEL), mod.spec(1, 0, tm), mod.spec(1, 1, tm), mod.spec(1, 2, tm),
            row(BRANCH_W), row(BRANCH_W), row(BRANCH_W), row(BRANCH_W),
            _resident((None, D_MODEL, MG_WIDTH), lambda i: (l, 0, 0)),
            _resident((None, N_BRANCH, BRANCH_W, D_MODEL), lambda i: (l, 0, 0, 0)),
            _resident((None, D_MODEL, D_MODEL), lambda i: (l, 0, 0)),
            pl.BlockSpec((None, None, 1, D_MODEL), lambda i: (l, 1, 0, 0)),
            pl.BlockSpec((None, None, 1, D_MODEL), lambda i: (l, 1, 0, 0)),
        ],
        out_specs=row(D_MODEL),
        out_shape=jax.ShapeDtypeStruct((n, D_MODEL), F32),
        compiler_params=_cparams("parallel"),
        name="merge",
    )(x, mod.arr, mod.arr, mod.arr, ya, yb, yc, yd, w_mg, w_branch, w_out, ln_g, ln_b)


def _smix1_kernel(ret_ref, gqkv_ref, gab_ref, su_ref, lx_ref, cos_ref, sin_ref,
                  gcv_ref, sre_ref, sim_ref, lru_ref, lcv_ref,
                  gcw_ref, alog_ref, dtb_ref,
                  are_ref, aim_ref, bre_ref, bim_ref, cre_ref, cim_ref, d_ref, gw_ref, gb_ref,
                  lcw_ref, lcb_ref, wa_ref, wx_ref, ba_ref, bx_ref, lam_ref,
                  qkt_ref, gv_ref, gx_ref, yc_ref, lh_ref, gcv_out, sre_out, sim_out, lcv_out):
    ret = ret_ref[...]
    rq = _rotate(ret[:, 0:RET_QK], cos_ref[...], sin_ref[...])
    rk = _rotate(ret[:, RET_QK:2 * RET_QK], cos_ref[...], sin_ref[...]) * (RET_DK ** -0.5)

    x = gqkv_ref[...]
    conv = gcw_ref[3:4, :] * x
    for j in range(CONV_W - 1):
        conv = conv + gcw_ref[j:j + 1, :] * gcv_ref[j]
    gcv_out[0] = gcv_ref[1]
    gcv_out[1] = gcv_ref[2]
    gcv_out[2] = x
    cq = _gdn_activate(conv)
    gq, gk = cq[:, :GDN_K], cq[:, GDN_K:2 * GDN_K]
    gv_ref[...] = cq[:, 2 * GDN_K:]
    gl = _gate_lanes(gab_ref[...], alog_ref[...], dtb_ref[...])
    gx = jnp.concatenate([jnp.broadcast_to(gl[:, r:r + 1], (gl.shape[0], LANE))
                          for r in range(2 * GDN_HEADS)], axis=1)
    gx_ref[...] = jnp.where(_iota(gx.shape, 1) < GDN_HEADS * LANE, jnp.exp(gx), gx)
    qkt_ref[...] = jnp.concatenate([rq, rk, gq, gk], axis=1).T

    u = su_ref[...]
    h_re, h_im = [], []
    for j in range(SSM_BLK):
        sl = slice(j * SSM_BLK_STATE, (j + 1) * SSM_BLK_STATE)
        uj = u[:, j * LANE:(j + 1) * LANE]
        a_re, a_im = are_ref[:, sl], aim_ref[:, sl]
        p_re, p_im = sre_ref[:, sl], sim_ref[:, sl]
        h_re.append(a_re * p_re - a_im * p_im + _bdot(uj, bre_ref[j]))
        h_im.append(a_re * p_im + a_im * p_re + _bdot(uj, bim_ref[j]))
    h_re = jnp.concatenate(h_re, axis=1)
    h_im = jnp.concatenate(h_im, axis=1)
    sre_out[...] = h_re
    sim_out[...] = h_im
    yc_ref[...] = _s5_readout(u, h_re, h_im, cre_ref, cim_ref, d_ref[...], gw_ref, gb_ref[...])

    lx = lx_ref[...]
    conv = lcw_ref[3:4, :] * lx
    for j in range(CONV_W - 1):
        conv = conv + lcw_ref[j:j + 1, :] * lcv_ref[j]
    lcv_out[0] = lcv_ref[1]
    lcv_out[1] = lcv_ref[2]
    lcv_out[2] = lx
    a, b = _lru_gates(conv + lcb_ref[...], wa_ref, wx_ref, ba_ref[...], bx_ref[...], lam_ref[...])
    lh_ref[...] = a * lru_ref[...] + b


def _smix1(pieces, cos_row, sin_row, states, gdn_w, s5_w, lru_w, layer):
    rows = pieces["ret"].shape[0]
    acts = [pieces["ret"], pieces["gqkv"], pieces["gab"], pieces["su"], pieces["lx"], cos_row, sin_row]
    stacked = [*states, *gdn_w, *s5_w, *lru_w]
    ins = acts + stacked
    full = lambda a: pl.BlockSpec(a.shape, lambda i: (0,) * a.ndim)
    sds = lambda *s: jax.ShapeDtypeStruct(s, F32)
    out_shape = [sds(QKT_ROWS, rows), sds(rows, GDN_HEADS * GDN_DV), sds(rows, 8 * LANE),
                 sds(rows, SSM_WIDTH), sds(rows, LRU_WIDTH),
                 sds(CONV_W - 1, rows, GDN_QKV), sds(rows, SSM_STATE), sds(rows, SSM_STATE),
                 sds(CONV_W - 1, rows, LRU_WIDTH)]
    return pl.pallas_call(
        _smix1_kernel,
        grid=(1,),
        in_specs=[full(a) for a in acts] + [_layer_spec(a, layer) for a in stacked],
        out_specs=[pl.BlockSpec(s.shape, lambda i, nd=len(s.shape): (0,) * nd) for s in out_shape],
        out_shape=out_shape,
        compiler_params=_cparams("arbitrary"),
        name="sample_mix",
    )(*ins)


def _smix2_kernel(qkt_ref, rv_ref, gv_ref, gx_ref, rg_ref, gz_ref, lh_ref, lg_ref, nw_ref, sret_ref, sgdn_ref,
                  *refs, first):
    ya_ref, yb_ref, yd_ref, nret_ref, ngdn_ref, o_ret, o_gdn = refs[-7:]
    if first:
        for l in range(1, nret_ref.shape[0]):
            nret_ref[l] = jnp.zeros(nret_ref.shape[1:], F32)
            ngdn_ref[l] = jnp.zeros(ngdn_ref.shape[1:], F32)
        nret_ref, ngdn_ref = nret_ref.at[0], ngdn_ref.at[0]
    i = pl.program_id(0)
    bt = rv_ref.shape[0]
    rows = qkt_ref.shape[1]
    gammas = _ret_gammas()
    q_hi, q_lo = _split(qkt_ref[...])
    qk_split = jnp.concatenate([q_hi, q_lo], axis=1)

    for j in range(bt):
        b = i * bt + j
        onehot = jnp.where(_iota((2 * rows, LANE), 0) % rows == b, 1.0, 0.0).astype(BF16)
        cols = jnp.dot(qk_split, onehot, preferred_element_type=F32)
        r = slice(j, j + 1)
        for h in range(RET_HEADS):
            q_c = cols[h * RET_DK:(h + 1) * RET_DK]
            k_c = cols[RET_QK + h * RET_DK:RET_QK + (h + 1) * RET_DK]
            v_r = rv_ref[r, h * RET_DV:(h + 1) * RET_DV]
            s_new = gammas[h] * sret_ref[j, h] + k_c * v_r
            nret_ref[j, h] = s_new
            o_ret[r, h * RET_DV:(h + 1) * RET_DV] = jnp.sum(q_c * s_new, axis=0, keepdims=True)
        base = 2 * RET_QK
        for h in range(GDN_HEADS):
            q_c = cols[base + h * GDN_DK:base + (h + 1) * GDN_DK]
            k_c = cols[base + GDN_K + h * GDN_DK:base + GDN_K + (h + 1) * GDN_DK]
            v_r = gv_ref[r, h * GDN_DV:(h + 1) * GDN_DV]
            e_g = gx_ref[r, h * LANE:(h + 1) * LANE]
            beta = gx_ref[r, (GDN_HEADS + h) * LANE:(GDN_HEADS + h + 1) * LANE]
            s = sgdn_ref[j, h]
            v_new = beta * (v_r - e_g * jnp.sum(k_c * s, axis=0, keepdims=True))
            s_new = e_g * s + k_c * v_new
            ngdn_ref[j, h] = s_new
            o_gdn[r, h * GDN_DV:(h + 1) * GDN_DV] = jnp.sum(q_c * s_new, axis=0, keepdims=True)
    rg, gz = rg_ref[...], gz_ref[...]
    o_r, o_g = o_ret[...], o_gdn[...]
    ya_ref[...] = jnp.concatenate(
        [_ln(o_r[:, h * RET_DV:(h + 1) * RET_DV]) for h in range(RET_HEADS)], axis=1) * _silu(rg)
    yb_ref[...] = jnp.concatenate(
        [_rms(o_g[:, h * GDN_DV:(h + 1) * GDN_DV]) * nw_ref[...] for h in range(GDN_HEADS)], axis=1) * _silu(gz)
    yd_ref[...] = lh_ref[...] * jax.nn.gelu(lg_ref[...])


def _smix2(qkt, rv, gv, gx, rg, gz, lh, lg, norm_w, s_ret, s_gdn, layer, bt, prev):
    rows = rv.shape[0]
    row = lambda w: pl.BlockSpec((bt, w), lambda i: (i, 0))
    sds = lambda *s: jax.ShapeDtypeStruct(s, F32)
    ret_blk = pl.BlockSpec((None, bt, RET_HEADS, RET_DK, RET_DV), lambda i: (layer, i, 0, 0, 0))
    gdn_blk = pl.BlockSpec((None, bt, GDN_HEADS, GDN_DK, GDN_DV), lambda i: (layer, i, 0, 0, 0))
    ins = [qkt, rv, gv, gx, rg, gz, lh, lg, norm_w, s_ret, s_gdn]
    in_specs = [pl.BlockSpec(qkt.shape, lambda i: (0, 0)),
                pl.BlockSpec((bt, RET_V), lambda i: (i, 1)),
                row(GDN_HEADS * GDN_DV), row(8 * LANE),
                pl.BlockSpec((bt, RET_V), lambda i: (i, 2)),
                row(GDN_HEADS * GDN_DV), row(LRU_WIDTH), row(LRU_WIDTH),
                _layer_spec(norm_w, layer), ret_blk, gdn_blk]
    aliases = {}
    if prev is None:
        depth = s_ret.shape[0]
        out_ret = pl.BlockSpec((depth, bt, RET_HEADS, RET_DK, RET_DV), lambda i: (0, i, 0, 0, 0))
        out_gdn = pl.BlockSpec((depth, bt, GDN_HEADS, GDN_DK, GDN_DV), lambda i: (0, i, 0, 0, 0))
    else:
        out_ret, out_gdn = ret_blk, gdn_blk
        aliases = {len(ins): 3, len(ins) + 1: 4}
        ins += list(prev)
        in_specs += [pl.BlockSpec(memory_space=pl.ANY)] * 2
    return pl.pallas_call(
        functools.partial(_smix2_kernel, first=prev is None),
        grid=(rows // bt,),
        in_specs=in_specs,
        out_specs=[row(RET_V), row(GDN_HEADS * GDN_DV), row(LRU_WIDTH), out_ret, out_gdn],
        out_shape=[sds(rows, RET_V), sds(rows, GDN_HEADS * GDN_DV), sds(rows, LRU_WIDTH),
                   sds(*s_ret.shape), sds(*s_gdn.shape)],
        input_output_aliases=aliases,
        scratch_shapes=[pltpu.VMEM((bt, RET_V), F32), pltpu.VMEM((bt, GDN_HEADS * GDN_DV), F32)],
        compiler_params=_cparams("parallel"),
        name="sample_state",
    )(*ins)


def _block_diag(w):
    nb, bs, _ = w.shape
    return jnp.einsum("nij,nm->nimj", w, jnp.eye(nb, dtype=w.dtype)).reshape(nb * bs, nb * bs)


def kernel(x_prompt, x_sample, c_prompt, c_sample, state_ret, state_gdn, state_gdn_conv, state_ssm_re, state_ssm_im, state_lru, state_lru_conv, w_ada, b_ada, ln_g, ln_b, w_ffn_up, w_ffn_down, w_in, gdn_conv_w, gdn_a_log, gdn_dt_bias, gdn_norm_w, ssm_lam_re, ssm_lam_im, ssm_log_step, ssm_b_re, ssm_b_im, ssm_c_re, ssm_c_im, ssm_d, ssm_glu_w, ssm_glu_b, lru_conv_w, lru_conv_b, lru_wa, lru_ba, lru_wx, lru_bx, lru_lam, w_branch, w_out):
    nb, t, _ = x_prompt.shape
    ns = x_sample.shape[0]
    depth = w_ada.shape[0]
    assert t % RET_CHUNK == 0 and x_sample.shape[1] == 1
    alpha = (2 * depth) ** 0.25
    tm = 256 if t % 256 == 0 else RET_CHUNK
    tm_ffn = 512 if t % 512 == 0 else tm
    tc = 128 if t % 128 == 0 else 64

    wup = w_ffn_up.astype(BF16)
    wdn = w_ffn_down.astype(BF16)
    wbr = w_branch.astype(BF16)
    wout = w_out.astype(BF16)
    w_packed, w_mg = _pack_w_in(w_in)
    ln_g4 = ln_g.reshape(depth, N_SUB, 1, D_MODEL)
    ln_b4 = ln_b.reshape(depth, N_SUB, 1, D_MODEL)

    mod_p, mod_s = _ada(c_prompt, c_sample, w_ada, b_ada)
    mod_p = mod_p.reshape(depth, nb, 3 * N_SUB, 1, D_MODEL).transpose(0, 2, 1, 3, 4)

    row3 = lambda a: a.reshape(depth, 1, -1)
    lane_rows = lambda v: jnp.pad(v, ((0, 0), (0, LANE - v.shape[1]))).reshape(depth, 1, LANE)
    alog_rows, dtb_rows = lane_rows(gdn_a_log), lane_rows(gdn_dt_bias)
    norm_w = row3(gdn_norm_w)
    s5_w = (*jax.vmap(_s5_params)(ssm_lam_re, ssm_lam_im, ssm_log_step, ssm_b_re, ssm_b_im, ssm_c_re, ssm_c_im),
            row3(ssm_d), ssm_glu_w.astype(BF16), row3(ssm_glu_b))
    lru_w = (lru_conv_w, row3(lru_conv_b),
             jax.vmap(_block_diag)(lru_wa).astype(BF16), jax.vmap(_block_diag)(lru_wx).astype(BF16),
             row3(lru_ba), row3(lru_bx), row3(lru_lam))
    s_states = (state_gdn_conv.transpose(0, 2, 1, 3),
                state_ssm_re.reshape(depth, ns, SSM_STATE), state_ssm_im.reshape(depth, ns, SSM_STATE),
                state_lru, state_lru_conv.transpose(0, 2, 1, 3))

    cos_s, sin_s = _rope_tables(jnp.full((1,), PAST_LEN))
    xp = x_prompt.reshape(nb * t, D_MODEL)
    xs = x_sample.reshape(ns, D_MODEL)
    new_p, new_s = [], []
    big_s = None
    for l in range(depth):
        mp = _Mod(mod_p, l, False, t)
        ms = _Mod(mod_s, l, True)

        xp = _ffn(xp, mp, 0, 0, wup, wdn, ln_g4, ln_b4, alpha, tm_ffn)
        pc = _inproj(xp, mp, w_packed, tm_ffn, conv=(gdn_conv_w, nb, t))
        gcv_p = pc["gdn_conv"]
        ya, ret_p = _retention_prompt(pc["ret"], nb, t)
        yb, gdn_p = _gdn_prompt(pc["gqkv"], pc["gab"], pc["gz"], alog_rows, dtb_rows, norm_w, l, nb, t)
        to_t = lambda a: a.reshape(nb, t, -1)
        from_t = lambda a: a.reshape(nb * t, -1)
        yc_t, sre_p, sim_p, yd_t, lru_p, lcv_p = _scan_mixers(
            to_t(pc["su"]), to_t(pc["lx"]), to_t(pc["lg"]), s5_w, lru_w, l, tc)
        xp = _merge(xp, mp, ya, yb, from_t(yc_t), from_t(yd_t), w_mg, wbr, wout, ln_g4, ln_b4, alpha, tm_ffn)
        xp = _ffn(xp, mp, 2, 1, wup, wdn, ln_g4, ln_b4, alpha, tm_ffn)
        new_p.append((ret_p, gdn_p, gcv_p,
                      sre_p.reshape(nb, SSM_GROUPS, SSM_P), sim_p.reshape(nb, SSM_GROUPS, SSM_P),
                      lru_p, lcv_p.transpose(1, 0, 2)))

        xs = _ffn(xs, ms, 0, 0, wup, wdn, ln_g4, ln_b4, alpha, ns)
        sc = _inproj(xs, ms, w_packed, ns)
        (qkt, gv, gx, yc, lh, gcv_s, sre_s, sim_s, lcv_s) = _smix1(
            sc, cos_s, sin_s, s_states, (gdn_conv_w, alog_rows, dtb_rows), s5_w, lru_w, l)
        ya, yb, yd, *big_s = _smix2(qkt, sc["ret"], gv, gx, sc["ret"], sc["gz"], lh, sc["lg"],
                                    norm_w, state_ret, state_gdn, l, SUBLANE, big_s)
        xs = _merge(xs, ms, ya, yb, yc, yd, w_mg, wbr, wout, ln_g4, ln_b4, alpha, ns)
        xs = _ffn(xs, ms, 2, 1, wup, wdn, ln_g4, ln_b4, alpha, ns)
        new_s.append((gcv_s.transpose(1, 0, 2),
                      sre_s.reshape(ns, SSM_GROUPS, SSM_P), sim_s.reshape(ns, SSM_GROUPS, SSM_P),
                      lh, lcv_s.transpose(1, 0, 2)))

    ret_p, gdn_p, gcv_p, sre_p, sim_p, lru_p, lcv_p = [jnp.stack(z) for z in zip(*new_p)]
    gcv_s, sre_s, sim_s, lru_s, lcv_s = [jnp.stack(z) for z in zip(*new_s)]
    ret_s, gdn_s = big_s
    return (xp.reshape(nb, t, D_MODEL), xs.reshape(ns, 1, D_MODEL),
            ret_p, ret_s, gdn_p, gdn_s, gcv_p, gcv_s, sre_p, sre_s, sim_p, sim_s,
            lru_p, lru_s, lcv_p, lcv_s)
```

```python
import functools
import math

import numpy as np
import jax
import jax.numpy as jnp
from jax import lax
from jax.experimental import pallas as pl
from jax.experimental.pallas import tpu as pltpu

F32 = jnp.float32
BF16 = jnp.bfloat16

D_MODEL = 1024
RET_HEADS, RET_DK, RET_DV, RET_CHUNK = 4, 64, 128, 128
RET_STEP_CHUNKS = 4
ROPE_BASE = 10000.0
GDN_HEADS, GDN_DK, GDN_DV, GDN_CHUNK = 4, 128, 128, 64
GDN_STEP_CHUNKS = 4
GDN_QKV = 2 * GDN_HEADS * GDN_DK + GDN_HEADS * GDN_DV
CONV_W = 4
SSM_GROUP, SSM_GROUPS, SSM_P = 16, 32, 64
SSM_WIDTH = SSM_GROUP * SSM_GROUPS
SSM_STATE = SSM_GROUPS * SSM_P
LRU_WIDTH = 512
LRU_C = 8.0
N_BRANCH, BRANCH_W = 4, 512
D_FF = 2816
N_SUB = 3
LN_EPS = 1e-5
PAST_LEN = 16384

RET_QK = RET_HEADS * RET_DK
RET_V = RET_HEADS * RET_DV
RET_W = 2 * RET_QK + 2 * RET_V
GDN_K = GDN_HEADS * GDN_DK
QKT_ROWS = 2 * RET_QK + 2 * GDN_K
LANE = 128
SUBLANE = 8
FF_CHUNKS = ((0, 768), (768, 1024), (1792, 1024))
VMEM_LIMIT = 56 * 1024 * 1024


def _cparams(*sem):
    return pltpu.CompilerParams(dimension_semantics=sem, vmem_limit_bytes=VMEM_LIMIT)


def _layer_spec(a, layer):
    nd = a.ndim
    return pl.BlockSpec((None,) + a.shape[1:], lambda *_: (layer,) + (0,) * (nd - 1))


def _resident(shape, index_map):
    return pl.BlockSpec(shape, index_map, pipeline_mode=pl.Buffered(1))


def _ln(x):
    mu = jnp.mean(x, axis=-1, keepdims=True)
    xc = x - mu
    return xc * lax.rsqrt(jnp.mean(xc * xc, axis=-1, keepdims=True) + LN_EPS)


def _sigmoid(x):
    return 0.5 * jnp.tanh(0.5 * x) + 0.5


def _silu(x):
    return x * _sigmoid(x)


def _softplus(x):
    return jnp.maximum(x, 0.0) + jnp.log1p(jnp.exp(-jnp.abs(x)))


def _bdot(a, w):
    return jnp.dot(a.astype(BF16), w.astype(BF16), preferred_element_type=F32)


def _split(x):
    hi = x.astype(BF16)
    return hi, (x - hi.astype(F32)).astype(BF16)


def _dot_nt(a, b):
    return lax.dot_general(a, b, (((1,), (1,)), ((), ())), preferred_element_type=F32)


def _iota(shape, dim):
    return lax.broadcasted_iota(jnp.int32, shape, dim)


def _to_time_major(x):
    nb, tc, width = x.shape
    return jnp.swapaxes(x, 0, 1).reshape(tc * nb, width)


def _from_time_major(y, nb):
    rows, width = y.shape
    return jnp.swapaxes(y.reshape(rows // nb, nb, width), 0, 1)


def _ada_kernel(cp_ref, cs_ref, w_ref, b_ref, op_ref, os_ref):
    w = w_ref[...].astype(BF16)
    op_ref[...] = _bdot(_silu(cp_ref[...]), w) + b_ref[...]
    os_ref[...] = _bdot(_silu(cs_ref[...]), w) + b_ref[...]


def _ada(c_prompt, c_sample, w_ada, b_ada):
    depth, _, n_out = w_ada.shape
    tn = 1152
    cond = lambda c: pl.BlockSpec(c.shape, lambda l, j: (0, 0))
    out = lambda c: pl.BlockSpec((None, c.shape[0], tn), lambda l, j: (l, 0, j))
    return pl.pallas_call(
        _ada_kernel,
        grid=(depth, n_out // tn),
        in_specs=[
            cond(c_prompt), cond(c_sample),
            pl.BlockSpec((None, D_MODEL, tn), lambda l, j: (l, 0, j)),
            pl.BlockSpec((None, 1, tn), lambda l, j: (l, 0, j)),
        ],
        out_specs=[out(c_prompt), out(c_sample)],
        out_shape=[jax.ShapeDtypeStruct((depth, c.shape[0], n_out), F32) for c in (c_prompt, c_sample)],
        compiler_params=_cparams("parallel", "parallel"),
        name="ada",
    )(c_prompt, c_sample, w_ada, b_ada.reshape(depth, 1, n_out))


class _Mod:
    def __init__(self, arr, layer, per_row, rows_per_batch=None):
        self.arr, self.layer, self.per_row, self.rpb = arr, layer, per_row, rows_per_batch

    def spec(self, sub, which, tm):
        k, l = 3 * sub + which, self.layer
        if self.per_row:
            return pl.BlockSpec((None, tm, D_MODEL), lambda i: (l, i, k))
        tiles = self.rpb // tm
        return pl.BlockSpec((None, None, None, 1, D_MODEL), lambda i: (l, k, i // tiles, 0, 0))


def _row_parts(ref, parts):
    if ref.shape[0] == 1:
        return [ref[...]] * parts
    hm = ref.shape[0] // parts
    return [ref[k * hm:(k + 1) * hm, :] for k in range(parts)]


def _ffn_kernel(x_ref, sh_ref, sc_ref, gt_ref, wup_ref, wdn_ref, g_ref, b_ref, o_ref, *, alpha, parts):
    ks = range(parts)
    xs, shs, scs, gts = (_row_parts(r, parts) for r in (x_ref, sh_ref, sc_ref, gt_ref))
    hs = [(_ln(xs[k]) * (1.0 + scs[k]) + shs[k]).astype(BF16) for k in ks]
    accs = [jnp.zeros(xs[k].shape, F32) for k in ks]
    for lo, width in FF_CHUNKS:
        a = [jnp.dot(hs[k], wup_ref[:, lo:lo + width], preferred_element_type=F32) for k in ks]
        b = [jnp.dot(hs[k], wup_ref[:, D_FF + lo:D_FF + lo + width], preferred_element_type=F32) for k in ks]
        act = [(_silu(a[k]) * b[k]).astype(BF16) for k in ks]
        accs = [accs[k] + jnp.dot(act[k], wdn_ref[lo:lo + width, :], preferred_element_type=F32) for k in ks]
    hm = x_ref.shape[0] // parts
    for k in ks:
        z = alpha * xs[k] + 0.5 * gts[k] * accs[k]
        o_ref[k * hm:(k + 1) * hm, :] = _ln(z) * g_ref[...] + b_ref[...]


def _ffn(x, mod, sub, which, wup, wdn, ln_g, ln_b, alpha, tm):
    n = x.shape[0]
    l = mod.layer
    row = pl.BlockSpec((tm, D_MODEL), lambda i: (i, 0))
    return pl.pallas_call(
        functools.partial(_ffn_kernel, alpha=alpha, parts=max(1, tm // 256)),
        grid=(n // tm,),
        in_specs=[
            row, mod.spec(sub, 0, tm), mod.spec(sub, 1, tm), mod.spec(sub, 2, tm),
            _resident((None, None, D_MODEL, 2 * D_FF), lambda i: (l, which, 0, 0)),
            _resident((None, None, D_FF, D_MODEL), lambda i: (l, which, 0, 0)),
            pl.BlockSpec((None, None, 1, D_MODEL), lambda i: (l, sub, 0, 0)),
            pl.BlockSpec((None, None, 1, D_MODEL), lambda i: (l, sub, 0, 0)),
        ],
        out_specs=row,
        out_shape=jax.ShapeDtypeStruct((n, D_MODEL), F32),
        compiler_params=_cparams("parallel"),
        name="ffn",
    )(x, mod.arr, mod.arr, mod.arr, wup, wdn, ln_g, ln_b)


_IN_PIECES = (("ret", RET_W), ("gqkv", GDN_QKV), ("gab", LANE), ("gz", 512), ("su", SSM_WIDTH),
              ("lx", LRU_WIDTH), ("lg", LRU_WIDTH))
IN_MAIN = sum(w for _, w in _IN_PIECES)
MG_WIDTH = N_BRANCH * D_MODEL


PACK_ROWS = 512


def _pack_kernel(wt_ref, main_ref, mg_ref):
    gate_lo = RET_W + GDN_QKV
    gate_hi = gate_lo + 2 * GDN_HEADS
    mg_lo = wt_ref.shape[0] - MG_WIDTH

    def move(dst_ref, dst, src, width):
        for c in range(0, width, PACK_ROWS):
            dst_ref[:, dst + c:dst + c + PACK_ROWS] = wt_ref[src + c:src + c + PACK_ROWS, :].T.astype(BF16)

    move(main_ref, 0, 0, gate_lo)
    g = wt_ref[gate_lo:gate_lo + LANE, :].T
    main_ref[:, gate_lo:gate_lo + LANE] = jnp.where(_iota(g.shape, 1) < 2 * GDN_HEADS, g, 0.0).astype(BF16)
    move(main_ref, gate_lo + LANE, gate_hi, mg_lo - gate_hi)
    move(mg_ref, 0, mg_lo, MG_WIDTH)


def _pack_w_in(w_in):
    depth, d_in, d_out = w_in.shape
    tk = 256
    blk = lambda w: pl.BlockSpec((None, tk, w), lambda l, i: (l, i, 0))
    return pl.pallas_call(
        _pack_kernel,
        grid=(depth, d_in // tk),
        in_specs=[pl.BlockSpec((None, d_out, tk), lambda l, i: (l, 0, i))],
        out_specs=[blk(IN_MAIN), blk(MG_WIDTH)],
        out_shape=[jax.ShapeDtypeStruct((depth, d_in, IN_MAIN), BF16),
                   jax.ShapeDtypeStruct((depth, d_in, MG_WIDTH), BF16)],
        compiler_params=_cparams("parallel", "parallel"),
        name="pack_w_in",
    )(jnp.swapaxes(w_in, 1, 2))


def _l2n(x):
    return x * lax.rsqrt(jnp.sum(x * x, axis=-1, keepdims=True) + 1e-6)


def _gdn_activate(conv):
    cq = _silu(conv)
    q = [_l2n(cq[:, h * GDN_DK:(h + 1) * GDN_DK]) * (GDN_DK ** -0.5) for h in range(GDN_HEADS)]
    k = [_l2n(cq[:, GDN_K + h * GDN_DK:GDN_K + (h + 1) * GDN_DK]) for h in range(GDN_HEADS)]
    return jnp.concatenate(q + k + [cq[:, 2 * GDN_K:]], axis=1)


def _inproj_kernel(x_ref, sh_ref, sc_ref, w_ref, *refs, seq_tiles):
    parts = 2 if x_ref.shape[0] >= 512 else 1
    xs, shs, scs = (_row_parts(r, parts) for r in (x_ref, sh_ref, sc_ref))
    hs = [(_ln(xs[k]) * (1.0 + scs[k]) + shs[k]).astype(BF16) for k in range(parts)]
    n_out = len(_IN_PIECES)
    if seq_tiles is None:
        o_refs = refs
    else:
        cw_ref, o_refs, cv_ref, xbuf = refs[0], refs[1:1 + n_out], refs[1 + n_out], refs[2 + n_out]

        @pl.when(pl.program_id(0) % seq_tiles == 0)
        def _():
            xbuf[0:SUBLANE, :] = jnp.zeros((SUBLANE, GDN_QKV), F32)

    offsets = np.cumsum([0] + [w for _, w in _IN_PIECES])
    order = sorted(range(n_out), key=lambda p: _IN_PIECES[p][0] != "gqkv")
    for p in order:
        (name, width), o_ref, lo = _IN_PIECES[p], o_refs[p], int(offsets[p])
        y = jnp.concatenate([jnp.dot(h, w_ref[:, lo:lo + width], preferred_element_type=F32) for h in hs], axis=0)
        if name == "gqkv" and seq_tiles is not None:
            tm = y.shape[0]
            xbuf[SUBLANE:SUBLANE + tm, :] = y
            conv = cw_ref[CONV_W - 1:CONV_W, :] * y
            for j in range(CONV_W - 1):
                r0 = SUBLANE - (CONV_W - 1) + j
                conv = conv + cw_ref[j:j + 1, :] * xbuf[r0:r0 + tm, :]
            cv_ref[0] = xbuf[SUBLANE + tm - (CONV_W - 1):SUBLANE + tm, :]
            xbuf[0:SUBLANE, :] = y[tm - SUBLANE:, :]
            y = _gdn_activate(conv)
        o_ref[...] = y


def _inproj(x, mod, w_packed, tm, conv=None):
    n = x.shape[0]
    l = mod.layer
    widths = [w for _, w in _IN_PIECES]
    ins = [x, mod.arr, mod.arr, w_packed]
    in_specs = [pl.BlockSpec((tm, D_MODEL), lambda i: (i, 0)),
                mod.spec(1, 0, tm), mod.spec(1, 1, tm),
                _resident((None, D_MODEL, IN_MAIN), lambda i: (l, 0, 0))]
    out_specs = [pl.BlockSpec((tm, w), lambda i: (i, 0)) for w in widths]
    out_shape = [jax.ShapeDtypeStruct((n, w), F32) for w in widths]
    scratch, seq_tiles = [], None
    if conv is not None:
        conv_w, nb, t = conv
        seq_tiles = t // tm
        ins.append(conv_w)
        in_specs.append(_layer_spec(conv_w, l))
        out_specs.append(pl.BlockSpec((1, CONV_W - 1, GDN_QKV), lambda i: (i // seq_tiles, 0, 0)))
        out_shape.append(jax.ShapeDtypeStruct((nb, CONV_W - 1, GDN_QKV), F32))
        scratch.append(pltpu.VMEM((SUBLANE + tm, GDN_QKV), F32))
    outs = pl.pallas_call(
        functools.partial(_inproj_kernel, seq_tiles=seq_tiles),
        grid=(n // tm,),
        in_specs=in_specs,
        out_specs=out_specs,
        out_shape=out_shape,
        scratch_shapes=scratch,
        compiler_params=_cparams("arbitrary"),
        name="inproj",
    )(*ins)
    return dict(zip([nm for nm, _ in _IN_PIECES] + ["gdn_conv"], outs))


def _rope_tables(pos):
    half = RET_DK // 2
    inv = ROPE_BASE ** (-jnp.arange(half, dtype=F32) / half)
    ang = pos.astype(F32)[:, None] * inv[None, :]
    cos, sin = jnp.cos(ang), jnp.sin(ang)
    cos_t = jnp.tile(jnp.concatenate([cos, cos], axis=-1), (1, RET_HEADS))
    sin_t = jnp.tile(jnp.concatenate([-sin, sin], axis=-1), (1, RET_HEADS))
    return cos_t, sin_t


def _rotate(z, cos, sin):
    half = RET_DK // 2
    first = (_iota(z.shape, 1) % RET_DK) < half
    swapped = jnp.where(first, pltpu.roll(z, RET_QK - half, 1), pltpu.roll(z, half, 1))
    return z * cos + swapped * sin


def _ret_gammas():
    return [1.0 - 2.0 ** (-5.0 - h) for h in range(RET_HEADS)]


def _ret_consts(chunk, n_sub):
    log_g = np.log(np.array(_ret_gammas(), np.float64))
    idx = np.arange(chunk, dtype=np.float64)
    rel = idx[:, None] - idx[None, :]
    dmask = np.where(rel[None] >= 0, np.exp(log_g[:, None, None] * np.maximum(rel, 0.0)[None]), 0.0)
    qdec = np.repeat(np.exp(log_g[None, :] * (idx[:, None] + 1.0)), RET_DK, axis=1)
    kdec = np.repeat(np.exp(log_g[None, :] * (chunk - 1.0 - idx[:, None])), RET_DK, axis=1)
    qdec, kdec = np.tile(qdec, (n_sub, 1)), np.tile(kdec, (n_sub, 1))
    head_r = np.arange(RET_QK) // RET_DK
    head_c = np.arange(RET_V) // RET_DV
    bd = (head_r[:, None] == head_c[None, :]).astype(np.float64)
    cd = bd * np.exp(log_g * chunk)[head_r][:, None]
    f = lambda a: jnp.asarray(a, F32)
    return f(dmask), f(qdec), f(kdec), f(cd), f(bd)


def _ret_kernel(x_ref, cos_ref, sin_ref, dmask_ref, qdec_ref, kdec_ref, cd_ref, bd_ref,
                y_ref, st_ref, s_scr):
    c = pl.program_id(1)
    ch = RET_CHUNK
    n_sub = x_ref.shape[0] // ch

    @pl.when(c == 0)
    def _():
        s_scr[...] = jnp.zeros(s_scr.shape, F32)

    x = x_ref[...]
    q = _rotate(x[:, 0:RET_QK], cos_ref[...], sin_ref[...])
    k = _rotate(x[:, RET_QK:2 * RET_QK], cos_ref[...], sin_ref[...]) * (RET_DK ** -0.5)
    v = x[:, 2 * RET_QK:2 * RET_QK + RET_V].astype(BF16)
    gate = x[:, 2 * RET_QK + RET_V:]
    q_dec = (q * qdec_ref[...]).astype(BF16)
    k_dec = k * kdec_ref[...]
    kb = k.astype(BF16)
    head = _iota((ch, RET_QK), 1) // RET_DK
    intra, upd = {}, {}
    for n in range(n_sub):
        sl = slice(n * ch, (n + 1) * ch)
        q_heads = jnp.concatenate([jnp.where(head == h, q[sl], 0.0) for h in range(RET_HEADS)], axis=0)
        inner_all = _dot_nt(q_heads.astype(BF16), kb[sl])
        for h in range(RET_HEADS):
            inner = inner_all[h * ch:(h + 1) * ch] * dmask_ref[h]
            intra[n, h] = _bdot(inner, v[sl, h * RET_DV:(h + 1) * RET_DV])
        upd[n] = jnp.dot(k_dec[sl].T.astype(BF16), v[sl], preferred_element_type=F32) * bd_ref[...]
    s = s_scr[...]
    cross = {}
    for n in range(n_sub):
        cross[n] = jnp.dot(q_dec[n * ch:(n + 1) * ch], s.astype(BF16), preferred_element_type=F32)
        s = s * cd_ref[...] + upd[n]
    s_scr[...] = s
    for n in range(n_sub):
        sl = slice(n * ch, (n + 1) * ch)
        outs = [_ln(intra[n, h] + cross[n][:, h * RET_DV:(h + 1) * RET_DV]) for h in range(RET_HEADS)]
        y_ref[sl, :] = jnp.concatenate(outs, axis=1) * _silu(gate[sl])

    @pl.when(c == pl.num_programs(1) - 1)
    def _():
        for h in range(RET_HEADS):
            st_ref[0, h] = s[h * RET_DK:(h + 1) * RET_DK, h * RET_DV:(h + 1) * RET_DV]


def _retention_prompt(ret, nb, t):
    n_sub = RET_STEP_CHUNKS if t % (RET_STEP_CHUNKS * RET_CHUNK) == 0 else 1
    chunk = n_sub * RET_CHUNK
    nc = t // chunk
    cos_t, sin_t = _rope_tables(jnp.arange(t))
    consts = _ret_consts(RET_CHUNK, n_sub)
    full = lambda a: pl.BlockSpec(a.shape, lambda b, c: (0,) * a.ndim)
    return pl.pallas_call(
        _ret_kernel,
        grid=(nb, nc),
        in_specs=[
            pl.BlockSpec((chunk, RET_W), lambda b, c: (b * nc + c, 0)),
            pl.BlockSpec((chunk, RET_QK), lambda b, c: (c, 0)),
            pl.BlockSpec((chunk, RET_QK), lambda b, c: (c, 0)),
        ] + [full(a) for a in consts],
        out_specs=[
            pl.BlockSpec((chunk, RET_V), lambda b, c: (b * nc + c, 0)),
            pl.BlockSpec((1, RET_HEADS, RET_DK, RET_DV), lambda b, c: (b, 0, 0, 0)),
        ],
        out_shape=[
            jax.ShapeDtypeStruct((nb * t, RET_V), F32),
            jax.ShapeDtypeStruct((nb, RET_HEADS, RET_DK, RET_DV), F32),
        ],
        scratch_shapes=[pltpu.VMEM((RET_QK, RET_V), F32)],
        compiler_params=_cparams("parallel", "arbitrary"),
        name="retention",
    )(ret, cos_t, sin_t, *consts)


def _rms(x):
    return x * lax.rsqrt(jnp.mean(x * x, axis=-1, keepdims=True) + 1e-6)


def _gate_lanes(gab, alog_row, dtb_row):
    g = -jnp.exp(alog_row) * _softplus(gab + dtb_row)
    return jnp.where(_iota(gab.shape, 1) < GDN_HEADS, g, _sigmoid(gab))


def _gdn_kernel(x_ref, gab_ref, gz_ref, alog_ref, dtb_ref, nw_ref, y_ref, st_ref, s_scr):
    c = pl.program_id(1)
    ch = GDN_CHUNK

    @pl.when(c == 0)
    def _():
        s_scr[...] = jnp.zeros(s_scr.shape, F32)

    rows = x_ref.shape[0]
    cq = x_ref[...]

    gl = _gate_lanes(gab_ref[...], alog_ref[...], dtb_ref[...])
    col = _iota((ch, 2 * ch), 1) % ch
    tril = _iota((ch, 2 * ch), 0) >= col
    strict = _iota((ch, 2 * ch), 0) > col
    is_g = _iota(gl.shape, 1) < GDN_HEADS
    r_i, c_i = _iota((rows, rows), 0), _iota((rows, rows), 1)
    chunk_tril = ((r_i >= c_i) & (r_i // ch == c_i // ch)).astype(F32)
    csum = jnp.dot(chunk_tril, jnp.where(is_g, gl, 0.0), preferred_element_type=F32,
                   precision=lax.Precision.HIGHEST)
    gsel = jnp.where(is_g, csum, gl)
    n_sub = rows // ch
    heads = range(GDN_HEADS)
    probs = [(h, n) for h in heads for n in range(n_sub)]

    q_all, k_all, g_all, qe_all, rhs_all, kbeta_all = {}, {}, {}, {}, {}, {}
    for h in heads:
        q_all[h] = cq[:, h * GDN_DK:(h + 1) * GDN_DK]
        k_all[h] = cq[:, GDN_K + h * GDN_DK:GDN_K + (h + 1) * GDN_DK]
        v_h = cq[:, 2 * GDN_K + h * GDN_DV:2 * GDN_K + (h + 1) * GDN_DV]
        g_all[h] = jnp.broadcast_to(gsel[:, h:h + 1], (rows, LANE))
        b_h = jnp.broadcast_to(gsel[:, GDN_HEADS + h:GDN_HEADS + h + 1], (rows, LANE))
        e_h = jnp.exp(g_all[h])
        kbeta_all[h] = k_all[h] * b_h
        rhs_all[h] = jnp.concatenate([v_h * b_h, kbeta_all[h] * e_h], axis=1)
        qe_all[h] = q_all[h] * e_h

    decay, pp, uw, attn, kd_t, e_last = {}, {}, {}, {}, {}, {}
    for h, n in probs:
        sl = slice(n * ch, (n + 1) * ch)
        g_b = g_all[h][sl]
        g_cols = jnp.concatenate([g_b, g_b], axis=0).T[:ch, :]
        diff = g_b - g_cols
        decay[h, n] = jnp.where(tril, jnp.exp(jnp.where(tril, diff, 0.0)), 0.0)
        g_last = g_b[ch - 1:ch, :]
        e_last[h, n] = jnp.exp(g_last)
        kd_t[h, n] = (k_all[h][sl] * jnp.exp(g_last - g_b)).T.astype(BF16)
    for h, n in probs:
        sl = slice(n * ch, (n + 1) * ch)
        kb_hi, kb_lo = _split(kbeta_all[h][sl])
        k_hi, k_lo = _split(jnp.concatenate([k_all[h][sl], k_all[h][sl]], axis=0))
        kk = _dot_nt(jnp.concatenate([kb_hi, kb_hi, kb_lo, kb_lo], axis=1),
                     jnp.concatenate([k_hi, k_lo, k_hi, k_lo], axis=1))
        pp[h, n] = -jnp.where(strict, kk * decay[h, n], 0.0)
        attn[h, n] = _dot_nt(q_all[h][sl].astype(BF16), k_all[h][sl].astype(BF16)) * decay[h, n][:, :ch]
    for stage in range(int(math.log2(ch))):
        for h, n in probs:
            p_hi, p_lo = _split(pp[h, n])
            lhs = jnp.concatenate([p_hi, p_lo], axis=1)
            y = rhs_all[h][n * ch:(n + 1) * ch] if stage == 0 else uw[h, n]
            y_hi, y_lo = _split(y)
            uw[h, n] = y + jnp.dot(lhs, jnp.concatenate([y_hi, y_lo, y_hi, y_lo], axis=0),
                                   preferred_element_type=F32)
            if stage + 1 < int(math.log2(ch)):
                pp[h, n] = jnp.dot(lhs, jnp.concatenate([p_hi, p_lo, p_hi, p_lo], axis=0),
                                   preferred_element_type=F32)
    s = {h: s_scr[h] for h in heads}
    o_parts = {h: [] for h in heads}
    for n in range(n_sub):
        for h in heads:
            sl = slice(n * ch, (n + 1) * ch)
            u, w = uw[h, n][:, :GDN_DV], uw[h, n][:, GDN_DV:]
            ws_qs = _bdot(jnp.concatenate([w, qe_all[h][sl]], axis=0), s[h])
            v_new = u - ws_qs[:ch]
            o_parts[h].append(ws_qs[ch:] + _bdot(attn[h, n], v_new))
            s[h] = s[h] * e_last[h, n] + jnp.dot(kd_t[h, n], v_new.astype(BF16), preferred_element_type=F32)
    norm_w = nw_ref[...]
    gz = gz_ref[...]
    outs = []
    for h in heads:
        s_scr[h] = s[h]
        o = jnp.concatenate(o_parts[h], axis=0)
        outs.append(_rms(o) * norm_w * _silu(gz[:, h * GDN_DV:(h + 1) * GDN_DV]))
    y_ref[...] = jnp.concatenate(outs, axis=1)

    @pl.when(c == pl.num_programs(1) - 1)
    def _():
        st_ref[0] = s_scr[...]


def _gdn_prompt(gqkv, gab, gz, alog_row, dtb_row, norm_w, layer, nb, t):
    ch = GDN_STEP_CHUNKS * GDN_CHUNK if t % (GDN_STEP_CHUNKS * GDN_CHUNK) == 0 else GDN_CHUNK
    nc = t // ch
    row = lambda w: pl.BlockSpec((ch, w), lambda b, c: (b * nc + c, 0))
    return pl.pallas_call(
        _gdn_kernel,
        grid=(nb, nc),
        in_specs=[row(GDN_QKV), row(LANE), row(GDN_HEADS * GDN_DV)]
        + [_layer_spec(a, layer) for a in (alog_row, dtb_row, norm_w)],
        out_specs=[
            row(GDN_HEADS * GDN_DV),
            pl.BlockSpec((1, GDN_HEADS, GDN_DK, GDN_DV), lambda b, c: (b, 0, 0, 0)),
        ],
        out_shape=[
            jax.ShapeDtypeStruct((nb * t, GDN_HEADS * GDN_DV), F32),
            jax.ShapeDtypeStruct((nb, GDN_HEADS, GDN_DK, GDN_DV), F32),
        ],
        scratch_shapes=[pltpu.VMEM((GDN_HEADS, GDN_DK, GDN_DV), F32)],
        compiler_params=_cparams("parallel", "arbitrary"),
        name="gdn",
    )(gqkv, gab, gz, alog_row, dtb_row, norm_w)


SSM_BLK = 4
SSM_BLK_STATE = SSM_STATE // SSM_BLK


def _s5_params(lam_re, lam_im, log_step, b_re, b_im, c_re, c_im):
    step = jnp.exp(log_step.astype(F32))[:, None]
    mag = jnp.exp(lam_re * step)
    ab_re = mag * jnp.cos(lam_im * step)
    ab_im = mag * jnp.sin(lam_im * step)
    den = lam_re * lam_re + lam_im * lam_im
    nr = ab_re - 1.0
    f_re = (nr * lam_re + ab_im * lam_im) / den
    f_im = (ab_im * lam_re - nr * lam_im) / den
    bb_re = f_re[..., None] * b_re - f_im[..., None] * b_im
    bb_im = f_re[..., None] * b_im + f_im[..., None] * b_re
    gpb = SSM_GROUPS // SSM_BLK
    eye = jnp.eye(gpb, dtype=F32)

    def in_mat(bb):
        bb = bb.reshape(SSM_BLK, gpb, SSM_P, SSM_GROUP)
        m = jnp.einsum("jgpc,gk->jgckp", bb, eye)
        return m.reshape(SSM_BLK, gpb * SSM_GROUP, gpb * SSM_P).astype(BF16)

    def out_mat(cc):
        cc = cc.reshape(SSM_BLK, gpb, SSM_GROUP, SSM_P)
        m = jnp.einsum("jgcp,gk->jgpkc", cc, eye)
        return m.reshape(SSM_BLK, gpb * SSM_P, gpb * SSM_GROUP).astype(BF16)

    return (ab_re.reshape(1, SSM_STATE), ab_im.reshape(1, SSM_STATE),
            in_mat(bb_re), in_mat(bb_im), out_mat(c_re), out_mat(c_im))


def _s5_readout(u, h_re, h_im, cre_ref, cim_ref, d_row, gw_ref, gb_row):
    ys = []
    for j in range(SSM_BLK):
        sl = slice(j * SSM_BLK_STATE, (j + 1) * SSM_BLK_STATE)
        ys.append(_bdot(h_re[:, sl], cre_ref[j]) - _bdot(h_im[:, sl], cim_ref[j]))
    y = jax.nn.gelu(jnp.concatenate(ys, axis=1) + d_row * u)
    return y * _sigmoid(_bdot(y, gw_ref[...]) + gb_row)


def _scan_mixers_kernel(u_ref, x_ref, g_ref,
                        are_ref, aim_ref, bre_ref, bim_ref, cre_ref, cim_ref, d_ref, gw_ref, gb_ref,
                        cw_ref, cb_ref, wa_ref, wx_ref, ba_ref, bx_ref, lam_ref,
                        y_ref, hre_ref, him_ref, yd_ref, h_out, cv_out,
                        bu_re, bu_im, h_re, h_im, xbuf, a_buf, b_buf, h_scr, *, nb):
    i = pl.program_id(0)
    tc = u_ref.shape[1]
    rows = tc * nb
    tail = (CONV_W - 1) * nb

    @pl.when(i == 0)
    def _():
        h_re[...] = jnp.zeros(h_re.shape, F32)
        h_im[...] = jnp.zeros(h_im.shape, F32)
        xbuf[0:tail, :] = jnp.zeros((tail, LRU_WIDTH), F32)
        h_scr[...] = jnp.zeros(h_scr.shape, F32)

    u = _to_time_major(u_ref[...])
    for j in range(SSM_BLK):
        uj = u[:, j * LANE:(j + 1) * LANE]
        sl = slice(j * SSM_BLK_STATE, (j + 1) * SSM_BLK_STATE)
        bu_re[:, sl] = _bdot(uj, bre_ref[j])
        bu_im[:, sl] = _bdot(uj, bim_ref[j])
    x = _to_time_major(x_ref[...])
    xbuf[tail:tail + rows, :] = x
    conv = cw_ref[CONV_W - 1:CONV_W, :] * x
    for j in range(CONV_W - 1):
        conv = conv + cw_ref[j:j + 1, :] * xbuf[j * nb:j * nb + rows, :]
    cv_out[...] = xbuf[rows:rows + tail, :].reshape(CONV_W - 1, nb, LRU_WIDTH)
    xbuf[0:tail, :] = x[rows - tail:, :]
    a, b = _lru_gates(conv + cb_ref[...], wa_ref, wx_ref, ba_ref[...], bx_ref[...], lam_ref[...])
    a_buf[...] = a
    b_buf[...] = b

    for j in range(SSM_BLK):
        sl = slice(j * SSM_BLK_STATE, (j + 1) * SSM_BLK_STATE)
        a_re = jnp.broadcast_to(are_ref[:, sl], (nb, SSM_BLK_STATE))
        a_im = jnp.broadcast_to(aim_ref[:, sl], (nb, SSM_BLK_STATE))

        def body(t, carry):
            hr, hi = carry
            r = pl.ds(pl.multiple_of(t * nb, nb), nb)
            nr = a_re * hr - a_im * hi + bu_re[r, sl]
            ni = a_re * hi + a_im * hr + bu_im[r, sl]
            bu_re[r, sl] = nr
            bu_im[r, sl] = ni
            return nr, ni

        hr, hi = lax.fori_loop(0, tc, body, (h_re[:, sl], h_im[:, sl]), unroll=4)
        h_re[:, sl] = hr
        h_im[:, sl] = hi

    def lru_body(t, h):
        r = pl.ds(pl.multiple_of(t * nb, nb), nb)
        h = a_buf[r, :] * h + b_buf[r, :]
        b_buf[r, :] = h
        return h

    h = lax.fori_loop(0, tc, lru_body, h_scr[...], unroll=8)
    h_scr[...] = h
    h_out[...] = h

    y = _s5_readout(u, bu_re[...], bu_im[...], cre_ref, cim_ref, d_ref[...], gw_ref, gb_ref[...])
    y_ref[...] = _from_time_major(y, nb)
    hre_ref[...] = h_re[...]
    him_ref[...] = h_im[...]
    yd = b_buf[...] * jax.nn.gelu(_to_time_major(g_ref[...]))
    yd_ref[...] = _from_time_major(yd, nb)


def _scan_mixers(u_t, x_t, g_t, s5_w, lru_w, layer, tc):
    nb, t, _ = u_t.shape
    rows = tc * nb
    blk = pl.BlockSpec((nb, tc, SSM_WIDTH), lambda i: (0, i, 0))
    st = pl.BlockSpec((nb, SSM_STATE), lambda i: (0, 0))
    sds = lambda *s: jax.ShapeDtypeStruct(s, F32)
    return pl.pallas_call(
        functools.partial(_scan_mixers_kernel, nb=nb),
        grid=(t // tc,),
        in_specs=[blk, blk, blk] + [_layer_spec(a, layer) for a in (*s5_w, *lru_w)],
        out_specs=[blk, st, st, blk,
                   pl.BlockSpec((nb, LRU_WIDTH), lambda i: (0, 0)),
                   pl.BlockSpec((CONV_W - 1, nb, LRU_WIDTH), lambda i: (0, 0, 0))],
        out_shape=[sds(nb, t, SSM_WIDTH), sds(nb, SSM_STATE), sds(nb, SSM_STATE),
                   sds(nb, t, LRU_WIDTH), sds(nb, LRU_WIDTH), sds(CONV_W - 1, nb, LRU_WIDTH)],
        scratch_shapes=[pltpu.VMEM((rows, SSM_STATE), F32), pltpu.VMEM((rows, SSM_STATE), F32),
                        pltpu.VMEM((nb, SSM_STATE), F32), pltpu.VMEM((nb, SSM_STATE), F32),
                        pltpu.VMEM(((CONV_W - 1) * nb + rows, LRU_WIDTH), F32),
                        pltpu.VMEM((rows, LRU_WIDTH), F32), pltpu.VMEM((rows, LRU_WIDTH), F32),
                        pltpu.VMEM((nb, LRU_WIDTH), F32)],
        compiler_params=_cparams("arbitrary"),
        name="scan_mixers",
    )(u_t, x_t, g_t, *s5_w, *lru_w)


def _lru_gates(cx, wa_ref, wx_ref, ba_row, bx_row, lam_row):
    r = _sigmoid(_bdot(cx, wa_ref[...]) + ba_row)
    i = _sigmoid(_bdot(cx, wx_ref[...]) + bx_row)
    log_a = -LRU_C * r * _softplus(-lam_row)
    a = jnp.exp(log_a)
    th = jnp.tanh(log_a)
    b = jnp.sqrt(-2.0 * th / (1.0 - th)) * (i * cx)
    return a, b


def _merge_kernel(x_ref, sh_ref, sc_ref, gt_ref, ya_ref, yb_ref, yc_ref, yd_ref, wmg_ref, wb_ref, wo_ref,
                  g_ref, b_ref, o_ref, *, alpha, parts):
    ks = range(parts)
    xs, shs, scs, gts = (_row_parts(r, parts) for r in (x_ref, sh_ref, sc_ref, gt_ref))
    hs = [(_ln(xs[k]) * (1.0 + scs[k]) + shs[k]).astype(BF16) for k in ks]
    accs = [jnp.zeros(xs[k].shape, F32) for k in ks]
    for n, y_ref in enumerate((ya_ref, yb_ref, yc_ref, yd_ref)):
        ys = _row_parts(y_ref, parts)
        logits = [jnp.dot(hs[k], wmg_ref[:, n * D_MODEL:(n + 1) * D_MODEL], preferred_element_type=F32)
                  for k in ks]
        accs = [accs[k] + _sigmoid(logits[k]) * _bdot(ys[k], wb_ref[n]) for k in ks]
    outs = [_bdot(accs[k], wo_ref[...]) for k in ks]
    hm = x_ref.shape[0] // parts
    for k in ks:
        z = alpha * xs[k] + gts[k] * outs[k]
        o_ref[k * hm:(k + 1) * hm, :] = _ln(z) * g_ref[...] + b_ref[...]


def _merge(x, mod, ya, yb, yc, yd, w_mg, w_branch, w_out, ln_g, ln_b, alpha, tm):
    n = x.shape[0]
    l = mod.layer
    row = lambda w: pl.BlockSpec((tm, w), lambda i: (i, 0))
    return pl.pallas_call(
        functools.partial(_merge_kernel, alpha=alpha, parts=2 if tm >= 512 else 1),
        grid=(n // tm,),
        in_specs=[
            row(D_MODEL), mod.spec(1, 0, tm), mod.spec(1, 1, tm), mod.spec(1, 2, tm),
            row(BRANCH_W), row(BRANCH_W), row(BRANCH_W), row(BRANCH_W),
            _resident((None, D_MODEL, MG_WIDTH), lambda i: (l, 0, 0)),
            _resident((None, N_BRANCH, BRANCH_W, D_MODEL), lambda i: (l, 0, 0, 0)),
            _resident((None, D_MODEL, D_MODEL), lambda i: (l, 0, 0)),
            pl.BlockSpec((None, None, 1, D_MODEL), lambda i: (l, 1, 0, 0)),
            pl.BlockSpec((None, None, 1, D_MODEL), lambda i: (l, 1, 0, 0)),
        ],
        out_specs=row(D_MODEL),
        out_shape=jax.ShapeDtypeStruct((n, D_MODEL), F32),
        compiler_params=_cparams("parallel"),
        name="merge",
    )(x, mod.arr, mod.arr, mod.arr, ya, yb, yc, yd, w_mg, w_branch, w_out, ln_g, ln_b)


def _smix1_kernel(ret_ref, gqkv_ref, gab_ref, su_ref, lx_ref, cos_ref, sin_ref,
                  gcv_ref, sre_ref, sim_ref, lru_ref, lcv_ref,
                  gcw_ref, alog_ref, dtb_ref,
                  are_ref, aim_ref, bre_ref, bim_ref, cre_ref, cim_ref, d_ref, gw_ref, gb_ref,
                  lcw_ref, lcb_ref, wa_ref, wx_ref, ba_ref, bx_ref, lam_ref,
                  qkt_ref, gv_ref, gx_ref, yc_ref, lh_ref, gcv_out, sre_out, sim_out, lcv_out):
    ret = ret_ref[...]
    rq = _rotate(ret[:, 0:RET_QK], cos_ref[...], sin_ref[...])
    rk = _rotate(ret[:, RET_QK:2 * RET_QK], cos_ref[...], sin_ref[...]) * (RET_DK ** -0.5)

    x = gqkv_ref[...]
    conv = gcw_ref[3:4, :] * x
    for j in range(CONV_W - 1):
        conv = conv + gcw_ref[j:j + 1, :] * gcv_ref[j]
    gcv_out[0] = gcv_ref[1]
    gcv_out[1] = gcv_ref[2]
    gcv_out[2] = x
    cq = _gdn_activate(conv)
    gq, gk = cq[:, :GDN_K], cq[:, GDN_K:2 * GDN_K]
    gv_ref[...] = cq[:, 2 * GDN_K:]
    gl = _gate_lanes(gab_ref[...], alog_ref[...], dtb_ref[...])
    gx = jnp.concatenate([jnp.broadcast_to(gl[:, r:r + 1], (gl.shape[0], LANE))
                          for r in range(2 * GDN_HEADS)], axis=1)
    gx_ref[...] = jnp.where(_iota(gx.shape, 1) < GDN_HEADS * LANE, jnp.exp(gx), gx)
    qkt_ref[...] = jnp.concatenate([rq, rk, gq, gk], axis=1).T

    u = su_ref[...]
    h_re, h_im = [], []
    for j in range(SSM_BLK):
        sl = slice(j * SSM_BLK_STATE, (j + 1) * SSM_BLK_STATE)
        uj = u[:, j * LANE:(j + 1) * LANE]
        a_re, a_im = are_ref[:, sl], aim_ref[:, sl]
        p_re, p_im = sre_ref[:, sl], sim_ref[:, sl]
        h_re.append(a_re * p_re - a_im * p_im + _bdot(uj, bre_ref[j]))
        h_im.append(a_re * p_im + a_im * p_re + _bdot(uj, bim_ref[j]))
    h_re = jnp.concatenate(h_re, axis=1)
    h_im = jnp.concatenate(h_im, axis=1)
    sre_out[...] = h_re
    sim_out[...] = h_im
    yc_ref[...] = _s5_readout(u, h_re, h_im, cre_ref, cim_ref, d_ref[...], gw_ref, gb_ref[...])

    lx = lx_ref[...]
    conv = lcw_ref[3:4, :] * lx
    for j in range(CONV_W - 1):
        conv = conv + lcw_ref[j:j + 1, :] * lcv_ref[j]
    lcv_out[0] = lcv_ref[1]
    lcv_out[1] = lcv_ref[2]
    lcv_out[2] = lx
    a, b = _lru_gates(conv + lcb_ref[...], wa_ref, wx_ref, ba_ref[...], bx_ref[...], lam_ref[...])
    lh_ref[...] = a * lru_ref[...] + b


def _smix1(pieces, cos_row, sin_row, states, gdn_w, s5_w, lru_w, layer):
    rows = pieces["ret"].shape[0]
    acts = [pieces["ret"], pieces["gqkv"], pieces["gab"], pieces["su"], pieces["lx"], cos_row, sin_row]
    stacked = [*states, *gdn_w, *s5_w, *lru_w]
    ins = acts + stacked
    full = lambda a: pl.BlockSpec(a.shape, lambda i: (0,) * a.ndim)
    sds = lambda *s: jax.ShapeDtypeStruct(s, F32)
    out_shape = [sds(QKT_ROWS, rows), sds(rows, GDN_HEADS * GDN_DV), sds(rows, 8 * LANE),
                 sds(rows, SSM_WIDTH), sds(rows, LRU_WIDTH),
                 sds(CONV_W - 1, rows, GDN_QKV), sds(rows, SSM_STATE), sds(rows, SSM_STATE),
                 sds(CONV_W - 1, rows, LRU_WIDTH)]
    return pl.pallas_call(
        _smix1_kernel,
        grid=(1,),
        in_specs=[full(a) for a in acts] + [_layer_spec(a, layer) for a in stacked],
        out_specs=[pl.BlockSpec(s.shape, lambda i, nd=len(s.shape): (0,) * nd) for s in out_shape],
        out_shape=out_shape,
        compiler_params=_cparams("arbitrary"),
        name="sample_mix",
    )(*ins)


def _smix2_kernel(qkt_ref, rv_ref, gv_ref, gx_ref, rg_ref, gz_ref, lh_ref, lg_ref, nw_ref, sret_ref, sgdn_ref,
                  *refs, first):
    ya_ref, yb_ref, yd_ref, nret_ref, ngdn_ref, o_ret, o_gdn = refs[-7:]
    if first:
        for l in range(1, nret_ref.shape[0]):
            nret_ref[l] = jnp.zeros(nret_ref.shape[1:], F32)
            ngdn_ref[l] = jnp.zeros(ngdn_ref.shape[1:], F32)
        nret_ref, ngdn_ref = nret_ref.at[0], ngdn_ref.at[0]
    i = pl.program_id(0)
    bt = rv_ref.shape[0]
    rows = qkt_ref.shape[1]
    gammas = _ret_gammas()
    q_hi, q_lo = _split(qkt_ref[...])
    qk_split = jnp.concatenate([q_hi, q_lo], axis=1)

    for j in range(bt):
        b = i * bt + j
        onehot = jnp.where(_iota((2 * rows, LANE), 0) % rows == b, 1.0, 0.0).astype(BF16)
        cols = jnp.dot(qk_split, onehot, preferred_element_type=F32)
        r = slice(j, j + 1)
        for h in range(RET_HEADS):
            q_c = cols[h * RET_DK:(h + 1) * RET_DK]
            k_c = cols[RET_QK + h * RET_DK:RET_QK + (h + 1) * RET_DK]
            v_r = rv_ref[r, h * RET_DV:(h + 1) * RET_DV]
            s_new = gammas[h] * sret_ref[j, h] + k_c * v_r
            nret_ref[j, h] = s_new
            o_ret[r, h * RET_DV:(h + 1) * RET_DV] = jnp.sum(q_c * s_new, axis=0, keepdims=True)
        base = 2 * RET_QK
        for h in range(GDN_HEADS):
            q_c = cols[base + h * GDN_DK:base + (h + 1) * GDN_DK]
            k_c = cols[base + GDN_K + h * GDN_DK:base + GDN_K + (h + 1) * GDN_DK]
            v_r = gv_ref[r, h * GDN_DV:(h + 1) * GDN_DV]
            e_g = gx_ref[r, h * LANE:(h + 1) * LANE]
            beta = gx_ref[r, (GDN_HEADS + h) * LANE:(GDN_HEADS + h + 1) * LANE]
            s = sgdn_ref[j, h]
            v_new = beta * (v_r - e_g * jnp.sum(k_c * s, axis=0, keepdims=True))
            s_new = e_g * s + k_c * v_new
            ngdn_ref[j, h] = s_new
            o_gdn[r, h * GDN_DV:(h + 1) * GDN_DV] = jnp.sum(q_c * s_new, axis=0, keepdims=True)
    rg, gz = rg_ref[...], gz_ref[...]
    o_r, o_g = o_ret[...], o_gdn[...]
    ya_ref[...] = jnp.concatenate(
        [_ln(o_r[:, h * RET_DV:(h + 1) * RET_DV]) for h in range(RET_HEADS)], axis=1) * _silu(rg)
    yb_ref[...] = jnp.concatenate(
        [_rms(o_g[:, h * GDN_DV:(h + 1) * GDN_DV]) * nw_ref[...] for h in range(GDN_HEADS)], axis=1) * _silu(gz)
    yd_ref[...] = lh_ref[...] * jax.nn.gelu(lg_ref[...])


def _smix2(qkt, rv, gv, gx, rg, gz, lh, lg, norm_w, s_ret, s_gdn, layer, bt, prev):
    rows = rv.shape[0]
    row = lambda w: pl.BlockSpec((bt, w), lambda i: (i, 0))
    sds = lambda *s: jax.ShapeDtypeStruct(s, F32)
    ret_blk = pl.BlockSpec((None, bt, RET_HEADS, RET_DK, RET_DV), lambda i: (layer, i, 0, 0, 0))
    gdn_blk = pl.BlockSpec((None, bt, GDN_HEADS, GDN_DK, GDN_DV), lambda i: (layer, i, 0, 0, 0))
    ins = [qkt, rv, gv, gx, rg, gz, lh, lg, norm_w, s_ret, s_gdn]
    in_specs = [pl.BlockSpec(qkt.shape, lambda i: (0, 0)),
                pl.BlockSpec((bt, RET_V), lambda i: (i, 1)),
                row(GDN_HEADS * GDN_DV), row(8 * LANE),
                pl.BlockSpec((bt, RET_V), lambda i: (i, 2)),
                row(GDN_HEADS * GDN_DV), row(LRU_WIDTH), row(LRU_WIDTH),
                _layer_spec(norm_w, layer), ret_blk, gdn_blk]
    aliases = {}
    if prev is None:
        depth = s_ret.shape[0]
        out_ret = pl.BlockSpec((depth, bt, RET_HEADS, RET_DK, RET_DV), lambda i: (0, i, 0, 0, 0))
        out_gdn = pl.BlockSpec((depth, bt, GDN_HEADS, GDN_DK, GDN_DV), lambda i: (0, i, 0, 0, 0))
    else:
        out_ret, out_gdn = ret_blk, gdn_blk
        aliases = {len(ins): 3, len(ins) + 1: 4}
        ins += list(prev)
        in_specs += [pl.BlockSpec(memory_space=pl.ANY)] * 2
    return pl.pallas_call(
        functools.partial(_smix2_kernel, first=prev is None),
        grid=(rows // bt,),
        in_specs=in_specs,
        out_specs=[row(RET_V), row(GDN_HEADS * GDN_DV), row(LRU_WIDTH), out_ret, out_gdn],
        out_shape=[sds(rows, RET_V), sds(rows, GDN_HEADS * GDN_DV), sds(rows, LRU_WIDTH),
                   sds(*s_ret.shape), sds(*s_gdn.shape)],
        input_output_aliases=aliases,
        scratch_shapes=[pltpu.VMEM((bt, RET_V), F32), pltpu.VMEM((bt, GDN_HEADS * GDN_DV), F32)],
        compiler_params=_cparams("parallel"),
        name="sample_state",
    )(*ins)


def _block_diag(w):
    nb, bs, _ = w.shape
    return jnp.einsum("nij,nm->nimj", w, jnp.eye(nb, dtype=w.dtype)).reshape(nb * bs, nb * bs)


def kernel(x_prompt, x_sample, c_prompt, c_sample, state_ret, state_gdn, state_gdn_conv, state_ssm_re, state_ssm_im, state_lru, state_lru_conv, w_ada, b_ada, ln_g, ln_b, w_ffn_up, w_ffn_down, w_in, gdn_conv_w, gdn_a_log, gdn_dt_bias, gdn_norm_w, ssm_lam_re, ssm_lam_im, ssm_log_step, ssm_b_re, ssm_b_im, ssm_c_re, ssm_c_im, ssm_d, ssm_glu_w, ssm_glu_b, lru_conv_w, lru_conv_b, lru_wa, lru_ba, lru_wx, lru_bx, lru_lam, w_branch, w_out):
    nb, t, _ = x_prompt.shape
    ns = x_sample.shape[0]
    depth = w_ada.shape[0]
    assert t % RET_CHUNK == 0 and x_sample.shape[1] == 1
    alpha = (2 * depth) ** 0.25
    tm = 256 if t % 256 == 0 else RET_CHUNK
    tm_big = 512 if t % 512 == 0 else tm
    tm_ffn = 1024 if t % 1024 == 0 else tm_big
    tc = 128 if t % 128 == 0 else 64

    wup = w_ffn_up.astype(BF16)
    wdn = w_ffn_down.astype(BF16)
    wbr = w_branch.astype(BF16)
    wout = w_out.astype(BF16)
    w_packed, w_mg = _pack_w_in(w_in)
    ln_g4 = ln_g.reshape(depth, N_SUB, 1, D_MODEL)
    ln_b4 = ln_b.reshape(depth, N_SUB, 1, D_MODEL)

    mod_p, mod_s = _ada(c_prompt, c_sample, w_ada, b_ada)
    mod_p = mod_p.reshape(depth, nb, 3 * N_SUB, 1, D_MODEL).transpose(0, 2, 1, 3, 4)

    row3 = lambda a: a.reshape(depth, 1, -1)
    lane_rows = lambda v: jnp.pad(v, ((0, 0), (0, LANE - v.shape[1]))).reshape(depth, 1, LANE)
    alog_rows, dtb_rows = lane_rows(gdn_a_log), lane_rows(gdn_dt_bias)
    norm_w = row3(gdn_norm_w)
    s5_w = (*jax.vmap(_s5_params)(ssm_lam_re, ssm_lam_im, ssm_log_step, ssm_b_re, ssm_b_im, ssm_c_re, ssm_c_im),
            row3(ssm_d), ssm_glu_w.astype(BF16), row3(ssm_glu_b))
    lru_w = (lru_conv_w, row3(lru_conv_b),
             jax.vmap(_block_diag)(lru_wa).astype(BF16), jax.vmap(_block_diag)(lru_wx).astype(BF16),
             row3(lru_ba), row3(lru_bx), row3(lru_lam))
    s_states = (state_gdn_conv.transpose(0, 2, 1, 3),
                state_ssm_re.reshape(depth, ns, SSM_STATE), state_ssm_im.reshape(depth, ns, SSM_STATE),
                state_lru, state_lru_conv.transpose(0, 2, 1, 3))

    cos_s, sin_s = _rope_tables(jnp.full((1,), PAST_LEN))
    xp = x_prompt.reshape(nb * t, D_MODEL)
    xs = x_sample.reshape(ns, D_MODEL)
    new_p, new_s = [], []
    big_s = None
    for l in range(depth):
        mp = _Mod(mod_p, l, False, t)
        ms = _Mod(mod_s, l, True)

        xp = _ffn(xp, mp, 0, 0, wup, wdn, ln_g4, ln_b4, alpha, tm_ffn)
        pc = _inproj(xp, mp, w_packed, tm_big, conv=(gdn_conv_w, nb, t))
        gcv_p = pc["gdn_conv"]
        ya, ret_p = _retention_prompt(pc["ret"], nb, t)
        yb, gdn_p = _gdn_prompt(pc["gqkv"], pc["gab"], pc["gz"], alog_rows, dtb_rows, norm_w, l, nb, t)
        to_t = lambda a: a.reshape(nb, t, -1)
        from_t = lambda a: a.reshape(nb * t, -1)
        yc_t, sre_p, sim_p, yd_t, lru_p, lcv_p = _scan_mixers(
            to_t(pc["su"]), to_t(pc["lx"]), to_t(pc["lg"]), s5_w, lru_w, l, tc)
        xp = _merge(xp, mp, ya, yb, from_t(yc_t), from_t(yd_t), w_mg, wbr, wout, ln_g4, ln_b4, alpha, tm_big)
        xp = _ffn(xp, mp, 2, 1, wup, wdn, ln_g4, ln_b4, alpha, tm_ffn)
        new_p.append((ret_p, gdn_p, gcv_p,
                      sre_p.reshape(nb, SSM_GROUPS, SSM_P), sim_p.reshape(nb, SSM_GROUPS, SSM_P),
                      lru_p, lcv_p.transpose(1, 0, 2)))

        xs = _ffn(xs, ms, 0, 0, wup, wdn, ln_g4, ln_b4, alpha, ns)
        sc = _inproj(xs, ms, w_packed, ns)
        (qkt, gv, gx, yc, lh, gcv_s, sre_s, sim_s, lcv_s) = _smix1(
            sc, cos_s, sin_s, s_states, (gdn_conv_w, alog_rows, dtb_rows), s5_w, lru_w, l)
        ya, yb, yd, *big_s = _smix2(qkt, sc["ret"], gv, gx, sc["ret"], sc["gz"], lh, sc["lg"],
                                    norm_w, state_ret, state_gdn, l, SUBLANE, big_s)
        xs = _merge(xs, ms, ya, yb, yc, yd, w_mg, wbr, wout, ln_g4, ln_b4, alpha, ns)
        xs = _ffn(xs, ms, 2, 1, wup, wdn, ln_g4, ln_b4, alpha, ns)
        new_s.append((gcv_s.transpose(1, 0, 2),
                      sre_s.reshape(ns, SSM_GROUPS, SSM_P), sim_s.reshape(ns, SSM_GROUPS, SSM_P),
                      lh, lcv_s.transpose(1, 0, 2)))

    ret_p, gdn_p, gcv_p, sre_p, sim_p, lru_p, lcv_p = [jnp.stack(z) for z in zip(*new_p)]
    gcv_s, sre_s, sim_s, lru_s, lcv_s = [jnp.stack(z) for z in zip(*new_s)]
    ret_s, gdn_s = big_s
    return (xp.reshape(nb, t, D_MODEL), xs.reshape(ns, 1, D_MODEL),
            ret_p, ret_s, gdn_p, gdn_s, gcv_p, gcv_s, sre_p, sre_s, sim_p, sim_s,
            lru_p, lru_s, lcv_p, lcv_s)
```

```python
import functools
import math

import numpy as np
import jax
import jax.numpy as jnp
from jax import lax
from jax.experimental import pallas as pl
from jax.experimental.pallas import tpu as pltpu

F32 = jnp.float32
BF16 = jnp.bfloat16

D_MODEL = 1024
RET_HEADS, RET_DK, RET_DV, RET_CHUNK = 4, 64, 128, 128
RET_STEP_CHUNKS = 4
ROPE_BASE = 10000.0
GDN_HEADS, GDN_DK, GDN_DV, GDN_CHUNK = 4, 128, 128, 64
GDN_STEP_CHUNKS = 4
GDN_QKV = 2 * GDN_HEADS * GDN_DK + GDN_HEADS * GDN_DV
CONV_W = 4
SSM_GROUP, SSM_GROUPS, SSM_P = 16, 32, 64
SSM_WIDTH = SSM_GROUP * SSM_GROUPS
SSM_STATE = SSM_GROUPS * SSM_P
LRU_WIDTH = 512
LRU_C = 8.0
N_BRANCH, BRANCH_W = 4, 512
D_FF = 2816
N_SUB = 3
LN_EPS = 1e-5
PAST_LEN = 16384

RET_QK = RET_HEADS * RET_DK
RET_V = RET_HEADS * RET_DV
RET_W = 2 * RET_QK + 2 * RET_V
GDN_K = GDN_HEADS * GDN_DK
QKT_ROWS = 2 * RET_QK + 2 * GDN_K
LANE = 128
SUBLANE = 8
FF_CHUNKS = ((0, 768), (768, 1024), (1792, 1024))
VMEM_LIMIT = 56 * 1024 * 1024


def _cparams(*sem):
    return pltpu.CompilerParams(dimension_semantics=sem, vmem_limit_bytes=VMEM_LIMIT)


def _layer_spec(a, layer):
    nd = a.ndim
    return pl.BlockSpec((None,) + a.shape[1:], lambda *_: (layer,) + (0,) * (nd - 1))


def _resident(shape, index_map):
    return pl.BlockSpec(shape, index_map, pipeline_mode=pl.Buffered(1))


def _ln(x):
    mu = jnp.mean(x, axis=-1, keepdims=True)
    xc = x - mu
    return xc * lax.rsqrt(jnp.mean(xc * xc, axis=-1, keepdims=True) + LN_EPS)


def _sigmoid(x):
    return 0.5 * jnp.tanh(0.5 * x) + 0.5


def _silu(x):
    return x * _sigmoid(x)


def _softplus(x):
    return jnp.maximum(x, 0.0) + jnp.log1p(jnp.exp(-jnp.abs(x)))


def _bdot(a, w):
    return jnp.dot(a.astype(BF16), w.astype(BF16), preferred_element_type=F32)


def _split(x):
    hi = x.astype(BF16)
    return hi, (x - hi.astype(F32)).astype(BF16)


def _dot_nt(a, b):
    return lax.dot_general(a, b, (((1,), (1,)), ((), ())), preferred_element_type=F32)


def _iota(shape, dim):
    return lax.broadcasted_iota(jnp.int32, shape, dim)


def _to_time_major(x):
    nb, tc, width = x.shape
    return jnp.swapaxes(x, 0, 1).reshape(tc * nb, width)


def _from_time_major(y, nb):
    rows, width = y.shape
    return jnp.swapaxes(y.reshape(rows // nb, nb, width), 0, 1)


def _ada_kernel(cp_ref, cs_ref, w_ref, b_ref, op_ref, os_ref):
    w = w_ref[...].astype(BF16)
    op_ref[...] = _bdot(_silu(cp_ref[...]), w) + b_ref[...]
    os_ref[...] = _bdot(_silu(cs_ref[...]), w) + b_ref[...]


def _ada(c_prompt, c_sample, w_ada, b_ada):
    depth, _, n_out = w_ada.shape
    tn = 1152
    cond = lambda c: pl.BlockSpec(c.shape, lambda l, j: (0, 0))
    out = lambda c: pl.BlockSpec((None, c.shape[0], tn), lambda l, j: (l, 0, j))
    return pl.pallas_call(
        _ada_kernel,
        grid=(depth, n_out // tn),
        in_specs=[
            cond(c_prompt), cond(c_sample),
            pl.BlockSpec((None, D_MODEL, tn), lambda l, j: (l, 0, j)),
            pl.BlockSpec((None, 1, tn), lambda l, j: (l, 0, j)),
        ],
        out_specs=[out(c_prompt), out(c_sample)],
        out_shape=[jax.ShapeDtypeStruct((depth, c.shape[0], n_out), F32) for c in (c_prompt, c_sample)],
        compiler_params=_cparams("parallel", "parallel"),
        name="ada",
    )(c_prompt, c_sample, w_ada, b_ada.reshape(depth, 1, n_out))


class _Mod:
    def __init__(self, arr, layer, per_row, rows_per_batch=None):
        self.arr, self.layer, self.per_row, self.rpb = arr, layer, per_row, rows_per_batch

    def spec(self, sub, which, tm):
        k, l = 3 * sub + which, self.layer
        if self.per_row:
            return pl.BlockSpec((None, tm, D_MODEL), lambda i: (l, i, k))
        tiles = self.rpb // tm
        return pl.BlockSpec((None, None, None, 1, D_MODEL), lambda i: (l, k, i // tiles, 0, 0))


def _row_parts(ref, parts):
    if ref.shape[0] == 1:
        return [ref[...]] * parts
    hm = ref.shape[0] // parts
    return [ref[k * hm:(k + 1) * hm, :] for k in range(parts)]


def _ffn_kernel(x_ref, sh_ref, sc_ref, gt_ref, wup_ref, wdn_ref, g_ref, b_ref, o_ref, *, alpha, parts):
    ks = range(parts)
    xs, shs, scs, gts = (_row_parts(r, parts) for r in (x_ref, sh_ref, sc_ref, gt_ref))
    hs = [(_ln(xs[k]) * (1.0 + scs[k]) + shs[k]).astype(BF16) for k in ks]
    accs = [jnp.zeros(xs[k].shape, F32) for k in ks]
    for lo, width in FF_CHUNKS:
        a = [jnp.dot(hs[k], wup_ref[:, lo:lo + width], preferred_element_type=F32) for k in ks]
        b = [jnp.dot(hs[k], wup_ref[:, D_FF + lo:D_FF + lo + width], preferred_element_type=F32) for k in ks]
        act = [(_silu(a[k]) * b[k]).astype(BF16) for k in ks]
        accs = [accs[k] + jnp.dot(act[k], wdn_ref[lo:lo + width, :], preferred_element_type=F32) for k in ks]
    hm = x_ref.shape[0] // parts
    for k in ks:
        z = alpha * xs[k] + 0.5 * gts[k] * accs[k]
        o_ref[k * hm:(k + 1) * hm, :] = _ln(z) * g_ref[...] + b_ref[...]


def _ffn(x, mod, sub, which, wup, wdn, ln_g, ln_b, alpha, tm):
    n = x.shape[0]
    l = mod.layer
    row = pl.BlockSpec((tm, D_MODEL), lambda i: (i, 0))
    return pl.pallas_call(
        functools.partial(_ffn_kernel, alpha=alpha, parts=max(1, tm // 256)),
        grid=(n // tm,),
        in_specs=[
            row, mod.spec(sub, 0, tm), mod.spec(sub, 1, tm), mod.spec(sub, 2, tm),
            _resident((None, None, D_MODEL, 2 * D_FF), lambda i: (l, which, 0, 0)),
            _resident((None, None, D_FF, D_MODEL), lambda i: (l, which, 0, 0)),
            pl.BlockSpec((None, None, 1, D_MODEL), lambda i: (l, sub, 0, 0)),
            pl.BlockSpec((None, None, 1, D_MODEL), lambda i: (l, sub, 0, 0)),
        ],
        out_specs=row,
        out_shape=jax.ShapeDtypeStruct((n, D_MODEL), F32),
        compiler_params=_cparams("parallel"),
        name="ffn",
    )(x, mod.arr, mod.arr, mod.arr, wup, wdn, ln_g, ln_b)


_IN_PIECES = (("ret", RET_W), ("gqkv", GDN_QKV), ("gab", LANE), ("gz", 512), ("su", SSM_WIDTH),
              ("lx", LRU_WIDTH), ("lg", LRU_WIDTH))
IN_MAIN = sum(w for _, w in _IN_PIECES)
MG_WIDTH = N_BRANCH * D_MODEL


PACK_ROWS = 512


def _pack_kernel(wt_ref, main_ref, mg_ref):
    gate_lo = RET_W + GDN_QKV
    gate_hi = gate_lo + 2 * GDN_HEADS
    mg_lo = wt_ref.shape[0] - MG_WIDTH

    def move(dst_ref, dst, src, width):
        for c in range(0, width, PACK_ROWS):
            dst_ref[:, dst + c:dst + c + PACK_ROWS] = wt_ref[src + c:src + c + PACK_ROWS, :].T.astype(BF16)

    move(main_ref, 0, 0, gate_lo)
    g = wt_ref[gate_lo:gate_lo + LANE, :].T
    main_ref[:, gate_lo:gate_lo + LANE] = jnp.where(_iota(g.shape, 1) < 2 * GDN_HEADS, g, 0.0).astype(BF16)
    move(main_ref, gate_lo + LANE, gate_hi, mg_lo - gate_hi)
    move(mg_ref, 0, mg_lo, MG_WIDTH)


def _pack_w_in(w_in):
    depth, d_in, d_out = w_in.shape
    tk = 256
    blk = lambda w: pl.BlockSpec((None, tk, w), lambda l, i: (l, i, 0))
    return pl.pallas_call(
        _pack_kernel,
        grid=(depth, d_in // tk),
        in_specs=[pl.BlockSpec((None, d_out, tk), lambda l, i: (l, 0, i))],
        out_specs=[blk(IN_MAIN), blk(MG_WIDTH)],
        out_shape=[jax.ShapeDtypeStruct((depth, d_in, IN_MAIN), BF16),
                   jax.ShapeDtypeStruct((depth, d_in, MG_WIDTH), BF16)],
        compiler_params=_cparams("parallel", "parallel"),
        name="pack_w_in",
    )(jnp.swapaxes(w_in, 1, 2))


def _l2n(x):
    return x * lax.rsqrt(jnp.sum(x * x, axis=-1, keepdims=True) + 1e-6)


def _gdn_activate(conv):
    cq = _silu(conv)
    q = [_l2n(cq[:, h * GDN_DK:(h + 1) * GDN_DK]) * (GDN_DK ** -0.5) for h in range(GDN_HEADS)]
    k = [_l2n(cq[:, GDN_K + h * GDN_DK:GDN_K + (h + 1) * GDN_DK]) for h in range(GDN_HEADS)]
    return jnp.concatenate(q + k + [cq[:, 2 * GDN_K:]], axis=1)


def _inproj_kernel(x_ref, sh_ref, sc_ref, w_ref, *refs, seq_tiles):
    parts = 2 if x_ref.shape[0] >= 512 else 1
    xs, shs, scs = (_row_parts(r, parts) for r in (x_ref, sh_ref, sc_ref))
    hs = [(_ln(xs[k]) * (1.0 + scs[k]) + shs[k]).astype(BF16) for k in range(parts)]
    n_out = len(_IN_PIECES)
    if seq_tiles is None:
        o_refs = refs
    else:
        cw_ref, o_refs, cv_ref, xbuf = refs[0], refs[1:1 + n_out], refs[1 + n_out], refs[2 + n_out]

        @pl.when(pl.program_id(0) % seq_tiles == 0)
        def _():
            xbuf[0:SUBLANE, :] = jnp.zeros((SUBLANE, GDN_QKV), F32)

    offsets = np.cumsum([0] + [w for _, w in _IN_PIECES])
    order = sorted(range(n_out), key=lambda p: _IN_PIECES[p][0] != "gqkv")
    for p in order:
        (name, width), o_ref, lo = _IN_PIECES[p], o_refs[p], int(offsets[p])
        y = jnp.concatenate([jnp.dot(h, w_ref[:, lo:lo + width], preferred_element_type=F32) for h in hs], axis=0)
        if name == "gqkv" and seq_tiles is not None:
            tm = y.shape[0]
            xbuf[SUBLANE:SUBLANE + tm, :] = y
            conv = cw_ref[CONV_W - 1:CONV_W, :] * y
            for j in range(CONV_W - 1):
                r0 = SUBLANE - (CONV_W - 1) + j
                conv = conv + cw_ref[j:j + 1, :] * xbuf[r0:r0 + tm, :]
            cv_ref[0] = xbuf[SUBLANE + tm - (CONV_W - 1):SUBLANE + tm, :]
            xbuf[0:SUBLANE, :] = y[tm - SUBLANE:, :]
            y = _gdn_activate(conv)
        o_ref[...] = y


def _inproj(x, mod, w_packed, tm, conv=None):
    n = x.shape[0]
    l = mod.layer
    widths = [w for _, w in _IN_PIECES]
    ins = [x, mod.arr, mod.arr, w_packed]
    in_specs = [pl.BlockSpec((tm, D_MODEL), lambda i: (i, 0)),
                mod.spec(1, 0, tm), mod.spec(1, 1, tm),
                _resident((None, D_MODEL, IN_MAIN), lambda i: (l, 0, 0))]
    out_specs = [pl.BlockSpec((tm, w), lambda i: (i, 0)) for w in widths]
    out_shape = [jax.ShapeDtypeStruct((n, w), F32) for w in widths]
    scratch, seq_tiles = [], None
    if conv is not None:
        conv_w, nb, t = conv
        seq_tiles = t // tm
        ins.append(conv_w)
        in_specs.append(_layer_spec(conv_w, l))
        out_specs.append(pl.BlockSpec((1, CONV_W - 1, GDN_QKV), lambda i: (i // seq_tiles, 0, 0)))
        out_shape.append(jax.ShapeDtypeStruct((nb, CONV_W - 1, GDN_QKV), F32))
        scratch.append(pltpu.VMEM((SUBLANE + tm, GDN_QKV), F32))
    outs = pl.pallas_call(
        functools.partial(_inproj_kernel, seq_tiles=seq_tiles),
        grid=(n // tm,),
        in_specs=in_specs,
        out_specs=out_specs,
        out_shape=out_shape,
        scratch_shapes=scratch,
        compiler_params=_cparams("arbitrary"),
        name="inproj",
    )(*ins)
    return dict(zip([nm for nm, _ in _IN_PIECES] + ["gdn_conv"], outs))


def _rope_tables(pos):
    half = RET_DK // 2
    inv = ROPE_BASE ** (-jnp.arange(half, dtype=F32) / half)
    ang = pos.astype(F32)[:, None] * inv[None, :]
    cos, sin = jnp.cos(ang), jnp.sin(ang)
    cos_t = jnp.tile(jnp.concatenate([cos, cos], axis=-1), (1, RET_HEADS))
    sin_t = jnp.tile(jnp.concatenate([-sin, sin], axis=-1), (1, RET_HEADS))
    return cos_t, sin_t


def _rotate(z, cos, sin):
    half = RET_DK // 2
    first = (_iota(z.shape, 1) % RET_DK) < half
    swapped = jnp.where(first, pltpu.roll(z, RET_QK - half, 1), pltpu.roll(z, half, 1))
    return z * cos + swapped * sin


def _ret_gammas():
    return [1.0 - 2.0 ** (-5.0 - h) for h in range(RET_HEADS)]


def _ret_consts(chunk, n_sub):
    log_g = np.log(np.array(_ret_gammas(), np.float64))
    idx = np.arange(chunk, dtype=np.float64)
    rel = idx[:, None] - idx[None, :]
    dmask = np.where(rel[None] >= 0, np.exp(log_g[:, None, None] * np.maximum(rel, 0.0)[None]), 0.0)
    qdec = np.repeat(np.exp(log_g[None, :] * (idx[:, None] + 1.0)), RET_DK, axis=1)
    kdec = np.repeat(np.exp(log_g[None, :] * (chunk - 1.0 - idx[:, None])), RET_DK, axis=1)
    qdec, kdec = np.tile(qdec, (n_sub, 1)), np.tile(kdec, (n_sub, 1))
    head_r = np.arange(RET_QK) // RET_DK
    head_c = np.arange(RET_V) // RET_DV
    bd = (head_r[:, None] == head_c[None, :]).astype(np.float64)
    cd = bd * np.exp(log_g * chunk)[head_r][:, None]
    f = lambda a: jnp.asarray(a, F32)
    return f(dmask), f(qdec), f(kdec), f(cd), f(bd)


def _ret_kernel(x_ref, cos_ref, sin_ref, dmask_ref, qdec_ref, kdec_ref, cd_ref, bd_ref,
                y_ref, st_ref, s_scr):
    c = pl.program_id(1)
    ch = RET_CHUNK
    n_sub = x_ref.shape[0] // ch

    @pl.when(c == 0)
    def _():
        s_scr[...] = jnp.zeros(s_scr.shape, F32)

    x = x_ref[...]
    q = _rotate(x[:, 0:RET_QK], cos_ref[...], sin_ref[...])
    k = _rotate(x[:, RET_QK:2 * RET_QK], cos_ref[...], sin_ref[...]) * (RET_DK ** -0.5)
    v = x[:, 2 * RET_QK:2 * RET_QK + RET_V].astype(BF16)
    gate = x[:, 2 * RET_QK + RET_V:]
    q_dec = (q * qdec_ref[...]).astype(BF16)
    k_dec = k * kdec_ref[...]
    kb = k.astype(BF16)
    head = _iota((ch, RET_QK), 1) // RET_DK
    intra, upd = {}, {}
    for n in range(n_sub):
        sl = slice(n * ch, (n + 1) * ch)
        q_heads = jnp.concatenate([jnp.where(head == h, q[sl], 0.0) for h in range(RET_HEADS)], axis=0)
        inner_all = _dot_nt(q_heads.astype(BF16), kb[sl])
        for h in range(RET_HEADS):
            inner = inner_all[h * ch:(h + 1) * ch] * dmask_ref[h]
            intra[n, h] = _bdot(inner, v[sl, h * RET_DV:(h + 1) * RET_DV])
        upd[n] = jnp.dot(k_dec[sl].T.astype(BF16), v[sl], preferred_element_type=F32) * bd_ref[...]
    s = s_scr[...]
    cross = {}
    for n in range(n_sub):
        cross[n] = jnp.dot(q_dec[n * ch:(n + 1) * ch], s.astype(BF16), preferred_element_type=F32)
        s = s * cd_ref[...] + upd[n]
    s_scr[...] = s
    for n in range(n_sub):
        sl = slice(n * ch, (n + 1) * ch)
        outs = [_ln(intra[n, h] + cross[n][:, h * RET_DV:(h + 1) * RET_DV]) for h in range(RET_HEADS)]
        y_ref[sl, :] = (jnp.concatenate(outs, axis=1) * _silu(gate[sl])).astype(y_ref.dtype)

    @pl.when(c == pl.num_programs(1) - 1)
    def _():
        for h in range(RET_HEADS):
            st_ref[0, h] = s[h * RET_DK:(h + 1) * RET_DK, h * RET_DV:(h + 1) * RET_DV]


def _retention_prompt(ret, nb, t):
    n_sub = RET_STEP_CHUNKS if t % (RET_STEP_CHUNKS * RET_CHUNK) == 0 else 1
    chunk = n_sub * RET_CHUNK
    nc = t // chunk
    cos_t, sin_t = _rope_tables(jnp.arange(t))
    consts = _ret_consts(RET_CHUNK, n_sub)
    full = lambda a: pl.BlockSpec(a.shape, lambda b, c: (0,) * a.ndim)
    return pl.pallas_call(
        _ret_kernel,
        grid=(nb, nc),
        in_specs=[
            pl.BlockSpec((chunk, RET_W), lambda b, c: (b * nc + c, 0)),
            pl.BlockSpec((chunk, RET_QK), lambda b, c: (c, 0)),
            pl.BlockSpec((chunk, RET_QK), lambda b, c: (c, 0)),
        ] + [full(a) for a in consts],
        out_specs=[
            pl.BlockSpec((chunk, RET_V), lambda b, c: (b * nc + c, 0)),
            pl.BlockSpec((1, RET_HEADS, RET_DK, RET_DV), lambda b, c: (b, 0, 0, 0)),
        ],
        out_shape=[
            jax.ShapeDtypeStruct((nb * t, RET_V), BF16),
            jax.ShapeDtypeStruct((nb, RET_HEADS, RET_DK, RET_DV), F32),
        ],
        scratch_shapes=[pltpu.VMEM((RET_QK, RET_V), F32)],
        compiler_params=_cparams("parallel", "arbitrary"),
        name="retention",
    )(ret, cos_t, sin_t, *consts)


def _rms(x):
    return x * lax.rsqrt(jnp.mean(x * x, axis=-1, keepdims=True) + 1e-6)


def _gate_lanes(gab, alog_row, dtb_row):
    g = -jnp.exp(alog_row) * _softplus(gab + dtb_row)
    return jnp.where(_iota(gab.shape, 1) < GDN_HEADS, g, _sigmoid(gab))


def _gdn_kernel(x_ref, gab_ref, gz_ref, alog_ref, dtb_ref, nw_ref, y_ref, st_ref, s_scr):
    c = pl.program_id(1)
    ch = GDN_CHUNK

    @pl.when(c == 0)
    def _():
        s_scr[...] = jnp.zeros(s_scr.shape, F32)

    rows = x_ref.shape[0]
    cq = x_ref[...]

    gl = _gate_lanes(gab_ref[...], alog_ref[...], dtb_ref[...])
    col = _iota((ch, 2 * ch), 1) % ch
    tril = _iota((ch, 2 * ch), 0) >= col
    strict = _iota((ch, 2 * ch), 0) > col
    is_g = _iota(gl.shape, 1) < GDN_HEADS
    r_i, c_i = _iota((rows, rows), 0), _iota((rows, rows), 1)
    chunk_tril = ((r_i >= c_i) & (r_i // ch == c_i // ch)).astype(F32)
    csum = jnp.dot(chunk_tril, jnp.where(is_g, gl, 0.0), preferred_element_type=F32,
                   precision=lax.Precision.HIGHEST)
    gsel = jnp.where(is_g, csum, gl)
    n_sub = rows // ch
    heads = range(GDN_HEADS)
    probs = [(h, n) for h in heads for n in range(n_sub)]

    q_all, k_all, g_all, qe_all, rhs_all, kbeta_all = {}, {}, {}, {}, {}, {}
    for h in heads:
        q_all[h] = cq[:, h * GDN_DK:(h + 1) * GDN_DK]
        k_all[h] = cq[:, GDN_K + h * GDN_DK:GDN_K + (h + 1) * GDN_DK]
        v_h = cq[:, 2 * GDN_K + h * GDN_DV:2 * GDN_K + (h + 1) * GDN_DV]
        g_all[h] = jnp.broadcast_to(gsel[:, h:h + 1], (rows, LANE))
        b_h = jnp.broadcast_to(gsel[:, GDN_HEADS + h:GDN_HEADS + h + 1], (rows, LANE))
        e_h = jnp.exp(g_all[h])
        kbeta_all[h] = k_all[h] * b_h
        rhs_all[h] = jnp.concatenate([v_h * b_h, kbeta_all[h] * e_h], axis=1)
        qe_all[h] = q_all[h] * e_h

    decay, pp, uw, attn, kd_t, e_last = {}, {}, {}, {}, {}, {}
    for h, n in probs:
        sl = slice(n * ch, (n + 1) * ch)
        g_b = g_all[h][sl]
        g_cols = jnp.concatenate([g_b, g_b], axis=0).T[:ch, :]
        diff = g_b - g_cols
        decay[h, n] = jnp.where(tril, jnp.exp(jnp.where(tril, diff, 0.0)), 0.0)
        g_last = g_b[ch - 1:ch, :]
        e_last[h, n] = jnp.exp(g_last)
        kd_t[h, n] = (k_all[h][sl] * jnp.exp(g_last - g_b)).T.astype(BF16)
    for h, n in probs:
        sl = slice(n * ch, (n + 1) * ch)
        kb_hi, kb_lo = _split(kbeta_all[h][sl])
        k_hi, k_lo = _split(jnp.concatenate([k_all[h][sl], k_all[h][sl]], axis=0))
        kk = _dot_nt(jnp.concatenate([kb_hi, kb_hi, kb_lo, kb_lo], axis=1),
                     jnp.concatenate([k_hi, k_lo, k_hi, k_lo], axis=1))
        pp[h, n] = -jnp.where(strict, kk * decay[h, n], 0.0)
        attn[h, n] = _dot_nt(q_all[h][sl].astype(BF16), k_all[h][sl].astype(BF16)) * decay[h, n][:, :ch]
    for stage in range(int(math.log2(ch))):
        for h, n in probs:
            p_hi, p_lo = _split(pp[h, n])
            lhs = jnp.concatenate([p_hi, p_lo], axis=1)
            y = rhs_all[h][n * ch:(n + 1) * ch] if stage == 0 else uw[h, n]
            y_hi, y_lo = _split(y)
            uw[h, n] = y + jnp.dot(lhs, jnp.concatenate([y_hi, y_lo, y_hi, y_lo], axis=0),
                                   preferred_element_type=F32)
            if stage + 1 < int(math.log2(ch)):
                pp[h, n] = jnp.dot(lhs, jnp.concatenate([p_hi, p_lo, p_hi, p_lo], axis=0),
                                   preferred_element_type=F32)
    s = {h: s_scr[h] for h in heads}
    o_parts = {h: [] for h in heads}
    for n in range(n_sub):
        for h in heads:
            sl = slice(n * ch, (n + 1) * ch)
            u, w = uw[h, n][:, :GDN_DV], uw[h, n][:, GDN_DV:]
            ws_qs = _bdot(jnp.concatenate([w, qe_all[h][sl]], axis=0), s[h])
            v_new = u - ws_qs[:ch]
            o_parts[h].append(ws_qs[ch:] + _bdot(attn[h, n], v_new))
            s[h] = s[h] * e_last[h, n] + jnp.dot(kd_t[h, n], v_new.astype(BF16), preferred_element_type=F32)
    norm_w = nw_ref[...]
    gz = gz_ref[...]
    outs = []
    for h in heads:
        s_scr[h] = s[h]
        o = jnp.concatenate(o_parts[h], axis=0)
        outs.append(_rms(o) * norm_w * _silu(gz[:, h * GDN_DV:(h + 1) * GDN_DV]))
    y_ref[...] = jnp.concatenate(outs, axis=1).astype(y_ref.dtype)

    @pl.when(c == pl.num_programs(1) - 1)
    def _():
        st_ref[0] = s_scr[...]


def _gdn_prompt(gqkv, gab, gz, alog_row, dtb_row, norm_w, layer, nb, t):
    ch = GDN_STEP_CHUNKS * GDN_CHUNK if t % (GDN_STEP_CHUNKS * GDN_CHUNK) == 0 else GDN_CHUNK
    nc = t // ch
    row = lambda w: pl.BlockSpec((ch, w), lambda b, c: (b * nc + c, 0))
    return pl.pallas_call(
        _gdn_kernel,
        grid=(nb, nc),
        in_specs=[row(GDN_QKV), row(LANE), row(GDN_HEADS * GDN_DV)]
        + [_layer_spec(a, layer) for a in (alog_row, dtb_row, norm_w)],
        out_specs=[
            row(GDN_HEADS * GDN_DV),
            pl.BlockSpec((1, GDN_HEADS, GDN_DK, GDN_DV), lambda b, c: (b, 0, 0, 0)),
        ],
        out_shape=[
            jax.ShapeDtypeStruct((nb * t, GDN_HEADS * GDN_DV), BF16),
            jax.ShapeDtypeStruct((nb, GDN_HEADS, GDN_DK, GDN_DV), F32),
        ],
        scratch_shapes=[pltpu.VMEM((GDN_HEADS, GDN_DK, GDN_DV), F32)],
        compiler_params=_cparams("parallel", "arbitrary"),
        name="gdn",
    )(gqkv, gab, gz, alog_row, dtb_row, norm_w)


SSM_BLK = 4
SSM_BLK_STATE = SSM_STATE // SSM_BLK


def _s5_params(lam_re, lam_im, log_step, b_re, b_im, c_re, c_im):
    step = jnp.exp(log_step.astype(F32))[:, None]
    mag = jnp.exp(lam_re * step)
    ab_re = mag * jnp.cos(lam_im * step)
    ab_im = mag * jnp.sin(lam_im * step)
    den = lam_re * lam_re + lam_im * lam_im
    nr = ab_re - 1.0
    f_re = (nr * lam_re + ab_im * lam_im) / den
    f_im = (ab_im * lam_re - nr * lam_im) / den
    bb_re = f_re[..., None] * b_re - f_im[..., None] * b_im
    bb_im = f_re[..., None] * b_im + f_im[..., None] * b_re
    gpb = SSM_GROUPS // SSM_BLK
    eye = jnp.eye(gpb, dtype=F32)

    def in_mat(bb):
        bb = bb.reshape(SSM_BLK, gpb, SSM_P, SSM_GROUP)
        m = jnp.einsum("jgpc,gk->jgckp", bb, eye)
        return m.reshape(SSM_BLK, gpb * SSM_GROUP, gpb * SSM_P).astype(BF16)

    def out_mat(cc):
        cc = cc.reshape(SSM_BLK, gpb, SSM_GROUP, SSM_P)
        m = jnp.einsum("jgcp,gk->jgpkc", cc, eye)
        return m.reshape(SSM_BLK, gpb * SSM_P, gpb * SSM_GROUP).astype(BF16)

    return (ab_re.reshape(1, SSM_STATE), ab_im.reshape(1, SSM_STATE),
            in_mat(bb_re), in_mat(bb_im), out_mat(c_re), out_mat(c_im))


def _s5_readout(u, h_re, h_im, cre_ref, cim_ref, d_row, gw_ref, gb_row):
    ys = []
    for j in range(SSM_BLK):
        sl = slice(j * SSM_BLK_STATE, (j + 1) * SSM_BLK_STATE)
        ys.append(_bdot(h_re[:, sl], cre_ref[j]) - _bdot(h_im[:, sl], cim_ref[j]))
    y = jax.nn.gelu(jnp.concatenate(ys, axis=1) + d_row * u)
    return y * _sigmoid(_bdot(y, gw_ref[...]) + gb_row)


def _scan_mixers_kernel(u_ref, x_ref, g_ref,
                        are_ref, aim_ref, bre_ref, bim_ref, cre_ref, cim_ref, d_ref, gw_ref, gb_ref,
                        cw_ref, cb_ref, wa_ref, wx_ref, ba_ref, bx_ref, lam_ref,
                        y_ref, hre_ref, him_ref, yd_ref, h_out, cv_out,
                        bu_re, bu_im, h_re, h_im, xbuf, a_buf, b_buf, h_scr, *, nb):
    i = pl.program_id(0)
    tc = u_ref.shape[1]
    rows = tc * nb
    tail = (CONV_W - 1) * nb

    @pl.when(i == 0)
    def _():
        h_re[...] = jnp.zeros(h_re.shape, F32)
        h_im[...] = jnp.zeros(h_im.shape, F32)
        xbuf[0:tail, :] = jnp.zeros((tail, LRU_WIDTH), F32)
        h_scr[...] = jnp.zeros(h_scr.shape, F32)

    u = _to_time_major(u_ref[...])
    for j in range(SSM_BLK):
        uj = u[:, j * LANE:(j + 1) * LANE]
        sl = slice(j * SSM_BLK_STATE, (j + 1) * SSM_BLK_STATE)
        bu_re[:, sl] = _bdot(uj, bre_ref[j])
        bu_im[:, sl] = _bdot(uj, bim_ref[j])
    x = _to_time_major(x_ref[...])
    xbuf[tail:tail + rows, :] = x
    conv = cw_ref[CONV_W - 1:CONV_W, :] * x
    for j in range(CONV_W - 1):
        conv = conv + cw_ref[j:j + 1, :] * xbuf[j * nb:j * nb + rows, :]
    cv_out[...] = xbuf[rows:rows + tail, :].reshape(CONV_W - 1, nb, LRU_WIDTH)
    xbuf[0:tail, :] = x[rows - tail:, :]
    a, b = _lru_gates(conv + cb_ref[...], wa_ref, wx_ref, ba_ref[...], bx_ref[...], lam_ref[...])
    a_buf[...] = a
    b_buf[...] = b

    for j in range(SSM_BLK):
        sl = slice(j * SSM_BLK_STATE, (j + 1) * SSM_BLK_STATE)
        a_re = jnp.broadcast_to(are_ref[:, sl], (nb, SSM_BLK_STATE))
        a_im = jnp.broadcast_to(aim_ref[:, sl], (nb, SSM_BLK_STATE))

        def body(t, carry):
            hr, hi = carry
            r = pl.ds(pl.multiple_of(t * nb, nb), nb)
            nr = a_re * hr - a_im * hi + bu_re[r, sl]
            ni = a_re * hi + a_im * hr + bu_im[r, sl]
            bu_re[r, sl] = nr
            bu_im[r, sl] = ni
            return nr, ni

        hr, hi = lax.fori_loop(0, tc, body, (h_re[:, sl], h_im[:, sl]), unroll=4)
        h_re[:, sl] = hr
        h_im[:, sl] = hi

    def lru_body(t, h):
        r = pl.ds(pl.multiple_of(t * nb, nb), nb)
        h = a_buf[r, :] * h + b_buf[r, :]
        b_buf[r, :] = h
        return h

    h = lax.fori_loop(0, tc, lru_body, h_scr[...], unroll=8)
    h_scr[...] = h
    h_out[...] = h

    y = _s5_readout(u, bu_re[...], bu_im[...], cre_ref, cim_ref, d_ref[...], gw_ref, gb_ref[...])
    y_ref[...] = _from_time_major(y, nb).astype(y_ref.dtype)
    hre_ref[...] = h_re[...]
    him_ref[...] = h_im[...]
    yd = b_buf[...] * jax.nn.gelu(_to_time_major(g_ref[...]))
    yd_ref[...] = _from_time_major(yd, nb).astype(yd_ref.dtype)


def _scan_mixers(u_t, x_t, g_t, s5_w, lru_w, layer, tc):
    nb, t, _ = u_t.shape
    rows = tc * nb
    blk = pl.BlockSpec((nb, tc, SSM_WIDTH), lambda i: (0, i, 0))
    st = pl.BlockSpec((nb, SSM_STATE), lambda i: (0, 0))
    sds = lambda *s: jax.ShapeDtypeStruct(s, F32)
    return pl.pallas_call(
        functools.partial(_scan_mixers_kernel, nb=nb),
        grid=(t // tc,),
        in_specs=[blk, blk, blk] + [_layer_spec(a, layer) for a in (*s5_w, *lru_w)],
        out_specs=[blk, st, st, blk,
                   pl.BlockSpec((nb, LRU_WIDTH), lambda i: (0, 0)),
                   pl.BlockSpec((CONV_W - 1, nb, LRU_WIDTH), lambda i: (0, 0, 0))],
        out_shape=[jax.ShapeDtypeStruct((nb, t, SSM_WIDTH), BF16), sds(nb, SSM_STATE), sds(nb, SSM_STATE),
                   jax.ShapeDtypeStruct((nb, t, LRU_WIDTH), BF16), sds(nb, LRU_WIDTH),
                   sds(CONV_W - 1, nb, LRU_WIDTH)],
        scratch_shapes=[pltpu.VMEM((rows, SSM_STATE), F32), pltpu.VMEM((rows, SSM_STATE), F32),
                        pltpu.VMEM((nb, SSM_STATE), F32), pltpu.VMEM((nb, SSM_STATE), F32),
                        pltpu.VMEM(((CONV_W - 1) * nb + rows, LRU_WIDTH), F32),
                        pltpu.VMEM((rows, LRU_WIDTH), F32), pltpu.VMEM((rows, LRU_WIDTH), F32),
                        pltpu.VMEM((nb, LRU_WIDTH), F32)],
        compiler_params=_cparams("arbitrary"),
        name="scan_mixers",
    )(u_t, x_t, g_t, *s5_w, *lru_w)


def _lru_gates(cx, wa_ref, wx_ref, ba_row, bx_row, lam_row):
    r = _sigmoid(_bdot(cx, wa_ref[...]) + ba_row)
    i = _sigmoid(_bdot(cx, wx_ref[...]) + bx_row)
    log_a = -LRU_C * r * _softplus(-lam_row)
    a = jnp.exp(log_a)
    th = jnp.tanh(log_a)
    b = jnp.sqrt(-2.0 * th / (1.0 - th)) * (i * cx)
    return a, b


def _merge_kernel(x_ref, sh_ref, sc_ref, gt_ref, ya_ref, yb_ref, yc_ref, yd_ref, wmg_ref, wb_ref, wo_ref,
                  g_ref, b_ref, o_ref, *, alpha, parts):
    ks = range(parts)
    xs, shs, scs, gts = (_row_parts(r, parts) for r in (x_ref, sh_ref, sc_ref, gt_ref))
    hs = [(_ln(xs[k]) * (1.0 + scs[k]) + shs[k]).astype(BF16) for k in ks]
    accs = [jnp.zeros(xs[k].shape, F32) for k in ks]
    for n, y_ref in enumerate((ya_ref, yb_ref, yc_ref, yd_ref)):
        ys = _row_parts(y_ref, parts)
        logits = [jnp.dot(hs[k], wmg_ref[:, n * D_MODEL:(n + 1) * D_MODEL], preferred_element_type=F32)
                  for k in ks]
        accs = [accs[k] + _sigmoid(logits[k]) * _bdot(ys[k], wb_ref[n]) for k in ks]
    outs = [_bdot(accs[k], wo_ref[...]) for k in ks]
    hm = x_ref.shape[0] // parts
    for k in ks:
        z = alpha * xs[k] + gts[k] * outs[k]
        o_ref[k * hm:(k + 1) * hm, :] = _ln(z) * g_ref[...] + b_ref[...]


def _merge(x, mod, ya, yb, yc, yd, w_mg, w_branch, w_out, ln_g, ln_b, alpha, tm):
    n = x.shape[0]
    l = mod.layer
    row = lambda w: pl.BlockSpec((tm, w), lambda i: (i, 0))
    return pl.pallas_call(
        functools.partial(_merge_kernel, alpha=alpha, parts=max(1, tm // 256)),
        grid=(n // tm,),
        in_specs=[
            row(D_MODEL), mod.spec(1, 0, tm), mod.spec(1, 1, tm), mod.spec(1, 2, tm),
            row(BRANCH_W), row(BRANCH_W), row(BRANCH_W), row(BRANCH_W),
            _resident((None, D_MODEL, MG_WIDTH), lambda i: (l, 0, 0)),
            _resident((None, N_BRANCH, BRANCH_W, D_MODEL), lambda i: (l, 0, 0, 0)),
            _resident((None, D_MODEL, D_MODEL), lambda i: (l, 0, 0)),
            pl.BlockSpec((None, None, 1, D_MODEL), lambda i: (l, 1, 0, 0)),
            pl.BlockSpec((None, None, 1, D_MODEL), lambda i: (l, 1, 0, 0)),
        ],
        out_specs=row(D_MODEL),
        out_shape=jax.ShapeDtypeStruct((n, D_MODEL), F32),
        compiler_params=_cparams("parallel"),
        name="merge",
    )(x, mod.arr, mod.arr, mod.arr, ya, yb, yc, yd, w_mg, w_branch, w_out, ln_g, ln_b)


def _smix1_kernel(ret_ref, gqkv_ref, gab_ref, su_ref, lx_ref, cos_ref, sin_ref,
                  gcv_ref, sre_ref, sim_ref, lru_ref, lcv_ref,
                  gcw_ref, alog_ref, dtb_ref,
                  are_ref, aim_ref, bre_ref, bim_ref, cre_ref, cim_ref, d_ref, gw_ref, gb_ref,
                  lcw_ref, lcb_ref, wa_ref, wx_ref, ba_ref, bx_ref, lam_ref,
                  qkt_ref, gv_ref, gx_ref, yc_ref, lh_ref, gcv_out, sre_out, sim_out, lcv_out):
    ret = ret_ref[...]
    rq = _rotate(ret[:, 0:RET_QK], cos_ref[...], sin_ref[...])
    rk = _rotate(ret[:, RET_QK:2 * RET_QK], cos_ref[...], sin_ref[...]) * (RET_DK ** -0.5)

    x = gqkv_ref[...]
    conv = gcw_ref[3:4, :] * x
    for j in range(CONV_W - 1):
        conv = conv + gcw_ref[j:j + 1, :] * gcv_ref[j]
    gcv_out[0] = gcv_ref[1]
    gcv_out[1] = gcv_ref[2]
    gcv_out[2] = x
    cq = _gdn_activate(conv)
    gq, gk = cq[:, :GDN_K], cq[:, GDN_K:2 * GDN_K]
    gv_ref[...] = cq[:, 2 * GDN_K:]
    gl = _gate_lanes(gab_ref[...], alog_ref[...], dtb_ref[...])
    gx = jnp.concatenate([jnp.broadcast_to(gl[:, r:r + 1], (gl.shape[0], LANE))
                          for r in range(2 * GDN_HEADS)], axis=1)
    gx_ref[...] = jnp.where(_iota(gx.shape, 1) < GDN_HEADS * LANE, jnp.exp(gx), gx)
    qkt_ref[...] = jnp.concatenate([rq, rk, gq, gk], axis=1).T

    u = su_ref[...]
    h_re, h_im = [], []
    for j in range(SSM_BLK):
        sl = slice(j * SSM_BLK_STATE, (j + 1) * SSM_BLK_STATE)
        uj = u[:, j * LANE:(j + 1) * LANE]
        a_re, a_im = are_ref[:, sl], aim_ref[:, sl]
        p_re, p_im = sre_ref[:, sl], sim_ref[:, sl]
        h_re.append(a_re * p_re - a_im * p_im + _bdot(uj, bre_ref[j]))
        h_im.append(a_re * p_im + a_im * p_re + _bdot(uj, bim_ref[j]))
    h_re = jnp.concatenate(h_re, axis=1)
    h_im = jnp.concatenate(h_im, axis=1)
    sre_out[...] = h_re
    sim_out[...] = h_im
    yc_ref[...] = _s5_readout(u, h_re, h_im, cre_ref, cim_ref, d_ref[...], gw_ref, gb_ref[...])

    lx = lx_ref[...]
    conv = lcw_ref[3:4, :] * lx
    for j in range(CONV_W - 1):
        conv = conv + lcw_ref[j:j + 1, :] * lcv_ref[j]
    lcv_out[0] = lcv_ref[1]
    lcv_out[1] = lcv_ref[2]
    lcv_out[2] = lx
    a, b = _lru_gates(conv + lcb_ref[...], wa_ref, wx_ref, ba_ref[...], bx_ref[...], lam_ref[...])
    lh_ref[...] = a * lru_ref[...] + b


def _smix1(pieces, cos_row, sin_row, states, gdn_w, s5_w, lru_w, layer):
    rows = pieces["ret"].shape[0]
    acts = [pieces["ret"], pieces["gqkv"], pieces["gab"], pieces["su"], pieces["lx"], cos_row, sin_row]
    stacked = [*states, *gdn_w, *s5_w, *lru_w]
    ins = acts + stacked
    full = lambda a: pl.BlockSpec(a.shape, lambda i: (0,) * a.ndim)
    sds = lambda *s: jax.ShapeDtypeStruct(s, F32)
    out_shape = [sds(QKT_ROWS, rows), sds(rows, GDN_HEADS * GDN_DV), sds(rows, 8 * LANE),
                 sds(rows, SSM_WIDTH), sds(rows, LRU_WIDTH),
                 sds(CONV_W - 1, rows, GDN_QKV), sds(rows, SSM_STATE), sds(rows, SSM_STATE),
                 sds(CONV_W - 1, rows, LRU_WIDTH)]
    return pl.pallas_call(
        _smix1_kernel,
        grid=(1,),
        in_specs=[full(a) for a in acts] + [_layer_spec(a, layer) for a in stacked],
        out_specs=[pl.BlockSpec(s.shape, lambda i, nd=len(s.shape): (0,) * nd) for s in out_shape],
        out_shape=out_shape,
        compiler_params=_cparams("arbitrary"),
        name="sample_mix",
    )(*ins)


def _smix2_kernel(qkt_ref, rv_ref, gv_ref, gx_ref, rg_ref, gz_ref, lh_ref, lg_ref, nw_ref, sret_ref, sgdn_ref,
                  *refs, first):
    ya_ref, yb_ref, yd_ref, nret_ref, ngdn_ref, o_ret, o_gdn = refs[-7:]
    if first:
        for l in range(1, nret_ref.shape[0]):
            nret_ref[l] = jnp.zeros(nret_ref.shape[1:], F32)
            ngdn_ref[l] = jnp.zeros(ngdn_ref.shape[1:], F32)
        nret_ref, ngdn_ref = nret_ref.at[0], ngdn_ref.at[0]
    i = pl.program_id(0)
    bt = rv_ref.shape[0]
    rows = qkt_ref.shape[1]
    gammas = _ret_gammas()
    q_hi, q_lo = _split(qkt_ref[...])
    qk_split = jnp.concatenate([q_hi, q_lo], axis=1)

    for j in range(bt):
        b = i * bt + j
        onehot = jnp.where(_iota((2 * rows, LANE), 0) % rows == b, 1.0, 0.0).astype(BF16)
        cols = jnp.dot(qk_split, onehot, preferred_element_type=F32)
        r = slice(j, j + 1)
        for h in range(RET_HEADS):
            q_c = cols[h * RET_DK:(h + 1) * RET_DK]
            k_c = cols[RET_QK + h * RET_DK:RET_QK + (h + 1) * RET_DK]
            v_r = rv_ref[r, h * RET_DV:(h + 1) * RET_DV]
            s_new = gammas[h] * sret_ref[j, h] + k_c * v_r
            nret_ref[j, h] = s_new
            o_ret[r, h * RET_DV:(h + 1) * RET_DV] = jnp.sum(q_c * s_new, axis=0, keepdims=True)
        base = 2 * RET_QK
        for h in range(GDN_HEADS):
            q_c = cols[base + h * GDN_DK:base + (h + 1) * GDN_DK]
            k_c = cols[base + GDN_K + h * GDN_DK:base + GDN_K + (h + 1) * GDN_DK]
            v_r = gv_ref[r, h * GDN_DV:(h + 1) * GDN_DV]
            e_g = gx_ref[r, h * LANE:(h + 1) * LANE]
            beta = gx_ref[r, (GDN_HEADS + h) * LANE:(GDN_HEADS + h + 1) * LANE]
            s = sgdn_ref[j, h]
            v_new = beta * (v_r - e_g * jnp.sum(k_c * s, axis=0, keepdims=True))
            s_new = e_g * s + k_c * v_new
            ngdn_ref[j, h] = s_new
            o_gdn[r, h * GDN_DV:(h + 1) * GDN_DV] = jnp.sum(q_c * s_new, axis=0, keepdims=True)
    rg, gz = rg_ref[...], gz_ref[...]
    o_r, o_g = o_ret[...], o_gdn[...]
    ya_ref[...] = jnp.concatenate(
        [_ln(o_r[:, h * RET_DV:(h + 1) * RET_DV]) for h in range(RET_HEADS)], axis=1) * _silu(rg)
    yb_ref[...] = jnp.concatenate(
        [_rms(o_g[:, h * GDN_DV:(h + 1) * GDN_DV]) * nw_ref[...] for h in range(GDN_HEADS)], axis=1) * _silu(gz)
    yd_ref[...] = lh_ref[...] * jax.nn.gelu(lg_ref[...])


def _smix2(qkt, rv, gv, gx, rg, gz, lh, lg, norm_w, s_ret, s_gdn, layer, bt, prev):
    rows = rv.shape[0]
    row = lambda w: pl.BlockSpec((bt, w), lambda i: (i, 0))
    sds = lambda *s: jax.ShapeDtypeStruct(s, F32)
    ret_blk = pl.BlockSpec((None, bt, RET_HEADS, RET_DK, RET_DV), lambda i: (layer, i, 0, 0, 0))
    gdn_blk = pl.BlockSpec((None, bt, GDN_HEADS, GDN_DK, GDN_DV), lambda i: (layer, i, 0, 0, 0))
    ins = [qkt, rv, gv, gx, rg, gz, lh, lg, norm_w, s_ret, s_gdn]
    in_specs = [pl.BlockSpec(qkt.shape, lambda i: (0, 0)),
                pl.BlockSpec((bt, RET_V), lambda i: (i, 1)),
                row(GDN_HEADS * GDN_DV), row(8 * LANE),
                pl.BlockSpec((bt, RET_V), lambda i: (i, 2)),
                row(GDN_HEADS * GDN_DV), row(LRU_WIDTH), row(LRU_WIDTH),
                _layer_spec(norm_w, layer), ret_blk, gdn_blk]
    aliases = {}
    if prev is None:
        depth = s_ret.shape[0]
        out_ret = pl.BlockSpec((depth, bt, RET_HEADS, RET_DK, RET_DV), lambda i: (0, i, 0, 0, 0))
        out_gdn = pl.BlockSpec((depth, bt, GDN_HEADS, GDN_DK, GDN_DV), lambda i: (0, i, 0, 0, 0))
    else:
        out_ret, out_gdn = ret_blk, gdn_blk
        aliases = {len(ins): 3, len(ins) + 1: 4}
        ins += list(prev)
        in_specs += [pl.BlockSpec(memory_space=pl.ANY)] * 2
    return pl.pallas_call(
        functools.partial(_smix2_kernel, first=prev is None),
        grid=(rows // bt,),
        in_specs=in_specs,
        out_specs=[row(RET_V), row(GDN_HEADS * GDN_DV), row(LRU_WIDTH), out_ret, out_gdn],
        out_shape=[sds(rows, RET_V), sds(rows, GDN_HEADS * GDN_DV), sds(rows, LRU_WIDTH),
                   sds(*s_ret.shape), sds(*s_gdn.shape)],
        input_output_aliases=aliases,
        scratch_shapes=[pltpu.VMEM((bt, RET_V), F32), pltpu.VMEM((bt, GDN_HEADS * GDN_DV), F32)],
        compiler_params=_cparams("parallel"),
        name="sample_state",
    )(*ins)


def _block_diag(w):
    nb, bs, _ = w.shape
    return jnp.einsum("nij,nm->nimj", w, jnp.eye(nb, dtype=w.dtype)).reshape(nb * bs, nb * bs)


def kernel(x_prompt, x_sample, c_prompt, c_sample, state_ret, state_gdn, state_gdn_conv, state_ssm_re, state_ssm_im, state_lru, state_lru_conv, w_ada, b_ada, ln_g, ln_b, w_ffn_up, w_ffn_down, w_in, gdn_conv_w, gdn_a_log, gdn_dt_bias, gdn_norm_w, ssm_lam_re, ssm_lam_im, ssm_log_step, ssm_b_re, ssm_b_im, ssm_c_re, ssm_c_im, ssm_d, ssm_glu_w, ssm_glu_b, lru_conv_w, lru_conv_b, lru_wa, lru_ba, lru_wx, lru_bx, lru_lam, w_branch, w_out):
    nb, t, _ = x_prompt.shape
    ns = x_sample.shape[0]
    depth = w_ada.shape[0]
    assert t % RET_CHUNK == 0 and x_sample.shape[1] == 1
    alpha = (2 * depth) ** 0.25
    tm = 256 if t % 256 == 0 else RET_CHUNK
    tm_big = 512 if t % 512 == 0 else tm
    tm_ffn = 1024 if t % 1024 == 0 else tm_big
    tc = 128 if t % 128 == 0 else 64

    wup = w_ffn_up.astype(BF16)
    wdn = w_ffn_down.astype(BF16)
    wbr = w_branch.astype(BF16)
    wout = w_out.astype(BF16)
    w_packed, w_mg = _pack_w_in(w_in)
    ln_g4 = ln_g.reshape(depth, N_SUB, 1, D_MODEL)
    ln_b4 = ln_b.reshape(depth, N_SUB, 1, D_MODEL)

    mod_p, mod_s = _ada(c_prompt, c_sample, w_ada, b_ada)
    mod_p = mod_p.reshape(depth, nb, 3 * N_SUB, 1, D_MODEL).transpose(0, 2, 1, 3, 4)

    row3 = lambda a: a.reshape(depth, 1, -1)
    lane_rows = lambda v: jnp.pad(v, ((0, 0), (0, LANE - v.shape[1]))).reshape(depth, 1, LANE)
    alog_rows, dtb_rows = lane_rows(gdn_a_log), lane_rows(gdn_dt_bias)
    norm_w = row3(gdn_norm_w)
    s5_w = (*jax.vmap(_s5_params)(ssm_lam_re, ssm_lam_im, ssm_log_step, ssm_b_re, ssm_b_im, ssm_c_re, ssm_c_im),
            row3(ssm_d), ssm_glu_w.astype(BF16), row3(ssm_glu_b))
    lru_w = (lru_conv_w, row3(lru_conv_b),
             jax.vmap(_block_diag)(lru_wa).astype(BF16), jax.vmap(_block_diag)(lru_wx).astype(BF16),
             row3(lru_ba), row3(lru_bx), row3(lru_lam))
    s_states = (state_gdn_conv.transpose(0, 2, 1, 3),
                state_ssm_re.reshape(depth, ns, SSM_STATE), state_ssm_im.reshape(depth, ns, SSM_STATE),
                state_lru, state_lru_conv.transpose(0, 2, 1, 3))

    cos_s, sin_s = _rope_tables(jnp.full((1,), PAST_LEN))
    xp = x_prompt.reshape(nb * t, D_MODEL)
    xs = x_sample.reshape(ns, D_MODEL)
    new_p, new_s = [], []
    big_s = None
    for l in range(depth):
        mp = _Mod(mod_p, l, False, t)
        ms = _Mod(mod_s, l, True)

        xp = _ffn(xp, mp, 0, 0, wup, wdn, ln_g4, ln_b4, alpha, tm_ffn)
        pc = _inproj(xp, mp, w_packed, tm_big, conv=(gdn_conv_w, nb, t))
        gcv_p = pc["gdn_conv"]
        ya, ret_p = _retention_prompt(pc["ret"], nb, t)
        yb, gdn_p = _gdn_prompt(pc["gqkv"], pc["gab"], pc["gz"], alog_rows, dtb_rows, norm_w, l, nb, t)
        to_t = lambda a: a.reshape(nb, t, -1)
        from_t = lambda a: a.reshape(nb * t, -1)
        yc_t, sre_p, sim_p, yd_t, lru_p, lcv_p = _scan_mixers(
            to_t(pc["su"]), to_t(pc["lx"]), to_t(pc["lg"]), s5_w, lru_w, l, tc)
        xp = _merge(xp, mp, ya, yb, from_t(yc_t), from_t(yd_t), w_mg, wbr, wout, ln_g4, ln_b4, alpha, tm_ffn)
        xp = _ffn(xp, mp, 2, 1, wup, wdn, ln_g4, ln_b4, alpha, tm_ffn)
        new_p.append((ret_p, gdn_p, gcv_p,
                      sre_p.reshape(nb, SSM_GROUPS, SSM_P), sim_p.reshape(nb, SSM_GROUPS, SSM_P),
                      lru_p, lcv_p.transpose(1, 0, 2)))

        xs = _ffn(xs, ms, 0, 0, wup, wdn, ln_g4, ln_b4, alpha, ns)
        sc = _inproj(xs, ms, w_packed, ns)
        (qkt, gv, gx, yc, lh, gcv_s, sre_s, sim_s, lcv_s) = _smix1(
            sc, cos_s, sin_s, s_states, (gdn_conv_w, alog_rows, dtb_rows), s5_w, lru_w, l)
        ya, yb, yd, *big_s = _smix2(qkt, sc["ret"], gv, gx, sc["ret"], sc["gz"], lh, sc["lg"],
                                    norm_w, state_ret, state_gdn, l, SUBLANE, big_s)
        xs = _merge(xs, ms, ya, yb, yc, yd, w_mg, wbr, wout, ln_g4, ln_b4, alpha, ns)
        xs = _ffn(xs, ms, 2, 1, wup, wdn, ln_g4, ln_b4, alpha, ns)
        new_s.append((gcv_s.transpose(1, 0, 2),
                      sre_s.reshape(ns, SSM_GROUPS, SSM_P), sim_s.reshape(ns, SSM_GROUPS, SSM_P),
                      lh, lcv_s.transpose(1, 0, 2)))

    ret_p, gdn_p, gcv_p, sre_p, sim_p, lru_p, lcv_p = [jnp.stack(z) for z in zip(*new_p)]
    gcv_s, sre_s, sim_s, lru_s, lcv_s = [jnp.stack(z) for z in zip(*new_s)]
    ret_s, gdn_s = big_s
    return (xp.reshape(nb, t, D_MODEL), xs.reshape(ns, 1, D_MODEL),
            ret_p, ret_s, gdn_p, gdn_s, gcv_p, gcv_s, sre_p, sre_s, sim_p, sim_s,
            lru_p, lru_s, lcv_p, lcv_s)
```

```python
import functools
import math

import numpy as np
import jax
import jax.numpy as jnp
from jax import lax
from jax.experimental import pallas as pl
from jax.experimental.pallas import tpu as pltpu

F32 = jnp.float32
BF16 = jnp.bfloat16

D_MODEL = 1024
RET_HEADS, RET_DK, RET_DV, RET_CHUNK = 4, 64, 128, 128
RET_STEP_CHUNKS = 4
ROPE_BASE = 10000.0
GDN_HEADS, GDN_DK, GDN_DV, GDN_CHUNK = 4, 128, 128, 64
GDN_STEP_CHUNKS = 4
GDN_QKV = 2 * GDN_HEADS * GDN_DK + GDN_HEADS * GDN_DV
CONV_W = 4
SSM_GROUP, SSM_GROUPS, SSM_P = 16, 32, 64
SSM_WIDTH = SSM_GROUP * SSM_GROUPS
SSM_STATE = SSM_GROUPS * SSM_P
LRU_WIDTH = 512
LRU_C = 8.0
N_BRANCH, BRANCH_W = 4, 512
D_FF = 2816
N_SUB = 3
LN_EPS = 1e-5
PAST_LEN = 16384

RET_QK = RET_HEADS * RET_DK
RET_V = RET_HEADS * RET_DV
RET_W = 2 * RET_QK + 2 * RET_V
GDN_K = GDN_HEADS * GDN_DK
QKT_ROWS = 2 * RET_QK + 2 * GDN_K
LANE = 128
SUBLANE = 8
FF_CHUNKS = ((0, 768), (768, 1024), (1792, 1024))
VMEM_LIMIT = 56 * 1024 * 1024


def _cparams(*sem):
    return pltpu.CompilerParams(dimension_semantics=sem, vmem_limit_bytes=VMEM_LIMIT)


def _layer_spec(a, layer):
    nd = a.ndim
    return pl.BlockSpec((None,) + a.shape[1:], lambda *_: (layer,) + (0,) * (nd - 1))


def _resident(shape, index_map):
    return pl.BlockSpec(shape, index_map, pipeline_mode=pl.Buffered(1))


def _ln(x):
    mu = jnp.mean(x, axis=-1, keepdims=True)
    xc = x - mu
    return xc * lax.rsqrt(jnp.mean(xc * xc, axis=-1, keepdims=True) + LN_EPS)


def _sigmoid(x):
    return 0.5 * jnp.tanh(0.5 * x) + 0.5


def _silu(x):
    return x * _sigmoid(x)


def _softplus(x):
    return jnp.maximum(x, 0.0) + jnp.log1p(jnp.exp(-jnp.abs(x)))


def _bdot(a, w):
    return jnp.dot(a.astype(BF16), w.astype(BF16), preferred_element_type=F32)


def _split(x):
    hi = x.astype(BF16)
    return hi, (x - hi.astype(F32)).astype(BF16)


def _dot_nt(a, b):
    return lax.dot_general(a, b, (((1,), (1,)), ((), ())), preferred_element_type=F32)


def _iota(shape, dim):
    return lax.broadcasted_iota(jnp.int32, shape, dim)


def _to_time_major(x):
    nb, tc, width = x.shape
    return jnp.swapaxes(x, 0, 1).reshape(tc * nb, width)


def _from_time_major(y, nb):
    rows, width = y.shape
    return jnp.swapaxes(y.reshape(rows // nb, nb, width), 0, 1)


def _ada_kernel(cp_ref, cs_ref, w_ref, b_ref, op_ref, os_ref):
    w = w_ref[...].astype(BF16)
    op_ref[...] = _bdot(_silu(cp_ref[...]), w) + b_ref[...]
    os_ref[...] = _bdot(_silu(cs_ref[...]), w) + b_ref[...]


def _ada(c_prompt, c_sample, w_ada, b_ada):
    depth, _, n_out = w_ada.shape
    tn = 2304
    cond = lambda c: pl.BlockSpec(c.shape, lambda l, j: (0, 0))
    out = lambda c: pl.BlockSpec((None, c.shape[0], tn), lambda l, j: (l, 0, j))
    return pl.pallas_call(
        _ada_kernel,
        grid=(depth, n_out // tn),
        in_specs=[
            cond(c_prompt), cond(c_sample),
            pl.BlockSpec((None, D_MODEL, tn), lambda l, j: (l, 0, j)),
            pl.BlockSpec((None, 1, tn), lambda l, j: (l, 0, j)),
        ],
        out_specs=[out(c_prompt), out(c_sample)],
        out_shape=[jax.ShapeDtypeStruct((depth, c.shape[0], n_out), F32) for c in (c_prompt, c_sample)],
        compiler_params=_cparams("parallel", "parallel"),
        name="ada",
    )(c_prompt, c_sample, w_ada, b_ada.reshape(depth, 1, n_out))


class _Mod:
    def __init__(self, arr, layer, per_row, rows_per_batch=None):
        self.arr, self.layer, self.per_row, self.rpb = arr, layer, per_row, rows_per_batch

    def spec(self, sub, which, tm):
        k, l = 3 * sub + which, self.layer
        if self.per_row:
            return pl.BlockSpec((None, tm, D_MODEL), lambda i: (l, i, k))
        tiles = self.rpb // tm
        return pl.BlockSpec((None, None, None, 1, D_MODEL), lambda i: (l, k, i // tiles, 0, 0))


def _row_parts(ref, parts):
    if ref.shape[0] == 1:
        return [ref[...]] * parts
    hm = ref.shape[0] // parts
    return [ref[k * hm:(k + 1) * hm, :] for k in range(parts)]


def _ffn_kernel(x_ref, sh_ref, sc_ref, gt_ref, wup_ref, wdn_ref, g_ref, b_ref, o_ref, *, alpha, parts):
    ks = range(parts)
    xs, shs, scs, gts = (_row_parts(r, parts) for r in (x_ref, sh_ref, sc_ref, gt_ref))
    hs = [(_ln(xs[k]) * (1.0 + scs[k]) + shs[k]).astype(BF16) for k in ks]
    accs = [jnp.zeros(xs[k].shape, F32) for k in ks]
    for lo, width in FF_CHUNKS:
        a = [jnp.dot(hs[k], wup_ref[:, lo:lo + width], preferred_element_type=F32) for k in ks]
        b = [jnp.dot(hs[k], wup_ref[:, D_FF + lo:D_FF + lo + width], preferred_element_type=F32) for k in ks]
        act = [(_silu(a[k]) * b[k]).astype(BF16) for k in ks]
        accs = [accs[k] + jnp.dot(act[k], wdn_ref[lo:lo + width, :], preferred_element_type=F32) for k in ks]
    hm = x_ref.shape[0] // parts
    for k in ks:
        z = alpha * xs[k] + 0.5 * gts[k] * accs[k]
        o_ref[k * hm:(k + 1) * hm, :] = _ln(z) * g_ref[...] + b_ref[...]


def _ffn(x, mod, sub, which, wup, wdn, ln_g, ln_b, alpha, tm):
    n = x.shape[0]
    l = mod.layer
    row = pl.BlockSpec((tm, D_MODEL), lambda i: (i, 0))
    return pl.pallas_call(
        functools.partial(_ffn_kernel, alpha=alpha, parts=max(1, tm // 256)),
        grid=(n // tm,),
        in_specs=[
            row, mod.spec(sub, 0, tm), mod.spec(sub, 1, tm), mod.spec(sub, 2, tm),
            _resident((None, None, D_MODEL, 2 * D_FF), lambda i: (l, which, 0, 0)),
            _resident((None, None, D_FF, D_MODEL), lambda i: (l, which, 0, 0)),
            pl.BlockSpec((None, None, 1, D_MODEL), lambda i: (l, sub, 0, 0)),
            pl.BlockSpec((None, None, 1, D_MODEL), lambda i: (l, sub, 0, 0)),
        ],
        out_specs=row,
        out_shape=jax.ShapeDtypeStruct((n, D_MODEL), F32),
        compiler_params=_cparams("parallel"),
        name="ffn",
    )(x, mod.arr, mod.arr, mod.arr, wup, wdn, ln_g, ln_b)


_IN_PIECES = (("ret", RET_W), ("gqkv", GDN_QKV), ("gab", LANE), ("gz", 512), ("su", SSM_WIDTH),
              ("lx", LRU_WIDTH), ("lg", LRU_WIDTH))
IN_MAIN = sum(w for _, w in _IN_PIECES)
MG_WIDTH = N_BRANCH * D_MODEL


PACK_ROWS = 512


def _pack_kernel(wt_ref, main_ref, mg_ref):
    gate_lo = RET_W + GDN_QKV
    gate_hi = gate_lo + 2 * GDN_HEADS
    mg_lo = wt_ref.shape[0] - MG_WIDTH

    def move(dst_ref, dst, src, width):
        for c in range(0, width, PACK_ROWS):
            dst_ref[:, dst + c:dst + c + PACK_ROWS] = wt_ref[src + c:src + c + PACK_ROWS, :].T.astype(BF16)

    move(main_ref, 0, 0, gate_lo)
    g = wt_ref[gate_lo:gate_lo + LANE, :].T
    main_ref[:, gate_lo:gate_lo + LANE] = jnp.where(_iota(g.shape, 1) < 2 * GDN_HEADS, g, 0.0).astype(BF16)
    move(main_ref, gate_lo + LANE, gate_hi, mg_lo - gate_hi)
    move(mg_ref, 0, mg_lo, MG_WIDTH)


def _pack_w_in(w_in):
    depth, d_in, d_out = w_in.shape
    tk = 256
    blk = lambda w: pl.BlockSpec((None, tk, w), lambda l, i: (l, i, 0))
    return pl.pallas_call(
        _pack_kernel,
        grid=(depth, d_in // tk),
        in_specs=[pl.BlockSpec((None, d_out, tk), lambda l, i: (l, 0, i))],
        out_specs=[blk(IN_MAIN), blk(MG_WIDTH)],
        out_shape=[jax.ShapeDtypeStruct((depth, d_in, IN_MAIN), BF16),
                   jax.ShapeDtypeStruct((depth, d_in, MG_WIDTH), BF16)],
        compiler_params=_cparams("parallel", "parallel"),
        name="pack_w_in",
    )(jnp.swapaxes(w_in, 1, 2))


def _l2n(x):
    return x * lax.rsqrt(jnp.sum(x * x, axis=-1, keepdims=True) + 1e-6)


def _gdn_activate(conv):
    cq = _silu(conv)
    q = [_l2n(cq[:, h * GDN_DK:(h + 1) * GDN_DK]) * (GDN_DK ** -0.5) for h in range(GDN_HEADS)]
    k = [_l2n(cq[:, GDN_K + h * GDN_DK:GDN_K + (h + 1) * GDN_DK]) for h in range(GDN_HEADS)]
    return jnp.concatenate(q + k + [cq[:, 2 * GDN_K:]], axis=1)


def _inproj_kernel(x_ref, sh_ref, sc_ref, w_ref, *refs, seq_tiles):
    parts = 2 if x_ref.shape[0] >= 512 else 1
    xs, shs, scs = (_row_parts(r, parts) for r in (x_ref, sh_ref, sc_ref))
    hs = [(_ln(xs[k]) * (1.0 + scs[k]) + shs[k]).astype(BF16) for k in range(parts)]
    n_out = len(_IN_PIECES)
    if seq_tiles is None:
        o_refs = refs
    else:
        cw_ref, o_refs, cv_ref, xbuf = refs[0], refs[1:1 + n_out], refs[1 + n_out], refs[2 + n_out]

        @pl.when(pl.program_id(0) % seq_tiles == 0)
        def _():
            xbuf[0:SUBLANE, :] = jnp.zeros((SUBLANE, GDN_QKV), F32)

    offsets = np.cumsum([0] + [w for _, w in _IN_PIECES])
    order = sorted(range(n_out), key=lambda p: _IN_PIECES[p][0] != "gqkv")
    for p in order:
        (name, width), o_ref, lo = _IN_PIECES[p], o_refs[p], int(offsets[p])
        y = jnp.concatenate([jnp.dot(h, w_ref[:, lo:lo + width], preferred_element_type=F32) for h in hs], axis=0)
        if name == "gqkv" and seq_tiles is not None:
            tm = y.shape[0]
            xbuf[SUBLANE:SUBLANE + tm, :] = y
            conv = cw_ref[CONV_W - 1:CONV_W, :] * y
            for j in range(CONV_W - 1):
                r0 = SUBLANE - (CONV_W - 1) + j
                conv = conv + cw_ref[j:j + 1, :] * xbuf[r0:r0 + tm, :]
            cv_ref[0] = xbuf[SUBLANE + tm - (CONV_W - 1):SUBLANE + tm, :]
            xbuf[0:SUBLANE, :] = y[tm - SUBLANE:, :]
            y = _gdn_activate(conv)
        o_ref[...] = y


def _inproj(x, mod, w_packed, tm, conv=None):
    n = x.shape[0]
    l = mod.layer
    widths = [w for _, w in _IN_PIECES]
    ins = [x, mod.arr, mod.arr, w_packed]
    in_specs = [pl.BlockSpec((tm, D_MODEL), lambda i: (i, 0)),
                mod.spec(1, 0, tm), mod.spec(1, 1, tm),
                _resident((None, D_MODEL, IN_MAIN), lambda i: (l, 0, 0))]
    out_specs = [pl.BlockSpec((tm, w), lambda i: (i, 0)) for w in widths]
    out_shape = [jax.ShapeDtypeStruct((n, w), F32) for w in widths]
    scratch, seq_tiles = [], None
    if conv is not None:
        conv_w, nb, t = conv
        seq_tiles = t // tm
        ins.append(conv_w)
        in_specs.append(_layer_spec(conv_w, l))
        out_specs.append(pl.BlockSpec((1, CONV_W - 1, GDN_QKV), lambda i: (i // seq_tiles, 0, 0)))
        out_shape.append(jax.ShapeDtypeStruct((nb, CONV_W - 1, GDN_QKV), F32))
        scratch.append(pltpu.VMEM((SUBLANE + tm, GDN_QKV), F32))
    outs = pl.pallas_call(
        functools.partial(_inproj_kernel, seq_tiles=seq_tiles),
        grid=(n // tm,),
        in_specs=in_specs,
        out_specs=out_specs,
        out_shape=out_shape,
        scratch_shapes=scratch,
        compiler_params=_cparams("arbitrary"),
        name="inproj",
    )(*ins)
    return dict(zip([nm for nm, _ in _IN_PIECES] + ["gdn_conv"], outs))


def _rope_tables(pos):
    half = RET_DK // 2
    inv = ROPE_BASE ** (-jnp.arange(half, dtype=F32) / half)
    ang = pos.astype(F32)[:, None] * inv[None, :]
    cos, sin = jnp.cos(ang), jnp.sin(ang)
    cos_t = jnp.tile(jnp.concatenate([cos, cos], axis=-1), (1, RET_HEADS))
    sin_t = jnp.tile(jnp.concatenate([-sin, sin], axis=-1), (1, RET_HEADS))
    return cos_t, sin_t


def _rotate(z, cos, sin):
    half = RET_DK // 2
    first = (_iota(z.shape, 1) % RET_DK) < half
    swapped = jnp.where(first, pltpu.roll(z, RET_QK - half, 1), pltpu.roll(z, half, 1))
    return z * cos + swapped * sin


def _ret_gammas():
    return [1.0 - 2.0 ** (-5.0 - h) for h in range(RET_HEADS)]


def _ret_consts(chunk, n_sub):
    log_g = np.log(np.array(_ret_gammas(), np.float64))
    idx = np.arange(chunk, dtype=np.float64)
    rel = idx[:, None] - idx[None, :]
    dmask = np.where(rel[None] >= 0, np.exp(log_g[:, None, None] * np.maximum(rel, 0.0)[None]), 0.0)
    qdec = np.repeat(np.exp(log_g[None, :] * (idx[:, None] + 1.0)), RET_DK, axis=1)
    kdec = np.repeat(np.exp(log_g[None, :] * (chunk - 1.0 - idx[:, None])), RET_DK, axis=1)
    qdec, kdec = np.tile(qdec, (n_sub, 1)), np.tile(kdec, (n_sub, 1))
    head_r = np.arange(RET_QK) // RET_DK
    head_c = np.arange(RET_V) // RET_DV
    bd = (head_r[:, None] == head_c[None, :]).astype(np.float64)
    cd = bd * np.exp(log_g * chunk)[head_r][:, None]
    f = lambda a: jnp.asarray(a, F32)
    return f(dmask), f(qdec), f(kdec), f(cd), f(bd)


def _ret_kernel(x_ref, cos_ref, sin_ref, dmask_ref, qdec_ref, kdec_ref, cd_ref, bd_ref,
                y_ref, st_ref, s_scr):
    c = pl.program_id(1)
    ch = RET_CHUNK
    n_sub = x_ref.shape[0] // ch

    @pl.when(c == 0)
    def _():
        s_scr[...] = jnp.zeros(s_scr.shape, F32)

    x = x_ref[...]
    q = _rotate(x[:, 0:RET_QK], cos_ref[...], sin_ref[...])
    k = _rotate(x[:, RET_QK:2 * RET_QK], cos_ref[...], sin_ref[...]) * (RET_DK ** -0.5)
    v = x[:, 2 * RET_QK:2 * RET_QK + RET_V].astype(BF16)
    gate = x[:, 2 * RET_QK + RET_V:]
    q_dec = (q * qdec_ref[...]).astype(BF16)
    k_dec = k * kdec_ref[...]
    kb = k.astype(BF16)
    head = _iota((ch, RET_QK), 1) // RET_DK
    intra, upd = {}, {}
    for n in range(n_sub):
        sl = slice(n * ch, (n + 1) * ch)
        q_heads = jnp.concatenate([jnp.where(head == h, q[sl], 0.0) for h in range(RET_HEADS)], axis=0)
        inner_all = _dot_nt(q_heads.astype(BF16), kb[sl])
        for h in range(RET_HEADS):
            inner = inner_all[h * ch:(h + 1) * ch] * dmask_ref[h]
            intra[n, h] = _bdot(inner, v[sl, h * RET_DV:(h + 1) * RET_DV])
        upd[n] = jnp.dot(k_dec[sl].T.astype(BF16), v[sl], preferred_element_type=F32) * bd_ref[...]
    s = s_scr[...]
    cross = {}
    for n in range(n_sub):
        cross[n] = jnp.dot(q_dec[n * ch:(n + 1) * ch], s.astype(BF16), preferred_element_type=F32)
        s = s * cd_ref[...] + upd[n]
    s_scr[...] = s
    for n in range(n_sub):
        sl = slice(n * ch, (n + 1) * ch)
        outs = [_ln(intra[n, h] + cross[n][:, h * RET_DV:(h + 1) * RET_DV]) for h in range(RET_HEADS)]
        y_ref[sl, :] = (jnp.concatenate(outs, axis=1) * _silu(gate[sl])).astype(y_ref.dtype)

    @pl.when(c == pl.num_programs(1) - 1)
    def _():
        for h in range(RET_HEADS):
            st_ref[0, h] = s[h * RET_DK:(h + 1) * RET_DK, h * RET_DV:(h + 1) * RET_DV]


def _retention_prompt(ret, nb, t):
    n_sub = RET_STEP_CHUNKS if t % (RET_STEP_CHUNKS * RET_CHUNK) == 0 else 1
    chunk = n_sub * RET_CHUNK
    nc = t // chunk
    cos_t, sin_t = _rope_tables(jnp.arange(t))
    consts = _ret_consts(RET_CHUNK, n_sub)
    full = lambda a: pl.BlockSpec(a.shape, lambda b, c: (0,) * a.ndim)
    return pl.pallas_call(
        _ret_kernel,
        grid=(nb, nc),
        in_specs=[
            pl.BlockSpec((chunk, RET_W), lambda b, c: (b * nc + c, 0)),
            pl.BlockSpec((chunk, RET_QK), lambda b, c: (c, 0)),
            pl.BlockSpec((chunk, RET_QK), lambda b, c: (c, 0)),
        ] + [full(a) for a in consts],
        out_specs=[
            pl.BlockSpec((chunk, RET_V), lambda b, c: (b * nc + c, 0)),
            pl.BlockSpec((1, RET_HEADS, RET_DK, RET_DV), lambda b, c: (b, 0, 0, 0)),
        ],
        out_shape=[
            jax.ShapeDtypeStruct((nb * t, RET_V), BF16),
            jax.ShapeDtypeStruct((nb, RET_HEADS, RET_DK, RET_DV), F32),
        ],
        scratch_shapes=[pltpu.VMEM((RET_QK, RET_V), F32)],
        compiler_params=_cparams("parallel", "arbitrary"),
        name="retention",
    )(ret, cos_t, sin_t, *consts)


def _rms(x):
    return x * lax.rsqrt(jnp.mean(x * x, axis=-1, keepdims=True) + 1e-6)


def _gate_lanes(gab, alog_row, dtb_row):
    g = -jnp.exp(alog_row) * _softplus(gab + dtb_row)
    return jnp.where(_iota(gab.shape, 1) < GDN_HEADS, g, _sigmoid(gab))


def _gdn_kernel(x_ref, gab_ref, gz_ref, alog_ref, dtb_ref, nw_ref, y_ref, st_ref, s_scr):
    c = pl.program_id(1)
    ch = GDN_CHUNK

    @pl.when(c == 0)
    def _():
        s_scr[...] = jnp.zeros(s_scr.shape, F32)

    rows = x_ref.shape[0]
    cq = x_ref[...]

    gl = _gate_lanes(gab_ref[...], alog_ref[...], dtb_ref[...])
    col = _iota((ch, 2 * ch), 1) % ch
    tril = _iota((ch, 2 * ch), 0) >= col
    strict = _iota((ch, 2 * ch), 0) > col
    is_g = _iota(gl.shape, 1) < GDN_HEADS
    r_i, c_i = _iota((rows, rows), 0), _iota((rows, rows), 1)
    chunk_tril = ((r_i >= c_i) & (r_i // ch == c_i // ch)).astype(F32)
    csum = jnp.dot(chunk_tril, jnp.where(is_g, gl, 0.0), preferred_element_type=F32,
                   precision=lax.Precision.HIGHEST)
    gsel = jnp.where(is_g, csum, gl)
    n_sub = rows // ch
    heads = range(GDN_HEADS)
    probs = [(h, n) for h in heads for n in range(n_sub)]

    q_all, k_all, g_all, qe_all, rhs_all, kbeta_all = {}, {}, {}, {}, {}, {}
    for h in heads:
        q_all[h] = cq[:, h * GDN_DK:(h + 1) * GDN_DK]
        k_all[h] = cq[:, GDN_K + h * GDN_DK:GDN_K + (h + 1) * GDN_DK]
        v_h = cq[:, 2 * GDN_K + h * GDN_DV:2 * GDN_K + (h + 1) * GDN_DV]
        g_all[h] = jnp.broadcast_to(gsel[:, h:h + 1], (rows, LANE))
        b_h = jnp.broadcast_to(gsel[:, GDN_HEADS + h:GDN_HEADS + h + 1], (rows, LANE))
        e_h = jnp.exp(g_all[h])
        kbeta_all[h] = k_all[h] * b_h
        rhs_all[h] = jnp.concatenate([v_h * b_h, kbeta_all[h] * e_h], axis=1)
        qe_all[h] = q_all[h] * e_h

    decay, pp, uw, attn, kd_t, e_last = {}, {}, {}, {}, {}, {}
    for h, n in probs:
        sl = slice(n * ch, (n + 1) * ch)
        g_b = g_all[h][sl]
        g_cols = jnp.concatenate([g_b, g_b], axis=0).T[:ch, :]
        diff = g_b - g_cols
        decay[h, n] = jnp.where(tril, jnp.exp(jnp.where(tril, diff, 0.0)), 0.0)
        g_last = g_b[ch - 1:ch, :]
        e_last[h, n] = jnp.exp(g_last)
        kd_t[h, n] = (k_all[h][sl] * jnp.exp(g_last - g_b)).T.astype(BF16)
    for h, n in probs:
        sl = slice(n * ch, (n + 1) * ch)
        kb_hi, kb_lo = _split(kbeta_all[h][sl])
        k_hi, k_lo = _split(jnp.concatenate([k_all[h][sl], k_all[h][sl]], axis=0))
        kk = _dot_nt(jnp.concatenate([kb_hi, kb_hi, kb_lo, kb_lo], axis=1),
                     jnp.concatenate([k_hi, k_lo, k_hi, k_lo], axis=1))
        pp[h, n] = -jnp.where(strict, kk * decay[h, n], 0.0)
        attn[h, n] = _dot_nt(q_all[h][sl].astype(BF16), k_all[h][sl].astype(BF16)) * decay[h, n][:, :ch]
    for stage in range(int(math.log2(ch))):
        for h, n in probs:
            p_hi, p_lo = _split(pp[h, n])
            lhs = jnp.concatenate([p_hi, p_lo], axis=1)
            y = rhs_all[h][n * ch:(n + 1) * ch] if stage == 0 else uw[h, n]
            y_hi, y_lo = _split(y)
            uw[h, n] = y + jnp.dot(lhs, jnp.concatenate([y_hi, y_lo, y_hi, y_lo], axis=0),
                                   preferred_element_type=F32)
            if stage + 1 < int(math.log2(ch)):
                pp[h, n] = jnp.dot(lhs, jnp.concatenate([p_hi, p_lo, p_hi, p_lo], axis=0),
                                   preferred_element_type=F32)
    s = {h: s_scr[h] for h in heads}
    o_parts = {h: [] for h in heads}
    for n in range(n_sub):
        for h in heads:
            sl = slice(n * ch, (n + 1) * ch)
            u, w = uw[h, n][:, :GDN_DV], uw[h, n][:, GDN_DV:]
            ws_qs = _bdot(jnp.concatenate([w, qe_all[h][sl]], axis=0), s[h])
            v_new = u - ws_qs[:ch]
            o_parts[h].append(ws_qs[ch:] + _bdot(attn[h, n], v_new))
            s[h] = s[h] * e_last[h, n] + jnp.dot(kd_t[h, n], v_new.astype(BF16), preferred_element_type=F32)
    norm_w = nw_ref[...]
    gz = gz_ref[...]
    outs = []
    for h in heads:
        s_scr[h] = s[h]
        o = jnp.concatenate(o_parts[h], axis=0)
        outs.append(_rms(o) * norm_w * _silu(gz[:, h * GDN_DV:(h + 1) * GDN_DV]))
    y_ref[...] = jnp.concatenate(outs, axis=1).astype(y_ref.dtype)

    @pl.when(c == pl.num_programs(1) - 1)
    def _():
        st_ref[0] = s_scr[...]


def _gdn_prompt(gqkv, gab, gz, alog_row, dtb_row, norm_w, layer, nb, t):
    ch = GDN_STEP_CHUNKS * GDN_CHUNK if t % (GDN_STEP_CHUNKS * GDN_CHUNK) == 0 else GDN_CHUNK
    nc = t // ch
    row = lambda w: pl.BlockSpec((ch, w), lambda b, c: (b * nc + c, 0))
    return pl.pallas_call(
        _gdn_kernel,
        grid=(nb, nc),
        in_specs=[row(GDN_QKV), row(LANE), row(GDN_HEADS * GDN_DV)]
        + [_layer_spec(a, layer) for a in (alog_row, dtb_row, norm_w)],
        out_specs=[
            row(GDN_HEADS * GDN_DV),
            pl.BlockSpec((1, GDN_HEADS, GDN_DK, GDN_DV), lambda b, c: (b, 0, 0, 0)),
        ],
        out_shape=[
            jax.ShapeDtypeStruct((nb * t, GDN_HEADS * GDN_DV), F32),
            jax.ShapeDtypeStruct((nb, GDN_HEADS, GDN_DK, GDN_DV), F32),
        ],
        scratch_shapes=[pltpu.VMEM((GDN_HEADS, GDN_DK, GDN_DV), F32)],
        compiler_params=_cparams("parallel", "arbitrary"),
        name="gdn",
    )(gqkv, gab, gz, alog_row, dtb_row, norm_w)


SSM_BLK = 4
SSM_BLK_STATE = SSM_STATE // SSM_BLK


def _s5_params(lam_re, lam_im, log_step, b_re, b_im, c_re, c_im):
    step = jnp.exp(log_step.astype(F32))[:, None]
    mag = jnp.exp(lam_re * step)
    ab_re = mag * jnp.cos(lam_im * step)
    ab_im = mag * jnp.sin(lam_im * step)
    den = lam_re * lam_re + lam_im * lam_im
    nr = ab_re - 1.0
    f_re = (nr * lam_re + ab_im * lam_im) / den
    f_im = (ab_im * lam_re - nr * lam_im) / den
    bb_re = f_re[..., None] * b_re - f_im[..., None] * b_im
    bb_im = f_re[..., None] * b_im + f_im[..., None] * b_re
    gpb = SSM_GROUPS // SSM_BLK
    eye = jnp.eye(gpb, dtype=F32)

    def in_mat(bb):
        bb = bb.reshape(SSM_BLK, gpb, SSM_P, SSM_GROUP)
        m = jnp.einsum("jgpc,gk->jgckp", bb, eye)
        return m.reshape(SSM_BLK, gpb * SSM_GROUP, gpb * SSM_P).astype(BF16)

    def out_mat(cc):
        cc = cc.reshape(SSM_BLK, gpb, SSM_GROUP, SSM_P)
        m = jnp.einsum("jgcp,gk->jgpkc", cc, eye)
        return m.reshape(SSM_BLK, gpb * SSM_P, gpb * SSM_GROUP).astype(BF16)

    return (ab_re.reshape(1, SSM_STATE), ab_im.reshape(1, SSM_STATE),
            in_mat(bb_re), in_mat(bb_im), out_mat(c_re), out_mat(c_im))


def _s5_readout(u, h_re, h_im, cre_ref, cim_ref, d_row, gw_ref, gb_row):
    ys = []
    for j in range(SSM_BLK):
        sl = slice(j * SSM_BLK_STATE, (j + 1) * SSM_BLK_STATE)
        ys.append(_bdot(h_re[:, sl], cre_ref[j]) - _bdot(h_im[:, sl], cim_ref[j]))
    y = jax.nn.gelu(jnp.concatenate(ys, axis=1) + d_row * u)
    return y * _sigmoid(_bdot(y, gw_ref[...]) + gb_row)


def _scan_mixers_kernel(u_ref, x_ref, g_ref,
                        are_ref, aim_ref, bre_ref, bim_ref, cre_ref, cim_ref, d_ref, gw_ref, gb_ref,
                        cw_ref, cb_ref, wa_ref, wx_ref, ba_ref, bx_ref, lam_ref,
                        y_ref, hre_ref, him_ref, yd_ref, h_out, cv_out,
                        bu_re, bu_im, h_re, h_im, xbuf, a_buf, b_buf, h_scr, *, nb):
    i = pl.program_id(0)
    tc = u_ref.shape[1]
    rows = tc * nb
    tail = (CONV_W - 1) * nb

    @pl.when(i == 0)
    def _():
        h_re[...] = jnp.zeros(h_re.shape, F32)
        h_im[...] = jnp.zeros(h_im.shape, F32)
        xbuf[0:tail, :] = jnp.zeros((tail, LRU_WIDTH), F32)
        h_scr[...] = jnp.zeros(h_scr.shape, F32)

    u = _to_time_major(u_ref[...])
    for j in range(SSM_BLK):
        uj = u[:, j * LANE:(j + 1) * LANE]
        sl = slice(j * SSM_BLK_STATE, (j + 1) * SSM_BLK_STATE)
        bu_re[:, sl] = _bdot(uj, bre_ref[j])
        bu_im[:, sl] = _bdot(uj, bim_ref[j])
    x = _to_time_major(x_ref[...])
    xbuf[tail:tail + rows, :] = x
    conv = cw_ref[CONV_W - 1:CONV_W, :] * x
    for j in range(CONV_W - 1):
        conv = conv + cw_ref[j:j + 1, :] * xbuf[j * nb:j * nb + rows, :]
    cv_out[...] = xbuf[rows:rows + tail, :].reshape(CONV_W - 1, nb, LRU_WIDTH)
    xbuf[0:tail, :] = x[rows - tail:, :]
    a, b = _lru_gates(conv + cb_ref[...], wa_ref, wx_ref, ba_ref[...], bx_ref[...], lam_ref[...])
    a_buf[...] = a
    b_buf[...] = b

    for j in range(SSM_BLK):
        sl = slice(j * SSM_BLK_STATE, (j + 1) * SSM_BLK_STATE)
        a_re = jnp.broadcast_to(are_ref[:, sl], (nb, SSM_BLK_STATE))
        a_im = jnp.broadcast_to(aim_ref[:, sl], (nb, SSM_BLK_STATE))

        def body(t, carry):
            hr, hi = carry
            r = pl.ds(pl.multiple_of(t * nb, nb), nb)
            nr = a_re * hr - a_im * hi + bu_re[r, sl]
            ni = a_re * hi + a_im * hr + bu_im[r, sl]
            bu_re[r, sl] = nr
            bu_im[r, sl] = ni
            return nr, ni

        hr, hi = lax.fori_loop(0, tc, body, (h_re[:, sl], h_im[:, sl]), unroll=4)
        h_re[:, sl] = hr
        h_im[:, sl] = hi

    def lru_body(t, h):
        r = pl.ds(pl.multiple_of(t * nb, nb), nb)
        h = a_buf[r, :] * h + b_buf[r, :]
        b_buf[r, :] = h
        return h

    h = lax.fori_loop(0, tc, lru_body, h_scr[...], unroll=8)
    h_scr[...] = h
    h_out[...] = h

    y = _s5_readout(u, bu_re[...], bu_im[...], cre_ref, cim_ref, d_ref[...], gw_ref, gb_ref[...])
    y_ref[...] = _from_time_major(y, nb).astype(y_ref.dtype)
    hre_ref[...] = h_re[...]
    him_ref[...] = h_im[...]
    yd = b_buf[...] * jax.nn.gelu(_to_time_major(g_ref[...]))
    yd_ref[...] = _from_time_major(yd, nb).astype(yd_ref.dtype)


def _scan_mixers(u_t, x_t, g_t, s5_w, lru_w, layer, tc):
    nb, t, _ = u_t.shape
    rows = tc * nb
    blk = pl.BlockSpec((nb, tc, SSM_WIDTH), lambda i: (0, i, 0))
    st = pl.BlockSpec((nb, SSM_STATE), lambda i: (0, 0))
    sds = lambda *s: jax.ShapeDtypeStruct(s, F32)
    return pl.pallas_call(
        functools.partial(_scan_mixers_kernel, nb=nb),
        grid=(t // tc,),
        in_specs=[blk, blk, blk] + [_layer_spec(a, layer) for a in (*s5_w, *lru_w)],
        out_specs=[blk, st, st, blk,
                   pl.BlockSpec((nb, LRU_WIDTH), lambda i: (0, 0)),
                   pl.BlockSpec((CONV_W - 1, nb, LRU_WIDTH), lambda i: (0, 0, 0))],
        out_shape=[jax.ShapeDtypeStruct((nb, t, SSM_WIDTH), BF16), sds(nb, SSM_STATE), sds(nb, SSM_STATE),
                   jax.ShapeDtypeStruct((nb, t, LRU_WIDTH), BF16), sds(nb, LRU_WIDTH),
                   sds(CONV_W - 1, nb, LRU_WIDTH)],
        scratch_shapes=[pltpu.VMEM((rows, SSM_STATE), F32), pltpu.VMEM((rows, SSM_STATE), F32),
                        pltpu.VMEM((nb, SSM_STATE), F32), pltpu.VMEM((nb, SSM_STATE), F32),
                        pltpu.VMEM(((CONV_W - 1) * nb + rows, LRU_WIDTH), F32),
                        pltpu.VMEM((rows, LRU_WIDTH), F32), pltpu.VMEM((rows, LRU_WIDTH), F32),
                        pltpu.VMEM((nb, LRU_WIDTH), F32)],
        compiler_params=_cparams("arbitrary"),
        name="scan_mixers",
    )(u_t, x_t, g_t, *s5_w, *lru_w)


def _lru_gates(cx, wa_ref, wx_ref, ba_row, bx_row, lam_row):
    r = _sigmoid(_bdot(cx, wa_ref[...]) + ba_row)
    i = _sigmoid(_bdot(cx, wx_ref[...]) + bx_row)
    log_a = -LRU_C * r * _softplus(-lam_row)
    a = jnp.exp(log_a)
    th = jnp.tanh(log_a)
    b = jnp.sqrt(-2.0 * th / (1.0 - th)) * (i * cx)
    return a, b


def _merge_kernel(x_ref, sh_ref, sc_ref, gt_ref, ya_ref, yb_ref, yc_ref, yd_ref, wmg_ref, wb_ref, wo_ref,
                  g_ref, b_ref, o_ref, *, alpha, parts):
    ks = range(parts)
    xs, shs, scs, gts = (_row_parts(r, parts) for r in (x_ref, sh_ref, sc_ref, gt_ref))
    hs = [(_ln(xs[k]) * (1.0 + scs[k]) + shs[k]).astype(BF16) for k in ks]
    accs = [jnp.zeros(xs[k].shape, F32) for k in ks]
    for n, y_ref in enumerate((ya_ref, yb_ref, yc_ref, yd_ref)):
        ys = _row_parts(y_ref, parts)
        logits = [jnp.dot(hs[k], wmg_ref[:, n * D_MODEL:(n + 1) * D_MODEL], preferred_element_type=F32)
                  for k in ks]
        accs = [accs[k] + _sigmoid(logits[k]) * _bdot(ys[k], wb_ref[n]) for k in ks]
    outs = [_bdot(accs[k], wo_ref[...]) for k in ks]
    hm = x_ref.shape[0] // parts
    for k in ks:
        z = alpha * xs[k] + gts[k] * outs[k]
        o_ref[k * hm:(k + 1) * hm, :] = _ln(z) * g_ref[...] + b_ref[...]


def _merge(x, mod, ya, yb, yc, yd, w_mg, w_branch, w_out, ln_g, ln_b, alpha, tm):
    n = x.shape[0]
    l = mod.layer
    row = lambda w: pl.BlockSpec((tm, w), lambda i: (i, 0))
    return pl.pallas_call(
        functools.partial(_merge_kernel, alpha=alpha, parts=max(1, tm // 256)),
        grid=(n // tm,),
        in_specs=[
            row(D_MODEL), mod.spec(1, 0, tm), mod.spec(1, 1, tm), mod.spec(1, 2, tm),
            row(BRANCH_W), row(BRANCH_W), row(BRANCH_W), row(BRANCH_W),
            _resident((None, D_MODEL, MG_WIDTH), lambda i: (l, 0, 0)),
            _resident((None, N_BRANCH, BRANCH_W, D_MODEL), lambda i: (l, 0, 0, 0)),
            _resident((None, D_MODEL, D_MODEL), lambda i: (l, 0, 0)),
            pl.BlockSpec((None, None, 1, D_MODEL), lambda i: (l, 1, 0, 0)),
            pl.BlockSpec((None, None, 1, D_MODEL), lambda i: (l, 1, 0, 0)),
        ],
        out_specs=row(D_MODEL),
        out_shape=jax.ShapeDtypeStruct((n, D_MODEL), F32),
        compiler_params=_cparams("parallel"),
        name="merge",
    )(x, mod.arr, mod.arr, mod.arr, ya, yb, yc, yd, w_mg, w_branch, w_out, ln_g, ln_b)


def _smix1_kernel(ret_ref, gqkv_ref, gab_ref, su_ref, lx_ref, cos_ref, sin_ref,
                  gcv_ref, sre_ref, sim_ref, lru_ref, lcv_ref,
                  gcw_ref, alog_ref, dtb_ref,
                  are_ref, aim_ref, bre_ref, bim_ref, cre_ref, cim_ref, d_ref, gw_ref, gb_ref,
                  lcw_ref, lcb_ref, wa_ref, wx_ref, ba_ref, bx_ref, lam_ref,
                  qkt_ref, gv_ref, gx_ref, yc_ref, lh_ref, gcv_out, sre_out, sim_out, lcv_out):
    ret = ret_ref[...]
    rq = _rotate(ret[:, 0:RET_QK], cos_ref[...], sin_ref[...])
    rk = _rotate(ret[:, RET_QK:2 * RET_QK], cos_ref[...], sin_ref[...]) * (RET_DK ** -0.5)

    x = gqkv_ref[...]
    conv = gcw_ref[3:4, :] * x
    for j in range(CONV_W - 1):
        conv = conv + gcw_ref[j:j + 1, :] * gcv_ref[j]
    gcv_out[0] = gcv_ref[1]
    gcv_out[1] = gcv_ref[2]
    gcv_out[2] = x
    cq = _gdn_activate(conv)
    gq, gk = cq[:, :GDN_K], cq[:, GDN_K:2 * GDN_K]
    gv_ref[...] = cq[:, 2 * GDN_K:]
    gl = _gate_lanes(gab_ref[...], alog_ref[...], dtb_ref[...])
    gx = jnp.concatenate([jnp.broadcast_to(gl[:, r:r + 1], (gl.shape[0], LANE))
                          for r in range(2 * GDN_HEADS)], axis=1)
    gx_ref[...] = jnp.where(_iota(gx.shape, 1) < GDN_HEADS * LANE, jnp.exp(gx), gx)
    qkt_ref[...] = jnp.concatenate([rq, rk, gq, gk], axis=1).T

    u = su_ref[...]
    h_re, h_im = [], []
    for j in range(SSM_BLK):
        sl = slice(j * SSM_BLK_STATE, (j + 1) * SSM_BLK_STATE)
        uj = u[:, j * LANE:(j + 1) * LANE]
        a_re, a_im = are_ref[:, sl], aim_ref[:, sl]
        p_re, p_im = sre_ref[:, sl], sim_ref[:, sl]
        h_re.append(a_re * p_re - a_im * p_im + _bdot(uj, bre_ref[j]))
        h_im.append(a_re * p_im + a_im * p_re + _bdot(uj, bim_ref[j]))
    h_re = jnp.concatenate(h_re, axis=1)
    h_im = jnp.concatenate(h_im, axis=1)
    sre_out[...] = h_re
    sim_out[...] = h_im
    yc_ref[...] = _s5_readout(u, h_re, h_im, cre_ref, cim_ref, d_ref[...], gw_ref, gb_ref[...])

    lx = lx_ref[...]
    conv = lcw_ref[3:4, :] * lx
    for j in range(CONV_W - 1):
        conv = conv + lcw_ref[j:j + 1, :] * lcv_ref[j]
    lcv_out[0] = lcv_ref[1]
    lcv_out[1] = lcv_ref[2]
    lcv_out[2] = lx
    a, b = _lru_gates(conv + lcb_ref[...], wa_ref, wx_ref, ba_ref[...], bx_ref[...], lam_ref[...])
    lh_ref[...] = a * lru_ref[...] + b


def _smix1(pieces, cos_row, sin_row, states, gdn_w, s5_w, lru_w, layer):
    rows = pieces["ret"].shape[0]
    acts = [pieces["ret"], pieces["gqkv"], pieces["gab"], pieces["su"], pieces["lx"], cos_row, sin_row]
    stacked = [*states, *gdn_w, *s5_w, *lru_w]
    ins = acts + stacked
    full = lambda a: pl.BlockSpec(a.shape, lambda i: (0,) * a.ndim)
    sds = lambda *s: jax.ShapeDtypeStruct(s, F32)
    out_shape = [sds(QKT_ROWS, rows), sds(rows, GDN_HEADS * GDN_DV), sds(rows, 8 * LANE),
                 sds(rows, SSM_WIDTH), sds(rows, LRU_WIDTH),
                 sds(CONV_W - 1, rows, GDN_QKV), sds(rows, SSM_STATE), sds(rows, SSM_STATE),
                 sds(CONV_W - 1, rows, LRU_WIDTH)]
    return pl.pallas_call(
        _smix1_kernel,
        grid=(1,),
        in_specs=[full(a) for a in acts] + [_layer_spec(a, layer) for a in stacked],
        out_specs=[pl.BlockSpec(s.shape, lambda i, nd=len(s.shape): (0,) * nd) for s in out_shape],
        out_shape=out_shape,
        compiler_params=_cparams("arbitrary"),
        name="sample_mix",
    )(*ins)


def _smix2_kernel(qkt_ref, rv_ref, gv_ref, gx_ref, rg_ref, gz_ref, lh_ref, lg_ref, nw_ref, sret_ref, sgdn_ref,
                  *refs, first):
    ya_ref, yb_ref, yd_ref, nret_ref, ngdn_ref, o_ret, o_gdn = refs[-7:]
    if first:
        for l in range(1, nret_ref.shape[0]):
            nret_ref[l] = jnp.zeros(nret_ref.shape[1:], F32)
            ngdn_ref[l] = jnp.zeros(ngdn_ref.shape[1:], F32)
        nret_ref, ngdn_ref = nret_ref.at[0], ngdn_ref.at[0]
    i = pl.program_id(0)
    bt = rv_ref.shape[0]
    rows = qkt_ref.shape[1]
    gammas = _ret_gammas()
    q_hi, q_lo = _split(qkt_ref[...])
    qk_split = jnp.concatenate([q_hi, q_lo], axis=1)

    for j in range(bt):
        b = i * bt + j
        onehot = jnp.where(_iota((2 * rows, LANE), 0) % rows == b, 1.0, 0.0).astype(BF16)
        cols = jnp.dot(qk_split, onehot, preferred_element_type=F32)
        r = slice(j, j + 1)
        for h in range(RET_HEADS):
            q_c = cols[h * RET_DK:(h + 1) * RET_DK]
            k_c = cols[RET_QK + h * RET_DK:RET_QK + (h + 1) * RET_DK]
            v_r = rv_ref[r, h * RET_DV:(h + 1) * RET_DV]
            s_new = gammas[h] * sret_ref[j, h] + k_c * v_r
            nret_ref[j, h] = s_new
            o_ret[r, h * RET_DV:(h + 1) * RET_DV] = jnp.sum(q_c * s_new, axis=0, keepdims=True)
        base = 2 * RET_QK
        for h in range(GDN_HEADS):
            q_c = cols[base + h * GDN_DK:base + (h + 1) * GDN_DK]
            k_c = cols[base + GDN_K + h * GDN_DK:base + GDN_K + (h + 1) * GDN_DK]
            v_r = gv_ref[r, h * GDN_DV:(h + 1) * GDN_DV]
            e_g = gx_ref[r, h * LANE:(h + 1) * LANE]
            beta = gx_ref[r, (GDN_HEADS + h) * LANE:(GDN_HEADS + h + 1) * LANE]
            s = sgdn_ref[j, h]
            v_new = beta * (v_r - e_g * jnp.sum(k_c * s, axis=0, keepdims=True))
            s_new = e_g * s + k_c * v_new
            ngdn_ref[j, h] = s_new
            o_gdn[r, h * GDN_DV:(h + 1) * GDN_DV] = jnp.sum(q_c * s_new, axis=0, keepdims=True)
    rg, gz = rg_ref[...], gz_ref[...]
    o_r, o_g = o_ret[...], o_gdn[...]
    ya_ref[...] = jnp.concatenate(
        [_ln(o_r[:, h * RET_DV:(h + 1) * RET_DV]) for h in range(RET_HEADS)], axis=1) * _silu(rg)
    yb_ref[...] = jnp.concatenate(
        [_rms(o_g[:, h * GDN_DV:(h + 1) * GDN_DV]) * nw_ref[...] for h in range(GDN_HEADS)], axis=1) * _silu(gz)
    yd_ref[...] = lh_ref[...] * jax.nn.gelu(lg_ref[...])


def _smix2(qkt, rv, gv, gx, rg, gz, lh, lg, norm_w, s_ret, s_gdn, layer, bt, prev):
    rows = rv.shape[0]
    row = lambda w: pl.BlockSpec((bt, w), lambda i: (i, 0))
    sds = lambda *s: jax.ShapeDtypeStruct(s, F32)
    ret_blk = pl.BlockSpec((None, bt, RET_HEADS, RET_DK, RET_DV), lambda i: (layer, i, 0, 0, 0))
    gdn_blk = pl.BlockSpec((None, bt, GDN_HEADS, GDN_DK, GDN_DV), lambda i: (layer, i, 0, 0, 0))
    ins = [qkt, rv, gv, gx, rg, gz, lh, lg, norm_w, s_ret, s_gdn]
    in_specs = [pl.BlockSpec(qkt.shape, lambda i: (0, 0)),
                pl.BlockSpec((bt, RET_V), lambda i: (i, 1)),
                row(GDN_HEADS * GDN_DV), row(8 * LANE),
                pl.BlockSpec((bt, RET_V), lambda i: (i, 2)),
                row(GDN_HEADS * GDN_DV), row(LRU_WIDTH), row(LRU_WIDTH),
                _layer_spec(norm_w, layer), ret_blk, gdn_blk]
    aliases = {}
    if prev is None:
        depth = s_ret.shape[0]
        out_ret = pl.BlockSpec((depth, bt, RET_HEADS, RET_DK, RET_DV), lambda i: (0, i, 0, 0, 0))
        out_gdn = pl.BlockSpec((depth, bt, GDN_HEADS, GDN_DK, GDN_DV), lambda i: (0, i, 0, 0, 0))
    else:
        out_ret, out_gdn = ret_blk, gdn_blk
        aliases = {len(ins): 3, len(ins) + 1: 4}
        ins += list(prev)
        in_specs += [pl.BlockSpec(memory_space=pl.ANY)] * 2
    return pl.pallas_call(
        functools.partial(_smix2_kernel, first=prev is None),
        grid=(rows // bt,),
        in_specs=in_specs,
        out_specs=[row(RET_V), row(GDN_HEADS * GDN_DV), row(LRU_WIDTH), out_ret, out_gdn],
        out_shape=[sds(rows, RET_V), sds(rows, GDN_HEADS * GDN_DV), sds(rows, LRU_WIDTH),
                   sds(*s_ret.shape), sds(*s_gdn.shape)],
        input_output_aliases=aliases,
        scratch_shapes=[pltpu.VMEM((bt, RET_V), F32), pltpu.VMEM((bt, GDN_HEADS * GDN_DV), F32)],
        compiler_params=_cparams("parallel"),
        name="sample_state",
    )(*ins)


def _block_diag(w):
    nb, bs, _ = w.shape
    return jnp.einsum("nij,nm->nimj", w, jnp.eye(nb, dtype=w.dtype)).reshape(nb * bs, nb * bs)


def kernel(x_prompt, x_sample, c_prompt, c_sample, state_ret, state_gdn, state_gdn_conv, state_ssm_re, state_ssm_im, state_lru, state_lru_conv, w_ada, b_ada, ln_g, ln_b, w_ffn_up, w_ffn_down, w_in, gdn_conv_w, gdn_a_log, gdn_dt_bias, gdn_norm_w, ssm_lam_re, ssm_lam_im, ssm_log_step, ssm_b_re, ssm_b_im, ssm_c_re, ssm_c_im, ssm_d, ssm_glu_w, ssm_glu_b, lru_conv_w, lru_conv_b, lru_wa, lru_ba, lru_wx, lru_bx, lru_lam, w_branch, w_out):
    nb, t, _ = x_prompt.shape
    ns = x_sample.shape[0]
    depth = w_ada.shape[0]
    assert t % RET_CHUNK == 0 and x_sample.shape[1] == 1
    alpha = (2 * depth) ** 0.25
    tm = 256 if t % 256 == 0 else RET_CHUNK
    tm_big = 512 if t % 512 == 0 else tm
    tm_ffn = 1024 if t % 1024 == 0 else tm_big
    tc = 128 if t % 128 == 0 else 64

    wup = w_ffn_up.astype(BF16)
    wdn = w_ffn_down.astype(BF16)
    wbr = w_branch.astype(BF16)
    wout = w_out.astype(BF16)
    w_packed, w_mg = _pack_w_in(w_in)
    ln_g4 = ln_g.reshape(depth, N_SUB, 1, D_MODEL)
    ln_b4 = ln_b.reshape(depth, N_SUB, 1, D_MODEL)

    mod_p, mod_s = _ada(c_prompt, c_sample, w_ada, b_ada)
    mod_p = mod_p.reshape(depth, nb, 3 * N_SUB, 1, D_MODEL).transpose(0, 2, 1, 3, 4)

    row3 = lambda a: a.reshape(depth, 1, -1)
    lane_rows = lambda v: jnp.pad(v, ((0, 0), (0, LANE - v.shape[1]))).reshape(depth, 1, LANE)
    alog_rows, dtb_rows = lane_rows(gdn_a_log), lane_rows(gdn_dt_bias)
    norm_w = row3(gdn_norm_w)
    s5_w = (*jax.vmap(_s5_params)(ssm_lam_re, ssm_lam_im, ssm_log_step, ssm_b_re, ssm_b_im, ssm_c_re, ssm_c_im),
            row3(ssm_d), ssm_glu_w.astype(BF16), row3(ssm_glu_b))
    lru_w = (lru_conv_w, row3(lru_conv_b),
             jax.vmap(_block_diag)(lru_wa).astype(BF16), jax.vmap(_block_diag)(lru_wx).astype(BF16),
             row3(lru_ba), row3(lru_bx), row3(lru_lam))
    s_states = (state_gdn_conv.transpose(0, 2, 1, 3),
                state_ssm_re.reshape(depth, ns, SSM_STATE), state_ssm_im.reshape(depth, ns, SSM_STATE),
                state_lru, state_lru_conv.transpose(0, 2, 1, 3))

    cos_s, sin_s = _rope_tables(jnp.full((1,), PAST_LEN))
    xp = x_prompt.reshape(nb * t, D_MODEL)
    xs = x_sample.reshape(ns, D_MODEL)
    new_p, new_s = [], []
    big_s = None
    for l in range(depth):
        mp = _Mod(mod_p, l, False, t)
        ms = _Mod(mod_s, l, True)

        xp = _ffn(xp, mp, 0, 0, wup, wdn, ln_g4, ln_b4, alpha, tm_ffn)
        pc = _inproj(xp, mp, w_packed, tm_big, conv=(gdn_conv_w, nb, t))
        gcv_p = pc["gdn_conv"]
        ya, ret_p = _retention_prompt(pc["ret"], nb, t)
        yb, gdn_p = _gdn_prompt(pc["gqkv"], pc["gab"], pc["gz"], alog_rows, dtb_rows, norm_w, l, nb, t)
        to_t = lambda a: a.reshape(nb, t, -1)
        from_t = lambda a: a.reshape(nb * t, -1)
        yc_t, sre_p, sim_p, yd_t, lru_p, lcv_p = _scan_mixers(
            to_t(pc["su"]), to_t(pc["lx"]), to_t(pc["lg"]), s5_w, lru_w, l, tc)
        xp = _merge(xp, mp, ya, yb, from_t(yc_t), from_t(yd_t), w_mg, wbr, wout, ln_g4, ln_b4, alpha, tm_ffn)
        xp = _ffn(xp, mp, 2, 1, wup, wdn, ln_g4, ln_b4, alpha, tm_ffn)
        new_p.append((ret_p, gdn_p, gcv_p,
                      sre_p.reshape(nb, SSM_GROUPS, SSM_P), sim_p.reshape(nb, SSM_GROUPS, SSM_P),
                      lru_p, lcv_p.transpose(1, 0, 2)))

        xs = _ffn(xs, ms, 0, 0, wup, wdn, ln_g4, ln_b4, alpha, ns)
        sc = _inproj(xs, ms, w_packed, ns)
        (qkt, gv, gx, yc, lh, gcv_s, sre_s, sim_s, lcv_s) = _smix1(
            sc, cos_s, sin_s, s_states, (gdn_conv_w, alog_rows, dtb_rows), s5_w, lru_w, l)
        ya, yb, yd, *big_s = _smix2(qkt, sc["ret"], gv, gx, sc["ret"], sc["gz"], lh, sc["lg"],
                                    norm_w, state_ret, state_gdn, l, SUBLANE, big_s)
        xs = _merge(xs, ms, ya, yb, yc, yd, w_mg, wbr, wout, ln_g4, ln_b4, alpha, ns)
        xs = _ffn(xs, ms, 2, 1, wup, wdn, ln_g4, ln_b4, alpha, ns)
        new_s.append((gcv_s.transpose(1, 0, 2),
                      sre_s.reshape(ns, SSM_GROUPS, SSM_P), sim_s.reshape(ns, SSM_GROUPS, SSM_P),
                      lh, lcv_s.transpose(1, 0, 2)))

    ret_p, gdn_p, gcv_p, sre_p, sim_p, lru_p, lcv_p = [jnp.stack(z) for z in zip(*new_p)]
    gcv_s, sre_s, sim_s, lru_s, lcv_s = [jnp.stack(z) for z in zip(*new_s)]
    ret_s, gdn_s = big_s
    return (xp.reshape(nb, t, D_MODEL), xs.reshape(ns, 1, D_MODEL),
            ret_p, ret_s, gdn_p, gdn_s, gcv_p, gcv_s, sre_p, sre_s, sim_p, sim_s,
            lru_p, lru_s, lcv_p, lcv_s)
```
